```python
import math
import jax
import jax.numpy as jnp
from jax import lax
import numpy as np

D_MODEL = 2048
BATCH = 8
SEQ = 2048
DEPTH = 2

GRID_W = 64
CTX_LEN = 256
N_MOD = 9
FFN_DIM = 5632
A_HEADS = 8
A_HEAD_DIM = 64
A_VAL_DIM = 2 * A_HEAD_DIM
B_HEADS = 8
B_HEAD_DIM = 128
WIN_R = 8
WIN_C = 16
C_HEADS = 8
C_KEY_DIM = 64
C_VAL_DIM = 128
RET_CHUNK = 64
BRANCH_W = 1024
N_BRANCH = 3
ATTN_BLOCK = 128
ROPE_BASE = 10000.0
EPS = 1e-6
SPLIT_SIZES = (
    A_HEADS * 2 * A_HEAD_DIM,
    A_HEADS * 2 * A_HEAD_DIM,
    A_HEADS * A_VAL_DIM,
    B_HEADS * B_HEAD_DIM,
    B_HEADS * B_HEAD_DIM,
    B_HEADS * B_HEAD_DIM,
    C_HEADS * C_KEY_DIM,
    C_HEADS * C_KEY_DIM,
    C_HEADS * C_VAL_DIM,
    C_HEADS * C_VAL_DIM,
    N_BRANCH * D_MODEL,
)
IN_COLS = sum(SPLIT_SIZES)

kernel_name = 'hybrid_diffusion_prefix_trunk'


def rmsnorm(x, g):
    x32 = x.astype(jnp.float32)
    y = x32 * lax.rsqrt(jnp.mean(x32 * x32, axis=-1, keepdims=True) + EPS)
    return (y * g.astype(jnp.float32)).astype(x.dtype)


def adaln_in(t, gain, shift, scale):
    return rmsnorm(t, gain) * (1 + scale) + shift


def swiglu(u, w_in, w_out):
    a, b = jnp.split(u @ w_in, 2, axis=-1)
    return (jax.nn.silu(a) * b) @ w_out


def half_ffn_sublayer(t, pre_g, post_g, shift, scale, gate, w1, w2):
    u = adaln_in(t, pre_g, shift, scale)
    return t + 0.5 * gate * rmsnorm(swiglu(u, w1, w2), post_g)


def to_heads(t, n_heads):
    b, l, _ = t.shape
    return t.reshape(b, l, n_heads, -1).transpose(0, 2, 1, 3)


def from_heads(t):
    b, h, l, d = t.shape
    return t.transpose(0, 2, 1, 3).reshape(b, l, h * d)


def rope_1d(t, pos):
    half = t.shape[-1] // 2
    inv = ROPE_BASE ** (-jnp.arange(half, dtype=jnp.float32) / half)
    ang = pos[:, None] * inv[None, :]
    cos = jnp.cos(ang).astype(t.dtype)
    sin = jnp.sin(ang).astype(t.dtype)
    t1, t2 = t[..., :half], t[..., half:]
    return jnp.concatenate([t1 * cos - t2 * sin, t1 * sin + t2 * cos], axis=-1)


def axial_rope(t, prow, pcol):
    half = t.shape[-1] // 2
    return jnp.concatenate([rope_1d(t[..., :half], prow), rope_1d(t[..., half:], pcol)], axis=-1)


def softmax32(s):
    return jax.nn.softmax(s.astype(jnp.float32), axis=-1)


def diff_attention(qc, kc, vc, ql, kl, vl, lam, norm_g, lambda_init, with_ctx_out):
    scale = A_HEAD_DIM ** -0.5

    def attend(q, k, v):
        s1 = jnp.einsum('bhqd,bhkd->bhqk', q[..., :A_HEAD_DIM], k[..., :A_HEAD_DIM]) * scale
        s2 = jnp.einsum('bhqd,bhkd->bhqk', q[..., A_HEAD_DIM:], k[..., A_HEAD_DIM:]) * scale
        p = softmax32(s1) - lam * softmax32(s2)
        return jnp.einsum('bhqk,bhkv->bhqv', p.astype(v.dtype), v)

    def finish(o):
        return from_heads(rmsnorm(o, norm_g) * (1.0 - lambda_init))

    b, h, l, dq = ql.shape
    nb = l // ATTN_BLOCK
    k_all = jnp.concatenate([kc, kl], axis=2)
    v_all = jnp.concatenate([vc, vl], axis=2)
    q_blocks = ql.reshape(b, h, nb, ATTN_BLOCK, dq).transpose(2, 0, 1, 3, 4)
    ol = lax.map(lambda qb: attend(qb, k_all, v_all), q_blocks)
    ol = ol.transpose(1, 2, 0, 3, 4).reshape(b, h, l, -1)
    oc = finish(attend(qc, kc, vc)) if with_ctx_out else None
    return oc, finish(ol)


def neighbourhood_attention(qc, kc, vc, ql, kl, vl, rpb, rows, with_ctx_out):
    b, h, l, d = ql.shape
    scale = d ** -0.5
    wr = min(WIN_R, rows)
    ncb = GRID_W // WIN_C
    band = 2 * WIN_C
    qcol = jnp.arange(GRID_W).reshape(ncb, WIN_C)
    kcol = (jnp.clip(jnp.arange(ncb) * WIN_C - WIN_C // 2, 0, GRID_W - band)[:, None]
            + jnp.arange(band)[None, :])
    cstart = jnp.clip(qcol - WIN_C // 2, 0, GRID_W - WIN_C)
    col_ok = ((kcol[:, None, :] >= cstart[:, :, None])
              & (kcol[:, None, :] < cstart[:, :, None] + WIN_C))
    dc_idx = jnp.clip(kcol[:, None, :] - qcol[:, :, None] + WIN_C - 1, 0, 2 * WIN_C - 2)
    qg, kg, vg = (t.reshape(b, h, rows, GRID_W, d) for t in (ql, kl, vl))

    def row_block(r):
        rs = jnp.clip(r - wr // 2, 0, rows - wr)
        krow = lax.dynamic_slice_in_dim(kg, rs, wr, axis=2)[:, :, :, kcol]
        vrow = lax.dynamic_slice_in_dim(vg, rs, wr, axis=2)[:, :, :, kcol]
        qb = lax.dynamic_index_in_dim(qg, r, axis=2, keepdims=False).reshape(b, h, ncb, WIN_C, d)
        dr_idx = rs + jnp.arange(wr) - r + WIN_R - 1
        bias = rpb[:, dr_idx[None, None, :, None], dc_idx[:, :, None, :]]
        s_lat = (jnp.einsum('bhjqd,bhrjkd->bhjqrk', qb, krow).astype(jnp.float32) * scale
                 + bias.astype(jnp.float32))
        s_lat = jnp.where(col_ok[:, :, None, :], s_lat, -jnp.inf).reshape(b, h, ncb, WIN_C, wr * band)
        s_ctx = jnp.einsum('bhjqd,bhkd->bhjqk', qb, kc).astype(jnp.float32) * scale
        p = softmax32(jnp.concatenate([s_lat, s_ctx], axis=-1)).astype(vl.dtype)
        p_lat = p[..., :wr * band].reshape(b, h, ncb, WIN_C, wr, band)
        o = (jnp.einsum('bhjqrk,bhrjkd->bhjqd', p_lat, vrow)
             + jnp.einsum('bhjqk,bhkd->bhjqd', p[..., wr * band:], vc))
        return o.reshape(b, h, GRID_W, d)

    ol = lax.map(row_block, jnp.arange(rows))
    ol = from_heads(ol.transpose(1, 2, 0, 3, 4).reshape(b, h, l, d))
    oc = None
    if with_ctx_out:
        pc = softmax32(jnp.einsum('bhqd,bhkd->bhqk', qc, kc) * scale).astype(vc.dtype)
        oc = from_heads(jnp.einsum('bhqk,bhkd->bhqd', pc, vc))
    return oc, ol


def retention_chunkwise(q, k, v, log_g, s0):
    b, h, l, dk = q.shape
    dv = v.shape[-1]
    n = l // RET_CHUNK
    idx = jnp.arange(RET_CHUNK, dtype=jnp.float32)
    lg = log_g[:, None]
    dist = idx[:, None] - idx[None, :]
    intra = jnp.where(dist >= 0, jnp.exp(lg[:, :, None] * jnp.maximum(dist, 0.0)), 0.0)
    q_dec = jnp.exp(lg * (idx + 1.0))[..., None]
    k_dec = jnp.exp(lg * (RET_CHUNK - 1.0 - idx))[..., None]
    c_dec = jnp.exp(lg * RET_CHUNK)[..., None]

    def chunks(t):
        return t.reshape(b, h, n, RET_CHUNK, -1).transpose(2, 0, 1, 3, 4)

    def step(s, inp):
        qi, ki, vi = inp
        att = jnp.einsum('bhqd,bhkd->bhqk', qi, ki) * intra
        o = (jnp.einsum('bhqk,bhkv->bhqv', att, vi)
             + jnp.einsum('bhqd,bhdv->bhqv', qi * q_dec, s))
        s = s * c_dec + jnp.einsum('bhkd,bhkv->bhdv', ki * k_dec, vi)
        return s, o

    s, o = lax.scan(step, s0, (chunks(q), chunks(k), chunks(v)))
    return o.transpose(1, 2, 0, 3, 4).reshape(b, h, l, dv), s


def retention(qc, kc, vc, ql, kl, vl, decay_logit, norm_g, with_ctx_out):
    log_g = jax.nn.log_sigmoid(decay_logit.astype(jnp.float32))
    qc, kc, vc, ql, kl, vl = (t.astype(jnp.float32) for t in (qc, kc, vc, ql, kl, vl))
    flip = lambda t: jnp.flip(t, axis=2)
    b, h, _, dk = ql.shape
    s0 = jnp.zeros((b, h, dk, vl.shape[-1]), jnp.float32)
    oc_f, sc_f = retention_chunkwise(qc, kc, vc, log_g[0], s0)
    oc_b, sc_b = retention_chunkwise(flip(qc), flip(kc), flip(vc), log_g[1], s0)
    ol_f, _ = retention_chunkwise(ql, kl, vl, log_g[0], sc_f)
    ol_b, _ = retention_chunkwise(flip(ql), flip(kl), flip(vl), log_g[1], sc_b)
    finish = lambda o: from_heads(rmsnorm(o, norm_g))
    oc = finish(oc_f + flip(oc_b)) if with_ctx_out else None
    return oc, finish(ol_f + flip(ol_b))


def token_mixer(uc, ul, w_in, diff_lambda, diff_norm, na_rpb, ret_decay, ret_norm, w_branch, w_out,
                lambda_init, with_ctx_out):
    b, l, _ = ul.shape
    rows = l // GRID_W
    pos = jnp.arange(l)
    prow = (pos // GRID_W).astype(jnp.float32)
    pcol = (pos % GRID_W).astype(jnp.float32)
    offsets = [int(o) for o in np.cumsum(SPLIT_SIZES)[:-1]]
    aq, ak, av, bq, bk, bv, cq, ck, cv, cg, gates = jnp.split(ul @ w_in, offsets, axis=-1)
    aqc, akc, avc, bqc, bkc, bvc, cqc, ckc, cvc, cgc, gatesc = jnp.split(uc @ w_in, offsets, axis=-1)
    rope = lambda t: axial_rope(t, prow, pcol)
    rope2 = lambda t: jnp.concatenate([rope(t[..., :A_HEAD_DIM]), rope(t[..., A_HEAD_DIM:])], axis=-1)

    lv = diff_lambda.astype(jnp.float32)
    lam = jnp.exp(jnp.sum(lv[0] * lv[1])) - jnp.exp(jnp.sum(lv[2] * lv[3])) + lambda_init
    oa_c, oa_l = diff_attention(
        to_heads(aqc, A_HEADS), to_heads(akc, A_HEADS), to_heads(avc, A_HEADS),
        rope2(to_heads(aq, A_HEADS)), rope2(to_heads(ak, A_HEADS)), to_heads(av, A_HEADS),
        lam, diff_norm, lambda_init, with_ctx_out)

    ob_c, ob_l = neighbourhood_attention(
        to_heads(bqc, B_HEADS), to_heads(bkc, B_HEADS), to_heads(bvc, B_HEADS),
        to_heads(bq, B_HEADS), to_heads(bk, B_HEADS), to_heads(bv, B_HEADS),
        na_rpb, rows, with_ctx_out)

    kscale = C_KEY_DIM ** -0.5
    or_c, or_l = retention(
        to_heads(cqc, C_HEADS), to_heads(ckc, C_HEADS) * kscale, to_heads(cvc, C_HEADS),
        rope(to_heads(cq, C_HEADS)), rope(to_heads(ck, C_HEADS)) * kscale, to_heads(cv, C_HEADS),
        ret_decay, ret_norm, with_ctx_out)

    def merge(oa, ob, orr, g_ret, g_lin):
        yr = jax.nn.silu(g_ret) * orr.astype(g_ret.dtype)
        ga, gb, gr = jnp.split(jax.nn.sigmoid(g_lin), N_BRANCH, axis=-1)
        merged = ga * (oa @ w_branch[0]) + gb * (ob @ w_branch[1]) + gr * (yr @ w_branch[2])
        return merged @ w_out

    yl = merge(oa_l, ob_l, or_l, cg, gates)
    yc = merge(oa_c, ob_c, or_c, cgc, gatesc) if with_ctx_out else None
    return yc, yl


def setup_inputs(seed: int = 0) -> dict:
    key = jax.random.key(seed)
    ks = jax.random.split(key, 18)
    f32 = jnp.float32

    def normal(k, shape):
        return jax.random.normal(k, shape, f32)

    def dense(k, shape, fan_in, gain=1.0):
        return normal(k, shape) * (gain * fan_in ** -0.5)

    def near_one(k, shape):
        return 1.0 + 0.02 * normal(k, shape)

    gamma0 = 1.0 - 2.0 ** (-5.0 - jnp.arange(C_HEADS, dtype=f32))
    decay_logit = jnp.log(gamma0) - jnp.log1p(-gamma0)
    return {
        'x': normal(ks[0], (BATCH, SEQ, D_MODEL)),
        'c': normal(ks[1], (BATCH, D_MODEL)),
        'ctx': normal(ks[2], (BATCH, CTX_LEN, D_MODEL)),
        'c_ctx': normal(ks[3], (D_MODEL,)),
        'w_mod': dense(ks[4], (DEPTH, D_MODEL, N_MOD * D_MODEL), D_MODEL, 0.5),
        'b_mod': 0.01 * normal(ks[5], (DEPTH, N_MOD * D_MODEL)),
        'pre_norm': near_one(ks[6], (DEPTH, 3, D_MODEL)),
        'post_norm': near_one(ks[7], (DEPTH, 3, D_MODEL)),
        'ffn_w_in': dense(ks[8], (DEPTH, 2, D_MODEL, 2 * FFN_DIM), D_MODEL),
        'ffn_w_out': dense(ks[9], (DEPTH, 2, FFN_DIM, D_MODEL), FFN_DIM),
        'w_in': dense(ks[10], (DEPTH, D_MODEL, IN_COLS), D_MODEL),
        'diff_lambda': 0.1 * normal(ks[11], (DEPTH, 4, A_HEAD_DIM)),
        'diff_norm': near_one(ks[12], (DEPTH, A_VAL_DIM)),
        'na_rpb': 0.1 * normal(ks[13], (DEPTH, B_HEADS, 2 * WIN_R - 1, 2 * WIN_C - 1)),
        'ret_decay': decay_logit + 0.05 * normal(ks[14], (DEPTH, 2, C_HEADS)),
        'ret_norm': near_one(ks[15], (DEPTH, C_VAL_DIM)),
        'w_branch': dense(ks[16], (DEPTH, N_BRANCH, BRANCH_W, D_MODEL), BRANCH_W),
        'w_out': dense(ks[17], (DEPTH, D_MODEL, D_MODEL), D_MODEL),
    }


def reference(x, c, ctx, c_ctx, w_mod, b_mod, pre_norm, post_norm, ffn_w_in, ffn_w_out, w_in,
              diff_lambda, diff_norm, na_rpb, ret_decay, ret_norm, w_branch, w_out):
    b, _, d = x.shape
    h, hc = x, ctx
    for l in range(DEPTH):
        with_ctx_out = l < DEPTH - 1
        lambda_init = 0.8 - 0.6 * math.exp(-0.3 * l)
        mod = (jax.nn.silu(c) @ w_mod[l] + b_mod[l]).reshape(b, N_MOD, d).transpose(1, 0, 2)[:, :, None, :]
        modc = (jax.nn.silu(c_ctx) @ w_mod[l] + b_mod[l]).reshape(N_MOD, d)
        h = half_ffn_sublayer(h, pre_norm[l, 0], post_norm[l, 0], mod[0], mod[1], mod[2],
                              ffn_w_in[l, 0], ffn_w_out[l, 0])
        hc = half_ffn_sublayer(hc, pre_norm[l, 0], post_norm[l, 0], modc[0], modc[1], modc[2],
                               ffn_w_in[l, 0], ffn_w_out[l, 0])
        yc, yl = token_mixer(adaln_in(hc, pre_norm[l, 1], modc[3], modc[4]),
                             adaln_in(h, pre_norm[l, 1], mod[3], mod[4]),
                             w_in[l], diff_lambda[l], diff_norm[l], na_rpb[l], ret_decay[l], ret_norm[l],
                             w_branch[l], w_out[l], lambda_init, with_ctx_out)
        h = h + mod[5] * rmsnorm(yl, post_norm[l, 1])
        h = half_ffn_sublayer(h, pre_norm[l, 2], post_norm[l, 2], mod[6], mod[7], mod[8],
                              ffn_w_in[l, 1], ffn_w_out[l, 1])
        if with_ctx_out:
            hc = hc + modc[5] * rmsnorm(yc, post_norm[l, 1])
            hc = half_ffn_sublayer(hc, pre_norm[l, 2], post_norm[l, 2], modc[6], modc[7], modc[8],
                                   ffn_w_in[l, 1], ffn_w_out[l, 1])
    return h
```

```python
import functools
import math

import numpy as np
import jax
import jax.numpy as jnp
from jax import lax
from jax.experimental import pallas as pl
from jax.experimental.pallas import tpu as pltpu

D_MODEL = 2048
SEQ = 2048
DEPTH = 2
GRID_W = 64
CTX_LEN = 256
N_MOD = 9
FFN_DIM = 5632
A_HEADS = 8
A_HEAD_DIM = 64
B_HEADS = 8
B_HEAD_DIM = 128
WIN_R = 8
WIN_C = 16
C_HEADS = 8
C_KEY_DIM = 64
BRANCH_W = 1024
ROPE_BASE = 10000.0
EPS = 1e-6

COL_AQ, COL_AK, COL_AV = 0, 1024, 2048
COL_BQ, COL_BK, COL_BV = 3072, 4096, 5120
COL_CQ, COL_CK, COL_CV, COL_CG = 6144, 6656, 7168, 8192
COL_GATES = 9216
IN_COLS = 15360

LANES = 128
QBLK = 256
ROWS_PER_QBLK = QBLK // GRID_W
NBR_KROWS = ROWS_PER_QBLK + WIN_R - 1
NBR_KEYS = NBR_KROWS * GRID_W
VMEM_LIMIT = 56 * 1024 * 1024

BF16 = jnp.bfloat16
F32 = jnp.float32
NT_DIMS = (((1,), (1,)), ((), ()))
TN_DIMS = (((0,), (0,)), ((), ()))


def _params(sem):
    return pltpu.CompilerParams(dimension_semantics=sem, vmem_limit_bytes=VMEM_LIMIT)


def _rms(x):
    return x * lax.rsqrt(jnp.mean(x * x, axis=-1, keepdims=True) + EPS)


def _silu(x):
    return x * jax.nn.sigmoid(x)


def _mod_kernel(cc_ref, w_ref, b_ref, o_ref):
    s = _silu(cc_ref[...]).astype(BF16)
    o_ref[0] = jnp.dot(s, w_ref[0].astype(BF16), preferred_element_type=F32) + b_ref[0]


def _mod_call(cc, w_mod, b_mod):
    depth, d, n = w_mod.shape
    rows = cc.shape[0]
    tn = 1024
    return pl.pallas_call(
        _mod_kernel,
        grid=(depth, n // tn),
        in_specs=[
            pl.BlockSpec((rows, d), lambda l, j: (0, 0)),
            pl.BlockSpec((1, d, tn), lambda l, j: (l, 0, j)),
            pl.BlockSpec((1, 1, tn), lambda l, j: (l, 0, j)),
        ],
        out_specs=pl.BlockSpec((1, rows, tn), lambda l, j: (l, 0, j)),
        out_shape=jax.ShapeDtypeStruct((depth, rows, n), F32),
        compiler_params=_params(("parallel", "parallel")),
        name="mod_vectors",
    )(cc, w_mod, b_mod.reshape(depth, 1, n))


def _ffn_kernel(h_ref, mod_ref, pre_ref, post_ref, w1a_ref, w1b_ref, w2_ref, o_ref, u_ref, *, k0, nf):
    j = pl.program_id(1)

    @pl.when(j == 0)
    def _():
        shift = mod_ref[0, k0:k0 + 1, :]
        scale = mod_ref[0, k0 + 1:k0 + 2, :]
        u = _rms(h_ref[...]) * pre_ref[...] * (1.0 + scale) + shift
        u_ref[...] = u.astype(BF16)

    u = u_ref[...]
    a = jnp.dot(u, w1a_ref[...], preferred_element_type=F32)
    b = jnp.dot(u, w1b_ref[...], preferred_element_type=F32)
    y = jnp.dot((_silu(a) * b).astype(BF16), w2_ref[...], preferred_element_type=F32)

    @pl.when(j == 0)
    def _():
        o_ref[...] = y

    @pl.when(j > 0)
    def _():
        o_ref[...] += y

    @pl.when(j == nf - 1)
    def _():
        gate = mod_ref[0, k0 + 2:k0 + 3, :]
        o_ref[...] = h_ref[...] + 0.5 * gate * (_rms(o_ref[...]) * post_ref[...])


def _ffn_call(h, mod_l, k0, pre_g, post_g, w1, w2, nrows, tm, n_lat_seg):
    d = h.shape[1]
    f = w2.shape[0]
    tf = 512
    nf = f // tf
    seg = lambda i, j: (jnp.minimum((i * tm) // SEQ, n_lat_seg), 0, 0)
    return pl.pallas_call(
        functools.partial(_ffn_kernel, k0=k0, nf=nf),
        grid=(nrows // tm, nf),
        in_specs=[
            pl.BlockSpec((tm, d), lambda i, j: (i, 0)),
            pl.BlockSpec((1, N_MOD, d), seg),
            pl.BlockSpec((1, d), lambda i, j: (0, 0)),
            pl.BlockSpec((1, d), lambda i, j: (0, 0)),
            pl.BlockSpec((d, tf), lambda i, j: (0, j)),
            pl.BlockSpec((d, tf), lambda i, j: (0, j + nf)),
            pl.BlockSpec((tf, d), lambda i, j: (j, 0)),
        ],
        out_specs=pl.BlockSpec((tm, d), lambda i, j: (i, 0)),
        out_shape=jax.ShapeDtypeStruct((nrows, d), F32),
        scratch_shapes=[pltpu.VMEM((tm, d), BF16)],
        compiler_params=_params(("parallel", "arbitrary")),
        name="ffn_sublayer",
    )(h, mod_l, pre_g.reshape(1, d), post_g.reshape(1, d), w1, w1, w2)


def _inproj_kernel(h_ref, mod_ref, pre_ref, cos_ref, sa_ref, sb_ref, w_ref, o_ref, u_ref, *, tm, tn, n_lat_rows):
    i = pl.program_id(0)
    j = pl.program_id(1)

    @pl.when(j == 0)
    def _():
        shift = mod_ref[0, 3:4, :]
        scale = mod_ref[0, 4:5, :]
        u = _rms(h_ref[...]) * pre_ref[...] * (1.0 + scale) + shift
        u_ref[...] = u.astype(BF16)

    y = jnp.dot(u_ref[...], w_ref[...], preferred_element_type=F32)
    col = j * tn
    is_rot_col = (col < COL_AV) | ((col >= COL_CQ) & (col < COL_CV))
    rot = is_rot_col & (i * tm < n_lat_rows)

    @pl.when(rot)
    def _():
        cos, sa, sb = cos_ref[...], sa_ref[...], sb_ref[...]
        for c in range(tn // LANES):
            t = y[:, c * LANES:(c + 1) * LANES]
            r = t * cos + pltpu.roll(t, LANES - 16, 1) * sa + pltpu.roll(t, 16, 1) * sb
            o_ref[:, c * LANES:(c + 1) * LANES] = r.astype(BF16)

    @pl.when(jnp.logical_not(rot))
    def _():
        o_ref[...] = y.astype(BF16)


def _inproj_call(h, mod_l, pre_g, w, tabs, tm, n_lat_seg):
    nrows, d = h.shape
    n = w.shape[1]
    tn = 512
    n_lat_rows = n_lat_seg * SEQ
    seg = lambda i, j: (jnp.minimum((i * tm) // SEQ, n_lat_seg), 0, 0)
    pos = lambda i, j: (i % (SEQ // tm), 0)
    return pl.pallas_call(
        functools.partial(_inproj_kernel, tm=tm, tn=tn, n_lat_rows=n_lat_rows),
        grid=(nrows // tm, n // tn),
        in_specs=[
            pl.BlockSpec((tm, d), lambda i, j: (i, 0)),
            pl.BlockSpec((1, N_MOD, d), seg),
            pl.BlockSpec((1, d), lambda i, j: (0, 0)),
            pl.BlockSpec((tm, LANES), pos),
            pl.BlockSpec((tm, LANES), pos),
            pl.BlockSpec((tm, LANES), pos),
            pl.BlockSpec((d, tn), lambda i, j: (0, j)),
        ],
        out_specs=pl.BlockSpec((tm, tn), lambda i, j: (i, j)),
        out_shape=jax.ShapeDtypeStruct((nrows, n), BF16),
        scratch_shapes=[pltpu.VMEM((tm, d), BF16)],
        compiler_params=_params(("parallel", "arbitrary")),
        name="mixer_in_proj",
    )(h, mod_l, pre_g.reshape(1, d), *tabs, w)


def _rope_tables():
    pos = np.arange(SEQ)
    prow = jnp.asarray(pos // GRID_W, F32)
    pcol = jnp.asarray(pos % GRID_W, F32)
    half = 16
    inv = ROPE_BASE ** (-jnp.arange(half, dtype=F32) / half)
    lane = np.arange(LANES)
    freq = jnp.asarray(lane % half)
    use_col = jnp.asarray((lane % 64) >= 32)
    second = jnp.asarray((lane % 32) >= half)
    p = jnp.where(use_col[None, :], pcol[:, None], prow[:, None])
    ang = p * inv[freq][None, :]
    cos, sin = jnp.cos(ang), jnp.sin(ang)
    sa = jnp.where(second[None, :], 0.0, -sin)
    sb = jnp.where(second[None, :], sin, 0.0)
    return cos, sa, sb


def _qblk_index(nb, col0):
    lat_blocks = SEQ // QBLK

    def index(b, h, j):
        row = jnp.where(j < lat_blocks, b * lat_blocks + j, nb * lat_blocks + b)
        return (row, col0 + h)
    return index


def _diff_kernel(q_ref, kl_ref, vl_ref, kc_ref, vc_ref, lam_ref, g_ref, o_ref, k_all, v_all, *, lambda_init):
    j = pl.program_id(2)

    @pl.when(j == 0)
    def _():
        k_all[0:CTX_LEN, :] = kc_ref[...]
        k_all[CTX_LEN:, :] = kl_ref[...]
        v_all[0:CTX_LEN, :] = vc_ref[...]
        v_all[CTX_LEN:, :] = vl_ref[...]

    lv = lam_ref[...]
    lam = (jnp.exp(jnp.sum(lv[0:1] * lv[1:2], axis=-1, keepdims=True))
           - jnp.exp(jnp.sum(lv[2:3] * lv[3:4], axis=-1, keepdims=True)) + lambda_init)
    lane = lax.broadcasted_iota(jnp.int32, (1, LANES), 1)
    q = q_ref[...] * (A_HEAD_DIM ** -0.5)
    zero = jnp.zeros_like(q)
    q1 = jnp.where(lane < A_HEAD_DIM, q, zero)
    q2 = jnp.where(lane >= A_HEAD_DIM, q, zero)

    def attend(nk):
        k = k_all[0:nk, :]

        def softmax(qm):
            s = lax.dot_general(qm, k, NT_DIMS, preferred_element_type=F32)
            e = jnp.exp(s - jnp.max(s, axis=-1, keepdims=True))
            return e / jnp.sum(e, axis=-1, keepdims=True)

        p = softmax(q1) - lam * softmax(q2)
        o = jnp.dot(p.astype(BF16), v_all[0:nk, :], preferred_element_type=F32)
        o_ref[...] = (_rms(o) * g_ref[...] * (1.0 - lambda_init)).astype(BF16)

    lat_blocks = SEQ // QBLK

    @pl.when(j < lat_blocks)
    def _():
        attend(CTX_LEN + SEQ)

    @pl.when(j >= lat_blocks)
    def _():
        attend(CTX_LEN)


def _diff_call(proj, diff_lambda, diff_norm, lambda_init, nb, with_ctx):
    lat_blocks = SEQ // QBLK
    nj = lat_blocks + (1 if with_ctx else 0)
    nrows_out = nb * SEQ + (nb * CTX_LEN if with_ctx else 0)
    cb = lambda col: col // LANES
    ctx_blk0 = nb * SEQ // CTX_LEN
    return pl.pallas_call(
        functools.partial(_diff_kernel, lambda_init=lambda_init),
        grid=(nb, A_HEADS, nj),
        in_specs=[
            pl.BlockSpec((QBLK, LANES), _qblk_index(nb, cb(COL_AQ))),
            pl.BlockSpec((SEQ, LANES), lambda b, h, j: (b, cb(COL_AK) + h)),
            pl.BlockSpec((SEQ, LANES), lambda b, h, j: (b, cb(COL_AV) + h)),
            pl.BlockSpec((CTX_LEN, LANES), lambda b, h, j: (ctx_blk0 + b, cb(COL_AK) + h)),
            pl.BlockSpec((CTX_LEN, LANES), lambda b, h, j: (ctx_blk0 + b, cb(COL_AV) + h)),
            pl.BlockSpec((4, A_HEAD_DIM), lambda b, h, j: (0, 0)),
            pl.BlockSpec((1, LANES), lambda b, h, j: (0, 0)),
        ],
        out_specs=pl.BlockSpec((QBLK, LANES), _qblk_index(nb, 0)),
        out_shape=jax.ShapeDtypeStruct((nrows_out, BRANCH_W), BF16),
        scratch_shapes=[pltpu.VMEM((CTX_LEN + SEQ, LANES), BF16), pltpu.VMEM((CTX_LEN + SEQ, LANES), BF16)],
        compiler_params=_params(("parallel", "parallel", "arbitrary")),
        name="diff_attention",
    )(proj, proj, proj, proj, proj, diff_lambda, diff_norm.reshape(1, LANES))


def _nbr_bias_index():
    n_rows = SEQ // GRID_W
    r0s = (0, ROWS_PER_QBLK, n_rows - ROWS_PER_QBLK)
    qi = np.arange(QBLK) // GRID_W
    qc = np.arange(QBLK) % GRID_W
    kj = np.arange(NBR_KEYS) // GRID_W
    kc = np.arange(NBR_KEYS) % GRID_W
    dr = np.zeros((3, QBLK, NBR_KEYS), np.int32)
    ok = np.zeros((3, QBLK, NBR_KEYS), bool)
    cs = np.clip(qc - WIN_C // 2, 0, GRID_W - WIN_C)
    col_ok = (kc[None, :] >= cs[:, None]) & (kc[None, :] < cs[:, None] + WIN_C)
    dc = np.clip(kc[None, :] - qc[:, None] + WIN_C - 1, 0, 2 * WIN_C - 2).astype(np.int32)
    for cfg, r0 in enumerate(r0s):
        ks = np.clip(r0 - WIN_R // 2, 0, n_rows - NBR_KROWS)
        r = r0 + qi
        rs = np.clip(r - WIN_R // 2, 0, n_rows - WIN_R)
        kr = ks + kj
        row_ok = (kr[None, :] >= rs[:, None]) & (kr[None, :] < rs[:, None] + WIN_R)
        ok[cfg] = row_ok & col_ok
        dr[cfg] = np.clip(kr[None, :] - r[:, None] + WIN_R - 1, 0, 2 * WIN_R - 2)
    return dr, np.broadcast_to(dc, dr.shape), ok


def _nbr_bias_table(rpb):
    dr, dc, ok = _nbr_bias_index()
    return jnp.where(ok[None], rpb[:, dr, dc].astype(F32), -1e30)


def _nbr_kernel(q_ref, kl_ref, vl_ref, kc_ref, vc_ref, bias_ref, o_ref):
    j = pl.program_id(2)
    lat_blocks = SEQ // QBLK
    scale = B_HEAD_DIM ** -0.5
    q = q_ref[...]
    s_ctx = lax.dot_general(q, kc_ref[...], NT_DIMS, preferred_element_type=F32) * scale

    @pl.when(j < lat_blocks)
    def _():
        cfg = jnp.where(j == 0, 0, jnp.where(j == lat_blocks - 1, 2, 1))
        start = jnp.clip((j * ROWS_PER_QBLK - WIN_R // 2) * GRID_W, 0, SEQ - NBR_KEYS)
        rows = pl.ds(pl.multiple_of(start, GRID_W), NBR_KEYS)
        s_lat = (lax.dot_general(q, kl_ref[rows, :], NT_DIMS, preferred_element_type=F32) * scale
                 + bias_ref[0, cfg])
        m = jnp.maximum(jnp.max(s_lat, axis=-1, keepdims=True), jnp.max(s_ctx, axis=-1, keepdims=True))
        e_lat = jnp.exp(s_lat - m)
        e_ctx = jnp.exp(s_ctx - m)
        l = jnp.sum(e_lat, axis=-1, keepdims=True) + jnp.sum(e_ctx, axis=-1, keepdims=True)
        o = (jnp.dot(e_lat.astype(BF16), vl_ref[rows, :], preferred_element_type=F32)
             + jnp.dot(e_ctx.astype(BF16), vc_ref[...], preferred_element_type=F32))
        o_ref[...] = (o / l).astype(BF16)

    @pl.when(j >= lat_blocks)
    def _():
        e = jnp.exp(s_ctx - jnp.max(s_ctx, axis=-1, keepdims=True))
        o = jnp.dot(e.astype(BF16), vc_ref[...], preferred_element_type=F32)
        o_ref[...] = (o / jnp.sum(e, axis=-1, keepdims=True)).astype(BF16)


def _nbr_call(proj, bias_tab, nb, with_ctx):
    lat_blocks = SEQ // QBLK
    nj = lat_blocks + (1 if with_ctx else 0)
    nrows_out = nb * SEQ + (nb * CTX_LEN if with_ctx else 0)
    cb = lambda col: col // LANES
    ctx_blk0 = nb * SEQ // CTX_LEN
    swap = lambda f: (lambda h, b, j: f(b, h, j))
    return pl.pallas_call(
        _nbr_kernel,
        grid=(B_HEADS, nb, nj),
        in_specs=[
            pl.BlockSpec((QBLK, LANES), swap(_qblk_index(nb, cb(COL_BQ)))),
            pl.BlockSpec((SEQ, LANES), lambda h, b, j: (b, cb(COL_BK) + h)),
            pl.BlockSpec((SEQ, LANES), lambda h, b, j: (b, cb(COL_BV) + h)),
            pl.BlockSpec((CTX_LEN, LANES), lambda h, b, j: (ctx_blk0 + b, cb(COL_BK) + h)),
            pl.BlockSpec((CTX_LEN, LANES), lambda h, b, j: (ctx_blk0 + b, cb(COL_BV) + h)),
            pl.BlockSpec((1, 3, QBLK, NBR_KEYS), lambda h, b, j: (h, 0, 0, 0)),
        ],
        out_specs=pl.BlockSpec((QBLK, LANES), swap(_qblk_index(nb, 0))),
        out_shape=jax.ShapeDtypeStruct((nrows_out, BRANCH_W), BF16),
        compiler_params=_params(("parallel", "parallel", "arbitrary")),
        name="nbr_attention",
    )(proj, proj, proj, proj, proj, bias_tab)


def _ret_kernel(ql_ref, kl_ref, vl_ref, gl_ref, qc_ref, kc_ref, vc_ref, gc_ref, dec_ref, rn_ref, o_ref,
                buf_ref, of_ref, st_ref, *, with_ctx):
    j = pl.program_id(2)
    ch = QBLK
    n_chunks = SEQ // ch
    kscale = C_KEY_DIM ** -0.5

    @pl.when(j == 0)
    def _():
        ii = lax.broadcasted_iota(jnp.int32, (ch, ch), 0)
        jj = lax.broadcasted_iota(jnp.int32, (ch, ch), 1)
        dist = (ii - jj).astype(F32)
        pos = lax.broadcasted_iota(jnp.int32, (ch, 1), 0).astype(F32)
        lane = lax.broadcasted_iota(jnp.int32, (1, LANES), 1)
        rn = rn_ref[...]

        def log_sigmoid(x):
            return -(jnp.log1p(jnp.exp(-jnp.abs(x))) + jnp.maximum(-x, 0.0))

        heads = []
        for e in range(2):
            lgf = log_sigmoid(dec_ref[0, 0, e:e + 1, 0:1])
            lgb = log_sigmoid(dec_ref[0, 1, e:e + 1, 0:1])
            heads.append(dict(
                mask=(lane >= e * C_KEY_DIM) & (lane < (e + 1) * C_KEY_DIM),
                intra_f=jnp.where(dist >= 0, jnp.exp(lgf * jnp.maximum(dist, 0.0)), 0.0),
                intra_b=jnp.where(dist <= 0, jnp.exp(lgb * jnp.maximum(-dist, 0.0)), 0.0),
                qdec_f=jnp.exp(lgf * (pos + 1.0)), kdec_f=jnp.exp(lgf * (ch - 1.0 - pos)),
                cdec_f=jnp.exp(lgf * ch),
                qdec_b=jnp.exp(lgb * (ch - pos)), kdec_b=jnp.exp(lgb * pos),
                cdec_b=jnp.exp(lgb * ch),
                vs=slice(e * LANES, (e + 1) * LANES),
            ))

        def finish(o, g):
            return (_silu(g.astype(F32)) * (_rms(o) * rn)).astype(BF16)

        qc = qc_ref[...].astype(F32)
        kc = kc_ref[...].astype(F32) * kscale
        for e, hd in enumerate(heads):
            v = vc_ref[:, hd["vs"]]
            st_ref[e] = lax.dot_general((kc * hd["kdec_f"]).astype(BF16), v, TN_DIMS, preferred_element_type=F32)
            st_ref[2 + e] = lax.dot_general((kc * hd["kdec_b"]).astype(BF16), v, TN_DIMS,
                                            preferred_element_type=F32)
            if with_ctx:
                qe = jnp.where(hd["mask"], qc, 0.0).astype(BF16)
                att = lax.dot_general(qe, kc.astype(BF16), NT_DIMS, preferred_element_type=F32)
                att = att * (hd["intra_f"] + hd["intra_b"])
                o = jnp.dot(att.astype(BF16), v, preferred_element_type=F32)
                buf_ref[n_chunks * ch:(n_chunks + 1) * ch, hd["vs"]] = finish(o, gc_ref[:, hd["vs"]])

        def sweep(c, fwd):
            rows = pl.ds(pl.multiple_of(c * ch, ch), ch)
            q = ql_ref[rows, :].astype(F32)
            k = kl_ref[rows, :].astype(F32) * kscale
            kb = k.astype(BF16)
            for e, hd in enumerate(heads):
                d = "f" if fwd else "b"
                si = e if fwd else 2 + e
                v = vl_ref[rows, hd["vs"]]
                qe = jnp.where(hd["mask"], q, 0.0)
                att = lax.dot_general(qe.astype(BF16), kb, NT_DIMS, preferred_element_type=F32) * hd["intra_" + d]
                st = st_ref[si]
                o = (jnp.dot(att.astype(BF16), v, preferred_element_type=F32)
                     + jnp.dot((qe * hd["qdec_" + d]).astype(BF16), st.astype(BF16), preferred_element_type=F32))
                st_ref[si] = st * hd["cdec_" + d] + lax.dot_general(
                    (k * hd["kdec_" + d]).astype(BF16), v, TN_DIMS, preferred_element_type=F32)
                if fwd:
                    of_ref[rows, hd["vs"]] = o
                else:
                    buf_ref[rows, hd["vs"]] = finish(of_ref[rows, hd["vs"]] + o, gl_ref[rows, hd["vs"]])

        def fwd_body(c, carry):
            sweep(c, True)
            return carry

        def bwd_body(t, carry):
            sweep(n_chunks - 1 - t, False)
            return carry

        lax.fori_loop(0, n_chunks, fwd_body, 0)
        lax.fori_loop(0, n_chunks, bwd_body, 0)

    o_ref[...] = buf_ref[pl.ds(pl.multiple_of(j * ch, ch), ch), :]


def _ret_call(proj, ret_decay, ret_norm, nb, with_ctx):
    lat_blocks = SEQ // QBLK
    nj = lat_blocks + (1 if with_ctx else 0)
    nrows_out = nb * SEQ + (nb * CTX_LEN if with_ctx else 0)
    pairs = C_HEADS // 2
    ctx_blk0 = nb * SEQ // CTX_LEN
    w2 = 2 * LANES
    dec = jnp.broadcast_to(ret_decay.astype(F32).reshape(2, pairs, 2).transpose(1, 0, 2)[..., None],
                           (pairs, 2, 2, LANES))
    return pl.pallas_call(
        functools.partial(_ret_kernel, with_ctx=with_ctx),
        grid=(nb, pairs, nj),
        in_specs=[
            pl.BlockSpec((SEQ, LANES), lambda b, p, j: (b, COL_CQ // LANES + p)),
            pl.BlockSpec((SEQ, LANES), lambda b, p, j: (b, COL_CK // LANES + p)),
            pl.BlockSpec((SEQ, w2), lambda b, p, j: (b, COL_CV // w2 + p)),
            pl.BlockSpec((SEQ, w2), lambda b, p, j: (b, COL_CG // w2 + p)),
            pl.BlockSpec((CTX_LEN, LANES), lambda b, p, j: (ctx_blk0 + b, COL_CQ // LANES + p)),
            pl.BlockSpec((CTX_LEN, LANES), lambda b, p, j: (ctx_blk0 + b, COL_CK // LANES + p)),
            pl.BlockSpec((CTX_LEN, w2), lambda b, p, j: (ctx_blk0 + b, COL_CV // w2 + p)),
            pl.BlockSpec((CTX_LEN, w2), lambda b, p, j: (ctx_blk0 + b, COL_CG // w2 + p)),
            pl.BlockSpec((1, 2, 2, LANES), lambda b, p, j: (p, 0, 0, 0)),
            pl.BlockSpec((1, LANES), lambda b, p, j: (0, 0)),
        ],
        out_specs=pl.BlockSpec((QBLK, w2), _qblk_index(nb, 0)),
        out_shape=jax.ShapeDtypeStruct((nrows_out, BRANCH_W), BF16),
        scratch_shapes=[
            pltpu.VMEM((SEQ + CTX_LEN, w2), BF16),
            pltpu.VMEM((SEQ, w2), F32),
            pltpu.VMEM((4, LANES, LANES), F32),
        ],
        compiler_params=_params(("parallel", "parallel", "arbitrary")),
        name="retention",
    )(proj, proj, proj, proj, proj, proj, proj, proj, dec, ret_norm.reshape(1, LANES))


def _merge_kernel(h_ref, mod_ref, post_ref, oa_ref, ob_ref, yr_ref, ga_ref, gb_ref, gr_ref, wb_ref, wo_ref,
                  o_ref, *, nn):
    n = pl.program_id(1)

    def branch(x_ref, g_ref, k):
        g = jax.nn.sigmoid(g_ref[...].astype(F32))
        return g * jnp.dot(x_ref[...], wb_ref[k], preferred_element_type=F32)

    m = branch(oa_ref, ga_ref, 0) + branch(ob_ref, gb_ref, 1) + branch(yr_ref, gr_ref, 2)
    y = jnp.dot(m.astype(BF16), wo_ref[...], preferred_element_type=F32)

    @pl.when(n == 0)
    def _():
        o_ref[...] = y

    @pl.when(n > 0)
    def _():
        o_ref[...] += y

    @pl.when(n == nn - 1)
    def _():
        o_ref[...] = h_ref[...] + mod_ref[0, 5:6, :] * (_rms(o_ref[...]) * post_ref[...])


def _merge_call(h, mod_l, post_g, oa, ob, yr, proj, wb, wo, nrows, tm, n_lat_seg):
    d = h.shape[1]
    tn = 512
    nn = d // tn
    g0 = COL_GATES // tn
    seg = lambda i, n: (jnp.minimum((i * tm) // SEQ, n_lat_seg), 0, 0)
    row = lambda i, n: (i, 0)
    return pl.pallas_call(
        functools.partial(_merge_kernel, nn=nn),
        grid=(nrows // tm, nn),
        in_specs=[
            pl.BlockSpec((tm, d), row),
            pl.BlockSpec((1, N_MOD, d), seg),
            pl.BlockSpec((1, d), lambda i, n: (0, 0)),
            pl.BlockSpec((tm, BRANCH_W), row),
            pl.BlockSpec((tm, BRANCH_W), row),
            pl.BlockSpec((tm, BRANCH_W), row),
            pl.BlockSpec((tm, tn), lambda i, n: (i, g0 + n)),
            pl.BlockSpec((tm, tn), lambda i, n: (i, g0 + nn + n)),
            pl.BlockSpec((tm, tn), lambda i, n: (i, g0 + 2 * nn + n)),
            pl.BlockSpec((3, BRANCH_W, tn), lambda i, n: (0, 0, n)),
            pl.BlockSpec((tn, d), lambda i, n: (n, 0)),
        ],
        out_specs=pl.BlockSpec((tm, d), row),
        out_shape=jax.ShapeDtypeStruct((nrows, d), F32),
        compiler_params=_params(("parallel", "arbitrary")),
        name="mixer_merge",
    )(h, mod_l, post_g.reshape(1, d), oa, ob, yr, proj, proj, proj, wb, wo)


def kernel(x, c, ctx, c_ctx, w_mod, b_mod, pre_norm, post_norm, ffn_w_in, ffn_w_out, w_in, diff_lambda,
           diff_norm, na_rpb, ret_decay, ret_norm, w_branch, w_out):
    nb, seq, d = x.shape
    assert (seq, d) == (SEQ, D_MODEL) and ctx.shape == (nb, CTX_LEN, d)
    n_lat = nb * SEQ
    n_all = n_lat + nb * CTX_LEN
    tm = min(512, nb * CTX_LEN)

    h = jnp.concatenate([x.reshape(n_lat, d), ctx.reshape(nb * CTX_LEN, d)], axis=0)
    pad = (-(nb + 1)) % 8
    cc = jnp.concatenate([c, c_ctx[None, :], jnp.zeros((pad, d), c.dtype)], axis=0)
    mods = _mod_call(cc, w_mod, b_mod)
    tabs = _rope_tables()

    for l in range(DEPTH):
        with_ctx = l < DEPTH - 1
        lambda_init = 0.8 - 0.6 * math.exp(-0.3 * l)
        nrows_out = n_all if with_ctx else n_lat
        mod_l = mods[l, :nb + 1].reshape(nb + 1, N_MOD, d)
        ffn = functools.partial(_ffn_call, tm=tm, n_lat_seg=nb)

        h = ffn(h, mod_l, 0, pre_norm[l, 0], post_norm[l, 0], ffn_w_in[l, 0].astype(BF16),
                ffn_w_out[l, 0].astype(BF16), n_all)
        proj = _inproj_call(h, mod_l, pre_norm[l, 1], w_in[l].astype(BF16), tabs, tm, nb)
        oa = _diff_call(proj, diff_lambda[l], diff_norm[l], lambda_init, nb, with_ctx)
        ob = _nbr_call(proj, _nbr_bias_table(na_rpb[l]), nb, with_ctx)
        yr = _ret_call(proj, ret_decay[l], ret_norm[l], nb, with_ctx)
        h = _merge_call(h, mod_l, post_norm[l, 1], oa, ob, yr, proj, w_branch[l].astype(BF16),
                        w_out[l].astype(BF16), nrows_out, tm, nb)
        h = ffn(h, mod_l, 6, pre_norm[l, 2], post_norm[l, 2], ffn_w_in[l, 1].astype(BF16),
                ffn_w_out[l, 1].astype(BF16), nrows_out)
    return h.reshape(nb, SEQ, d)
```

```python
import functools
import math

import numpy as np
import jax
import jax.numpy as jnp
from jax import lax
from jax.experimental import pallas as pl
from jax.experimental.pallas import tpu as pltpu

D_MODEL = 2048
SEQ = 2048
DEPTH = 2
GRID_W = 64
CTX_LEN = 256
N_MOD = 9
FFN_DIM = 5632
A_HEADS = 8
A_HEAD_DIM = 64
B_HEADS = 8
B_HEAD_DIM = 128
WIN_R = 8
WIN_C = 16
C_HEADS = 8
C_KEY_DIM = 64
BRANCH_W = 1024
ROPE_BASE = 10000.0
EPS = 1e-6

COL_AQ, COL_AK, COL_AV = 0, 1024, 2048
COL_BQ, COL_BK, COL_BV = 3072, 4096, 5120
COL_CQ, COL_CK, COL_CV, COL_CG = 6144, 6656, 7168, 8192
COL_GATES = 9216
IN_COLS = 15360

LANES = 128
QBLK = 256
ROWS_PER_QBLK = QBLK // GRID_W
NBR_KROWS = ROWS_PER_QBLK + WIN_R - 1
NBR_KEYS = NBR_KROWS * GRID_W
VMEM_LIMIT = 56 * 1024 * 1024
NBR_MASKED = -1e30

BF16 = jnp.bfloat16
F32 = jnp.float32
NT_DIMS = (((1,), (1,)), ((), ()))
TN_DIMS = (((0,), (0,)), ((), ()))


def _params(sem):
    return pltpu.CompilerParams(dimension_semantics=sem, vmem_limit_bytes=VMEM_LIMIT)


def _rms(x):
    return x * lax.rsqrt(jnp.mean(x * x, axis=-1, keepdims=True) + EPS)


def _silu(x):
    return x * jax.nn.sigmoid(x)


def _mod_kernel(cc_ref, w_ref, b_ref, o_ref):
    s = _silu(cc_ref[...]).astype(BF16)
    o_ref[0] = jnp.dot(s, w_ref[0].astype(BF16), preferred_element_type=F32) + b_ref[0]


def _mod_call(cc, w_mod, b_mod):
    depth, d, n = w_mod.shape
    rows = cc.shape[0]
    tn = 1024
    return pl.pallas_call(
        _mod_kernel,
        grid=(depth, n // tn),
        in_specs=[
            pl.BlockSpec((rows, d), lambda l, j: (0, 0)),
            pl.BlockSpec((1, d, tn), lambda l, j: (l, 0, j)),
            pl.BlockSpec((1, 1, tn), lambda l, j: (l, 0, j)),
        ],
        out_specs=pl.BlockSpec((1, rows, tn), lambda l, j: (l, 0, j)),
        out_shape=jax.ShapeDtypeStruct((depth, rows, n), F32),
        compiler_params=_params(("parallel", "parallel")),
        name="mod_vectors",
    )(cc, w_mod, b_mod.reshape(depth, 1, n))


def _ffn_kernel(h_ref, mod_ref, pre_ref, post_ref, w1a_ref, w1b_ref, w2_ref, o_ref, u_ref, *, k0, nf):
    j = pl.program_id(1)

    @pl.when(j == 0)
    def _():
        shift = mod_ref[0, k0:k0 + 1, :]
        scale = mod_ref[0, k0 + 1:k0 + 2, :]
        u = _rms(h_ref[...]) * pre_ref[...] * (1.0 + scale) + shift
        u_ref[...] = u.astype(BF16)

    u = u_ref[...]
    a = jnp.dot(u, w1a_ref[...], preferred_element_type=F32)
    b = jnp.dot(u, w1b_ref[...], preferred_element_type=F32)
    y = jnp.dot((_silu(a) * b).astype(BF16), w2_ref[...], preferred_element_type=F32)

    @pl.when(j == 0)
    def _():
        o_ref[...] = y

    @pl.when(j > 0)
    def _():
        o_ref[...] += y

    @pl.when(j == nf - 1)
    def _():
        gate = mod_ref[0, k0 + 2:k0 + 3, :]
        o_ref[...] = h_ref[...] + 0.5 * gate * (_rms(o_ref[...]) * post_ref[...])


def _ffn_call(h, mod_l, k0, pre_g, post_g, w1, w2, nrows, tm, n_lat_seg):
    d = h.shape[1]
    f = w2.shape[0]
    tf = 512
    nf = f // tf
    seg = lambda i, j: (jnp.minimum((i * tm) // SEQ, n_lat_seg), 0, 0)
    return pl.pallas_call(
        functools.partial(_ffn_kernel, k0=k0, nf=nf),
        grid=(nrows // tm, nf),
        in_specs=[
            pl.BlockSpec((tm, d), lambda i, j: (i, 0)),
            pl.BlockSpec((1, N_MOD, d), seg),
            pl.BlockSpec((1, d), lambda i, j: (0, 0)),
            pl.BlockSpec((1, d), lambda i, j: (0, 0)),
            pl.BlockSpec((d, tf), lambda i, j: (0, j)),
            pl.BlockSpec((d, tf), lambda i, j: (0, j + nf)),
            pl.BlockSpec((tf, d), lambda i, j: (j, 0)),
        ],
        out_specs=pl.BlockSpec((tm, d), lambda i, j: (i, 0)),
        out_shape=jax.ShapeDtypeStruct((nrows, d), F32),
        scratch_shapes=[pltpu.VMEM((tm, d), BF16)],
        compiler_params=_params(("parallel", "arbitrary")),
        name="ffn_sublayer",
    )(h, mod_l, pre_g.reshape(1, d), post_g.reshape(1, d), w1, w1, w2)


def _inproj_kernel(h_ref, mod_ref, pre_ref, cos_ref, sa_ref, sb_ref, w_ref, o_ref, u_ref, *, tm, tn, n_lat_rows):
    i = pl.program_id(0)
    j = pl.program_id(1)

    @pl.when(j == 0)
    def _():
        shift = mod_ref[0, 3:4, :]
        scale = mod_ref[0, 4:5, :]
        u = _rms(h_ref[...]) * pre_ref[...] * (1.0 + scale) + shift
        u_ref[...] = u.astype(BF16)

    y = jnp.dot(u_ref[...], w_ref[...], preferred_element_type=F32)
    col = j * tn
    is_rot_col = (col < COL_AV) | ((col >= COL_CQ) & (col < COL_CV))
    rot = is_rot_col & (i * tm < n_lat_rows)

    @pl.when(rot)
    def _():
        cos, sa, sb = cos_ref[...], sa_ref[...], sb_ref[...]
        for c in range(tn // LANES):
            t = y[:, c * LANES:(c + 1) * LANES]
            r = t * cos + pltpu.roll(t, LANES - 16, 1) * sa + pltpu.roll(t, 16, 1) * sb
            o_ref[:, c * LANES:(c + 1) * LANES] = r.astype(BF16)

    @pl.when(jnp.logical_not(rot))
    def _():
        o_ref[...] = y.astype(BF16)


def _inproj_call(h, mod_l, pre_g, w, tabs, tm, n_lat_seg):
    nrows, d = h.shape
    n = w.shape[1]
    tn = 1024
    n_lat_rows = n_lat_seg * SEQ
    seg = lambda i, j: (jnp.minimum((i * tm) // SEQ, n_lat_seg), 0, 0)
    pos = lambda i, j: (i % (SEQ // tm), 0)
    return pl.pallas_call(
        functools.partial(_inproj_kernel, tm=tm, tn=tn, n_lat_rows=n_lat_rows),
        grid=(nrows // tm, n // tn),
        in_specs=[
            pl.BlockSpec((tm, d), lambda i, j: (i, 0)),
            pl.BlockSpec((1, N_MOD, d), seg),
            pl.BlockSpec((1, d), lambda i, j: (0, 0)),
            pl.BlockSpec((tm, LANES), pos),
            pl.BlockSpec((tm, LANES), pos),
            pl.BlockSpec((tm, LANES), pos),
            pl.BlockSpec((d, tn), lambda i, j: (0, j)),
        ],
        out_specs=pl.BlockSpec((tm, tn), lambda i, j: (i, j)),
        out_shape=jax.ShapeDtypeStruct((nrows, n), BF16),
        scratch_shapes=[pltpu.VMEM((tm, d), BF16)],
        compiler_params=_params(("parallel", "arbitrary")),
        name="mixer_in_proj",
    )(h, mod_l, pre_g.reshape(1, d), *tabs, w)


def _rope_tables():
    pos = np.arange(SEQ)
    prow = jnp.asarray(pos // GRID_W, F32)
    pcol = jnp.asarray(pos % GRID_W, F32)
    half = 16
    inv = ROPE_BASE ** (-jnp.arange(half, dtype=F32) / half)
    lane = np.arange(LANES)
    freq = jnp.asarray(lane % half)
    use_col = jnp.asarray((lane % 64) >= 32)
    second = jnp.asarray((lane % 32) >= half)
    p = jnp.where(use_col[None, :], pcol[:, None], prow[:, None])
    ang = p * inv[freq][None, :]
    cos, sin = jnp.cos(ang), jnp.sin(ang)
    sa = jnp.where(second[None, :], 0.0, -sin)
    sb = jnp.where(second[None, :], sin, 0.0)
    return cos, sa, sb


def _qblk_index(nb, col0):
    lat_blocks = SEQ // QBLK

    def index(b, h, j):
        row = jnp.where(j < lat_blocks, b * lat_blocks + j, nb * lat_blocks + b)
        return (row, col0 + h)
    return index


def _diff_kernel(q_ref, kl_ref, vl_ref, kc_ref, vc_ref, lam_ref, g_ref, o_ref, k_all, v_all, *, lambda_init):
    j = pl.program_id(2)

    @pl.when(j == 0)
    def _():
        k_all[0:CTX_LEN, :] = kc_ref[...]
        k_all[CTX_LEN:, :] = kl_ref[...]
        v_all[0:CTX_LEN, :] = vc_ref[...]
        v_all[CTX_LEN:, :] = vl_ref[...]

    lv = lam_ref[...]
    lam = (jnp.exp(jnp.sum(lv[0:1] * lv[1:2], axis=-1, keepdims=True))
           - jnp.exp(jnp.sum(lv[2:3] * lv[3:4], axis=-1, keepdims=True)) + lambda_init)
    lane = lax.broadcasted_iota(jnp.int32, (1, LANES), 1)
    q = q_ref[...] * (A_HEAD_DIM ** -0.5)
    zero = jnp.zeros_like(q)
    q1 = jnp.where(lane < A_HEAD_DIM, q, zero)
    q2 = jnp.where(lane >= A_HEAD_DIM, q, zero)

    def attend(nk):
        k = k_all[0:nk, :]

        def softmax(qm):
            s = lax.dot_general(qm, k, NT_DIMS, preferred_element_type=F32)
            e = jnp.exp(s - jnp.max(s, axis=-1, keepdims=True))
            return e / jnp.sum(e, axis=-1, keepdims=True)

        p = softmax(q1) - lam * softmax(q2)
        o = jnp.dot(p.astype(BF16), v_all[0:nk, :], preferred_element_type=F32)
        o_ref[...] = (_rms(o) * g_ref[...] * (1.0 - lambda_init)).astype(BF16)

    lat_blocks = SEQ // QBLK

    @pl.when(j < lat_blocks)
    def _():
        attend(CTX_LEN + SEQ)

    @pl.when(j >= lat_blocks)
    def _():
        attend(CTX_LEN)


def _diff_call(proj, diff_lambda, diff_norm, lambda_init, nb, with_ctx):
    lat_blocks = SEQ // QBLK
    nj = lat_blocks + (1 if with_ctx else 0)
    nrows_out = nb * SEQ + (nb * CTX_LEN if with_ctx else 0)
    cb = lambda col: col // LANES
    ctx_blk0 = nb * SEQ // CTX_LEN
    return pl.pallas_call(
        functools.partial(_diff_kernel, lambda_init=lambda_init),
        grid=(nb, A_HEADS, nj),
        in_specs=[
            pl.BlockSpec((QBLK, LANES), _qblk_index(nb, cb(COL_AQ))),
            pl.BlockSpec((SEQ, LANES), lambda b, h, j: (b, cb(COL_AK) + h)),
            pl.BlockSpec((SEQ, LANES), lambda b, h, j: (b, cb(COL_AV) + h)),
            pl.BlockSpec((CTX_LEN, LANES), lambda b, h, j: (ctx_blk0 + b, cb(COL_AK) + h)),
            pl.BlockSpec((CTX_LEN, LANES), lambda b, h, j: (ctx_blk0 + b, cb(COL_AV) + h)),
            pl.BlockSpec((4, A_HEAD_DIM), lambda b, h, j: (0, 0)),
            pl.BlockSpec((1, LANES), lambda b, h, j: (0, 0)),
        ],
        out_specs=pl.BlockSpec((QBLK, LANES), _qblk_index(nb, 0)),
        out_shape=jax.ShapeDtypeStruct((nrows_out, BRANCH_W), BF16),
        scratch_shapes=[pltpu.VMEM((CTX_LEN + SEQ, LANES), BF16), pltpu.VMEM((CTX_LEN + SEQ, LANES), BF16)],
        compiler_params=_params(("parallel", "parallel", "arbitrary")),
        name="diff_attention",
    )(proj, proj, proj, proj, proj, diff_lambda, diff_norm.reshape(1, LANES))


def _nbr_block_rows(cfg, qi, kj):
    n_rows = SEQ // GRID_W
    r0 = (0, ROWS_PER_QBLK, n_rows - ROWS_PER_QBLK)[cfg]
    ks = min(max(r0 - WIN_R // 2, 0), n_rows - NBR_KROWS)
    r = r0 + qi
    rs = min(max(r - WIN_R // 2, 0), n_rows - WIN_R)
    kr = ks + kj
    return rs <= kr < rs + WIN_R, kr - r + WIN_R - 1


def _nbr_bias_kernel(rpb_ref, o_ref):
    n_dr, n_dc = 2 * WIN_R - 1, 2 * WIN_C - 1
    base = pl.program_id(0) * (n_dr * n_dc)
    qc = lax.broadcasted_iota(jnp.int32, (GRID_W, LANES), 0)
    lane = lax.broadcasted_iota(jnp.int32, (GRID_W, LANES), 1)
    kc = lane % GRID_W
    dc = kc - qc + (WIN_C - 1)
    cs = jnp.clip(qc - WIN_C // 2, 0, GRID_W - WIN_C)
    col_ok = (kc >= cs) & (kc < cs + WIN_C)
    masked = jnp.full((GRID_W, LANES), NBR_MASKED, F32)

    def row_pattern(dr):
        acc = masked
        for x in range(n_dc):
            acc = jnp.where(dc == x, rpb_ref[base + dr * n_dc + x], acc)
        return jnp.where(col_ok, acc, masked)

    pats = [row_pattern(dr) for dr in range(n_dr)]

    def pattern(cfg, qi, kj):
        if kj >= NBR_KROWS:
            return masked
        ok, dr = _nbr_block_rows(cfg, qi, kj)
        return pats[dr] if ok else masked

    for cfg in range(3):
        for qi in range(ROWS_PER_QBLK):
            rows = slice(qi * GRID_W, (qi + 1) * GRID_W)
            for m in range(pl.cdiv(NBR_KEYS, LANES)):
                tile = jnp.where(lane < GRID_W, pattern(cfg, qi, 2 * m), pattern(cfg, qi, 2 * m + 1))
                width = min(LANES, NBR_KEYS - m * LANES)
                o_ref[0, cfg, rows, m * LANES:m * LANES + width] = tile[:, :width]


def _nbr_bias_call(rpb):
    heads = rpb.shape[0]
    return pl.pallas_call(
        _nbr_bias_kernel,
        grid=(heads,),
        in_specs=[pl.BlockSpec(memory_space=pltpu.SMEM)],
        out_specs=pl.BlockSpec((1, 3, QBLK, NBR_KEYS), lambda h: (h, 0, 0, 0)),
        out_shape=jax.ShapeDtypeStruct((heads, 3, QBLK, NBR_KEYS), F32),
        compiler_params=_params(("parallel",)),
        name="nbr_bias_table",
    )(rpb.astype(F32).reshape(-1))


def _nbr_kernel(q_ref, kl_ref, vl_ref, kc_ref, vc_ref, bias_ref, o_ref):
    j = pl.program_id(2)
    lat_blocks = SEQ // QBLK
    scale = B_HEAD_DIM ** -0.5
    q = q_ref[...]
    s_ctx = lax.dot_general(q, kc_ref[...], NT_DIMS, preferred_element_type=F32) * scale

    @pl.when(j < lat_blocks)
    def _():
        cfg = jnp.where(j == 0, 0, jnp.where(j == lat_blocks - 1, 2, 1))
        start = jnp.clip((j * ROWS_PER_QBLK - WIN_R // 2) * GRID_W, 0, SEQ - NBR_KEYS)
        rows = pl.ds(pl.multiple_of(start, GRID_W), NBR_KEYS)
        s_lat = (lax.dot_general(q, kl_ref[rows, :], NT_DIMS, preferred_element_type=F32) * scale
                 + bias_ref[0, cfg])
        m = jnp.maximum(jnp.max(s_lat, axis=-1, keepdims=True), jnp.max(s_ctx, axis=-1, keepdims=True))
        e_lat = jnp.exp(s_lat - m)
        e_ctx = jnp.exp(s_ctx - m)
        l = jnp.sum(e_lat, axis=-1, keepdims=True) + jnp.sum(e_ctx, axis=-1, keepdims=True)
        o = (jnp.dot(e_lat.astype(BF16), vl_ref[rows, :], preferred_element_type=F32)
             + jnp.dot(e_ctx.astype(BF16), vc_ref[...], preferred_element_type=F32))
        o_ref[...] = (o / l).astype(BF16)

    @pl.when(j >= lat_blocks)
    def _():
        e = jnp.exp(s_ctx - jnp.max(s_ctx, axis=-1, keepdims=True))
        o = jnp.dot(e.astype(BF16), vc_ref[...], preferred_element_type=F32)
        o_ref[...] = (o / jnp.sum(e, axis=-1, keepdims=True)).astype(BF16)


def _nbr_call(proj, bias_tab, nb, with_ctx):
    lat_blocks = SEQ // QBLK
    nj = lat_blocks + (1 if with_ctx else 0)
    nrows_out = nb * SEQ + (nb * CTX_LEN if with_ctx else 0)
    cb = lambda col: col // LANES
    ctx_blk0 = nb * SEQ // CTX_LEN
    swap = lambda f: (lambda h, b, j: f(b, h, j))
    return pl.pallas_call(
        _nbr_kernel,
        grid=(B_HEADS, nb, nj),
        in_specs=[
            pl.BlockSpec((QBLK, LANES), swap(_qblk_index(nb, cb(COL_BQ)))),
            pl.BlockSpec((SEQ, LANES), lambda h, b, j: (b, cb(COL_BK) + h)),
            pl.BlockSpec((SEQ, LANES), lambda h, b, j: (b, cb(COL_BV) + h)),
            pl.BlockSpec((CTX_LEN, LANES), lambda h, b, j: (ctx_blk0 + b, cb(COL_BK) + h)),
            pl.BlockSpec((CTX_LEN, LANES), lambda h, b, j: (ctx_blk0 + b, cb(COL_BV) + h)),
            pl.BlockSpec((1, 3, QBLK, NBR_KEYS), lambda h, b, j: (h, 0, 0, 0)),
        ],
        out_specs=pl.BlockSpec((QBLK, LANES), swap(_qblk_index(nb, 0))),
        out_shape=jax.ShapeDtypeStruct((nrows_out, BRANCH_W), BF16),
        compiler_params=_params(("parallel", "parallel", "arbitrary")),
        name="nbr_attention",
    )(proj, proj, proj, proj, proj, bias_tab)


def _ret_kernel(ql_ref, kl_ref, vl_ref, gl_ref, qc_ref, kc_ref, vc_ref, gc_ref, dec_ref, rn_ref, o_ref,
                buf_ref, of_ref, st_ref, *, with_ctx):
    j = pl.program_id(2)
    ch = QBLK
    n_chunks = SEQ // ch
    kscale = C_KEY_DIM ** -0.5

    @pl.when(j == 0)
    def _():
        ii = lax.broadcasted_iota(jnp.int32, (ch, ch), 0)
        jj = lax.broadcasted_iota(jnp.int32, (ch, ch), 1)
        dist = (ii - jj).astype(F32)
        pos = lax.broadcasted_iota(jnp.int32, (ch, 1), 0).astype(F32)
        lane = lax.broadcasted_iota(jnp.int32, (1, LANES), 1)
        rn = rn_ref[...]

        def log_sigmoid(x):
            return -(jnp.log1p(jnp.exp(-jnp.abs(x))) + jnp.maximum(-x, 0.0))

        heads = []
        for e in range(2):
            lgf = log_sigmoid(dec_ref[0, 0, e:e + 1, 0:1])
            lgb = log_sigmoid(dec_ref[0, 1, e:e + 1, 0:1])
            heads.append(dict(
                mask=(lane >= e * C_KEY_DIM) & (lane < (e + 1) * C_KEY_DIM),
                intra_f=jnp.where(dist >= 0, jnp.exp(lgf * jnp.maximum(dist, 0.0)), 0.0),
                intra_b=jnp.where(dist <= 0, jnp.exp(lgb * jnp.maximum(-dist, 0.0)), 0.0),
                qdec_f=jnp.exp(lgf * (pos + 1.0)), kdec_f=jnp.exp(lgf * (ch - 1.0 - pos)),
                cdec_f=jnp.exp(lgf * ch),
                qdec_b=jnp.exp(lgb * (ch - pos)), kdec_b=jnp.exp(lgb * pos),
                cdec_b=jnp.exp(lgb * ch),
                vs=slice(e * LANES, (e + 1) * LANES),
            ))

        def finish(o, g):
            return (_silu(g.astype(F32)) * (_rms(o) * rn)).astype(BF16)

        qc = qc_ref[...].astype(F32)
        kc = kc_ref[...].astype(F32) * kscale
        for e, hd in enumerate(heads):
            v = vc_ref[:, hd["vs"]]
            st_ref[e] = lax.dot_general((kc * hd["kdec_f"]).astype(BF16), v, TN_DIMS, preferred_element_type=F32)
            st_ref[2 + e] = lax.dot_general((kc * hd["kdec_b"]).astype(BF16), v, TN_DIMS,
                                            preferred_element_type=F32)
            if with_ctx:
                qe = jnp.where(hd["mask"], qc, 0.0).astype(BF16)
                att = lax.dot_general(qe, kc.astype(BF16), NT_DIMS, preferred_element_type=F32)
                att = att * (hd["intra_f"] + hd["intra_b"])
                o = jnp.dot(att.astype(BF16), v, preferred_element_type=F32)
                buf_ref[n_chunks * ch:(n_chunks + 1) * ch, hd["vs"]] = finish(o, gc_ref[:, hd["vs"]])

        def sweep(c, fwd):
            rows = pl.ds(pl.multiple_of(c * ch, ch), ch)
            q = ql_ref[rows, :].astype(F32)
            k = kl_ref[rows, :].astype(F32) * kscale
            kb = k.astype(BF16)
            for e, hd in enumerate(heads):
                d = "f" if fwd else "b"
                si = e if fwd else 2 + e
                v = vl_ref[rows, hd["vs"]]
                qe = jnp.where(hd["mask"], q, 0.0)
                att = lax.dot_general(qe.astype(BF16), kb, NT_DIMS, preferred_element_type=F32) * hd["intra_" + d]
                st = st_ref[si]
                o = (jnp.dot(att.astype(BF16), v, preferred_element_type=F32)
                     + jnp.dot((qe * hd["qdec_" + d]).astype(BF16), st.astype(BF16), preferred_element_type=F32))
                st_ref[si] = st * hd["cdec_" + d] + lax.dot_general(
                    (k * hd["kdec_" + d]).astype(BF16), v, TN_DIMS, preferred_element_type=F32)
                if fwd:
                    of_ref[rows, hd["vs"]] = o
                else:
                    buf_ref[rows, hd["vs"]] = finish(of_ref[rows, hd["vs"]] + o, gl_ref[rows, hd["vs"]])

        def fwd_body(c, carry):
            sweep(c, True)
            return carry

        def bwd_body(t, carry):
            sweep(n_chunks - 1 - t, False)
            return carry

        lax.fori_loop(0, n_chunks, fwd_body, 0)
        lax.fori_loop(0, n_chunks, bwd_body, 0)

    o_ref[...] = buf_ref[pl.ds(pl.multiple_of(j * ch, ch), ch), :]


def _ret_call(proj, ret_decay, ret_norm, nb, with_ctx):
    lat_blocks = SEQ // QBLK
    nj = lat_blocks + (1 if with_ctx else 0)
    nrows_out = nb * SEQ + (nb * CTX_LEN if with_ctx else 0)
    pairs = C_HEADS // 2
    ctx_blk0 = nb * SEQ // CTX_LEN
    w2 = 2 * LANES
    dec = jnp.broadcast_to(ret_decay.astype(F32).reshape(2, pairs, 2).transpose(1, 0, 2)[..., None],
                           (pairs, 2, 2, LANES))
    return pl.pallas_call(
        functools.partial(_ret_kernel, with_ctx=with_ctx),
        grid=(nb, pairs, nj),
        in_specs=[
            pl.BlockSpec((SEQ, LANES), lambda b, p, j: (b, COL_CQ // LANES + p)),
            pl.BlockSpec((SEQ, LANES), lambda b, p, j: (b, COL_CK // LANES + p)),
            pl.BlockSpec((SEQ, w2), lambda b, p, j: (b, COL_CV // w2 + p)),
            pl.BlockSpec((SEQ, w2), lambda b, p, j: (b, COL_CG // w2 + p)),
            pl.BlockSpec((CTX_LEN, LANES), lambda b, p, j: (ctx_blk0 + b, COL_CQ // LANES + p)),
            pl.BlockSpec((CTX_LEN, LANES), lambda b, p, j: (ctx_blk0 + b, COL_CK // LANES + p)),
            pl.BlockSpec((CTX_LEN, w2), lambda b, p, j: (ctx_blk0 + b, COL_CV // w2 + p)),
            pl.BlockSpec((CTX_LEN, w2), lambda b, p, j: (ctx_blk0 + b, COL_CG // w2 + p)),
            pl.BlockSpec((1, 2, 2, LANES), lambda b, p, j: (p, 0, 0, 0)),
            pl.BlockSpec((1, LANES), lambda b, p, j: (0, 0)),
        ],
        out_specs=pl.BlockSpec((QBLK, w2), _qblk_index(nb, 0)),
        out_shape=jax.ShapeDtypeStruct((nrows_out, BRANCH_W), BF16),
        scratch_shapes=[
            pltpu.VMEM((SEQ + CTX_LEN, w2), BF16),
            pltpu.VMEM((SEQ, w2), F32),
            pltpu.VMEM((4, LANES, LANES), F32),
        ],
        compiler_params=_params(("parallel", "parallel", "arbitrary")),
        name="retention",
    )(proj, proj, proj, proj, proj, proj, proj, proj, dec, ret_norm.reshape(1, LANES))


def _merge_kernel(h_ref, mod_ref, post_ref, oa_ref, ob_ref, yr_ref, ga_ref, gb_ref, gr_ref, wb_ref, wo_ref,
                  o_ref, *, nn):
    n = pl.program_id(1)

    def branch(x_ref, g_ref, k):
        g = jax.nn.sigmoid(g_ref[...].astype(F32))
        return g * jnp.dot(x_ref[...], wb_ref[k], preferred_element_type=F32)

    m = branch(oa_ref, ga_ref, 0) + branch(ob_ref, gb_ref, 1) + branch(yr_ref, gr_ref, 2)
    y = jnp.dot(m.astype(BF16), wo_ref[...], preferred_element_type=F32)

    @pl.when(n == 0)
    def _():
        o_ref[...] = y

    @pl.when(n > 0)
    def _():
        o_ref[...] += y

    @pl.when(n == nn - 1)
    def _():
        o_ref[...] = h_ref[...] + mod_ref[0, 5:6, :] * (_rms(o_ref[...]) * post_ref[...])


def _merge_call(h, mod_l, post_g, oa, ob, yr, proj, wb, wo, nrows, tm, n_lat_seg):
    d = h.shape[1]
    tn = 512
    nn = d // tn
    g0 = COL_GATES // tn
    seg = lambda i, n: (jnp.minimum((i * tm) // SEQ, n_lat_seg), 0, 0)
    row = lambda i, n: (i, 0)
    return pl.pallas_call(
        functools.partial(_merge_kernel, nn=nn),
        grid=(nrows // tm, nn),
        in_specs=[
            pl.BlockSpec((tm, d), row),
            pl.BlockSpec((1, N_MOD, d), seg),
            pl.BlockSpec((1, d), lambda i, n: (0, 0)),
            pl.BlockSpec((tm, BRANCH_W), row),
            pl.BlockSpec((tm, BRANCH_W), row),
            pl.BlockSpec((tm, BRANCH_W), row),
            pl.BlockSpec((tm, tn), lambda i, n: (i, g0 + n)),
            pl.BlockSpec((tm, tn), lambda i, n: (i, g0 + nn + n)),
            pl.BlockSpec((tm, tn), lambda i, n: (i, g0 + 2 * nn + n)),
            pl.BlockSpec((3, BRANCH_W, tn), lambda i, n: (0, 0, n)),
            pl.BlockSpec((tn, d), lambda i, n: (n, 0)),
        ],
        out_specs=pl.BlockSpec((tm, d), row),
        out_shape=jax.ShapeDtypeStruct((nrows, d), F32),
        compiler_params=_params(("parallel", "arbitrary")),
        name="mixer_merge",
    )(h, mod_l, post_g.reshape(1, d), oa, ob, yr, proj, proj, proj, wb, wo)


def kernel(x, c, ctx, c_ctx, w_mod, b_mod, pre_norm, post_norm, ffn_w_in, ffn_w_out, w_in, diff_lambda,
           diff_norm, na_rpb, ret_decay, ret_norm, w_branch, w_out):
    nb, seq, d = x.shape
    assert (seq, d) == (SEQ, D_MODEL) and ctx.shape == (nb, CTX_LEN, d)
    n_lat = nb * SEQ
    n_all = n_lat + nb * CTX_LEN
    tm = min(512, nb * CTX_LEN)

    h = jnp.concatenate([x.reshape(n_lat, d), ctx.reshape(nb * CTX_LEN, d)], axis=0)
    pad = (-(nb + 1)) % 8
    cc = jnp.concatenate([c, c_ctx[None, :], jnp.zeros((pad, d), c.dtype)], axis=0)
    mods = _mod_call(cc, w_mod, b_mod)
    tabs = _rope_tables()

    for l in range(DEPTH):
        with_ctx = l < DEPTH - 1
        lambda_init = 0.8 - 0.6 * math.exp(-0.3 * l)
        nrows_out = n_all if with_ctx else n_lat
        mod_l = mods[l, :nb + 1].reshape(nb + 1, N_MOD, d)
        ffn = functools.partial(_ffn_call, tm=tm, n_lat_seg=nb)

        h = ffn(h, mod_l, 0, pre_norm[l, 0], post_norm[l, 0], ffn_w_in[l, 0].astype(BF16),
                ffn_w_out[l, 0].astype(BF16), n_all)
        proj = _inproj_call(h, mod_l, pre_norm[l, 1], w_in[l].astype(BF16), tabs, min(1024, nb * CTX_LEN), nb)
        oa = _diff_call(proj, diff_lambda[l], diff_norm[l], lambda_init, nb, with_ctx)
        ob = _nbr_call(proj, _nbr_bias_call(na_rpb[l]), nb, with_ctx)
        yr = _ret_call(proj, ret_decay[l], ret_norm[l], nb, with_ctx)
        h = _merge_call(h, mod_l, post_norm[l, 1], oa, ob, yr, proj, w_branch[l].astype(BF16),
                        w_out[l].astype(BF16), nrows_out, tm, nb)
        h = ffn(h, mod_l, 6, pre_norm[l, 2], post_norm[l, 2], ffn_w_in[l, 1].astype(BF16),
                ffn_w_out[l, 1].astype(BF16), nrows_out)
    return h.reshape(nb, SEQ, d)
```

```python
import functools
import math

import numpy as np
import jax
import jax.numpy as jnp
from jax import lax
from jax.experimental import pallas as pl
from jax.experimental.pallas import tpu as pltpu

D_MODEL = 2048
SEQ = 2048
DEPTH = 2
GRID_W = 64
CTX_LEN = 256
N_MOD = 9
FFN_DIM = 5632
A_HEADS = 8
A_HEAD_DIM = 64
B_HEADS = 8
B_HEAD_DIM = 128
WIN_R = 8
WIN_C = 16
C_HEADS = 8
C_KEY_DIM = 64
BRANCH_W = 1024
ROPE_BASE = 10000.0
EPS = 1e-6

COL_AQ, COL_AK, COL_AV = 0, 1024, 2048
COL_BQ, COL_BK, COL_BV = 3072, 4096, 5120
COL_CQ, COL_CK, COL_CV, COL_CG = 6144, 6656, 7168, 8192
COL_GATES = 9216
IN_COLS = 15360

LANES = 128
QBLK = 256
ROWS_PER_QBLK = QBLK // GRID_W
NBR_KROWS = ROWS_PER_QBLK + WIN_R - 1
NBR_KEYS = NBR_KROWS * GRID_W
VMEM_LIMIT = 56 * 1024 * 1024
DIFF_ROW_SPLIT = 2
NBR_MASKED = -1e30

BF16 = jnp.bfloat16
F32 = jnp.float32
NT_DIMS = (((1,), (1,)), ((), ()))
TN_DIMS = (((0,), (0,)), ((), ()))


def _params(sem):
    return pltpu.CompilerParams(dimension_semantics=sem, vmem_limit_bytes=VMEM_LIMIT)


def _rms(x):
    return x * lax.rsqrt(jnp.mean(x * x, axis=-1, keepdims=True) + EPS)


def _silu(x):
    return x * jax.nn.sigmoid(x)


def _mod_kernel(cc_ref, w_ref, b_ref, o_ref):
    s = _silu(cc_ref[...]).astype(BF16)
    o_ref[0] = jnp.dot(s, w_ref[0].astype(BF16), preferred_element_type=F32) + b_ref[0]


def _mod_call(cc, w_mod, b_mod):
    depth, d, n = w_mod.shape
    rows = cc.shape[0]
    tn = 1024
    return pl.pallas_call(
        _mod_kernel,
        grid=(depth, n // tn),
        in_specs=[
            pl.BlockSpec((rows, d), lambda l, j: (0, 0)),
            pl.BlockSpec((1, d, tn), lambda l, j: (l, 0, j)),
            pl.BlockSpec((1, 1, tn), lambda l, j: (l, 0, j)),
        ],
        out_specs=pl.BlockSpec((1, rows, tn), lambda l, j: (l, 0, j)),
        out_shape=jax.ShapeDtypeStruct((depth, rows, n), F32),
        compiler_params=_params(("parallel", "parallel")),
        name="mod_vectors",
    )(cc, w_mod, b_mod.reshape(depth, 1, n))


def _ffn_kernel(h_ref, mod_ref, pre_ref, post_ref, w1a_ref, w1b_ref, w2_ref, o_ref, u_ref, hm_ref, *,
                k0, nf, tf, nn, tn):
    j = pl.program_id(1)

    @pl.when(j == 0)
    def _():
        shift = mod_ref[0, k0:k0 + 1, :]
        scale = mod_ref[0, k0 + 1:k0 + 2, :]
        u = _rms(h_ref[...]) * pre_ref[...] * (1.0 + scale) + shift
        u_ref[...] = u.astype(BF16)

    @pl.when(j < nf)
    def _():
        u = u_ref[...]
        a = jnp.dot(u, w1a_ref[...], preferred_element_type=F32)
        b = jnp.dot(u, w1b_ref[...], preferred_element_type=F32)
        hm_ref[j] = (_silu(a) * b).astype(BF16)

    for n in range(nn):
        @pl.when(j == nf + n)
        def _():
            y = jnp.dot(hm_ref[0], w2_ref[0:tf, :], preferred_element_type=F32)
            for k in range(1, nf):
                y += jnp.dot(hm_ref[k], w2_ref[k * tf:(k + 1) * tf, :], preferred_element_type=F32)
            o_ref[:, n * tn:(n + 1) * tn] = y

    @pl.when(j == nf + nn - 1)
    def _():
        gate = mod_ref[0, k0 + 2:k0 + 3, :]
        o_ref[...] = h_ref[...] + 0.5 * gate * (_rms(o_ref[...]) * post_ref[...])


def _ffn_call(h, mod_l, k0, pre_g, post_g, w1, w2, nrows, tm, n_lat_seg):
    d = h.shape[1]
    f = w2.shape[0]
    tf, tn = 512, 512
    nf, nn = f // tf, d // tn
    seg = lambda i, j: (jnp.minimum((i * tm) // SEQ, n_lat_seg), 0, 0)
    return pl.pallas_call(
        functools.partial(_ffn_kernel, k0=k0, nf=nf, tf=tf, nn=nn, tn=tn),
        grid=(nrows // tm, nf + nn),
        in_specs=[
            pl.BlockSpec((tm, d), lambda i, j: (i, 0)),
            pl.BlockSpec((1, N_MOD, d), seg),
            pl.BlockSpec((1, d), lambda i, j: (0, 0)),
            pl.BlockSpec((1, d), lambda i, j: (0, 0)),
            pl.BlockSpec((d, tf), lambda i, j: (0, jnp.minimum(j, nf - 1))),
            pl.BlockSpec((d, tf), lambda i, j: (0, jnp.minimum(j, nf - 1) + nf)),
            pl.BlockSpec((f, tn), lambda i, j: (0, jnp.maximum(j - nf, 0))),
        ],
        out_specs=pl.BlockSpec((tm, d), lambda i, j: (i, 0)),
        out_shape=jax.ShapeDtypeStruct((nrows, d), F32),
        scratch_shapes=[pltpu.VMEM((tm, d), BF16), pltpu.VMEM((nf, tm, tf), BF16)],
        compiler_params=_params(("parallel", "arbitrary")),
        name="ffn_sublayer",
    )(h, mod_l, pre_g.reshape(1, d), post_g.reshape(1, d), w1, w1, w2)


def _inproj_kernel(h_ref, mod_ref, pre_ref, w_ref, o_ref, u_ref):
    @pl.when(pl.program_id(1) == 0)
    def _():
        shift = mod_ref[0, 3:4, :]
        scale = mod_ref[0, 4:5, :]
        u = _rms(h_ref[...]) * pre_ref[...] * (1.0 + scale) + shift
        u_ref[...] = u.astype(BF16)

    o_ref[...] = jnp.dot(u_ref[...], w_ref[...], preferred_element_type=F32).astype(BF16)


def _inproj_call(h, mod_l, pre_g, w, tm, n_lat_seg):
    nrows, d = h.shape
    n = w.shape[1]
    tn = 1024
    seg = lambda i, j: (jnp.minimum((i * tm) // SEQ, n_lat_seg), 0, 0)
    return pl.pallas_call(
        _inproj_kernel,
        grid=(nrows // tm, n // tn),
        in_specs=[
            pl.BlockSpec((tm, d), lambda i, j: (i, 0)),
            pl.BlockSpec((1, N_MOD, d), seg),
            pl.BlockSpec((1, d), lambda i, j: (0, 0)),
            pl.BlockSpec((d, tn), lambda i, j: (0, j)),
        ],
        out_specs=pl.BlockSpec((tm, tn), lambda i, j: (i, j)),
        out_shape=jax.ShapeDtypeStruct((nrows, n), BF16),
        scratch_shapes=[pltpu.VMEM((tm, d), BF16)],
        compiler_params=_params(("parallel", "arbitrary")),
        name="mixer_in_proj",
    )(h, mod_l, pre_g.reshape(1, d), w)


def _rope(t, cos, sa, sb):
    return t * cos + pltpu.roll(t, LANES - 16, 1) * sa + pltpu.roll(t, 16, 1) * sb


def _rope_tables():
    pos = np.arange(SEQ)
    prow = jnp.asarray(pos // GRID_W, F32)
    pcol = jnp.asarray(pos % GRID_W, F32)
    half = 16
    inv = ROPE_BASE ** (-jnp.arange(half, dtype=F32) / half)
    lane = np.arange(LANES)
    freq = jnp.asarray(lane % half)
    use_col = jnp.asarray((lane % 64) >= 32)
    second = jnp.asarray((lane % 32) >= half)
    p = jnp.where(use_col[None, :], pcol[:, None], prow[:, None])
    ang = p * inv[freq][None, :]
    cos, sin = jnp.cos(ang), jnp.sin(ang)
    sa = jnp.where(second[None, :], 0.0, -sin)
    sb = jnp.where(second[None, :], sin, 0.0)
    return cos, sa, sb


def _qblk_index(nb, col0):
    lat_blocks = SEQ // QBLK

    def index(b, h, j):
        row = jnp.where(j < lat_blocks, b * lat_blocks + j, nb * lat_blocks + b)
        return (row, col0 + h)
    return index


def _diff_kernel(q_ref, kl_ref, vl_ref, kc_ref, vc_ref, cos_ref, sa_ref, sb_ref, lam_ref, g_ref, o_ref,
                 k_all, v_ones, *, lambda_init):
    j = pl.program_id(2)
    lat_blocks = SEQ // QBLK

    @pl.when(j == 0)
    def _():
        k_all[0:CTX_LEN, :] = kc_ref[...]
        k_all[CTX_LEN:, :] = _rope(kl_ref[...].astype(F32), cos_ref[...], sa_ref[...], sb_ref[...]).astype(BF16)
        v_ones[0:CTX_LEN, 0:LANES] = vc_ref[...]
        v_ones[CTX_LEN:, 0:LANES] = vl_ref[...]
        v_ones[:, LANES:] = jnp.ones((CTX_LEN + SEQ, LANES), BF16)

    lv = lam_ref[...]
    lam = (jnp.exp(jnp.sum(lv[0:1] * lv[1:2], axis=-1, keepdims=True))
           - jnp.exp(jnp.sum(lv[2:3] * lv[3:4], axis=-1, keepdims=True)) + lambda_init)
    lane = lax.broadcasted_iota(jnp.int32, (1, LANES), 1)

    def attend(q, nk):
        q = q * (A_HEAD_DIM ** -0.5)
        k = k_all[0:nk, :]
        v1 = v_ones[0:nk, :]

        rows = QBLK // DIFF_ROW_SPLIT
        qms = [jnp.where(mask, q[r * rows:(r + 1) * rows], 0.0).astype(BF16)
               for r in range(DIFF_ROW_SPLIT) for mask in (lane < A_HEAD_DIM, lane >= A_HEAD_DIM)]
        ss = [lax.dot_general(qm, k, NT_DIMS, preferred_element_type=F32) for qm in qms]
        es = [jnp.exp(s - jnp.max(s, axis=-1, keepdims=True)).astype(BF16) for s in ss]
        ols = [jnp.dot(e, v1, preferred_element_type=F32) for e in es]
        os = [ol[:, :LANES] / ol[:, LANES:] for ol in ols]
        o = jnp.concatenate([os[2 * r] - lam * os[2 * r + 1] for r in range(DIFF_ROW_SPLIT)], axis=0)
        o_ref[...] = (_rms(o) * g_ref[...] * (1.0 - lambda_init)).astype(BF16)

    @pl.when(j < lat_blocks)
    def _():
        rows = pl.ds(pl.multiple_of(j * QBLK, QBLK), QBLK)
        attend(_rope(q_ref[...].astype(F32), cos_ref[rows, :], sa_ref[rows, :], sb_ref[rows, :]), CTX_LEN + SEQ)

    @pl.when(j >= lat_blocks)
    def _():
        attend(q_ref[...].astype(F32), CTX_LEN)


def _diff_call(proj, tabs, diff_lambda, diff_norm, lambda_init, nb, with_ctx):
    lat_blocks = SEQ // QBLK
    nj = lat_blocks + (1 if with_ctx else 0)
    nrows_out = nb * SEQ + (nb * CTX_LEN if with_ctx else 0)
    cb = lambda col: col // LANES
    ctx_blk0 = nb * SEQ // CTX_LEN
    whole = lambda b, h, j: (0, 0)
    return pl.pallas_call(
        functools.partial(_diff_kernel, lambda_init=lambda_init),
        grid=(nb, A_HEADS, nj),
        in_specs=[
            pl.BlockSpec((QBLK, LANES), _qblk_index(nb, cb(COL_AQ))),
            pl.BlockSpec((SEQ, LANES), lambda b, h, j: (b, cb(COL_AK) + h)),
            pl.BlockSpec((SEQ, LANES), lambda b, h, j: (b, cb(COL_AV) + h)),
            pl.BlockSpec((CTX_LEN, LANES), lambda b, h, j: (ctx_blk0 + b, cb(COL_AK) + h)),
            pl.BlockSpec((CTX_LEN, LANES), lambda b, h, j: (ctx_blk0 + b, cb(COL_AV) + h)),
            pl.BlockSpec((SEQ, LANES), whole),
            pl.BlockSpec((SEQ, LANES), whole),
            pl.BlockSpec((SEQ, LANES), whole),
            pl.BlockSpec((4, A_HEAD_DIM), whole),
            pl.BlockSpec((1, LANES), whole),
        ],
        out_specs=pl.BlockSpec((QBLK, LANES), _qblk_index(nb, 0)),
        out_shape=jax.ShapeDtypeStruct((nrows_out, BRANCH_W), BF16),
        scratch_shapes=[pltpu.VMEM((CTX_LEN + SEQ, LANES), BF16), pltpu.VMEM((CTX_LEN + SEQ, 2 * LANES), BF16)],
        compiler_params=_params(("parallel", "parallel", "arbitrary")),
        name="diff_attention",
    )(proj, proj, proj, proj, proj, *tabs, diff_lambda, diff_norm.reshape(1, LANES))


def _nbr_block_rows(cfg, qi, kj):
    n_rows = SEQ // GRID_W
    r0 = (0, ROWS_PER_QBLK, n_rows - ROWS_PER_QBLK)[cfg]
    ks = min(max(r0 - WIN_R // 2, 0), n_rows - NBR_KROWS)
    r = r0 + qi
    rs = min(max(r - WIN_R // 2, 0), n_rows - WIN_R)
    kr = ks + kj
    return rs <= kr < rs + WIN_R, kr - r + WIN_R - 1


def _nbr_bias_kernel(rpb_ref, o_ref):
    n_dr, n_dc = 2 * WIN_R - 1, 2 * WIN_C - 1
    base = pl.program_id(0) * (n_dr * n_dc)
    qc = lax.broadcasted_iota(jnp.int32, (GRID_W, LANES), 0)
    lane = lax.broadcasted_iota(jnp.int32, (GRID_W, LANES), 1)
    kc = lane % GRID_W
    dc = kc - qc + (WIN_C - 1)
    cs = jnp.clip(qc - WIN_C // 2, 0, GRID_W - WIN_C)
    col_ok = (kc >= cs) & (kc < cs + WIN_C)
    masked = jnp.full((GRID_W, LANES), NBR_MASKED, F32)

    def row_pattern(dr):
        acc = masked
        for x in range(n_dc):
            acc = jnp.where(dc == x, rpb_ref[base + dr * n_dc + x], acc)
        return jnp.where(col_ok, acc, masked)

    pats = [row_pattern(dr) for dr in range(n_dr)]

    def pattern(cfg, qi, kj):
        if kj >= NBR_KROWS:
            return masked
        ok, dr = _nbr_block_rows(cfg, qi, kj)
        return pats[dr] if ok else masked

    for cfg in range(3):
        for qi in range(ROWS_PER_QBLK):
            rows = slice(qi * GRID_W, (qi + 1) * GRID_W)
            for m in range(pl.cdiv(NBR_KEYS, LANES)):
                tile = jnp.where(lane < GRID_W, pattern(cfg, qi, 2 * m), pattern(cfg, qi, 2 * m + 1))
                width = min(LANES, NBR_KEYS - m * LANES)
                o_ref[0, cfg, rows, m * LANES:m * LANES + width] = tile[:, :width]


def _nbr_bias_call(rpb):
    heads = rpb.shape[0]
    return pl.pallas_call(
        _nbr_bias_kernel,
        grid=(heads,),
        in_specs=[pl.BlockSpec(memory_space=pltpu.SMEM)],
        out_specs=pl.BlockSpec((1, 3, QBLK, NBR_KEYS), lambda h: (h, 0, 0, 0)),
        out_shape=jax.ShapeDtypeStruct((heads, 3, QBLK, NBR_KEYS), F32),
        compiler_params=_params(("parallel",)),
        name="nbr_bias_table",
    )(rpb.astype(F32).reshape(-1))


def _nbr_kernel(q_ref, kl_ref, vl_ref, kc_ref, vc_ref, bias_ref, o_ref):
    j = pl.program_id(2)
    lat_blocks = SEQ // QBLK
    scale = B_HEAD_DIM ** -0.5
    q = q_ref[...]
    s_ctx = lax.dot_general(q, kc_ref[...], NT_DIMS, preferred_element_type=F32) * scale

    @pl.when(j < lat_blocks)
    def _():
        cfg = jnp.where(j == 0, 0, jnp.where(j == lat_blocks - 1, 2, 1))
        start = jnp.clip((j * ROWS_PER_QBLK - WIN_R // 2) * GRID_W, 0, SEQ - NBR_KEYS)
        rows = pl.ds(pl.multiple_of(start, GRID_W), NBR_KEYS)
        s_lat = (lax.dot_general(q, kl_ref[rows, :], NT_DIMS, preferred_element_type=F32) * scale
                 + bias_ref[0, cfg])
        m = jnp.maximum(jnp.max(s_lat, axis=-1, keepdims=True), jnp.max(s_ctx, axis=-1, keepdims=True))
        e_lat = jnp.exp(s_lat - m)
        e_ctx = jnp.exp(s_ctx - m)
        l = jnp.sum(e_lat, axis=-1, keepdims=True) + jnp.sum(e_ctx, axis=-1, keepdims=True)
        o = (jnp.dot(e_lat.astype(BF16), vl_ref[rows, :], preferred_element_type=F32)
             + jnp.dot(e_ctx.astype(BF16), vc_ref[...], preferred_element_type=F32))
        o_ref[...] = (o / l).astype(BF16)

    @pl.when(j >= lat_blocks)
    def _():
        e = jnp.exp(s_ctx - jnp.max(s_ctx, axis=-1, keepdims=True))
        o = jnp.dot(e.astype(BF16), vc_ref[...], preferred_element_type=F32)
        o_ref[...] = (o / jnp.sum(e, axis=-1, keepdims=True)).astype(BF16)


def _nbr_call(proj, bias_tab, nb, with_ctx):
    lat_blocks = SEQ // QBLK
    nj = lat_blocks + (1 if with_ctx else 0)
    nrows_out = nb * SEQ + (nb * CTX_LEN if with_ctx else 0)
    cb = lambda col: col // LANES
    ctx_blk0 = nb * SEQ // CTX_LEN
    swap = lambda f: (lambda h, b, j: f(b, h, j))
    return pl.pallas_call(
        _nbr_kernel,
        grid=(B_HEADS, nb, nj),
        in_specs=[
            pl.BlockSpec((QBLK, LANES), swap(_qblk_index(nb, cb(COL_BQ)))),
            pl.BlockSpec((SEQ, LANES), lambda h, b, j: (b, cb(COL_BK) + h)),
            pl.BlockSpec((SEQ, LANES), lambda h, b, j: (b, cb(COL_BV) + h)),
            pl.BlockSpec((CTX_LEN, LANES), lambda h, b, j: (ctx_blk0 + b, cb(COL_BK) + h)),
            pl.BlockSpec((CTX_LEN, LANES), lambda h, b, j: (ctx_blk0 + b, cb(COL_BV) + h)),
            pl.BlockSpec((1, 3, QBLK, NBR_KEYS), lambda h, b, j: (h, 0, 0, 0)),
        ],
        out_specs=pl.BlockSpec((QBLK, LANES), swap(_qblk_index(nb, 0))),
        out_shape=jax.ShapeDtypeStruct((nrows_out, BRANCH_W), BF16),
        compiler_params=_params(("parallel", "parallel", "arbitrary")),
        name="nbr_attention",
    )(proj, proj, proj, proj, proj, bias_tab)


def _ret_kernel(ql_ref, kl_ref, vl_ref, gl_ref, qc_ref, kc_ref, vc_ref, gc_ref, cos_ref, sa_ref, sb_ref,
                dec_ref, rn_ref, o_ref, buf_ref, of_ref, st_ref, *, with_ctx):
    j = pl.program_id(2)
    ch = QBLK
    n_chunks = SEQ // ch
    kscale = C_KEY_DIM ** -0.5

    @pl.when(j == 0)
    def _():
        ii = lax.broadcasted_iota(jnp.int32, (ch, ch), 0)
        jj = lax.broadcasted_iota(jnp.int32, (ch, ch), 1)
        dist = (ii - jj).astype(F32)
        pos = lax.broadcasted_iota(jnp.int32, (ch, 1), 0).astype(F32)
        lane = lax.broadcasted_iota(jnp.int32, (1, LANES), 1)
        rn = rn_ref[...]

        def log_sigmoid(x):
            return -(jnp.log1p(jnp.exp(-jnp.abs(x))) + jnp.maximum(-x, 0.0))

        heads = []
        for e in range(2):
            lgf = log_sigmoid(dec_ref[0, 0, e:e + 1, 0:1])
            lgb = log_sigmoid(dec_ref[0, 1, e:e + 1, 0:1])
            heads.append(dict(
                mask=(lane >= e * C_KEY_DIM) & (lane < (e + 1) * C_KEY_DIM),
                intra_f=jnp.where(dist >= 0, jnp.exp(lgf * jnp.maximum(dist, 0.0)), 0.0),
                intra_b=jnp.where(dist <= 0, jnp.exp(lgb * jnp.maximum(-dist, 0.0)), 0.0),
                qdec_f=jnp.exp(lgf * (pos + 1.0)), kdec_f=jnp.exp(lgf * (ch - 1.0 - pos)),
                cdec_f=jnp.exp(lgf * ch),
                qdec_b=jnp.exp(lgb * (ch - pos)), kdec_b=jnp.exp(lgb * pos),
                cdec_b=jnp.exp(lgb * ch),
                vs=slice(e * LANES, (e + 1) * LANES),
            ))

        def finish(o, g):
            return (_silu(g.astype(F32)) * (_rms(o) * rn)).astype(BF16)

        qc = qc_ref[...].astype(F32)
        kc = kc_ref[...].astype(F32) * kscale
        for e, hd in enumerate(heads):
            v = vc_ref[:, hd["vs"]]
            st_ref[e] = lax.dot_general((kc * hd["kdec_f"]).astype(BF16), v, TN_DIMS, preferred_element_type=F32)
            st_ref[2 + e] = lax.dot_general((kc * hd["kdec_b"]).astype(BF16), v, TN_DIMS,
                                            preferred_element_type=F32)
            if with_ctx:
                qe = jnp.where(hd["mask"], qc, 0.0).astype(BF16)
                att = lax.dot_general(qe, kc.astype(BF16), NT_DIMS, preferred_element_type=F32)
                att = att * (hd["intra_f"] + hd["intra_b"])
                o = jnp.dot(att.astype(BF16), v, preferred_element_type=F32)
                buf_ref[n_chunks * ch:(n_chunks + 1) * ch, hd["vs"]] = finish(o, gc_ref[:, hd["vs"]])

        def sweep(c, fwd):
            rows = pl.ds(pl.multiple_of(c * ch, ch), ch)
            tabs = (cos_ref[rows, :], sa_ref[rows, :], sb_ref[rows, :])
            q = _rope(ql_ref[rows, :].astype(F32), *tabs)
            k = _rope(kl_ref[rows, :].astype(F32), *tabs) * kscale
            kb = k.astype(BF16)
            for e, hd in enumerate(heads):
                d = "f" if fwd else "b"
                si = e if fwd else 2 + e
                v = vl_ref[rows, hd["vs"]]
                qe = jnp.where(hd["mask"], q, 0.0)
                att = lax.dot_general(qe.astype(BF16), kb, NT_DIMS, preferred_element_type=F32) * hd["intra_" + d]
                st = st_ref[si]
                o = (jnp.dot(att.astype(BF16), v, preferred_element_type=F32)
                     + jnp.dot((qe * hd["qdec_" + d]).astype(BF16), st.astype(BF16), preferred_element_type=F32))
                st_ref[si] = st * hd["cdec_" + d] + lax.dot_general(
                    (k * hd["kdec_" + d]).astype(BF16), v, TN_DIMS, preferred_element_type=F32)
                if fwd:
                    of_ref[rows, hd["vs"]] = o
                else:
                    buf_ref[rows, hd["vs"]] = finish(of_ref[rows, hd["vs"]] + o, gl_ref[rows, hd["vs"]])

        def fwd_body(c, carry):
            sweep(c, True)
            return carry

        def bwd_body(t, carry):
            sweep(n_chunks - 1 - t, False)
            return carry

        lax.fori_loop(0, n_chunks, fwd_body, 0)
        lax.fori_loop(0, n_chunks, bwd_body, 0)

    o_ref[...] = buf_ref[pl.ds(pl.multiple_of(j * ch, ch), ch), :]


def _ret_call(proj, tabs, ret_decay, ret_norm, nb, with_ctx):
    lat_blocks = SEQ // QBLK
    nj = lat_blocks + (1 if with_ctx else 0)
    nrows_out = nb * SEQ + (nb * CTX_LEN if with_ctx else 0)
    pairs = C_HEADS // 2
    ctx_blk0 = nb * SEQ // CTX_LEN
    w2 = 2 * LANES
    dec = jnp.broadcast_to(ret_decay.astype(F32).reshape(2, pairs, 2).transpose(1, 0, 2)[..., None],
                           (pairs, 2, 2, LANES))
    return pl.pallas_call(
        functools.partial(_ret_kernel, with_ctx=with_ctx),
        grid=(nb, pairs, nj),
        in_specs=[
            pl.BlockSpec((SEQ, LANES), lambda b, p, j: (b, COL_CQ // LANES + p)),
            pl.BlockSpec((SEQ, LANES), lambda b, p, j: (b, COL_CK // LANES + p)),
            pl.BlockSpec((SEQ, w2), lambda b, p, j: (b, COL_CV // w2 + p)),
            pl.BlockSpec((SEQ, w2), lambda b, p, j: (b, COL_CG // w2 + p)),
            pl.BlockSpec((CTX_LEN, LANES), lambda b, p, j: (ctx_blk0 + b, COL_CQ // LANES + p)),
            pl.BlockSpec((CTX_LEN, LANES), lambda b, p, j: (ctx_blk0 + b, COL_CK // LANES + p)),
            pl.BlockSpec((CTX_LEN, w2), lambda b, p, j: (ctx_blk0 + b, COL_CV // w2 + p)),
            pl.BlockSpec((CTX_LEN, w2), lambda b, p, j: (ctx_blk0 + b, COL_CG // w2 + p)),
            pl.BlockSpec((SEQ, LANES), lambda b, p, j: (0, 0)),
            pl.BlockSpec((SEQ, LANES), lambda b, p, j: (0, 0)),
            pl.BlockSpec((SEQ, LANES), lambda b, p, j: (0, 0)),
            pl.BlockSpec((1, 2, 2, LANES), lambda b, p, j: (p, 0, 0, 0)),
            pl.BlockSpec((1, LANES), lambda b, p, j: (0, 0)),
        ],
        out_specs=pl.BlockSpec((QBLK, w2), _qblk_index(nb, 0)),
        out_shape=jax.ShapeDtypeStruct((nrows_out, BRANCH_W), BF16),
        scratch_shapes=[
            pltpu.VMEM((SEQ + CTX_LEN, w2), BF16),
            pltpu.VMEM((SEQ, w2), F32),
            pltpu.VMEM((4, LANES, LANES), F32),
        ],
        compiler_params=_params(("parallel", "parallel", "arbitrary")),
        name="retention",
    )(proj, proj, proj, proj, proj, proj, proj, proj, *tabs, dec, ret_norm.reshape(1, LANES))


def _merge_kernel(h_ref, mod_ref, post_ref, oa_ref, ob_ref, yr_ref, ga_ref, gb_ref, gr_ref, wb_ref, wo_ref,
                  o_ref, m_ref, *, nn, tn):
    j = pl.program_id(1)

    @pl.when(j < nn)
    def _():
        def branch(x_ref, g_ref, k):
            g = jax.nn.sigmoid(g_ref[...].astype(F32))
            return g * jnp.dot(x_ref[...], wb_ref[k], preferred_element_type=F32)

        m_ref[j] = (branch(oa_ref, ga_ref, 0) + branch(ob_ref, gb_ref, 1) + branch(yr_ref, gr_ref, 2)).astype(BF16)

    for n in range(nn):
        @pl.when(j == nn + n)
        def _():
            y = jnp.dot(m_ref[0], wo_ref[0:tn, :], preferred_element_type=F32)
            for k in range(1, nn):
                y += jnp.dot(m_ref[k], wo_ref[k * tn:(k + 1) * tn, :], preferred_element_type=F32)
            o_ref[:, n * tn:(n + 1) * tn] = y

    @pl.when(j == 2 * nn - 1)
    def _():
        o_ref[...] = h_ref[...] + mod_ref[0, 5:6, :] * (_rms(o_ref[...]) * post_ref[...])


def _merge_call(h, mod_l, post_g, oa, ob, yr, proj, wb, wo, nrows, tm, n_lat_seg):
    d = h.shape[1]
    tn = 512
    nn = d // tn
    g0 = COL_GATES // tn
    seg = lambda i, j: (jnp.minimum((i * tm) // SEQ, n_lat_seg), 0, 0)
    row = lambda i, j: (i, 0)
    first = lambda j: jnp.minimum(j, nn - 1)
    return pl.pallas_call(
        functools.partial(_merge_kernel, nn=nn, tn=tn),
        grid=(nrows // tm, 2 * nn),
        in_specs=[
            pl.BlockSpec((tm, d), row),
            pl.BlockSpec((1, N_MOD, d), seg),
            pl.BlockSpec((1, d), lambda i, j: (0, 0)),
            pl.BlockSpec((tm, BRANCH_W), row),
            pl.BlockSpec((tm, BRANCH_W), row),
            pl.BlockSpec((tm, BRANCH_W), row),
            pl.BlockSpec((tm, tn), lambda i, j: (i, g0 + first(j))),
            pl.BlockSpec((tm, tn), lambda i, j: (i, g0 + nn + first(j))),
            pl.BlockSpec((tm, tn), lambda i, j: (i, g0 + 2 * nn + first(j))),
            pl.BlockSpec((3, BRANCH_W, tn), lambda i, j: (0, 0, first(j))),
            pl.BlockSpec((d, tn), lambda i, j: (0, jnp.maximum(j - nn, 0))),
        ],
        out_specs=pl.BlockSpec((tm, d), row),
        out_shape=jax.ShapeDtypeStruct((nrows, d), F32),
        scratch_shapes=[pltpu.VMEM((nn, tm, tn), BF16)],
        compiler_params=_params(("parallel", "arbitrary")),
        name="mixer_merge",
    )(h, mod_l, post_g.reshape(1, d), oa, ob, yr, proj, proj, proj, wb, wo)


def kernel(x, c, ctx, c_ctx, w_mod, b_mod, pre_norm, post_norm, ffn_w_in, ffn_w_out, w_in, diff_lambda,
           diff_norm, na_rpb, ret_decay, ret_norm, w_branch, w_out):
    nb, seq, d = x.shape
    assert (seq, d) == (SEQ, D_MODEL) and ctx.shape == (nb, CTX_LEN, d)
    n_lat = nb * SEQ
    n_all = n_lat + nb * CTX_LEN
    tm = min(512, nb * CTX_LEN)

    h = jnp.concatenate([x.reshape(n_lat, d), ctx.reshape(nb * CTX_LEN, d)], axis=0)
    pad = (-(nb + 1)) % 8
    cc = jnp.concatenate([c, c_ctx[None, :], jnp.zeros((pad, d), c.dtype)], axis=0)
    mods = _mod_call(cc, w_mod, b_mod)
    tabs = _rope_tables()

    for l in range(DEPTH):
        with_ctx = l < DEPTH - 1
        lambda_init = 0.8 - 0.6 * math.exp(-0.3 * l)
        nrows_out = n_all if with_ctx else n_lat
        mod_l = mods[l, :nb + 1].reshape(nb + 1, N_MOD, d)
        ffn = functools.partial(_ffn_call, tm=tm, n_lat_seg=nb)

        h = ffn(h, mod_l, 0, pre_norm[l, 0], post_norm[l, 0], ffn_w_in[l, 0].astype(BF16),
                ffn_w_out[l, 0].astype(BF16), n_all)
        proj = _inproj_call(h, mod_l, pre_norm[l, 1], w_in[l].astype(BF16), min(1024, nb * CTX_LEN), nb)
        oa = _diff_call(proj, tabs, diff_lambda[l], diff_norm[l], lambda_init, nb, with_ctx)
        ob = _nbr_call(proj, _nbr_bias_call(na_rpb[l]), nb, with_ctx)
        yr = _ret_call(proj, tabs, ret_decay[l], ret_norm[l], nb, with_ctx)
        h = _merge_call(h, mod_l, post_norm[l, 1], oa, ob, yr, proj, w_branch[l].astype(BF16),
                        w_out[l].astype(BF16), nrows_out, tm, nb)
        h = ffn(h, mod_l, 6, pre_norm[l, 2], post_norm[l, 2], ffn_w_in[l, 1].astype(BF16),
                ffn_w_out[l, 1].astype(BF16), nrows_out)
    return h.reshape(nb, SEQ, d)
```

```python
import functools
import math

import numpy as np
import jax
import jax.numpy as jnp
from jax import lax
from jax.experimental import pallas as pl
from jax.experimental.pallas import tpu as pltpu

D_MODEL = 2048
SEQ = 2048
DEPTH = 2
GRID_W = 64
CTX_LEN = 256
N_MOD = 9
FFN_DIM = 5632
A_HEADS = 8
A_HEAD_DIM = 64
B_HEADS = 8
B_HEAD_DIM = 128
WIN_R = 8
WIN_C = 16
C_HEADS = 8
C_KEY_DIM = 64
BRANCH_W = 1024
ROPE_BASE = 10000.0
EPS = 1e-6

COL_AQ, COL_AK, COL_AV = 0, 1024, 2048
COL_BQ, COL_BK, COL_BV = 3072, 4096, 5120
COL_CQ, COL_CK, COL_CV, COL_CG = 6144, 6656, 7168, 8192
COL_GATES = 9216
IN_COLS = 15360

LANES = 128
QBLK = 256
ROWS_PER_QBLK = QBLK // GRID_W
NBR_KROWS = ROWS_PER_QBLK + WIN_R - 1
NBR_KEYS = NBR_KROWS * GRID_W
VMEM_LIMIT = 56 * 1024 * 1024
DIFF_ROW_SPLIT = 2
NBR_MASKED = -1e30

BF16 = jnp.bfloat16
F32 = jnp.float32
NT_DIMS = (((1,), (1,)), ((), ()))
TN_DIMS = (((0,), (0,)), ((), ()))


def _params(sem):
    return pltpu.CompilerParams(dimension_semantics=sem, vmem_limit_bytes=VMEM_LIMIT)


def _rms(x):
    return x * lax.rsqrt(jnp.mean(x * x, axis=-1, keepdims=True) + EPS)


def _silu(x):
    return x * jax.nn.sigmoid(x)


def _mod_kernel(cc_ref, w_ref, b_ref, o_ref):
    s = _silu(cc_ref[...]).astype(BF16)
    o_ref[0] = jnp.dot(s, w_ref[0].astype(BF16), preferred_element_type=F32) + b_ref[0]


def _mod_call(cc, w_mod, b_mod):
    depth, d, n = w_mod.shape
    rows = cc.shape[0]
    tn = 1024
    return pl.pallas_call(
        _mod_kernel,
        grid=(depth, n // tn),
        in_specs=[
            pl.BlockSpec((rows, d), lambda l, j: (0, 0)),
            pl.BlockSpec((1, d, tn), lambda l, j: (l, 0, j)),
            pl.BlockSpec((1, 1, tn), lambda l, j: (l, 0, j)),
        ],
        out_specs=pl.BlockSpec((1, rows, tn), lambda l, j: (l, 0, j)),
        out_shape=jax.ShapeDtypeStruct((depth, rows, n), F32),
        compiler_params=_params(("parallel", "parallel")),
        name="mod_vectors",
    )(cc, w_mod, b_mod.reshape(depth, 1, n))


def _ffn_kernel(h_ref, mod_ref, pre_ref, post_ref, w1a_ref, w1b_ref, w2_ref, o_ref, u_ref, hm_ref, *,
                k0, nf, tf, nn, tn):
    j = pl.program_id(1)

    @pl.when(j == 0)
    def _():
        shift = mod_ref[0, k0:k0 + 1, :]
        scale = mod_ref[0, k0 + 1:k0 + 2, :]
        u = _rms(h_ref[...]) * pre_ref[...] * (1.0 + scale) + shift
        u_ref[...] = u.astype(BF16)

    @pl.when(j < nf)
    def _():
        u = u_ref[...]
        a = jnp.dot(u, w1a_ref[...], preferred_element_type=F32)
        b = jnp.dot(u, w1b_ref[...], preferred_element_type=F32)
        hm_ref[j] = (_silu(a) * b).astype(BF16)

    for n in range(nn):
        @pl.when(j == nf + n)
        def _():
            y = jnp.dot(hm_ref[0], w2_ref[0:tf, :], preferred_element_type=F32)
            for k in range(1, nf):
                y += jnp.dot(hm_ref[k], w2_ref[k * tf:(k + 1) * tf, :], preferred_element_type=F32)
            o_ref[:, n * tn:(n + 1) * tn] = y

    @pl.when(j == nf + nn - 1)
    def _():
        gate = mod_ref[0, k0 + 2:k0 + 3, :]
        o_ref[...] = h_ref[...] + 0.5 * gate * (_rms(o_ref[...]) * post_ref[...])


def _ffn_call(h, mod_l, k0, pre_g, post_g, w1, w2, nrows, tm, n_lat_seg):
    d = h.shape[1]
    f = w2.shape[0]
    tf, tn = 512, 512
    nf, nn = f // tf, d // tn
    seg = lambda i, j: (jnp.minimum((i * tm) // SEQ, n_lat_seg), 0, 0)
    return pl.pallas_call(
        functools.partial(_ffn_kernel, k0=k0, nf=nf, tf=tf, nn=nn, tn=tn),
        grid=(nrows // tm, nf + nn),
        in_specs=[
            pl.BlockSpec((tm, d), lambda i, j: (i, 0)),
            pl.BlockSpec((1, N_MOD, d), seg),
            pl.BlockSpec((1, d), lambda i, j: (0, 0)),
            pl.BlockSpec((1, d), lambda i, j: (0, 0)),
            pl.BlockSpec((d, tf), lambda i, j: (0, jnp.minimum(j, nf - 1))),
            pl.BlockSpec((d, tf), lambda i, j: (0, jnp.minimum(j, nf - 1) + nf)),
            pl.BlockSpec((f, tn), lambda i, j: (0, jnp.maximum(j - nf, 0))),
        ],
        out_specs=pl.BlockSpec((tm, d), lambda i, j: (i, 0)),
        out_shape=jax.ShapeDtypeStruct((nrows, d), F32),
        scratch_shapes=[pltpu.VMEM((tm, d), BF16), pltpu.VMEM((nf, tm, tf), BF16)],
        compiler_params=_params(("parallel", "arbitrary")),
        name="ffn_sublayer",
    )(h, mod_l, pre_g.reshape(1, d), post_g.reshape(1, d), w1, w1, w2)


def _inproj_kernel(h_ref, mod_ref, pre_ref, w_ref, o_ref, u_ref):
    @pl.when(pl.program_id(1) == 0)
    def _():
        shift = mod_ref[0, 3:4, :]
        scale = mod_ref[0, 4:5, :]
        u = _rms(h_ref[...]) * pre_ref[...] * (1.0 + scale) + shift
        u_ref[...] = u.astype(BF16)

    o_ref[...] = jnp.dot(u_ref[...], w_ref[...], preferred_element_type=F32).astype(BF16)


def _inproj_call(h, mod_l, pre_g, w, tm, n_lat_seg):
    nrows, d = h.shape
    n = w.shape[1]
    tn = 1024
    seg = lambda i, j: (jnp.minimum((i * tm) // SEQ, n_lat_seg), 0, 0)
    return pl.pallas_call(
        _inproj_kernel,
        grid=(nrows // tm, n // tn),
        in_specs=[
            pl.BlockSpec((tm, d), lambda i, j: (i, 0)),
            pl.BlockSpec((1, N_MOD, d), seg),
            pl.BlockSpec((1, d), lambda i, j: (0, 0)),
            pl.BlockSpec((d, tn), lambda i, j: (0, j)),
        ],
        out_specs=pl.BlockSpec((tm, tn), lambda i, j: (i, j)),
        out_shape=jax.ShapeDtypeStruct((nrows, n), BF16),
        scratch_shapes=[pltpu.VMEM((tm, d), BF16)],
        compiler_params=_params(("parallel", "arbitrary")),
        name="mixer_in_proj",
    )(h, mod_l, pre_g.reshape(1, d), w)


def _rope(t, cos, sa, sb):
    return t * cos + pltpu.roll(t, LANES - 16, 1) * sa + pltpu.roll(t, 16, 1) * sb


def _rope_tables():
    pos = np.arange(SEQ)
    prow = jnp.asarray(pos // GRID_W, F32)
    pcol = jnp.asarray(pos % GRID_W, F32)
    half = 16
    inv = ROPE_BASE ** (-jnp.arange(half, dtype=F32) / half)
    lane = np.arange(LANES)
    freq = jnp.asarray(lane % half)
    use_col = jnp.asarray((lane % 64) >= 32)
    second = jnp.asarray((lane % 32) >= half)
    p = jnp.where(use_col[None, :], pcol[:, None], prow[:, None])
    ang = p * inv[freq][None, :]
    cos, sin = jnp.cos(ang), jnp.sin(ang)
    sa = jnp.where(second[None, :], 0.0, -sin)
    sb = jnp.where(second[None, :], sin, 0.0)
    return cos, sa, sb


def _mixer_specs(nb, width, cols, order):
    ctx_blk0 = nb * SEQ // CTX_LEN

    def spec(rows, col, w, ctx):
        def index(*g):
            b, h = order(*g)
            return ((ctx_blk0 + b) if ctx else b, col // w + h)
        return pl.BlockSpec((rows, w), index)

    return ([spec(SEQ, c, w, False) for c, w in zip(cols, width)]
            + [spec(CTX_LEN, c, w, True) for c, w in zip(cols, width)])


def _mixer_outs(nb, width, order, with_ctx):
    def index(*g):
        b, h = order(*g)
        return (b, h)
    specs = [pl.BlockSpec((SEQ, width), index)]
    shapes = [jax.ShapeDtypeStruct((nb * SEQ, BRANCH_W), BF16)]
    if with_ctx:
        specs.append(pl.BlockSpec((CTX_LEN, width), index))
        shapes.append(jax.ShapeDtypeStruct((nb * CTX_LEN, BRANCH_W), BF16))
    return specs, shapes


def _diff_kernel(ql_ref, kl_ref, vl_ref, qc_ref, kc_ref, vc_ref, cos_ref, sa_ref, sb_ref, lam_ref, g_ref,
                 *rest, lambda_init, with_ctx):
    ol_ref, oc_ref = (rest[0], rest[1]) if with_ctx else (rest[0], None)
    k_all, v_ones = rest[-2:]
    k_all[0:CTX_LEN, :] = kc_ref[...]
    k_all[CTX_LEN:, :] = _rope(kl_ref[...].astype(F32), cos_ref[...], sa_ref[...], sb_ref[...]).astype(BF16)
    v_ones[0:CTX_LEN, 0:LANES] = vc_ref[...]
    v_ones[CTX_LEN:, 0:LANES] = vl_ref[...]
    v_ones[:, LANES:] = jnp.ones((CTX_LEN + SEQ, LANES), BF16)

    lv = lam_ref[...]
    lam = (jnp.exp(jnp.sum(lv[0:1] * lv[1:2], axis=-1, keepdims=True))
           - jnp.exp(jnp.sum(lv[2:3] * lv[3:4], axis=-1, keepdims=True)) + lambda_init)
    lane = lax.broadcasted_iota(jnp.int32, (1, LANES), 1)

    def attend(q, nk):
        q = q * (A_HEAD_DIM ** -0.5)
        k = k_all[0:nk, :]
        v1 = v_ones[0:nk, :]

        rows = QBLK // DIFF_ROW_SPLIT
        qms = [jnp.where(mask, q[r * rows:(r + 1) * rows], 0.0).astype(BF16)
               for r in range(DIFF_ROW_SPLIT) for mask in (lane < A_HEAD_DIM, lane >= A_HEAD_DIM)]
        ss = [lax.dot_general(qm, k, NT_DIMS, preferred_element_type=F32) for qm in qms]
        es = [jnp.exp(s - jnp.max(s, axis=-1, keepdims=True)).astype(BF16) for s in ss]
        ols = [jnp.dot(e, v1, preferred_element_type=F32) for e in es]
        os = [ol[:, :LANES] / ol[:, LANES:] for ol in ols]
        o = jnp.concatenate([os[2 * r] - lam * os[2 * r + 1] for r in range(DIFF_ROW_SPLIT)], axis=0)
        return (_rms(o) * g_ref[...] * (1.0 - lambda_init)).astype(BF16)

    def block(t, carry):
        rows = pl.ds(pl.multiple_of(t * QBLK, QBLK), QBLK)
        q = _rope(ql_ref[rows, :].astype(F32), cos_ref[rows, :], sa_ref[rows, :], sb_ref[rows, :])
        ol_ref[rows, :] = attend(q, CTX_LEN + SEQ)
        return carry

    lax.fori_loop(0, SEQ // QBLK, block, 0)
    if with_ctx:
        oc_ref[...] = attend(qc_ref[...].astype(F32), CTX_LEN)


def _diff_call(proj, tabs, diff_lambda, diff_norm, lambda_init, nb, with_ctx):
    order = lambda b, h: (b, h)
    whole = lambda b, h: (0, 0)
    out_specs, out_shapes = _mixer_outs(nb, LANES, order, with_ctx)
    return pl.pallas_call(
        functools.partial(_diff_kernel, lambda_init=lambda_init, with_ctx=with_ctx),
        grid=(nb, A_HEADS),
        in_specs=_mixer_specs(nb, (LANES,) * 3, (COL_AQ, COL_AK, COL_AV), order) + [
            pl.BlockSpec((SEQ, LANES), whole),
            pl.BlockSpec((SEQ, LANES), whole),
            pl.BlockSpec((SEQ, LANES), whole),
            pl.BlockSpec((4, A_HEAD_DIM), whole),
            pl.BlockSpec((1, LANES), whole),
        ],
        out_specs=out_specs,
        out_shape=out_shapes,
        scratch_shapes=[pltpu.VMEM((CTX_LEN + SEQ, LANES), BF16), pltpu.VMEM((CTX_LEN + SEQ, 2 * LANES), BF16)],
        compiler_params=_params(("parallel", "parallel")),
        name="diff_attention",
    )(*([proj] * 6), *tabs, diff_lambda, diff_norm.reshape(1, LANES))


def _nbr_block_rows(cfg, qi, kj):
    n_rows = SEQ // GRID_W
    r0 = (0, ROWS_PER_QBLK, n_rows - ROWS_PER_QBLK)[cfg]
    ks = min(max(r0 - WIN_R // 2, 0), n_rows - NBR_KROWS)
    r = r0 + qi
    rs = min(max(r - WIN_R // 2, 0), n_rows - WIN_R)
    kr = ks + kj
    return rs <= kr < rs + WIN_R, kr - r + WIN_R - 1


def _nbr_bias_kernel(rpb_ref, o_ref):
    n_dr, n_dc = 2 * WIN_R - 1, 2 * WIN_C - 1
    base = pl.program_id(0) * (n_dr * n_dc)
    qc = lax.broadcasted_iota(jnp.int32, (GRID_W, LANES), 0)
    lane = lax.broadcasted_iota(jnp.int32, (GRID_W, LANES), 1)
    kc = lane % GRID_W
    dc = kc - qc + (WIN_C - 1)
    cs = jnp.clip(qc - WIN_C // 2, 0, GRID_W - WIN_C)
    col_ok = (kc >= cs) & (kc < cs + WIN_C)
    masked = jnp.full((GRID_W, LANES), NBR_MASKED, F32)

    def row_pattern(dr):
        acc = masked
        for x in range(n_dc):
            acc = jnp.where(dc == x, rpb_ref[base + dr * n_dc + x], acc)
        return jnp.where(col_ok, acc, masked)

    pats = [row_pattern(dr) for dr in range(n_dr)]

    def pattern(cfg, qi, kj):
        if kj >= NBR_KROWS:
            return masked
        ok, dr = _nbr_block_rows(cfg, qi, kj)
        return pats[dr] if ok else masked

    for cfg in range(3):
        for qi in range(ROWS_PER_QBLK):
            rows = slice(qi * GRID_W, (qi + 1) * GRID_W)
            for m in range(pl.cdiv(NBR_KEYS, LANES)):
                tile = jnp.where(lane < GRID_W, pattern(cfg, qi, 2 * m), pattern(cfg, qi, 2 * m + 1))
                width = min(LANES, NBR_KEYS - m * LANES)
                o_ref[0, cfg, rows, m * LANES:m * LANES + width] = tile[:, :width]


def _nbr_bias_call(rpb):
    heads = rpb.shape[0]
    return pl.pallas_call(
        _nbr_bias_kernel,
        grid=(heads,),
        in_specs=[pl.BlockSpec(memory_space=pltpu.SMEM)],
        out_specs=pl.BlockSpec((1, 3, QBLK, NBR_KEYS), lambda h: (h, 0, 0, 0)),
        out_shape=jax.ShapeDtypeStruct((heads, 3, QBLK, NBR_KEYS), F32),
        compiler_params=_params(("parallel",)),
        name="nbr_bias_table",
    )(rpb.astype(F32).reshape(-1))


def _nbr_kernel(ql_ref, kl_ref, vl_ref, qc_ref, kc_ref, vc_ref, bias_ref, *rest, with_ctx):
    ol_ref, oc_ref = (rest[0], rest[1]) if with_ctx else (rest[0], None)
    v1l, v1c = rest[-2:]
    scale = B_HEAD_DIM ** -0.5
    v1l[:, 0:LANES] = vl_ref[...]
    v1l[:, LANES:] = jnp.ones((SEQ, LANES), BF16)
    v1c[:, 0:LANES] = vc_ref[...]
    v1c[:, LANES:] = jnp.ones((CTX_LEN, LANES), BF16)
    kc = kc_ref[...]
    lat_blocks = SEQ // QBLK

    for blk in range(lat_blocks):
        cfg = 0 if blk == 0 else (2 if blk == lat_blocks - 1 else 1)
        start = min(max((blk * ROWS_PER_QBLK - WIN_R // 2) * GRID_W, 0), SEQ - NBR_KEYS)
        q = ql_ref[blk * QBLK:(blk + 1) * QBLK, :]
        s_lat = (lax.dot_general(q, kl_ref[start:start + NBR_KEYS, :], NT_DIMS, preferred_element_type=F32) * scale
                 + bias_ref[0, cfg])
        s_ctx = lax.dot_general(q, kc, NT_DIMS, preferred_element_type=F32) * scale
        m = jnp.maximum(jnp.max(s_lat, axis=-1, keepdims=True), jnp.max(s_ctx, axis=-1, keepdims=True))
        ol = (jnp.dot(jnp.exp(s_lat - m).astype(BF16), v1l[start:start + NBR_KEYS, :], preferred_element_type=F32)
              + jnp.dot(jnp.exp(s_ctx - m).astype(BF16), v1c[...], preferred_element_type=F32))
        ol_ref[blk * QBLK:(blk + 1) * QBLK, :] = (ol[:, :LANES] / ol[:, LANES:]).astype(BF16)

    if with_ctx:
        s = lax.dot_general(qc_ref[...], kc, NT_DIMS, preferred_element_type=F32) * scale
        e = jnp.exp(s - jnp.max(s, axis=-1, keepdims=True)).astype(BF16)
        ol = jnp.dot(e, v1c[...], preferred_element_type=F32)
        oc_ref[...] = (ol[:, :LANES] / ol[:, LANES:]).astype(BF16)


def _nbr_call(proj, bias_tab, nb, with_ctx):
    order = lambda h, b: (b, h)
    out_specs, out_shapes = _mixer_outs(nb, LANES, order, with_ctx)
    return pl.pallas_call(
        functools.partial(_nbr_kernel, with_ctx=with_ctx),
        grid=(B_HEADS, nb),
        in_specs=_mixer_specs(nb, (LANES,) * 3, (COL_BQ, COL_BK, COL_BV), order) + [
            pl.BlockSpec((1, 3, QBLK, NBR_KEYS), lambda h, b: (h, 0, 0, 0)),
        ],
        out_specs=out_specs,
        out_shape=out_shapes,
        scratch_shapes=[pltpu.VMEM((SEQ, 2 * LANES), BF16), pltpu.VMEM((CTX_LEN, 2 * LANES), BF16)],
        compiler_params=_params(("parallel", "parallel")),
        name="nbr_attention",
    )(*([proj] * 6), bias_tab)


def _ret_kernel(ql_ref, kl_ref, vl_ref, gl_ref, qc_ref, kc_ref, vc_ref, gc_ref, cos_ref, sa_ref, sb_ref,
                dec_ref, rn_ref, *rest, with_ctx):
    ol_ref, oc_ref = (rest[0], rest[1]) if with_ctx else (rest[0], None)
    of_ref, ob_ref, st_ref = rest[-3:]
    ch = QBLK
    n_chunks = SEQ // ch
    kscale = C_KEY_DIM ** -0.5

    def run():
        ii = lax.broadcasted_iota(jnp.int32, (ch, ch), 0)
        jj = lax.broadcasted_iota(jnp.int32, (ch, ch), 1)
        dist = (ii - jj).astype(F32)
        pos = lax.broadcasted_iota(jnp.int32, (ch, 1), 0).astype(F32)
        lane = lax.broadcasted_iota(jnp.int32, (1, LANES), 1)
        rn = rn_ref[...]

        def log_sigmoid(x):
            return -(jnp.log1p(jnp.exp(-jnp.abs(x))) + jnp.maximum(-x, 0.0))

        heads = []
        for e in range(2):
            lgf = log_sigmoid(dec_ref[0, 0, e:e + 1, 0:1])
            lgb = log_sigmoid(dec_ref[0, 1, e:e + 1, 0:1])
            heads.append(dict(
                mask=(lane >= e * C_KEY_DIM) & (lane < (e + 1) * C_KEY_DIM),
                intra_f=jnp.where(dist >= 0, jnp.exp(lgf * jnp.maximum(dist, 0.0)), 0.0),
                intra_b=jnp.where(dist <= 0, jnp.exp(lgb * jnp.maximum(-dist, 0.0)), 0.0),
                qdec_f=jnp.exp(lgf * (pos + 1.0)), kdec_f=jnp.exp(lgf * (ch - 1.0 - pos)),
                cdec_f=jnp.exp(lgf * ch),
                qdec_b=jnp.exp(lgb * (ch - pos)), kdec_b=jnp.exp(lgb * pos),
                cdec_b=jnp.exp(lgb * ch),
                vs=slice(e * LANES, (e + 1) * LANES),
            ))

        def finish(o, g):
            return (_silu(g.astype(F32)) * (_rms(o) * rn)).astype(BF16)

        qc = qc_ref[...].astype(F32)
        kc = kc_ref[...].astype(F32) * kscale
        for e, hd in enumerate(heads):
            v = vc_ref[:, hd["vs"]]
            st_ref[e] = lax.dot_general((kc * hd["kdec_f"]).astype(BF16), v, TN_DIMS, preferred_element_type=F32)
            st_ref[2 + e] = lax.dot_general((kc * hd["kdec_b"]).astype(BF16), v, TN_DIMS,
                                            preferred_element_type=F32)
            if with_ctx:
                qe = jnp.where(hd["mask"], qc, 0.0).astype(BF16)
                att = lax.dot_general(qe, kc.astype(BF16), NT_DIMS, preferred_element_type=F32)
                att = att * (hd["intra_f"] + hd["intra_b"])
                o = jnp.dot(att.astype(BF16), v, preferred_element_type=F32)
                oc_ref[:, hd["vs"]] = finish(o, gc_ref[:, hd["vs"]])

        def chunk_rows(c):
            return pl.ds(pl.multiple_of(c * ch, ch), ch)

        def sweep(c, fwd):
            rows = chunk_rows(c)
            tabs = (cos_ref[rows, :], sa_ref[rows, :], sb_ref[rows, :])
            q = _rope(ql_ref[rows, :].astype(F32), *tabs)
            k = _rope(kl_ref[rows, :].astype(F32), *tabs) * kscale
            kb = k.astype(BF16)
            for e, hd in enumerate(heads):
                d = "f" if fwd else "b"
                si = e if fwd else 2 + e
                v = vl_ref[rows, hd["vs"]]
                qe = jnp.where(hd["mask"], q, 0.0)
                att = lax.dot_general(qe.astype(BF16), kb, NT_DIMS, preferred_element_type=F32) * hd["intra_" + d]
                st = st_ref[si]
                o = (jnp.dot(att.astype(BF16), v, preferred_element_type=F32)
                     + jnp.dot((qe * hd["qdec_" + d]).astype(BF16), st.astype(BF16), preferred_element_type=F32))
                st_ref[si] = st * hd["cdec_" + d] + lax.dot_general(
                    (k * hd["kdec_" + d]).astype(BF16), v, TN_DIMS, preferred_element_type=F32)
                (of_ref if fwd else ob_ref)[rows, hd["vs"]] = o

        def finish_chunk(c):
            rows = chunk_rows(c)
            for hd in heads:
                vs = hd["vs"]
                ol_ref[rows, vs] = finish(of_ref[rows, vs] + ob_ref[rows, vs], gl_ref[rows, vs])

        def first_half(t, carry):
            sweep(t, True)
            sweep(n_chunks - 1 - t, False)
            return carry

        def second_half(t, carry):
            sweep(t, True)
            sweep(n_chunks - 1 - t, False)
            finish_chunk(t)
            finish_chunk(n_chunks - 1 - t)
            return carry

        lax.fori_loop(0, n_chunks // 2, first_half, 0)
        lax.fori_loop(n_chunks // 2, n_chunks, second_half, 0)

    run()


def _ret_call(proj, tabs, ret_decay, ret_norm, nb, with_ctx):
    pairs = C_HEADS // 2
    w2 = 2 * LANES
    order = lambda b, p: (b, p)
    whole = lambda b, p: (0, 0)
    out_specs, out_shapes = _mixer_outs(nb, w2, order, with_ctx)
    dec = jnp.broadcast_to(ret_decay.astype(F32).reshape(2, pairs, 2).transpose(1, 0, 2)[..., None],
                           (pairs, 2, 2, LANES))
    return pl.pallas_call(
        functools.partial(_ret_kernel, with_ctx=with_ctx),
        grid=(nb, pairs),
        in_specs=_mixer_specs(nb, (LANES, LANES, w2, w2), (COL_CQ, COL_CK, COL_CV, COL_CG), order) + [
            pl.BlockSpec((SEQ, LANES), whole),
            pl.BlockSpec((SEQ, LANES), whole),
            pl.BlockSpec((SEQ, LANES), whole),
            pl.BlockSpec((1, 2, 2, LANES), lambda b, p: (p, 0, 0, 0)),
            pl.BlockSpec((1, LANES), whole),
        ],
        out_specs=out_specs,
        out_shape=out_shapes,
        scratch_shapes=[
            pltpu.VMEM((SEQ, w2), F32),
            pltpu.VMEM((SEQ, w2), F32),
            pltpu.VMEM((4, LANES, LANES), F32),
        ],
        compiler_params=_params(("parallel", "parallel")),
        name="retention",
    )(*([proj] * 8), *tabs, dec, ret_norm.reshape(1, LANES))


def _merge_kernel(h_ref, mod_ref, post_ref, g_lo_ref, g_hi_ref, wb_ref, wo_ref, *rest, n_lat_tiles, with_ctx):
    d = h_ref.shape[1]
    o_ref = rest[-1]
    if with_ctx:
        is_lat = pl.program_id(0) < n_lat_tiles
        xs = [jnp.where(is_lat, rest[2 * k][...], rest[2 * k + 1][...]) for k in range(3)]
    else:
        xs = [rest[k][...] for k in range(3)]
    g_lo = g_lo_ref[...]
    g_hi = g_hi_ref[...]
    gates = (g_lo[:, :d], jnp.concatenate([g_lo[:, d:], g_hi[:, :d // 2]], axis=1), g_hi[:, d // 2:])
    m = None
    for k in range(3):
        t = jax.nn.sigmoid(gates[k].astype(F32)) * jnp.dot(xs[k], wb_ref[k], preferred_element_type=F32)
        m = t if m is None else m + t
    y = jnp.dot(m.astype(BF16), wo_ref[...], preferred_element_type=F32)
    o_ref[...] = h_ref[...] + mod_ref[0, 5:6, :] * (_rms(y) * post_ref[...])


def _merge_call(h, mod_l, post_g, branches, proj, wb, wo, nrows, n_lat_seg):
    d = h.shape[1]
    tm = CTX_LEN
    with_ctx = len(branches[0]) == 2
    n_lat_tiles = n_lat_seg * SEQ // tm
    gw = 3 * d // 2
    g0 = COL_GATES // gw
    resident = dict(pipeline_mode=pl.Buffered(1))
    row = lambda i: (i, 0)
    lat_row = lambda i: (jnp.minimum(i, n_lat_tiles - 1), 0)
    ctx_row = lambda i: (jnp.maximum(i - n_lat_tiles, 0), 0)
    branch_specs, branch_args = [], []
    for outs in branches:
        branch_specs.append(pl.BlockSpec((tm, BRANCH_W), lat_row))
        if with_ctx:
            branch_specs.append(pl.BlockSpec((tm, BRANCH_W), ctx_row))
        branch_args.extend(outs)
    return pl.pallas_call(
        functools.partial(_merge_kernel, n_lat_tiles=n_lat_tiles, with_ctx=with_ctx),
        grid=(nrows // tm,),
        in_specs=[
            pl.BlockSpec((tm, d), row),
            pl.BlockSpec((1, N_MOD, d), lambda i: (jnp.minimum((i * tm) // SEQ, n_lat_seg), 0, 0)),
            pl.BlockSpec((1, d), lambda i: (0, 0)),
            pl.BlockSpec((tm, gw), lambda i: (i, g0)),
            pl.BlockSpec((tm, gw), lambda i: (i, g0 + 1)),
            pl.BlockSpec((3, BRANCH_W, d), lambda i: (0, 0, 0), **resident),
            pl.BlockSpec((d, d), lambda i: (0, 0), **resident),
        ] + branch_specs,
        out_specs=pl.BlockSpec((tm, d), row),
        out_shape=jax.ShapeDtypeStruct((nrows, d), F32),
        compiler_params=_params(("parallel",)),
        name="mixer_merge",
    )(h, mod_l, post_g.reshape(1, d), proj, proj, wb, wo, *branch_args)


def kernel(x, c, ctx, c_ctx, w_mod, b_mod, pre_norm, post_norm, ffn_w_in, ffn_w_out, w_in, diff_lambda,
           diff_norm, na_rpb, ret_decay, ret_norm, w_branch, w_out):
    nb, seq, d = x.shape
    assert (seq, d) == (SEQ, D_MODEL) and ctx.shape == (nb, CTX_LEN, d)
    n_lat = nb * SEQ
    n_all = n_lat + nb * CTX_LEN
    tm = min(512, nb * CTX_LEN)

    h = jnp.concatenate([x.reshape(n_lat, d), ctx.reshape(nb * CTX_LEN, d)], axis=0)
    pad = (-(nb + 1)) % 8
    cc = jnp.concatenate([c, c_ctx[None, :], jnp.zeros((pad, d), c.dtype)], axis=0)
    mods = _mod_call(cc, w_mod, b_mod)
    tabs = _rope_tables()

    for l in range(DEPTH):
        with_ctx = l < DEPTH - 1
        lambda_init = 0.8 - 0.6 * math.exp(-0.3 * l)
        nrows_out = n_all if with_ctx else n_lat
        mod_l = mods[l, :nb + 1].reshape(nb + 1, N_MOD, d)
        ffn = functools.partial(_ffn_call, tm=tm, n_lat_seg=nb)

        h = ffn(h, mod_l, 0, pre_norm[l, 0], post_norm[l, 0], ffn_w_in[l, 0].astype(BF16),
                ffn_w_out[l, 0].astype(BF16), n_all)
        proj = _inproj_call(h, mod_l, pre_norm[l, 1], w_in[l].astype(BF16), min(1024, nb * CTX_LEN), nb)
        oa = _diff_call(proj, tabs, diff_lambda[l], diff_norm[l], lambda_init, nb, with_ctx)
        ob = _nbr_call(proj, _nbr_bias_call(na_rpb[l]), nb, with_ctx)
        yr = _ret_call(proj, tabs, ret_decay[l], ret_norm[l], nb, with_ctx)
        h = _merge_call(h, mod_l, post_norm[l, 1], (oa, ob, yr), proj, w_branch[l].astype(BF16),
                        w_out[l].astype(BF16), nrows_out, nb)
        h = ffn(h, mod_l, 6, pre_norm[l, 2], post_norm[l, 2], ffn_w_in[l, 1].astype(BF16),
                ffn_w_out[l, 1].astype(BF16), nrows_out)
    return h.reshape(nb, SEQ, d)
```

```python
import functools
import math

import numpy as np
import jax
import jax.numpy as jnp
from jax import lax
from jax.experimental import pallas as pl
from jax.experimental.pallas import tpu as pltpu

D_MODEL = 2048
SEQ = 2048
DEPTH = 2
GRID_W = 64
CTX_LEN = 256
N_MOD = 9
FFN_DIM = 5632
A_HEADS = 8
A_HEAD_DIM = 64
B_HEADS = 8
B_HEAD_DIM = 128
WIN_R = 8
WIN_C = 16
C_HEADS = 8
C_KEY_DIM = 64
BRANCH_W = 1024
ROPE_BASE = 10000.0
EPS = 1e-6

COL_AQ, COL_AK, COL_AV = 0, 1024, 2048
COL_BQ, COL_BK, COL_BV = 3072, 4096, 5120
COL_CQ, COL_CK, COL_CV, COL_CG = 6144, 6656, 7168, 8192
COL_GATES = 9216
IN_COLS = 15360

LANES = 128
BF16_SUBLANES = 16
QBLK = 256
ROWS_PER_QBLK = QBLK // GRID_W
NBR_KROWS = ROWS_PER_QBLK + WIN_R - 1
NBR_KEYS = NBR_KROWS * GRID_W
VMEM_LIMIT = 56 * 1024 * 1024
DIFF_ROW_SPLIT = 2
NBR_MASKED = -1e30

BF16 = jnp.bfloat16
F32 = jnp.float32
NT_DIMS = (((1,), (1,)), ((), ()))
TN_DIMS = (((0,), (0,)), ((), ()))


def _params(sem):
    return pltpu.CompilerParams(dimension_semantics=sem, vmem_limit_bytes=VMEM_LIMIT)


def _rms(x):
    return x * lax.rsqrt(jnp.mean(x * x, axis=-1, keepdims=True) + EPS)


def _silu(x):
    return x * jax.nn.sigmoid(x)


def _row_chunks(n_rows, rows_per, body):
    def step(r, carry):
        body(pl.ds(pl.multiple_of(r * rows_per, rows_per), rows_per))
        return carry
    lax.fori_loop(0, n_rows // rows_per, step, 0, unroll=4)


def _adaln_to(u_ref, h_ref, pre, shift, scale):
    gain = pre * (1.0 + scale)

    def body(rows):
        u_ref[rows, :] = (_rms(h_ref[rows, :]) * gain + shift).astype(u_ref.dtype)

    _row_chunks(h_ref.shape[0], BF16_SUBLANES, body)


def _mod_kernel(cc_ref, w_ref, b_ref, o_ref):
    s = _silu(cc_ref[...]).astype(BF16)
    o_ref[0] = jnp.dot(s, w_ref[0].astype(BF16), preferred_element_type=F32) + b_ref[0]


def _mod_call(cc, w_mod, b_mod):
    depth, d, n = w_mod.shape
    rows = cc.shape[0]
    tn = 1024
    return pl.pallas_call(
        _mod_kernel,
        grid=(depth, n // tn),
        in_specs=[
            pl.BlockSpec((rows, d), lambda l, j: (0, 0)),
            pl.BlockSpec((1, d, tn), lambda l, j: (l, 0, j)),
            pl.BlockSpec((1, 1, tn), lambda l, j: (l, 0, j)),
        ],
        out_specs=pl.BlockSpec((1, rows, tn), lambda l, j: (l, 0, j)),
        out_shape=jax.ShapeDtypeStruct((depth, rows, n), F32),
        compiler_params=_params(("parallel", "parallel")),
        name="mod_vectors",
    )(cc, w_mod, b_mod.reshape(depth, 1, n))


def _ffn_kernel(h_ref, mod_ref, pre_ref, post_ref, w1a_ref, w1b_ref, w2_ref, o_ref, u_ref, hm_ref, *,
                k0, nf, tf, nn, tn):
    j = pl.program_id(1)

    @pl.when(j == 0)
    def _():
        _adaln_to(u_ref, h_ref, pre_ref[...], mod_ref[0, k0:k0 + 1, :], mod_ref[0, k0 + 1:k0 + 2, :])

    @pl.when(j < nf)
    def _():
        u = u_ref[...]
        a = jnp.dot(u, w1a_ref[...], preferred_element_type=F32)
        b = jnp.dot(u, w1b_ref[...], preferred_element_type=F32)
        hm_ref[j] = (_silu(a) * b).astype(BF16)

    for n in range(nn):
        @pl.when(j == nf + n)
        def _():
            y = jnp.dot(hm_ref[0], w2_ref[0:tf, :], preferred_element_type=F32)
            for k in range(1, nf):
                y += jnp.dot(hm_ref[k], w2_ref[k * tf:(k + 1) * tf, :], preferred_element_type=F32)
            o_ref[:, n * tn:(n + 1) * tn] = y

    @pl.when(j == nf + nn - 1)
    def _():
        gate = mod_ref[0, k0 + 2:k0 + 3, :]
        o_ref[...] = h_ref[...] + 0.5 * gate * (_rms(o_ref[...]) * post_ref[...])


def _ffn_call(h, mod_l, k0, pre_g, post_g, w1, w2, lk, nrows, tm, n_lat_seg):
    d = h.shape[1]
    f = w2.shape[2]
    tf, tn = 512, 512
    nf, nn = f // tf, d // tn
    seg = lambda i, j: (jnp.minimum((i * tm) // SEQ, n_lat_seg), 0, 0)
    return pl.pallas_call(
        functools.partial(_ffn_kernel, k0=k0, nf=nf, tf=tf, nn=nn, tn=tn),
        grid=(nrows // tm, nf + nn),
        in_specs=[
            pl.BlockSpec((tm, d), lambda i, j: (i, 0)),
            pl.BlockSpec((1, N_MOD, d), seg),
            pl.BlockSpec((1, d), lambda i, j: (0, 0)),
            pl.BlockSpec((1, d), lambda i, j: (0, 0)),
            pl.BlockSpec((None, None, d, tf), lambda i, j: (*lk, 0, jnp.minimum(j, nf - 1))),
            pl.BlockSpec((None, None, d, tf), lambda i, j: (*lk, 0, jnp.minimum(j, nf - 1) + nf)),
            pl.BlockSpec((None, None, f, tn), lambda i, j: (*lk, 0, jnp.maximum(j - nf, 0))),
        ],
        out_specs=pl.BlockSpec((tm, d), lambda i, j: (i, 0)),
        out_shape=jax.ShapeDtypeStruct((nrows, d), F32),
        scratch_shapes=[pltpu.VMEM((tm, d), BF16), pltpu.VMEM((nf, tm, tf), BF16)],
        compiler_params=_params(("parallel", "arbitrary")),
        name="ffn_sublayer",
    )(h, mod_l, pre_g.reshape(1, d), post_g.reshape(1, d), w1, w1, w2)


def _inproj_kernel(h_ref, mod_ref, pre_ref, w_ref, o_ref, u_ref):
    @pl.when(pl.program_id(1) == 0)
    def _():
        _adaln_to(u_ref, h_ref, pre_ref[...], mod_ref[0, 3:4, :], mod_ref[0, 4:5, :])

    o_ref[...] = jnp.dot(u_ref[...], w_ref[...], preferred_element_type=F32).astype(BF16)


def _inproj_call(h, mod_l, pre_g, w, layer, tm, n_lat_seg):
    nrows, d = h.shape
    n = w.shape[2]
    tn = 1024
    seg = lambda i, j: (jnp.minimum((i * tm) // SEQ, n_lat_seg), 0, 0)
    return pl.pallas_call(
        _inproj_kernel,
        grid=(nrows // tm, n // tn),
        in_specs=[
            pl.BlockSpec((tm, d), lambda i, j: (i, 0)),
            pl.BlockSpec((1, N_MOD, d), seg),
            pl.BlockSpec((1, d), lambda i, j: (0, 0)),
            pl.BlockSpec((None, d, tn), lambda i, j: (layer, 0, j)),
        ],
        out_specs=pl.BlockSpec((tm, tn), lambda i, j: (i, j)),
        out_shape=jax.ShapeDtypeStruct((nrows, n), BF16),
        scratch_shapes=[pltpu.VMEM((tm, d), BF16)],
        compiler_params=_params(("parallel", "arbitrary")),
        name="mixer_in_proj",
    )(h, mod_l, pre_g.reshape(1, d), w)


def _rope(t, cos, sa, sb):
    return t * cos + pltpu.roll(t, LANES - 16, 1) * sa + pltpu.roll(t, 16, 1) * sb


def _rope_tables():
    pos = np.arange(SEQ)
    prow = jnp.asarray(pos // GRID_W, F32)
    pcol = jnp.asarray(pos % GRID_W, F32)
    half = 16
    inv = ROPE_BASE ** (-jnp.arange(half, dtype=F32) / half)
    lane = np.arange(LANES)
    freq = jnp.asarray(lane % half)
    use_col = jnp.asarray((lane % 64) >= 32)
    second = jnp.asarray((lane % 32) >= half)
    p = jnp.where(use_col[None, :], pcol[:, None], prow[:, None])
    ang = p * inv[freq][None, :]
    cos, sin = jnp.cos(ang), jnp.sin(ang)
    sa = jnp.where(second[None, :], 0.0, -sin)
    sb = jnp.where(second[None, :], sin, 0.0)
    return cos, sa, sb


def _mixer_specs(nb, width, cols, order):
    ctx_blk0 = nb * SEQ // CTX_LEN

    def spec(rows, col, w, ctx):
        def index(*g):
            b, h = order(*g)
            return ((ctx_blk0 + b) if ctx else b, col // w + h)
        return pl.BlockSpec((rows, w), index)

    return ([spec(SEQ, c, w, False) for c, w in zip(cols, width)]
            + [spec(CTX_LEN, c, w, True) for c, w in zip(cols, width)])


def _mixer_outs(nb, width, order, with_ctx):
    def index(*g):
        b, h = order(*g)
        return (b, h)
    specs = [pl.BlockSpec((SEQ, width), index)]
    shapes = [jax.ShapeDtypeStruct((nb * SEQ, BRANCH_W), BF16)]
    if with_ctx:
        specs.append(pl.BlockSpec((CTX_LEN, width), index))
        shapes.append(jax.ShapeDtypeStruct((nb * CTX_LEN, BRANCH_W), BF16))
    return specs, shapes


def _diff_kernel(ql_ref, kl_ref, vl_ref, qc_ref, kc_ref, vc_ref, cos_ref, sa_ref, sb_ref, lam_ref, g_ref,
                 *rest, lambda_init, with_ctx):
    ol_ref, oc_ref = (rest[0], rest[1]) if with_ctx else (rest[0], None)
    k_all, v_ones = rest[-2:]
    k_all[0:CTX_LEN, :] = kc_ref[...]
    k_all[CTX_LEN:, :] = _rope(kl_ref[...].astype(F32), cos_ref[...], sa_ref[...], sb_ref[...]).astype(BF16)
    v_ones[0:CTX_LEN, 0:LANES] = vc_ref[...]
    v_ones[CTX_LEN:, 0:LANES] = vl_ref[...]
    v_ones[:, LANES:] = jnp.ones((CTX_LEN + SEQ, LANES), BF16)

    lv = lam_ref[...]
    lam = (jnp.exp(jnp.sum(lv[0:1] * lv[1:2], axis=-1, keepdims=True))
           - jnp.exp(jnp.sum(lv[2:3] * lv[3:4], axis=-1, keepdims=True)) + lambda_init)
    lane = lax.broadcasted_iota(jnp.int32, (1, LANES), 1)

    def attend(q, nk):
        q = q * (A_HEAD_DIM ** -0.5)
        k = k_all[0:nk, :]
        v1 = v_ones[0:nk, :]

        rows = QBLK // DIFF_ROW_SPLIT
        qms = [jnp.where(mask, q[r * rows:(r + 1) * rows], 0.0).astype(BF16)
               for r in range(DIFF_ROW_SPLIT) for mask in (lane < A_HEAD_DIM, lane >= A_HEAD_DIM)]
        ss = [lax.dot_general(qm, k, NT_DIMS, preferred_element_type=F32) for qm in qms]
        es = [jnp.exp(s - jnp.max(s, axis=-1, keepdims=True)).astype(BF16) for s in ss]
        ols = [jnp.dot(e, v1, preferred_element_type=F32) for e in es]
        os = [ol[:, :LANES] / ol[:, LANES:] for ol in ols]
        o = jnp.concatenate([os[2 * r] - lam * os[2 * r + 1] for r in range(DIFF_ROW_SPLIT)], axis=0)
        return (_rms(o) * g_ref[...] * (1.0 - lambda_init)).astype(BF16)

    def block(t, carry):
        rows = pl.ds(pl.multiple_of(t * QBLK, QBLK), QBLK)
        q = _rope(ql_ref[rows, :].astype(F32), cos_ref[rows, :], sa_ref[rows, :], sb_ref[rows, :])
        ol_ref[rows, :] = attend(q, CTX_LEN + SEQ)
        return carry

    lax.fori_loop(0, SEQ // QBLK, block, 0, unroll=True)
    if with_ctx:
        oc_ref[...] = attend(qc_ref[...].astype(F32), CTX_LEN)


def _diff_call(proj, tabs, diff_lambda, diff_norm, lambda_init, nb, with_ctx):
    order = lambda b, h: (b, h)
    whole = lambda b, h: (0, 0)
    out_specs, out_shapes = _mixer_outs(nb, LANES, order, with_ctx)
    return pl.pallas_call(
        functools.partial(_diff_kernel, lambda_init=lambda_init, with_ctx=with_ctx),
        grid=(nb, A_HEADS),
        in_specs=_mixer_specs(nb, (LANES,) * 3, (COL_AQ, COL_AK, COL_AV), order) + [
            pl.BlockSpec((SEQ, LANES), whole),
            pl.BlockSpec((SEQ, LANES), whole),
            pl.BlockSpec((SEQ, LANES), whole),
            pl.BlockSpec((4, A_HEAD_DIM), whole),
            pl.BlockSpec((1, LANES), whole),
        ],
        out_specs=out_specs,
        out_shape=out_shapes,
        scratch_shapes=[pltpu.VMEM((CTX_LEN + SEQ, LANES), BF16), pltpu.VMEM((CTX_LEN + SEQ, 2 * LANES), BF16)],
        compiler_params=_params(("parallel", "parallel")),
        name="diff_attention",
    )(*([proj] * 6), *tabs, diff_lambda, diff_norm.reshape(1, LANES))


def _nbr_block_rows(cfg, qi, kj):
    n_rows = SEQ // GRID_W
    r0 = (0, ROWS_PER_QBLK, n_rows - ROWS_PER_QBLK)[cfg]
    ks = min(max(r0 - WIN_R // 2, 0), n_rows - NBR_KROWS)
    r = r0 + qi
    rs = min(max(r - WIN_R // 2, 0), n_rows - WIN_R)
    kr = ks + kj
    return rs <= kr < rs + WIN_R, kr - r + WIN_R - 1


def _nbr_bias_kernel(rpb_ref, o_ref):
    n_dr, n_dc = 2 * WIN_R - 1, 2 * WIN_C - 1
    base = pl.program_id(0) * (n_dr * n_dc)
    qc = lax.broadcasted_iota(jnp.int32, (GRID_W, LANES), 0)
    lane = lax.broadcasted_iota(jnp.int32, (GRID_W, LANES), 1)
    kc = lane % GRID_W
    dc = kc - qc + (WIN_C - 1)
    cs = jnp.clip(qc - WIN_C // 2, 0, GRID_W - WIN_C)
    col_ok = (kc >= cs) & (kc < cs + WIN_C)
    masked = jnp.full((GRID_W, LANES), NBR_MASKED, F32)

    def row_pattern(dr):
        acc = masked
        for x in range(n_dc):
            acc = jnp.where(dc == x, rpb_ref[base + dr * n_dc + x], acc)
        return jnp.where(col_ok, acc, masked)

    pats = [row_pattern(dr) for dr in range(n_dr)]

    def pattern(cfg, qi, kj):
        if kj >= NBR_KROWS:
            return masked
        ok, dr = _nbr_block_rows(cfg, qi, kj)
        return pats[dr] if ok else masked

    for cfg in range(3):
        for qi in range(ROWS_PER_QBLK):
            rows = slice(qi * GRID_W, (qi + 1) * GRID_W)
            for m in range(pl.cdiv(NBR_KEYS, LANES)):
                tile = jnp.where(lane < GRID_W, pattern(cfg, qi, 2 * m), pattern(cfg, qi, 2 * m + 1))
                width = min(LANES, NBR_KEYS - m * LANES)
                o_ref[0, cfg, rows, m * LANES:m * LANES + width] = tile[:, :width]


def _nbr_bias_call(rpb):
    heads = rpb.shape[0]
    return pl.pallas_call(
        _nbr_bias_kernel,
        grid=(heads,),
        in_specs=[pl.BlockSpec(memory_space=pltpu.SMEM)],
        out_specs=pl.BlockSpec((1, 3, QBLK, NBR_KEYS), lambda h: (h, 0, 0, 0)),
        out_shape=jax.ShapeDtypeStruct((heads, 3, QBLK, NBR_KEYS), F32),
        compiler_params=_params(("parallel",)),
        name="nbr_bias_table",
    )(rpb.astype(F32).reshape(-1))


def _nbr_kernel(ql_ref, kl_ref, vl_ref, qc_ref, kc_ref, vc_ref, bias_ref, *rest, with_ctx):
    ol_ref, oc_ref = (rest[0], rest[1]) if with_ctx else (rest[0], None)
    v1l, v1c = rest[-2:]
    scale = B_HEAD_DIM ** -0.5
    v1l[:, 0:LANES] = vl_ref[...]
    v1l[:, LANES:] = jnp.ones((SEQ, LANES), BF16)
    v1c[:, 0:LANES] = vc_ref[...]
    v1c[:, LANES:] = jnp.ones((CTX_LEN, LANES), BF16)
    kc = kc_ref[...]
    lat_blocks = SEQ // QBLK

    for blk in range(lat_blocks):
        cfg = 0 if blk == 0 else (2 if blk == lat_blocks - 1 else 1)
        start = min(max((blk * ROWS_PER_QBLK - WIN_R // 2) * GRID_W, 0), SEQ - NBR_KEYS)
        q = ql_ref[blk * QBLK:(blk + 1) * QBLK, :]
        s_lat = (lax.dot_general(q, kl_ref[start:start + NBR_KEYS, :], NT_DIMS, preferred_element_type=F32) * scale
                 + bias_ref[0, cfg])
        s_ctx = lax.dot_general(q, kc, NT_DIMS, preferred_element_type=F32) * scale
        m = jnp.maximum(jnp.max(s_lat, axis=-1, keepdims=True), jnp.max(s_ctx, axis=-1, keepdims=True))
        ol = (jnp.dot(jnp.exp(s_lat - m).astype(BF16), v1l[start:start + NBR_KEYS, :], preferred_element_type=F32)
              + jnp.dot(jnp.exp(s_ctx - m).astype(BF16), v1c[...], preferred_element_type=F32))
        ol_ref[blk * QBLK:(blk + 1) * QBLK, :] = (ol[:, :LANES] / ol[:, LANES:]).astype(BF16)

    if with_ctx:
        s = lax.dot_general(qc_ref[...], kc, NT_DIMS, preferred_element_type=F32) * scale
        e = jnp.exp(s - jnp.max(s, axis=-1, keepdims=True)).astype(BF16)
        ol = jnp.dot(e, v1c[...], preferred_element_type=F32)
        oc_ref[...] = (ol[:, :LANES] / ol[:, LANES:]).astype(BF16)


def _nbr_call(proj, bias_tab, nb, with_ctx):
    order = lambda h, b: (b, h)
    out_specs, out_shapes = _mixer_outs(nb, LANES, order, with_ctx)
    return pl.pallas_call(
        functools.partial(_nbr_kernel, with_ctx=with_ctx),
        grid=(B_HEADS, nb),
        in_specs=_mixer_specs(nb, (LANES,) * 3, (COL_BQ, COL_BK, COL_BV), order) + [
            pl.BlockSpec((1, 3, QBLK, NBR_KEYS), lambda h, b: (h, 0, 0, 0)),
        ],
        out_specs=out_specs,
        out_shape=out_shapes,
        scratch_shapes=[pltpu.VMEM((SEQ, 2 * LANES), BF16), pltpu.VMEM((CTX_LEN, 2 * LANES), BF16)],
        compiler_params=_params(("parallel", "parallel")),
        name="nbr_attention",
    )(*([proj] * 6), bias_tab)


def _ret_kernel(ql_ref, kl_ref, vl_ref, gl_ref, qc_ref, kc_ref, vc_ref, gc_ref, cos_ref, sa_ref, sb_ref,
                dec_ref, rn_ref, *rest, with_ctx):
    ol_ref, oc_ref = (rest[0], rest[1]) if with_ctx else (rest[0], None)
    of_ref, ob_ref, st_ref = rest[-3:]
    ch = QBLK
    n_chunks = SEQ // ch
    kscale = C_KEY_DIM ** -0.5

    def run():
        ii = lax.broadcasted_iota(jnp.int32, (ch, ch), 0)
        jj = lax.broadcasted_iota(jnp.int32, (ch, ch), 1)
        dist = (ii - jj).astype(F32)
        pos = lax.broadcasted_iota(jnp.int32, (ch, 1), 0).astype(F32)
        lane = lax.broadcasted_iota(jnp.int32, (1, LANES), 1)
        rn = rn_ref[...]

        def log_sigmoid(x):
            return -(jnp.log1p(jnp.exp(-jnp.abs(x))) + jnp.maximum(-x, 0.0))

        heads = []
        for e in range(2):
            lgf = log_sigmoid(dec_ref[0, 0, e:e + 1, 0:1])
            lgb = log_sigmoid(dec_ref[0, 1, e:e + 1, 0:1])
            heads.append(dict(
                mask=(lane >= e * C_KEY_DIM) & (lane < (e + 1) * C_KEY_DIM),
                intra_f=jnp.where(dist >= 0, jnp.exp(lgf * jnp.maximum(dist, 0.0)), 0.0),
                intra_b=jnp.where(dist <= 0, jnp.exp(lgb * jnp.maximum(-dist, 0.0)), 0.0),
                qdec_f=jnp.exp(lgf * (pos + 1.0)), kdec_f=jnp.exp(lgf * (ch - 1.0 - pos)),
                cdec_f=jnp.exp(lgf * ch),
                qdec_b=jnp.exp(lgb * (ch - pos)), kdec_b=jnp.exp(lgb * pos),
                cdec_b=jnp.exp(lgb * ch),
                vs=slice(e * LANES, (e + 1) * LANES),
            ))

        def finish(o, g):
            return (_silu(g.astype(F32)) * (_rms(o) * rn)).astype(BF16)

        qc = qc_ref[...].astype(F32)
        kc = kc_ref[...].astype(F32) * kscale
        for e, hd in enumerate(heads):
            v = vc_ref[:, hd["vs"]]
            st_ref[e] = lax.dot_general((kc * hd["kdec_f"]).astype(BF16), v, TN_DIMS, preferred_element_type=F32)
            st_ref[2 + e] = lax.dot_general((kc * hd["kdec_b"]).astype(BF16), v, TN_DIMS,
                                            preferred_element_type=F32)
            if with_ctx:
                qe = jnp.where(hd["mask"], qc, 0.0).astype(BF16)
                att = lax.dot_general(qe, kc.astype(BF16), NT_DIMS, preferred_element_type=F32)
                att = att * (hd["intra_f"] + hd["intra_b"])
                o = jnp.dot(att.astype(BF16), v, preferred_element_type=F32)
                oc_ref[:, hd["vs"]] = finish(o, gc_ref[:, hd["vs"]])

        def chunk_rows(c):
            return pl.ds(pl.multiple_of(c * ch, ch), ch)

        def sweep(c, fwd):
            rows = chunk_rows(c)
            tabs = (cos_ref[rows, :], sa_ref[rows, :], sb_ref[rows, :])
            q = _rope(ql_ref[rows, :].astype(F32), *tabs)
            k = _rope(kl_ref[rows, :].astype(F32), *tabs) * kscale
            kb = k.astype(BF16)
            for e, hd in enumerate(heads):
                d = "f" if fwd else "b"
                si = e if fwd else 2 + e
                v = vl_ref[rows, hd["vs"]]
                qe = jnp.where(hd["mask"], q, 0.0)
                att = lax.dot_general(qe.astype(BF16), kb, NT_DIMS, preferred_element_type=F32) * hd["intra_" + d]
                st = st_ref[si]
                o = (jnp.dot(att.astype(BF16), v, preferred_element_type=F32)
                     + jnp.dot((qe * hd["qdec_" + d]).astype(BF16), st.astype(BF16), preferred_element_type=F32))
                st_ref[si] = st * hd["cdec_" + d] + lax.dot_general(
                    (k * hd["kdec_" + d]).astype(BF16), v, TN_DIMS, preferred_element_type=F32)
                (of_ref if fwd else ob_ref)[rows, hd["vs"]] = o

        def finish_chunk(c):
            rows = chunk_rows(c)
            for hd in heads:
                vs = hd["vs"]
                ol_ref[rows, vs] = finish(of_ref[rows, vs] + ob_ref[rows, vs], gl_ref[rows, vs])

        def first_half(t, carry):
            sweep(t, True)
            sweep(n_chunks - 1 - t, False)
            return carry

        def second_half(t, carry):
            sweep(t, True)
            sweep(n_chunks - 1 - t, False)
            finish_chunk(t)
            finish_chunk(n_chunks - 1 - t)
            return carry

        lax.fori_loop(0, n_chunks // 2, first_half, 0)
        lax.fori_loop(n_chunks // 2, n_chunks, second_half, 0)

    run()


def _ret_call(proj, tabs, ret_decay, ret_norm, nb, with_ctx):
    pairs = C_HEADS // 2
    w2 = 2 * LANES
    order = lambda b, p: (b, p)
    whole = lambda b, p: (0, 0)
    out_specs, out_shapes = _mixer_outs(nb, w2, order, with_ctx)
    dec = jnp.broadcast_to(ret_decay.astype(F32).reshape(2, pairs, 2).transpose(1, 0, 2)[..., None],
                           (pairs, 2, 2, LANES))
    return pl.pallas_call(
        functools.partial(_ret_kernel, with_ctx=with_ctx),
        grid=(nb, pairs),
        in_specs=_mixer_specs(nb, (LANES, LANES, w2, w2), (COL_CQ, COL_CK, COL_CV, COL_CG), order) + [
            pl.BlockSpec((SEQ, LANES), whole),
            pl.BlockSpec((SEQ, LANES), whole),
            pl.BlockSpec((SEQ, LANES), whole),
            pl.BlockSpec((1, 2, 2, LANES), lambda b, p: (p, 0, 0, 0)),
            pl.BlockSpec((1, LANES), whole),
        ],
        out_specs=out_specs,
        out_shape=out_shapes,
        scratch_shapes=[
            pltpu.VMEM((SEQ, w2), F32),
            pltpu.VMEM((SEQ, w2), F32),
            pltpu.VMEM((4, LANES, LANES), F32),
        ],
        compiler_params=_params(("parallel", "parallel")),
        name="retention",
    )(*([proj] * 8), *tabs, dec, ret_norm.reshape(1, LANES))


def _merge_kernel(h_ref, mod_ref, post_ref, g_lo_ref, g_hi_ref, wb_ref, wo_ref, *rest, n_lat_tiles, with_ctx):
    d = h_ref.shape[1]
    o_ref = rest[-1]
    if with_ctx:
        is_lat = pl.program_id(0) < n_lat_tiles
        xs = [jnp.where(is_lat, rest[2 * k][...], rest[2 * k + 1][...]) for k in range(3)]
    else:
        xs = [rest[k][...] for k in range(3)]
    g_lo = g_lo_ref[...]
    g_hi = g_hi_ref[...]
    gates = (g_lo[:, :d], jnp.concatenate([g_lo[:, d:], g_hi[:, :d // 2]], axis=1), g_hi[:, d // 2:])
    m = None
    for k in range(3):
        t = jax.nn.sigmoid(gates[k].astype(F32)) * jnp.dot(xs[k], wb_ref[k], preferred_element_type=F32)
        m = t if m is None else m + t
    y = jnp.dot(m.astype(BF16), wo_ref[...], preferred_element_type=F32)
    o_ref[...] = h_ref[...] + mod_ref[0, 5:6, :] * (_rms(y) * post_ref[...])


def _merge_call(h, mod_l, post_g, branches, proj, wb, wo, layer, nrows, n_lat_seg):
    d = h.shape[1]
    tm = CTX_LEN
    with_ctx = len(branches[0]) == 2
    n_lat_tiles = n_lat_seg * SEQ // tm
    gw = 3 * d // 2
    g0 = COL_GATES // gw
    resident = dict(pipeline_mode=pl.Buffered(1))
    row = lambda i: (i, 0)
    lat_row = lambda i: (jnp.minimum(i, n_lat_tiles - 1), 0)
    ctx_row = lambda i: (jnp.maximum(i - n_lat_tiles, 0), 0)
    branch_specs, branch_args = [], []
    for outs in branches:
        branch_specs.append(pl.BlockSpec((tm, BRANCH_W), lat_row))
        if with_ctx:
            branch_specs.append(pl.BlockSpec((tm, BRANCH_W), ctx_row))
        branch_args.extend(outs)
    return pl.pallas_call(
        functools.partial(_merge_kernel, n_lat_tiles=n_lat_tiles, with_ctx=with_ctx),
        grid=(nrows // tm,),
        in_specs=[
            pl.BlockSpec((tm, d), row),
            pl.BlockSpec((1, N_MOD, d), lambda i: (jnp.minimum((i * tm) // SEQ, n_lat_seg), 0, 0)),
            pl.BlockSpec((1, d), lambda i: (0, 0)),
            pl.BlockSpec((tm, gw), lambda i: (i, g0)),
            pl.BlockSpec((tm, gw), lambda i: (i, g0 + 1)),
            pl.BlockSpec((None, 3, BRANCH_W, d), lambda i: (layer, 0, 0, 0), **resident),
            pl.BlockSpec((None, d, d), lambda i: (layer, 0, 0), **resident),
        ] + branch_specs,
        out_specs=pl.BlockSpec((tm, d), row),
        out_shape=jax.ShapeDtypeStruct((nrows, d), F32),
        compiler_params=_params(("parallel",)),
        name="mixer_merge",
    )(h, mod_l, post_g.reshape(1, d), proj, proj, wb, wo, *branch_args)


def kernel(x, c, ctx, c_ctx, w_mod, b_mod, pre_norm, post_norm, ffn_w_in, ffn_w_out, w_in, diff_lambda,
           diff_norm, na_rpb, ret_decay, ret_norm, w_branch, w_out):
    nb, seq, d = x.shape
    assert (seq, d) == (SEQ, D_MODEL) and ctx.shape == (nb, CTX_LEN, d)
    n_lat = nb * SEQ
    n_all = n_lat + nb * CTX_LEN
    tm = min(512, nb * CTX_LEN)

    h = jnp.concatenate([x.reshape(n_lat, d), ctx.reshape(nb * CTX_LEN, d)], axis=0)
    pad = (-(nb + 1)) % 8
    cc = jnp.concatenate([c, c_ctx[None, :], jnp.zeros((pad, d), c.dtype)], axis=0)
    mods = _mod_call(cc, w_mod, b_mod)
    tabs = _rope_tables()
    w1, w2 = ffn_w_in.astype(BF16), ffn_w_out.astype(BF16)
    w_proj, wb, wo = w_in.astype(BF16), w_branch.astype(BF16), w_out.astype(BF16)

    for l in range(DEPTH):
        with_ctx = l < DEPTH - 1
        lambda_init = 0.8 - 0.6 * math.exp(-0.3 * l)
        nrows_out = n_all if with_ctx else n_lat
        mod_l = mods[l, :nb + 1].reshape(nb + 1, N_MOD, d)
        ffn = functools.partial(_ffn_call, tm=tm, n_lat_seg=nb)

        h = ffn(h, mod_l, 0, pre_norm[l, 0], post_norm[l, 0], w1, w2, (l, 0), n_all)
        proj = _inproj_call(h, mod_l, pre_norm[l, 1], w_proj, l, min(1024, nb * CTX_LEN), nb)
        oa = _diff_call(proj, tabs, diff_lambda[l], diff_norm[l], lambda_init, nb, with_ctx)
        ob = _nbr_call(proj, _nbr_bias_call(na_rpb[l]), nb, with_ctx)
        yr = _ret_call(proj, tabs, ret_decay[l], ret_norm[l], nb, with_ctx)
        h = _merge_call(h, mod_l, post_norm[l, 1], (oa, ob, yr), proj, wb, wo, l, nrows_out, nb)
        h = ffn(h, mod_l, 6, pre_norm[l, 2], post_norm[l, 2], w1, w2, (l, 1), nrows_out)
    return h.reshape(nb, SEQ, d)
```

```python
import functools
import math

import numpy as np
import jax
import jax.numpy as jnp
from jax import lax
from jax.experimental import pallas as pl
from jax.experimental.pallas import tpu as pltpu

D_MODEL = 2048
SEQ = 2048
DEPTH = 2
GRID_W = 64
CTX_LEN = 256
N_MOD = 9
FFN_DIM = 5632
A_HEADS = 8
A_HEAD_DIM = 64
B_HEADS = 8
B_HEAD_DIM = 128
WIN_R = 8
WIN_C = 16
C_HEADS = 8
C_KEY_DIM = 64
BRANCH_W = 1024
ROPE_BASE = 10000.0
EPS = 1e-6

COL_AQ, COL_AK, COL_AV = 0, 1024, 2048
COL_BQ, COL_BK, COL_BV = 3072, 4096, 5120
COL_CQ, COL_CK, COL_CV, COL_CG = 6144, 6656, 7168, 8192
COL_GATES = 9216
IN_COLS = 15360

LANES = 128
BF16_SUBLANES = 16
QBLK = 256
ROWS_PER_QBLK = QBLK // GRID_W
NBR_KROWS = ROWS_PER_QBLK + WIN_R - 1
NBR_KEYS = NBR_KROWS * GRID_W
VMEM_LIMIT = 56 * 1024 * 1024
FFN_TILE = 512
PROJ_TILE = 1024
DIFF_ROW_SPLIT = 2
NBR_MASKED = -1e30

BF16 = jnp.bfloat16
F32 = jnp.float32
NT_DIMS = (((1,), (1,)), ((), ()))
TN_DIMS = (((0,), (0,)), ((), ()))


def _params(sem):
    return pltpu.CompilerParams(dimension_semantics=sem, vmem_limit_bytes=VMEM_LIMIT)


def _rms(x):
    return x * lax.rsqrt(jnp.mean(x * x, axis=-1, keepdims=True) + EPS)


def _silu(x):
    return x * jax.nn.sigmoid(x)


def _col_tiles(w, width):
    *lead, k, n = w.shape
    return jnp.moveaxis(w.reshape(*lead, k, n // width, width), -2, -3)


def _row_chunks(n_rows, rows_per, body):
    def step(r, carry):
        body(pl.ds(pl.multiple_of(r * rows_per, rows_per), rows_per))
        return carry
    lax.fori_loop(0, n_rows // rows_per, step, 0, unroll=4)


def _adaln_to(u_ref, h_ref, pre, shift, scale):
    gain = pre * (1.0 + scale)

    def body(rows):
        u_ref[rows, :] = (_rms(h_ref[rows, :]) * gain + shift).astype(u_ref.dtype)

    _row_chunks(h_ref.shape[0], BF16_SUBLANES, body)


def _mod_kernel(cc_ref, w_ref, b_ref, o_ref):
    s = _silu(cc_ref[...]).astype(BF16)
    o_ref[0] = jnp.dot(s, w_ref[0].astype(BF16), preferred_element_type=F32) + b_ref[0]


def _mod_call(cc, w_mod, b_mod):
    depth, d, n = w_mod.shape
    rows = cc.shape[0]
    tn = 1024
    return pl.pallas_call(
        _mod_kernel,
        grid=(depth, n // tn),
        in_specs=[
            pl.BlockSpec((rows, d), lambda l, j: (0, 0)),
            pl.BlockSpec((1, d, tn), lambda l, j: (l, 0, j)),
            pl.BlockSpec((1, 1, tn), lambda l, j: (l, 0, j)),
        ],
        out_specs=pl.BlockSpec((1, rows, tn), lambda l, j: (l, 0, j)),
        out_shape=jax.ShapeDtypeStruct((depth, rows, n), F32),
        compiler_params=_params(("parallel", "parallel")),
        name="mod_vectors",
    )(cc, w_mod, b_mod.reshape(depth, 1, n))


def _ffn_kernel(h_ref, mod_ref, pre_ref, post_ref, w1a_ref, w1b_ref, w2_ref, o_ref, u_ref, hm_ref, *,
                k0, nf, tf, nn, tn):
    j = pl.program_id(1)

    @pl.when(j == 0)
    def _():
        _adaln_to(u_ref, h_ref, pre_ref[...], mod_ref[0, k0:k0 + 1, :], mod_ref[0, k0 + 1:k0 + 2, :])

    @pl.when(j < nf)
    def _():
        u = u_ref[...]
        a = jnp.dot(u, w1a_ref[...], preferred_element_type=F32)
        b = jnp.dot(u, w1b_ref[...], preferred_element_type=F32)
        hm_ref[j] = (_silu(a) * b).astype(BF16)

    for n in range(nn):
        @pl.when(j == nf + n)
        def _():
            y = jnp.dot(hm_ref[0], w2_ref[0:tf, :], preferred_element_type=F32)
            for k in range(1, nf):
                y += jnp.dot(hm_ref[k], w2_ref[k * tf:(k + 1) * tf, :], preferred_element_type=F32)
            o_ref[:, n * tn:(n + 1) * tn] = y

    @pl.when(j == nf + nn - 1)
    def _():
        gate = mod_ref[0, k0 + 2:k0 + 3, :]
        o_ref[...] = h_ref[...] + 0.5 * gate * (_rms(o_ref[...]) * post_ref[...])


def _ffn_call(h, mod_l, k0, pre_g, post_g, w1, w2, lk, nrows, tm, n_lat_seg):
    d = h.shape[1]
    f, tf, tn = w2.shape[3], w1.shape[4], w2.shape[4]
    nf, nn = f // tf, d // tn
    w2_tile = lambda j: jnp.where(j < nf, nn - 1, j - nf)
    seg = lambda i, j: (jnp.minimum((i * tm) // SEQ, n_lat_seg), 0, 0)
    return pl.pallas_call(
        functools.partial(_ffn_kernel, k0=k0, nf=nf, tf=tf, nn=nn, tn=tn),
        grid=(nrows // tm, nf + nn),
        in_specs=[
            pl.BlockSpec((tm, d), lambda i, j: (i, 0)),
            pl.BlockSpec((1, N_MOD, d), seg),
            pl.BlockSpec((1, d), lambda i, j: (0, 0)),
            pl.BlockSpec((1, d), lambda i, j: (0, 0)),
            pl.BlockSpec((None, None, None, d, tf), lambda i, j: (*lk, jnp.minimum(j, nf - 1), 0, 0)),
            pl.BlockSpec((None, None, None, d, tf), lambda i, j: (*lk, jnp.minimum(j, nf - 1) + nf, 0, 0)),
            pl.BlockSpec((None, None, None, f, tn), lambda i, j: (*lk, w2_tile(j), 0, 0)),
        ],
        out_specs=pl.BlockSpec((tm, d), lambda i, j: (i, 0)),
        out_shape=jax.ShapeDtypeStruct((nrows, d), F32),
        scratch_shapes=[pltpu.VMEM((tm, d), BF16), pltpu.VMEM((nf, tm, tf), BF16)],
        compiler_params=_params(("parallel", "arbitrary")),
        name="ffn_sublayer",
    )(h, mod_l, pre_g.reshape(1, d), post_g.reshape(1, d), w1, w1, w2)


def _inproj_kernel(h_ref, mod_ref, pre_ref, w_ref, o_ref, u_ref):
    @pl.when(pl.program_id(1) == 0)
    def _():
        _adaln_to(u_ref, h_ref, pre_ref[...], mod_ref[0, 3:4, :], mod_ref[0, 4:5, :])

    o_ref[...] = jnp.dot(u_ref[...], w_ref[...], preferred_element_type=F32).astype(BF16)


def _inproj_call(h, mod_l, pre_g, w, layer, tm, n_lat_seg):
    nrows, d = h.shape
    tn = w.shape[3]
    n = w.shape[1] * tn
    seg = lambda i, j: (jnp.minimum((i * tm) // SEQ, n_lat_seg), 0, 0)
    return pl.pallas_call(
        _inproj_kernel,
        grid=(nrows // tm, n // tn),
        in_specs=[
            pl.BlockSpec((tm, d), lambda i, j: (i, 0)),
            pl.BlockSpec((1, N_MOD, d), seg),
            pl.BlockSpec((1, d), lambda i, j: (0, 0)),
            pl.BlockSpec((None, None, d, tn), lambda i, j: (layer, j, 0, 0)),
        ],
        out_specs=pl.BlockSpec((tm, tn), lambda i, j: (i, j)),
        out_shape=jax.ShapeDtypeStruct((nrows, n), BF16),
        scratch_shapes=[pltpu.VMEM((tm, d), BF16)],
        compiler_params=_params(("parallel", "arbitrary")),
        name="mixer_in_proj",
    )(h, mod_l, pre_g.reshape(1, d), w)


def _rope(t, cos, sa, sb):
    return t * cos + pltpu.roll(t, LANES - 16, 1) * sa + pltpu.roll(t, 16, 1) * sb


def _rope_tables():
    pos = np.arange(SEQ)
    prow = jnp.asarray(pos // GRID_W, F32)
    pcol = jnp.asarray(pos % GRID_W, F32)
    half = 16
    inv = ROPE_BASE ** (-jnp.arange(half, dtype=F32) / half)
    lane = np.arange(LANES)
    freq = jnp.asarray(lane % half)
    use_col = jnp.asarray((lane % 64) >= 32)
    second = jnp.asarray((lane % 32) >= half)
    p = jnp.where(use_col[None, :], pcol[:, None], prow[:, None])
    ang = p * inv[freq][None, :]
    cos, sin = jnp.cos(ang), jnp.sin(ang)
    sa = jnp.where(second[None, :], 0.0, -sin)
    sb = jnp.where(second[None, :], sin, 0.0)
    return cos, sa, sb


def _mixer_specs(nb, width, cols, order):
    ctx_blk0 = nb * SEQ // CTX_LEN

    def spec(rows, col, w, ctx):
        def index(*g):
            b, h = order(*g)
            return ((ctx_blk0 + b) if ctx else b, col // w + h)
        return pl.BlockSpec((rows, w), index)

    return ([spec(SEQ, c, w, False) for c, w in zip(cols, width)]
            + [spec(CTX_LEN, c, w, True) for c, w in zip(cols, width)])


def _mixer_outs(nb, width, order, with_ctx):
    def index(*g):
        b, h = order(*g)
        return (b, h)
    specs = [pl.BlockSpec((SEQ, width), index)]
    shapes = [jax.ShapeDtypeStruct((nb * SEQ, BRANCH_W), BF16)]
    if with_ctx:
        specs.append(pl.BlockSpec((CTX_LEN, width), index))
        shapes.append(jax.ShapeDtypeStruct((nb * CTX_LEN, BRANCH_W), BF16))
    return specs, shapes


def _diff_kernel(ql_ref, kl_ref, vl_ref, qc_ref, kc_ref, vc_ref, cos_ref, sa_ref, sb_ref, lam_ref, g_ref,
                 *rest, lambda_init, with_ctx):
    ol_ref, oc_ref = (rest[0], rest[1]) if with_ctx else (rest[0], None)
    k_all, v_ones = rest[-2:]
    k_all[0:CTX_LEN, :] = kc_ref[...]
    k_all[CTX_LEN:, :] = _rope(kl_ref[...].astype(F32), cos_ref[...], sa_ref[...], sb_ref[...]).astype(BF16)
    v_ones[0:CTX_LEN, 0:LANES] = vc_ref[...]
    v_ones[CTX_LEN:, 0:LANES] = vl_ref[...]
    v_ones[:, LANES:] = jnp.ones((CTX_LEN + SEQ, LANES), BF16)

    lv = lam_ref[...]
    lam = (jnp.exp(jnp.sum(lv[0:1] * lv[1:2], axis=-1, keepdims=True))
           - jnp.exp(jnp.sum(lv[2:3] * lv[3:4], axis=-1, keepdims=True)) + lambda_init)
    lane = lax.broadcasted_iota(jnp.int32, (1, LANES), 1)

    def attend(q, nk):
        q = q * (A_HEAD_DIM ** -0.5)
        k = k_all[0:nk, :]
        v1 = v_ones[0:nk, :]

        rows = QBLK // DIFF_ROW_SPLIT
        qms = [jnp.where(mask, q[r * rows:(r + 1) * rows], 0.0).astype(BF16)
               for r in range(DIFF_ROW_SPLIT) for mask in (lane < A_HEAD_DIM, lane >= A_HEAD_DIM)]
        ss = [lax.dot_general(qm, k, NT_DIMS, preferred_element_type=F32) for qm in qms]
        es = [jnp.exp(s - jnp.max(s, axis=-1, keepdims=True)).astype(BF16) for s in ss]
        ols = [jnp.dot(e, v1, preferred_element_type=F32) for e in es]
        os = [ol[:, :LANES] / ol[:, LANES:] for ol in ols]
        o = jnp.concatenate([os[2 * r] - lam * os[2 * r + 1] for r in range(DIFF_ROW_SPLIT)], axis=0)
        return (_rms(o) * g_ref[...] * (1.0 - lambda_init)).astype(BF16)

    def block(t, carry):
        rows = pl.ds(pl.multiple_of(t * QBLK, QBLK), QBLK)
        q = _rope(ql_ref[rows, :].astype(F32), cos_ref[rows, :], sa_ref[rows, :], sb_ref[rows, :])
        ol_ref[rows, :] = attend(q, CTX_LEN + SEQ)
        return carry

    lax.fori_loop(0, SEQ // QBLK, block, 0, unroll=True)
    if with_ctx:
        oc_ref[...] = attend(qc_ref[...].astype(F32), CTX_LEN)


def _diff_call(proj, tabs, diff_lambda, diff_norm, lambda_init, nb, with_ctx):
    order = lambda b, h: (b, h)
    whole = lambda b, h: (0, 0)
    out_specs, out_shapes = _mixer_outs(nb, LANES, order, with_ctx)
    return pl.pallas_call(
        functools.partial(_diff_kernel, lambda_init=lambda_init, with_ctx=with_ctx),
        grid=(nb, A_HEADS),
        in_specs=_mixer_specs(nb, (LANES,) * 3, (COL_AQ, COL_AK, COL_AV), order) + [
            pl.BlockSpec((SEQ, LANES), whole),
            pl.BlockSpec((SEQ, LANES), whole),
            pl.BlockSpec((SEQ, LANES), whole),
            pl.BlockSpec((4, A_HEAD_DIM), whole),
            pl.BlockSpec((1, LANES), whole),
        ],
        out_specs=out_specs,
        out_shape=out_shapes,
        scratch_shapes=[pltpu.VMEM((CTX_LEN + SEQ, LANES), BF16), pltpu.VMEM((CTX_LEN + SEQ, 2 * LANES), BF16)],
        compiler_params=_params(("parallel", "parallel")),
        name="diff_attention",
    )(*([proj] * 6), *tabs, diff_lambda, diff_norm.reshape(1, LANES))


def _nbr_block_rows(cfg, qi, kj):
    n_rows = SEQ // GRID_W
    r0 = (0, ROWS_PER_QBLK, n_rows - ROWS_PER_QBLK)[cfg]
    ks = min(max(r0 - WIN_R // 2, 0), n_rows - NBR_KROWS)
    r = r0 + qi
    rs = min(max(r - WIN_R // 2, 0), n_rows - WIN_R)
    kr = ks + kj
    return rs <= kr < rs + WIN_R, kr - r + WIN_R - 1


def _nbr_bias_kernel(rpb_ref, o_ref):
    n_dr, n_dc = 2 * WIN_R - 1, 2 * WIN_C - 1
    base = pl.program_id(0) * (n_dr * n_dc)
    qc = lax.broadcasted_iota(jnp.int32, (GRID_W, LANES), 0)
    lane = lax.broadcasted_iota(jnp.int32, (GRID_W, LANES), 1)
    kc = lane % GRID_W
    dc = kc - qc + (WIN_C - 1)
    cs = jnp.clip(qc - WIN_C // 2, 0, GRID_W - WIN_C)
    col_ok = (kc >= cs) & (kc < cs + WIN_C)
    masked = jnp.full((GRID_W, LANES), NBR_MASKED, F32)

    def row_pattern(dr):
        acc = masked
        for x in range(n_dc):
            acc = jnp.where(dc == x, rpb_ref[base + dr * n_dc + x], acc)
        return jnp.where(col_ok, acc, masked)

    pats = [row_pattern(dr) for dr in range(n_dr)]

    def pattern(cfg, qi, kj):
        if kj >= NBR_KROWS:
            return masked
        ok, dr = _nbr_block_rows(cfg, qi, kj)
        return pats[dr] if ok else masked

    for cfg in range(3):
        for qi in range(ROWS_PER_QBLK):
            rows = slice(qi * GRID_W, (qi + 1) * GRID_W)
            for m in range(pl.cdiv(NBR_KEYS, LANES)):
                tile = jnp.where(lane < GRID_W, pattern(cfg, qi, 2 * m), pattern(cfg, qi, 2 * m + 1))
                width = min(LANES, NBR_KEYS - m * LANES)
                o_ref[0, cfg, rows, m * LANES:m * LANES + width] = tile[:, :width]


def _nbr_bias_call(rpb):
    heads = rpb.shape[0]
    return pl.pallas_call(
        _nbr_bias_kernel,
        grid=(heads,),
        in_specs=[pl.BlockSpec(memory_space=pltpu.SMEM)],
        out_specs=pl.BlockSpec((1, 3, QBLK, NBR_KEYS), lambda h: (h, 0, 0, 0)),
        out_shape=jax.ShapeDtypeStruct((heads, 3, QBLK, NBR_KEYS), F32),
        compiler_params=_params(("parallel",)),
        name="nbr_bias_table",
    )(rpb.astype(F32).reshape(-1))


def _nbr_kernel(ql_ref, kl_ref, vl_ref, qc_ref, kc_ref, vc_ref, bias_ref, *rest, with_ctx):
    ol_ref, oc_ref = (rest[0], rest[1]) if with_ctx else (rest[0], None)
    v1l, v1c = rest[-2:]
    scale = B_HEAD_DIM ** -0.5
    v1l[:, 0:LANES] = vl_ref[...]
    v1l[:, LANES:] = jnp.ones((SEQ, LANES), BF16)
    v1c[:, 0:LANES] = vc_ref[...]
    v1c[:, LANES:] = jnp.ones((CTX_LEN, LANES), BF16)
    kc = kc_ref[...]
    lat_blocks = SEQ // QBLK

    for blk in range(lat_blocks):
        cfg = 0 if blk == 0 else (2 if blk == lat_blocks - 1 else 1)
        start = min(max((blk * ROWS_PER_QBLK - WIN_R // 2) * GRID_W, 0), SEQ - NBR_KEYS)
        q = ql_ref[blk * QBLK:(blk + 1) * QBLK, :]
        s_lat = (lax.dot_general(q, kl_ref[start:start + NBR_KEYS, :], NT_DIMS, preferred_element_type=F32) * scale
                 + bias_ref[0, cfg])
        s_ctx = lax.dot_general(q, kc, NT_DIMS, preferred_element_type=F32) * scale
        m = jnp.maximum(jnp.max(s_lat, axis=-1, keepdims=True), jnp.max(s_ctx, axis=-1, keepdims=True))
        ol = (jnp.dot(jnp.exp(s_lat - m).astype(BF16), v1l[start:start + NBR_KEYS, :], preferred_element_type=F32)
              + jnp.dot(jnp.exp(s_ctx - m).astype(BF16), v1c[...], preferred_element_type=F32))
        ol_ref[blk * QBLK:(blk + 1) * QBLK, :] = (ol[:, :LANES] / ol[:, LANES:]).astype(BF16)

    if with_ctx:
        s = lax.dot_general(qc_ref[...], kc, NT_DIMS, preferred_element_type=F32) * scale
        e = jnp.exp(s - jnp.max(s, axis=-1, keepdims=True)).astype(BF16)
        ol = jnp.dot(e, v1c[...], preferred_element_type=F32)
        oc_ref[...] = (ol[:, :LANES] / ol[:, LANES:]).astype(BF16)


def _nbr_call(proj, bias_tab, nb, with_ctx):
    order = lambda h, b: (b, h)
    out_specs, out_shapes = _mixer_outs(nb, LANES, order, with_ctx)
    return pl.pallas_call(
        functools.partial(_nbr_kernel, with_ctx=with_ctx),
        grid=(B_HEADS, nb),
        in_specs=_mixer_specs(nb, (LANES,) * 3, (COL_BQ, COL_BK, COL_BV), order) + [
            pl.BlockSpec((1, 3, QBLK, NBR_KEYS), lambda h, b: (h, 0, 0, 0)),
        ],
        out_specs=out_specs,
        out_shape=out_shapes,
        scratch_shapes=[pltpu.VMEM((SEQ, 2 * LANES), BF16), pltpu.VMEM((CTX_LEN, 2 * LANES), BF16)],
        compiler_params=_params(("parallel", "parallel")),
        name="nbr_attention",
    )(*([proj] * 6), bias_tab)


def _ret_kernel(ql_ref, kl_ref, vl_ref, gl_ref, qc_ref, kc_ref, vc_ref, gc_ref, cos_ref, sa_ref, sb_ref,
                dec_ref, rn_ref, *rest, with_ctx):
    ol_ref, oc_ref = (rest[0], rest[1]) if with_ctx else (rest[0], None)
    of_ref, ob_ref, st_ref = rest[-3:]
    ch = QBLK
    n_chunks = SEQ // ch
    kscale = C_KEY_DIM ** -0.5

    def run():
        ii = lax.broadcasted_iota(jnp.int32, (ch, ch), 0)
        jj = lax.broadcasted_iota(jnp.int32, (ch, ch), 1)
        dist = (ii - jj).astype(F32)
        pos = lax.broadcasted_iota(jnp.int32, (ch, 1), 0).astype(F32)
        lane = lax.broadcasted_iota(jnp.int32, (1, LANES), 1)
        rn = rn_ref[...]

        def log_sigmoid(x):
            return -(jnp.log1p(jnp.exp(-jnp.abs(x))) + jnp.maximum(-x, 0.0))

        heads = []
        for e in range(2):
            lgf = log_sigmoid(dec_ref[0, 0, e:e + 1, 0:1])
            lgb = log_sigmoid(dec_ref[0, 1, e:e + 1, 0:1])
            heads.append(dict(
                mask=(lane >= e * C_KEY_DIM) & (lane < (e + 1) * C_KEY_DIM),
                intra_f=jnp.where(dist >= 0, jnp.exp(lgf * jnp.maximum(dist, 0.0)), 0.0),
                intra_b=jnp.where(dist <= 0, jnp.exp(lgb * jnp.maximum(-dist, 0.0)), 0.0),
                qdec_f=jnp.exp(lgf * (pos + 1.0)), kdec_f=jnp.exp(lgf * (ch - 1.0 - pos)),
                cdec_f=jnp.exp(lgf * ch),
                qdec_b=jnp.exp(lgb * (ch - pos)), kdec_b=jnp.exp(lgb * pos),
                cdec_b=jnp.exp(lgb * ch),
                vs=slice(e * LANES, (e + 1) * LANES),
            ))

        def finish(o, g):
            return (_silu(g.astype(F32)) * (_rms(o) * rn)).astype(BF16)

        qc = qc_ref[...].astype(F32)
        kc = kc_ref[...].astype(F32) * kscale
        for e, hd in enumerate(heads):
            v = vc_ref[:, hd["vs"]]
            st_ref[e] = lax.dot_general((kc * hd["kdec_f"]).astype(BF16), v, TN_DIMS, preferred_element_type=F32)
            st_ref[2 + e] = lax.dot_general((kc * hd["kdec_b"]).astype(BF16), v, TN_DIMS,
                                            preferred_element_type=F32)
            if with_ctx:
                qe = jnp.where(hd["mask"], qc, 0.0).astype(BF16)
                att = lax.dot_general(qe, kc.astype(BF16), NT_DIMS, preferred_element_type=F32)
                att = att * (hd["intra_f"] + hd["intra_b"])
                o = jnp.dot(att.astype(BF16), v, preferred_element_type=F32)
                oc_ref[:, hd["vs"]] = finish(o, gc_ref[:, hd["vs"]])

        def chunk_rows(c):
            return pl.ds(pl.multiple_of(c * ch, ch), ch)

        def sweep(c, fwd):
            rows = chunk_rows(c)
            tabs = (cos_ref[rows, :], sa_ref[rows, :], sb_ref[rows, :])
            q = _rope(ql_ref[rows, :].astype(F32), *tabs)
            k = _rope(kl_ref[rows, :].astype(F32), *tabs) * kscale
            kb = k.astype(BF16)
            for e, hd in enumerate(heads):
                d = "f" if fwd else "b"
                si = e if fwd else 2 + e
                v = vl_ref[rows, hd["vs"]]
                qe = jnp.where(hd["mask"], q, 0.0)
                att = lax.dot_general(qe.astype(BF16), kb, NT_DIMS, preferred_element_type=F32) * hd["intra_" + d]
                st = st_ref[si]
                o = (jnp.dot(att.astype(BF16), v, preferred_element_type=F32)
                     + jnp.dot((qe * hd["qdec_" + d]).astype(BF16), st.astype(BF16), preferred_element_type=F32))
                st_ref[si] = st * hd["cdec_" + d] + lax.dot_general(
                    (k * hd["kdec_" + d]).astype(BF16), v, TN_DIMS, preferred_element_type=F32)
                (of_ref if fwd else ob_ref)[rows, hd["vs"]] = o

        def finish_chunk(c):
            rows = chunk_rows(c)
            for hd in heads:
                vs = hd["vs"]
                ol_ref[rows, vs] = finish(of_ref[rows, vs] + ob_ref[rows, vs], gl_ref[rows, vs])

        def first_half(t, carry):
            sweep(t, True)
            sweep(n_chunks - 1 - t, False)
            return carry

        def second_half(t, carry):
            sweep(t, True)
            sweep(n_chunks - 1 - t, False)
            finish_chunk(t)
            finish_chunk(n_chunks - 1 - t)
            return carry

        lax.fori_loop(0, n_chunks // 2, first_half, 0)
        lax.fori_loop(n_chunks // 2, n_chunks, second_half, 0)

    run()


def _ret_call(proj, tabs, ret_decay, ret_norm, nb, with_ctx):
    pairs = C_HEADS // 2
    w2 = 2 * LANES
    order = lambda b, p: (b, p)
    whole = lambda b, p: (0, 0)
    out_specs, out_shapes = _mixer_outs(nb, w2, order, with_ctx)
    dec = jnp.broadcast_to(ret_decay.astype(F32).reshape(2, pairs, 2).transpose(1, 0, 2)[..., None],
                           (pairs, 2, 2, LANES))
    return pl.pallas_call(
        functools.partial(_ret_kernel, with_ctx=with_ctx),
        grid=(nb, pairs),
        in_specs=_mixer_specs(nb, (LANES, LANES, w2, w2), (COL_CQ, COL_CK, COL_CV, COL_CG), order) + [
            pl.BlockSpec((SEQ, LANES), whole),
            pl.BlockSpec((SEQ, LANES), whole),
            pl.BlockSpec((SEQ, LANES), whole),
            pl.BlockSpec((1, 2, 2, LANES), lambda b, p: (p, 0, 0, 0)),
            pl.BlockSpec((1, LANES), whole),
        ],
        out_specs=out_specs,
        out_shape=out_shapes,
        scratch_shapes=[
            pltpu.VMEM((SEQ, w2), F32),
            pltpu.VMEM((SEQ, w2), F32),
            pltpu.VMEM((4, LANES, LANES), F32),
        ],
        compiler_params=_params(("parallel", "parallel")),
        name="retention",
    )(*([proj] * 8), *tabs, dec, ret_norm.reshape(1, LANES))


def _merge_kernel(h_ref, mod_ref, post_ref, g_lo_ref, g_hi_ref, wb_ref, wo_ref, *rest, n_lat_tiles, with_ctx):
    d = h_ref.shape[1]
    o_ref = rest[-1]
    if with_ctx:
        is_lat = pl.program_id(0) < n_lat_tiles
        xs = [jnp.where(is_lat, rest[2 * k][...], rest[2 * k + 1][...]) for k in range(3)]
    else:
        xs = [rest[k][...] for k in range(3)]
    g_lo = g_lo_ref[...]
    g_hi = g_hi_ref[...]
    gates = (g_lo[:, :d], jnp.concatenate([g_lo[:, d:], g_hi[:, :d // 2]], axis=1), g_hi[:, d // 2:])
    m = None
    for k in range(3):
        t = jax.nn.sigmoid(gates[k].astype(F32)) * jnp.dot(xs[k], wb_ref[k], preferred_element_type=F32)
        m = t if m is None else m + t
    y = jnp.dot(m.astype(BF16), wo_ref[...], preferred_element_type=F32)
    o_ref[...] = h_ref[...] + mod_ref[0, 5:6, :] * (_rms(y) * post_ref[...])


def _merge_call(h, mod_l, post_g, branches, proj, wb, wo, layer, nrows, n_lat_seg):
    d = h.shape[1]
    tm = CTX_LEN
    with_ctx = len(branches[0]) == 2
    n_lat_tiles = n_lat_seg * SEQ // tm
    gw = 3 * d // 2
    g0 = COL_GATES // gw
    resident = dict(pipeline_mode=pl.Buffered(1))
    row = lambda i: (i, 0)
    lat_row = lambda i: (jnp.minimum(i, n_lat_tiles - 1), 0)
    ctx_row = lambda i: (jnp.maximum(i - n_lat_tiles, 0), 0)
    branch_specs, branch_args = [], []
    for outs in branches:
        branch_specs.append(pl.BlockSpec((tm, BRANCH_W), lat_row))
        if with_ctx:
            branch_specs.append(pl.BlockSpec((tm, BRANCH_W), ctx_row))
        branch_args.extend(outs)
    return pl.pallas_call(
        functools.partial(_merge_kernel, n_lat_tiles=n_lat_tiles, with_ctx=with_ctx),
        grid=(nrows // tm,),
        in_specs=[
            pl.BlockSpec((tm, d), row),
            pl.BlockSpec((1, N_MOD, d), lambda i: (jnp.minimum((i * tm) // SEQ, n_lat_seg), 0, 0)),
            pl.BlockSpec((1, d), lambda i: (0, 0)),
            pl.BlockSpec((tm, gw), lambda i: (i, g0)),
            pl.BlockSpec((tm, gw), lambda i: (i, g0 + 1)),
            pl.BlockSpec((None, 3, BRANCH_W, d), lambda i: (layer, 0, 0, 0), **resident),
            pl.BlockSpec((None, d, d), lambda i: (layer, 0, 0), **resident),
        ] + branch_specs,
        out_specs=pl.BlockSpec((tm, d), row),
        out_shape=jax.ShapeDtypeStruct((nrows, d), F32),
        compiler_params=_params(("parallel",)),
        name="mixer_merge",
    )(h, mod_l, post_g.reshape(1, d), proj, proj, wb, wo, *branch_args)


def kernel(x, c, ctx, c_ctx, w_mod, b_mod, pre_norm, post_norm, ffn_w_in, ffn_w_out, w_in, diff_lambda,
           diff_norm, na_rpb, ret_decay, ret_norm, w_branch, w_out):
    nb, seq, d = x.shape
    assert (seq, d) == (SEQ, D_MODEL) and ctx.shape == (nb, CTX_LEN, d)
    n_lat = nb * SEQ
    n_all = n_lat + nb * CTX_LEN
    tm = min(512, nb * CTX_LEN)

    h = jnp.concatenate([x.reshape(n_lat, d), ctx.reshape(nb * CTX_LEN, d)], axis=0)
    pad = (-(nb + 1)) % 8
    cc = jnp.concatenate([c, c_ctx[None, :], jnp.zeros((pad, d), c.dtype)], axis=0)
    mods = _mod_call(cc, w_mod, b_mod)
    tabs = _rope_tables()
    w1, w2 = _col_tiles(ffn_w_in.astype(BF16), FFN_TILE), _col_tiles(ffn_w_out.astype(BF16), FFN_TILE)
    w_proj, wb, wo = _col_tiles(w_in.astype(BF16), PROJ_TILE), w_branch.astype(BF16), w_out.astype(BF16)

    for l in range(DEPTH):
        with_ctx = l < DEPTH - 1
        lambda_init = 0.8 - 0.6 * math.exp(-0.3 * l)
        nrows_out = n_all if with_ctx else n_lat
        mod_l = mods[l, :nb + 1].reshape(nb + 1, N_MOD, d)
        ffn = functools.partial(_ffn_call, tm=tm, n_lat_seg=nb)

        h = ffn(h, mod_l, 0, pre_norm[l, 0], post_norm[l, 0], w1, w2, (l, 0), n_all)
        proj = _inproj_call(h, mod_l, pre_norm[l, 1], w_proj, l, min(1024, nb * CTX_LEN), nb)
        oa = _diff_call(proj, tabs, diff_lambda[l], diff_norm[l], lambda_init, nb, with_ctx)
        ob = _nbr_call(proj, _nbr_bias_call(na_rpb[l]), nb, with_ctx)
        yr = _ret_call(proj, tabs, ret_decay[l], ret_norm[l], nb, with_ctx)
        h = _merge_call(h, mod_l, post_norm[l, 1], (oa, ob, yr), proj, wb, wo, l, nrows_out, nb)
        h = ffn(h, mod_l, 6, pre_norm[l, 2], post_norm[l, 2], w1, w2, (l, 1), nrows_out)
    return h.reshape(nb, SEQ, d)
```

```python
import functools
import math

import numpy as np
import jax
import jax.numpy as jnp
from jax import lax
from jax.experimental import pallas as pl
from jax.experimental.pallas import tpu as pltpu

D_MODEL = 2048
SEQ = 2048
DEPTH = 2
GRID_W = 64
CTX_LEN = 256
N_MOD = 9
FFN_DIM = 5632
A_HEADS = 8
A_HEAD_DIM = 64
B_HEADS = 8
B_HEAD_DIM = 128
WIN_R = 8
WIN_C = 16
C_HEADS = 8
C_KEY_DIM = 64
BRANCH_W = 1024
ROPE_BASE = 10000.0
EPS = 1e-6

COL_AQ, COL_AK, COL_AV = 0, 1024, 2048
COL_BQ, COL_BK, COL_BV = 3072, 4096, 5120
COL_CQ, COL_CK, COL_CV, COL_CG = 6144, 6656, 7168, 8192
COL_GATES = 9216
IN_COLS = 15360

LANES = 128
BF16_SUBLANES = 16
QBLK = 256
ROWS_PER_QBLK = QBLK // GRID_W
NBR_KROWS = ROWS_PER_QBLK + WIN_R - 1
NBR_KEYS = NBR_KROWS * GRID_W
VMEM_LIMIT = 56 * 1024 * 1024
FFN_TILE = 512
FFN_STAGE_ROWS = 64
PROJ_TILE = 1024
DIFF_ROW_SPLIT = 2
NBR_MASKED = -1e30

BF16 = jnp.bfloat16
F32 = jnp.float32
NT_DIMS = (((1,), (1,)), ((), ()))
TN_DIMS = (((0,), (0,)), ((), ()))


def _params(sem):
    return pltpu.CompilerParams(dimension_semantics=sem, vmem_limit_bytes=VMEM_LIMIT)


def _rms(x):
    return x * lax.rsqrt(jnp.mean(x * x, axis=-1, keepdims=True) + EPS)


def _silu(x):
    return x * jax.nn.sigmoid(x)


def _row_chunks(row0, n_rows, rows_per, body, inline=False):
    def step(r, carry):
        body(pl.ds(pl.multiple_of(row0 + r * rows_per, rows_per), rows_per))
        return carry
    if inline:
        for r in range(n_rows // rows_per):
            step(r, 0)
    else:
        lax.fori_loop(0, n_rows // rows_per, step, 0, unroll=4)


def _adaln_rows(u_ref, dst, h_ref, mod_ref, pre_ref, k0, seg, row0, n_rows, inline):
    gain = pre_ref[...] * (1.0 + mod_ref[seg, k0 + 1:k0 + 2, :])
    shift = mod_ref[seg, k0:k0 + 1, :]

    def body(rows):
        u_ref[dst, rows, :] = (_rms(h_ref[rows, :]) * gain + shift).astype(u_ref.dtype)

    _row_chunks(row0, n_rows, BF16_SUBLANES, body, inline)


def _mod_kernel(cc_ref, w_ref, b_ref, o_ref):
    s = _silu(cc_ref[...]).astype(BF16)
    o_ref[0] = jnp.dot(s, w_ref[0].astype(BF16), preferred_element_type=F32) + b_ref[0]


def _mod_call(cc, w_mod, b_mod):
    depth, d, n = w_mod.shape
    rows = cc.shape[0]
    tn = 1024
    return pl.pallas_call(
        _mod_kernel,
        grid=(depth, n // tn),
        in_specs=[
            pl.BlockSpec((rows, d), lambda l, j: (0, 0)),
            pl.BlockSpec((1, d, tn), lambda l, j: (l, 0, j)),
            pl.BlockSpec((1, 1, tn), lambda l, j: (l, 0, j)),
        ],
        out_specs=pl.BlockSpec((1, rows, tn), lambda l, j: (l, 0, j)),
        out_shape=jax.ShapeDtypeStruct((depth, rows, n), F32),
        compiler_params=_params(("parallel", "parallel")),
        name="mod_vectors",
    )(cc, w_mod, b_mod.reshape(depth, 1, n))


def _ffn_kernel(h_ref, mod_ref, pre_ref, post_ref, w1a_ref, w1b_ref, w2_ref, o_ref, u_ref, hm_ref, y_ref, *,
                k0, nf, tf, nn, tn, tm, n_lat_seg):
    i = pl.program_id(0)
    j = pl.program_id(1)
    slot = i % 2
    d = h_ref.shape[1]

    def adaln_rows(tile, row0, n_rows, dst, inline):
        seg = jnp.minimum((tile * tm) // SEQ, n_lat_seg)
        _adaln_rows(u_ref, dst, h_ref, mod_ref, pre_ref, k0, seg, row0, n_rows, inline)

    @pl.when((i == 0) & (j == 0))
    def _():
        adaln_rows(0, 0, tm, 0, False)

    @pl.when(j < nf)
    def _():
        u = u_ref[slot]
        a = jnp.dot(u, w1a_ref[...], preferred_element_type=F32)
        b = jnp.dot(u, w1b_ref[...], preferred_element_type=F32)
        hm_ref[j] = (_silu(a) * b).astype(BF16)
        n_copy = tm // FFN_STAGE_ROWS
        rows = pl.ds(pl.multiple_of(jnp.minimum(j, n_copy - 1) * FFN_STAGE_ROWS, FFN_STAGE_ROWS), FFN_STAGE_ROWS)
        o_ref[rows, :] = h_ref[rows, :]

    @pl.when(j >= nf)
    def _():
        n = j - nf
        y = jnp.dot(hm_ref[0], w2_ref[0:tf, :], preferred_element_type=F32)
        for k in range(1, nf):
            y += jnp.dot(hm_ref[k], w2_ref[k * tf:(k + 1) * tf, :], preferred_element_type=F32)
        y_ref[n] = y
        adaln_rows(i + 1, n * (tm // nn), tm // nn, 1 - slot, True)

    @pl.when(j == nf + nn - 1)
    def _():
        seg = jnp.minimum((i * tm) // SEQ, n_lat_seg)
        gain = 0.5 * mod_ref[seg, k0 + 2:k0 + 3, :] * post_ref[...]
        ss = sum(jnp.sum(y_ref[n] * y_ref[n], axis=-1, keepdims=True) for n in range(nn))
        r = lax.rsqrt(ss / d + EPS)
        for n in range(nn):
            cols = slice(n * tn, (n + 1) * tn)
            o_ref[:, cols] = o_ref[:, cols] + (y_ref[n] * r) * gain[:, cols]


def _ffn_call(h, mod_l, k0, pre_g, post_g, w1, w2, lk, nrows, tm, n_lat_seg):
    d = h.shape[1]
    f = w2.shape[2]
    tf = tn = FFN_TILE
    nf, nn = f // tf, d // tn
    n_tiles = nrows // tm
    h_tile = lambda i, j: (jnp.minimum(i + jnp.where(j >= nf, 1, 0), n_tiles - 1), 0)
    w2_tile = lambda j: jnp.where(j < nf, nn - 1, j - nf)
    return pl.pallas_call(
        functools.partial(_ffn_kernel, k0=k0, nf=nf, tf=tf, nn=nn, tn=tn, tm=tm, n_lat_seg=n_lat_seg),
        grid=(n_tiles, nf + nn),
        in_specs=[
            pl.BlockSpec((tm, d), h_tile),
            pl.BlockSpec(mod_l.shape, lambda i, j: (0, 0, 0)),
            pl.BlockSpec((1, d), lambda i, j: (0, 0)),
            pl.BlockSpec((1, d), lambda i, j: (0, 0)),
            pl.BlockSpec((None, None, d, tf), lambda i, j: (*lk, 0, jnp.minimum(j, nf - 1))),
            pl.BlockSpec((None, None, d, tf), lambda i, j: (*lk, 0, jnp.minimum(j, nf - 1) + nf)),
            pl.BlockSpec((None, None, f, tn), lambda i, j: (*lk, 0, w2_tile(j))),
        ],
        out_specs=pl.BlockSpec((tm, d), lambda i, j: (i, 0)),
        out_shape=jax.ShapeDtypeStruct((nrows, d), F32),
        scratch_shapes=[pltpu.VMEM((2, tm, d), BF16), pltpu.VMEM((nf, tm, tf), BF16),
                        pltpu.VMEM((nn, tm, tn), F32)],
        compiler_params=_params(("arbitrary", "arbitrary")),
        name="ffn_sublayer",
    )(h, mod_l, pre_g.reshape(1, d), post_g.reshape(1, d), w1, w1, w2)


def _inproj_kernel(h_ref, mod_ref, pre_ref, w_ref, o_ref, u_ref, *, tm, n_lat_seg, n_steps, n_pre):
    i = pl.program_id(0)
    j = pl.program_id(1)
    slot = i % 2

    def adaln_rows(tile, row0, n_rows, dst, inline):
        seg = jnp.minimum((tile * tm) // SEQ, n_lat_seg)
        _adaln_rows(u_ref, dst, h_ref, mod_ref, pre_ref, 3, seg, row0, n_rows, inline)

    def project():
        o_ref[...] = jnp.dot(u_ref[slot], w_ref[...], preferred_element_type=F32).astype(BF16)

    @pl.when((i == 0) & (j == 0))
    def _():
        adaln_rows(0, 0, tm, 0, False)

    @pl.when(j < n_steps - n_pre)
    def _():
        project()

    @pl.when(j >= n_steps - n_pre)
    def _():
        project()
        adaln_rows(i + 1, (j - (n_steps - n_pre)) * (tm // n_pre), tm // n_pre, 1 - slot, True)


def _inproj_call(h, mod_l, pre_g, w, layer, tm, n_lat_seg):
    nrows, d = h.shape
    n = w.shape[2]
    tn = PROJ_TILE
    n_tiles, n_steps = nrows // tm, n // tn
    n_pre = 8
    h_tile = lambda i, j: (jnp.minimum(i + jnp.where(j >= n_steps - n_pre, 1, 0), n_tiles - 1), 0)
    return pl.pallas_call(
        functools.partial(_inproj_kernel, tm=tm, n_lat_seg=n_lat_seg, n_steps=n_steps, n_pre=n_pre),
        grid=(n_tiles, n_steps),
        in_specs=[
            pl.BlockSpec((tm, d), h_tile),
            pl.BlockSpec(mod_l.shape, lambda i, j: (0, 0, 0)),
            pl.BlockSpec((1, d), lambda i, j: (0, 0)),
            pl.BlockSpec((None, d, tn), lambda i, j: (layer, 0, j)),
        ],
        out_specs=pl.BlockSpec((tm, tn), lambda i, j: (i, j)),
        out_shape=jax.ShapeDtypeStruct((nrows, n), BF16),
        scratch_shapes=[pltpu.VMEM((2, tm, d), BF16)],
        compiler_params=_params(("arbitrary", "arbitrary")),
        name="mixer_in_proj",
    )(h, mod_l, pre_g.reshape(1, d), w)


def _rope(t, cos, sa, sb):
    return t * cos + pltpu.roll(t, LANES - 16, 1) * sa + pltpu.roll(t, 16, 1) * sb


def _rope_tables():
    pos = np.arange(SEQ)
    prow = jnp.asarray(pos // GRID_W, F32)
    pcol = jnp.asarray(pos % GRID_W, F32)
    half = 16
    inv = ROPE_BASE ** (-jnp.arange(half, dtype=F32) / half)
    lane = np.arange(LANES)
    freq = jnp.asarray(lane % half)
    use_col = jnp.asarray((lane % 64) >= 32)
    second = jnp.asarray((lane % 32) >= half)
    p = jnp.where(use_col[None, :], pcol[:, None], prow[:, None])
    ang = p * inv[freq][None, :]
    cos, sin = jnp.cos(ang), jnp.sin(ang)
    sa = jnp.where(second[None, :], 0.0, -sin)
    sb = jnp.where(second[None, :], sin, 0.0)
    return cos, sa, sb


def _mixer_specs(nb, width, cols, order):
    ctx_blk0 = nb * SEQ // CTX_LEN

    def spec(rows, col, w, ctx):
        def index(*g):
            b, h = order(*g)
            return ((ctx_blk0 + b) if ctx else b, col // w + h)
        return pl.BlockSpec((rows, w), index)

    return ([spec(SEQ, c, w, False) for c, w in zip(cols, width)]
            + [spec(CTX_LEN, c, w, True) for c, w in zip(cols, width)])


def _mixer_outs(nb, width, order, with_ctx):
    def index(*g):
        b, h = order(*g)
        return (b, h)
    specs = [pl.BlockSpec((SEQ, width), index)]
    shapes = [jax.ShapeDtypeStruct((nb * SEQ, BRANCH_W), BF16)]
    if with_ctx:
        specs.append(pl.BlockSpec((CTX_LEN, width), index))
        shapes.append(jax.ShapeDtypeStruct((nb * CTX_LEN, BRANCH_W), BF16))
    return specs, shapes


def _diff_kernel(ql_ref, kl_ref, vl_ref, qc_ref, kc_ref, vc_ref, cos_ref, sa_ref, sb_ref, lam_ref, g_ref,
                 *rest, lambda_init, with_ctx):
    ol_ref, oc_ref = (rest[0], rest[1]) if with_ctx else (rest[0], None)
    k_all, v_ones = rest[-2:]
    k_all[0:CTX_LEN, :] = kc_ref[...]
    k_all[CTX_LEN:, :] = _rope(kl_ref[...].astype(F32), cos_ref[...], sa_ref[...], sb_ref[...]).astype(BF16)
    v_ones[0:CTX_LEN, 0:LANES] = vc_ref[...]
    v_ones[CTX_LEN:, 0:LANES] = vl_ref[...]
    v_ones[:, LANES:] = jnp.ones((CTX_LEN + SEQ, LANES), BF16)

    lv = lam_ref[...]
    lam = (jnp.exp(jnp.sum(lv[0:1] * lv[1:2], axis=-1, keepdims=True))
           - jnp.exp(jnp.sum(lv[2:3] * lv[3:4], axis=-1, keepdims=True)) + lambda_init)
    lane = lax.broadcasted_iota(jnp.int32, (1, LANES), 1)

    def attend(q, nk):
        q = q * (A_HEAD_DIM ** -0.5)
        k = k_all[0:nk, :]
        v1 = v_ones[0:nk, :]

        rows = QBLK // DIFF_ROW_SPLIT
        qms = [jnp.where(mask, q[r * rows:(r + 1) * rows], 0.0).astype(BF16)
               for r in range(DIFF_ROW_SPLIT) for mask in (lane < A_HEAD_DIM, lane >= A_HEAD_DIM)]
        ss = [lax.dot_general(qm, k, NT_DIMS, preferred_element_type=F32) for qm in qms]
        es = [jnp.exp(s - jnp.max(s, axis=-1, keepdims=True)).astype(BF16) for s in ss]
        ols = [jnp.dot(e, v1, preferred_element_type=F32) for e in es]
        os = [ol[:, :LANES] / ol[:, LANES:] for ol in ols]
        o = jnp.concatenate([os[2 * r] - lam * os[2 * r + 1] for r in range(DIFF_ROW_SPLIT)], axis=0)
        return (_rms(o) * g_ref[...] * (1.0 - lambda_init)).astype(BF16)

    def block(t, carry):
        rows = pl.ds(pl.multiple_of(t * QBLK, QBLK), QBLK)
        q = _rope(ql_ref[rows, :].astype(F32), cos_ref[rows, :], sa_ref[rows, :], sb_ref[rows, :])
        ol_ref[rows, :] = attend(q, CTX_LEN + SEQ)
        return carry

    lax.fori_loop(0, SEQ // QBLK, block, 0, unroll=True)
    if with_ctx:
        oc_ref[...] = attend(qc_ref[...].astype(F32), CTX_LEN)


def _diff_call(proj, tabs, diff_lambda, diff_norm, lambda_init, nb, with_ctx):
    order = lambda b, h: (b, h)
    whole = lambda b, h: (0, 0)
    out_specs, out_shapes = _mixer_outs(nb, LANES, order, with_ctx)
    return pl.pallas_call(
        functools.partial(_diff_kernel, lambda_init=lambda_init, with_ctx=with_ctx),
        grid=(nb, A_HEADS),
        in_specs=_mixer_specs(nb, (LANES,) * 3, (COL_AQ, COL_AK, COL_AV), order) + [
            pl.BlockSpec((SEQ, LANES), whole),
            pl.BlockSpec((SEQ, LANES), whole),
            pl.BlockSpec((SEQ, LANES), whole),
            pl.BlockSpec((4, A_HEAD_DIM), whole),
            pl.BlockSpec((1, LANES), whole),
        ],
        out_specs=out_specs,
        out_shape=out_shapes,
        scratch_shapes=[pltpu.VMEM((CTX_LEN + SEQ, LANES), BF16), pltpu.VMEM((CTX_LEN + SEQ, 2 * LANES), BF16)],
        compiler_params=_params(("parallel", "parallel")),
        name="diff_attention",
    )(*([proj] * 6), *tabs, diff_lambda, diff_norm.reshape(1, LANES))


def _nbr_block_rows(cfg, qi, kj):
    n_rows = SEQ // GRID_W
    r0 = (0, ROWS_PER_QBLK, n_rows - ROWS_PER_QBLK)[cfg]
    ks = min(max(r0 - WIN_R // 2, 0), n_rows - NBR_KROWS)
    r = r0 + qi
    rs = min(max(r - WIN_R // 2, 0), n_rows - WIN_R)
    kr = ks + kj
    return rs <= kr < rs + WIN_R, kr - r + WIN_R - 1


def _nbr_bias_kernel(rpb_ref, o_ref):
    n_dr, n_dc = 2 * WIN_R - 1, 2 * WIN_C - 1
    base = pl.program_id(0) * (n_dr * n_dc)
    qc = lax.broadcasted_iota(jnp.int32, (GRID_W, LANES), 0)
    lane = lax.broadcasted_iota(jnp.int32, (GRID_W, LANES), 1)
    kc = lane % GRID_W
    dc = kc - qc + (WIN_C - 1)
    cs = jnp.clip(qc - WIN_C // 2, 0, GRID_W - WIN_C)
    col_ok = (kc >= cs) & (kc < cs + WIN_C)
    masked = jnp.full((GRID_W, LANES), NBR_MASKED, F32)

    def row_pattern(dr):
        acc = masked
        for x in range(n_dc):
            acc = jnp.where(dc == x, rpb_ref[base + dr * n_dc + x], acc)
        return jnp.where(col_ok, acc, masked)

    pats = [row_pattern(dr) for dr in range(n_dr)]

    def pattern(cfg, qi, kj):
        if kj >= NBR_KROWS:
            return masked
        ok, dr = _nbr_block_rows(cfg, qi, kj)
        return pats[dr] if ok else masked

    for cfg in range(3):
        for qi in range(ROWS_PER_QBLK):
            rows = slice(qi * GRID_W, (qi + 1) * GRID_W)
            for m in range(pl.cdiv(NBR_KEYS, LANES)):
                tile = jnp.where(lane < GRID_W, pattern(cfg, qi, 2 * m), pattern(cfg, qi, 2 * m + 1))
                width = min(LANES, NBR_KEYS - m * LANES)
                o_ref[0, cfg, rows, m * LANES:m * LANES + width] = tile[:, :width]


def _nbr_bias_call(rpb):
    heads = rpb.shape[0]
    return pl.pallas_call(
        _nbr_bias_kernel,
        grid=(heads,),
        in_specs=[pl.BlockSpec(memory_space=pltpu.SMEM)],
        out_specs=pl.BlockSpec((1, 3, QBLK, NBR_KEYS), lambda h: (h, 0, 0, 0)),
        out_shape=jax.ShapeDtypeStruct((heads, 3, QBLK, NBR_KEYS), F32),
        compiler_params=_params(("parallel",)),
        name="nbr_bias_table",
    )(rpb.astype(F32).reshape(-1))


def _nbr_kernel(ql_ref, kl_ref, vl_ref, qc_ref, kc_ref, vc_ref, bias_ref, *rest, with_ctx):
    ol_ref, oc_ref = (rest[0], rest[1]) if with_ctx else (rest[0], None)
    v1l, v1c = rest[-2:]
    scale = B_HEAD_DIM ** -0.5
    v1l[:, 0:LANES] = vl_ref[...]
    v1l[:, LANES:] = jnp.ones((SEQ, LANES), BF16)
    v1c[:, 0:LANES] = vc_ref[...]
    v1c[:, LANES:] = jnp.ones((CTX_LEN, LANES), BF16)
    kc = kc_ref[...]
    lat_blocks = SEQ // QBLK

    for blk in range(lat_blocks):
        cfg = 0 if blk == 0 else (2 if blk == lat_blocks - 1 else 1)
        start = min(max((blk * ROWS_PER_QBLK - WIN_R // 2) * GRID_W, 0), SEQ - NBR_KEYS)
        q = ql_ref[blk * QBLK:(blk + 1) * QBLK, :]
        s_lat = (lax.dot_general(q, kl_ref[start:start + NBR_KEYS, :], NT_DIMS, preferred_element_type=F32) * scale
                 + bias_ref[0, cfg])
        s_ctx = lax.dot_general(q, kc, NT_DIMS, preferred_element_type=F32) * scale
        m = jnp.maximum(jnp.max(s_lat, axis=-1, keepdims=True), jnp.max(s_ctx, axis=-1, keepdims=True))
        ol = (jnp.dot(jnp.exp(s_lat - m).astype(BF16), v1l[start:start + NBR_KEYS, :], preferred_element_type=F32)
              + jnp.dot(jnp.exp(s_ctx - m).astype(BF16), v1c[...], preferred_element_type=F32))
        ol_ref[blk * QBLK:(blk + 1) * QBLK, :] = (ol[:, :LANES] / ol[:, LANES:]).astype(BF16)

    if with_ctx:
        s = lax.dot_general(qc_ref[...], kc, NT_DIMS, preferred_element_type=F32) * scale
        e = jnp.exp(s - jnp.max(s, axis=-1, keepdims=True)).astype(BF16)
        ol = jnp.dot(e, v1c[...], preferred_element_type=F32)
        oc_ref[...] = (ol[:, :LANES] / ol[:, LANES:]).astype(BF16)


def _nbr_call(proj, bias_tab, nb, with_ctx):
    order = lambda h, b: (b, h)
    out_specs, out_shapes = _mixer_outs(nb, LANES, order, with_ctx)
    return pl.pallas_call(
        functools.partial(_nbr_kernel, with_ctx=with_ctx),
        grid=(B_HEADS, nb),
        in_specs=_mixer_specs(nb, (LANES,) * 3, (COL_BQ, COL_BK, COL_BV), order) + [
            pl.BlockSpec((1, 3, QBLK, NBR_KEYS), lambda h, b: (h, 0, 0, 0)),
        ],
        out_specs=out_specs,
        out_shape=out_shapes,
        scratch_shapes=[pltpu.VMEM((SEQ, 2 * LANES), BF16), pltpu.VMEM((CTX_LEN, 2 * LANES), BF16)],
        compiler_params=_params(("parallel", "parallel")),
        name="nbr_attention",
    )(*([proj] * 6), bias_tab)


def _ret_kernel(ql_ref, kl_ref, vl_ref, gl_ref, qc_ref, kc_ref, vc_ref, gc_ref, cos_ref, sa_ref, sb_ref,
                dec_ref, rn_ref, *rest, with_ctx):
    ol_ref, oc_ref = (rest[0], rest[1]) if with_ctx else (rest[0], None)
    of_ref, ob_ref, st_ref = rest[-3:]
    ch = QBLK
    n_chunks = SEQ // ch
    kscale = C_KEY_DIM ** -0.5

    def run():
        ii = lax.broadcasted_iota(jnp.int32, (ch, ch), 0)
        jj = lax.broadcasted_iota(jnp.int32, (ch, ch), 1)
        dist = (ii - jj).astype(F32)
        pos = lax.broadcasted_iota(jnp.int32, (ch, 1), 0).astype(F32)
        lane = lax.broadcasted_iota(jnp.int32, (1, LANES), 1)
        rn = rn_ref[...]

        def log_sigmoid(x):
            return -(jnp.log1p(jnp.exp(-jnp.abs(x))) + jnp.maximum(-x, 0.0))

        heads = []
        for e in range(2):
            lgf = log_sigmoid(dec_ref[0, 0, e:e + 1, 0:1])
            lgb = log_sigmoid(dec_ref[0, 1, e:e + 1, 0:1])
            heads.append(dict(
                mask=(lane >= e * C_KEY_DIM) & (lane < (e + 1) * C_KEY_DIM),
                intra_f=jnp.where(dist >= 0, jnp.exp(lgf * jnp.maximum(dist, 0.0)), 0.0),
                intra_b=jnp.where(dist <= 0, jnp.exp(lgb * jnp.maximum(-dist, 0.0)), 0.0),
                qdec_f=jnp.exp(lgf * (pos + 1.0)), kdec_f=jnp.exp(lgf * (ch - 1.0 - pos)),
                cdec_f=jnp.exp(lgf * ch),
                qdec_b=jnp.exp(lgb * (ch - pos)), kdec_b=jnp.exp(lgb * pos),
                cdec_b=jnp.exp(lgb * ch),
                vs=slice(e * LANES, (e + 1) * LANES),
            ))

        def finish(o, g):
            return (_silu(g.astype(F32)) * (_rms(o) * rn)).astype(BF16)

        qc = qc_ref[...].astype(F32)
        kc = kc_ref[...].astype(F32) * kscale
        for e, hd in enumerate(heads):
            v = vc_ref[:, hd["vs"]]
            st_ref[e] = lax.dot_general((kc * hd["kdec_f"]).astype(BF16), v, TN_DIMS, preferred_element_type=F32)
            st_ref[2 + e] = lax.dot_general((kc * hd["kdec_b"]).astype(BF16), v, TN_DIMS,
                                            preferred_element_type=F32)
            if with_ctx:
                qe = jnp.where(hd["mask"], qc, 0.0).astype(BF16)
                att = lax.dot_general(qe, kc.astype(BF16), NT_DIMS, preferred_element_type=F32)
                att = att * (hd["intra_f"] + hd["intra_b"])
                o = jnp.dot(att.astype(BF16), v, preferred_element_type=F32)
                oc_ref[:, hd["vs"]] = finish(o, gc_ref[:, hd["vs"]])

        def chunk_rows(c):
            return pl.ds(pl.multiple_of(c * ch, ch), ch)

        def sweep(c, fwd):
            rows = chunk_rows(c)
            tabs = (cos_ref[rows, :], sa_ref[rows, :], sb_ref[rows, :])
            q = _rope(ql_ref[rows, :].astype(F32), *tabs)
            k = _rope(kl_ref[rows, :].astype(F32), *tabs) * kscale
            kb = k.astype(BF16)
            for e, hd in enumerate(heads):
                d = "f" if fwd else "b"
                si = e if fwd else 2 + e
                v = vl_ref[rows, hd["vs"]]
                qe = jnp.where(hd["mask"], q, 0.0)
                att = lax.dot_general(qe.astype(BF16), kb, NT_DIMS, preferred_element_type=F32) * hd["intra_" + d]
                st = st_ref[si]
                o = (jnp.dot(att.astype(BF16), v, preferred_element_type=F32)
                     + jnp.dot((qe * hd["qdec_" + d]).astype(BF16), st.astype(BF16), preferred_element_type=F32))
                st_ref[si] = st * hd["cdec_" + d] + lax.dot_general(
                    (k * hd["kdec_" + d]).astype(BF16), v, TN_DIMS, preferred_element_type=F32)
                (of_ref if fwd else ob_ref)[rows, hd["vs"]] = o

        def finish_chunk(c):
            rows = chunk_rows(c)
            for hd in heads:
                vs = hd["vs"]
                ol_ref[rows, vs] = finish(of_ref[rows, vs] + ob_ref[rows, vs], gl_ref[rows, vs])

        def first_half(t, carry):
            sweep(t, True)
            sweep(n_chunks - 1 - t, False)
            return carry

        def second_half(t, carry):
            sweep(t, True)
            sweep(n_chunks - 1 - t, False)
            finish_chunk(t)
            finish_chunk(n_chunks - 1 - t)
            return carry

        lax.fori_loop(0, n_chunks // 2, first_half, 0)
        lax.fori_loop(n_chunks // 2, n_chunks, second_half, 0)

    run()


def _ret_call(proj, tabs, ret_decay, ret_norm, nb, with_ctx):
    pairs = C_HEADS // 2
    w2 = 2 * LANES
    order = lambda b, p: (b, p)
    whole = lambda b, p: (0, 0)
    out_specs, out_shapes = _mixer_outs(nb, w2, order, with_ctx)
    dec = jnp.broadcast_to(ret_decay.astype(F32).reshape(2, pairs, 2).transpose(1, 0, 2)[..., None],
                           (pairs, 2, 2, LANES))
    return pl.pallas_call(
        functools.partial(_ret_kernel, with_ctx=with_ctx),
        grid=(nb, pairs),
        in_specs=_mixer_specs(nb, (LANES, LANES, w2, w2), (COL_CQ, COL_CK, COL_CV, COL_CG), order) + [
            pl.BlockSpec((SEQ, LANES), whole),
            pl.BlockSpec((SEQ, LANES), whole),
            pl.BlockSpec((SEQ, LANES), whole),
            pl.BlockSpec((1, 2, 2, LANES), lambda b, p: (p, 0, 0, 0)),
            pl.BlockSpec((1, LANES), whole),
        ],
        out_specs=out_specs,
        out_shape=out_shapes,
        scratch_shapes=[
            pltpu.VMEM((SEQ, w2), F32),
            pltpu.VMEM((SEQ, w2), F32),
            pltpu.VMEM((4, LANES, LANES), F32),
        ],
        compiler_params=_params(("parallel", "parallel")),
        name="retention",
    )(*([proj] * 8), *tabs, dec, ret_norm.reshape(1, LANES))


def _merge_kernel(h_ref, mod_ref, post_ref, g_lo_ref, g_hi_ref, wb_ref, wo_ref, *rest, n_lat_tiles, with_ctx):
    d = h_ref.shape[1]
    o_ref = rest[-1]
    if with_ctx:
        is_lat = pl.program_id(0) < n_lat_tiles
        xs = [jnp.where(is_lat, rest[2 * k][...], rest[2 * k + 1][...]) for k in range(3)]
    else:
        xs = [rest[k][...] for k in range(3)]
    g_lo = g_lo_ref[...]
    g_hi = g_hi_ref[...]
    gates = (g_lo[:, :d], jnp.concatenate([g_lo[:, d:], g_hi[:, :d // 2]], axis=1), g_hi[:, d // 2:])
    m = None
    for k in range(3):
        t = jax.nn.sigmoid(gates[k].astype(F32)) * jnp.dot(xs[k], wb_ref[k], preferred_element_type=F32)
        m = t if m is None else m + t
    y = jnp.dot(m.astype(BF16), wo_ref[...], preferred_element_type=F32)
    o_ref[...] = h_ref[...] + mod_ref[0, 5:6, :] * (_rms(y) * post_ref[...])


def _merge_call(h, mod_l, post_g, branches, proj, wb, wo, layer, nrows, n_lat_seg):
    d = h.shape[1]
    tm = CTX_LEN
    with_ctx = len(branches[0]) == 2
    n_lat_tiles = n_lat_seg * SEQ // tm
    gw = 3 * d // 2
    g0 = COL_GATES // gw
    resident = dict(pipeline_mode=pl.Buffered(1))
    row = lambda i: (i, 0)
    lat_row = lambda i: (jnp.minimum(i, n_lat_tiles - 1), 0)
    ctx_row = lambda i: (jnp.maximum(i - n_lat_tiles, 0), 0)
    branch_specs, branch_args = [], []
    for outs in branches:
        branch_specs.append(pl.BlockSpec((tm, BRANCH_W), lat_row))
        if with_ctx:
            branch_specs.append(pl.BlockSpec((tm, BRANCH_W), ctx_row))
        branch_args.extend(outs)
    return pl.pallas_call(
        functools.partial(_merge_kernel, n_lat_tiles=n_lat_tiles, with_ctx=with_ctx),
        grid=(nrows // tm,),
        in_specs=[
            pl.BlockSpec((tm, d), row),
            pl.BlockSpec((1, N_MOD, d), lambda i: (jnp.minimum((i * tm) // SEQ, n_lat_seg), 0, 0)),
            pl.BlockSpec((1, d), lambda i: (0, 0)),
            pl.BlockSpec((tm, gw), lambda i: (i, g0)),
            pl.BlockSpec((tm, gw), lambda i: (i, g0 + 1)),
            pl.BlockSpec((None, 3, BRANCH_W, d), lambda i: (layer, 0, 0, 0), **resident),
            pl.BlockSpec((None, d, d), lambda i: (layer, 0, 0), **resident),
        ] + branch_specs,
        out_specs=pl.BlockSpec((tm, d), row),
        out_shape=jax.ShapeDtypeStruct((nrows, d), F32),
        compiler_params=_params(("parallel",)),
        name="mixer_merge",
    )(h, mod_l, post_g.reshape(1, d), proj, proj, wb, wo, *branch_args)


def kernel(x, c, ctx, c_ctx, w_mod, b_mod, pre_norm, post_norm, ffn_w_in, ffn_w_out, w_in, diff_lambda,
           diff_norm, na_rpb, ret_decay, ret_norm, w_branch, w_out):
    nb, seq, d = x.shape
    assert (seq, d) == (SEQ, D_MODEL) and ctx.shape == (nb, CTX_LEN, d)
    n_lat = nb * SEQ
    n_all = n_lat + nb * CTX_LEN
    tm = min(512, nb * CTX_LEN)

    h = jnp.concatenate([x.reshape(n_lat, d), ctx.reshape(nb * CTX_LEN, d)], axis=0)
    pad = (-(nb + 1)) % 8
    cc = jnp.concatenate([c, c_ctx[None, :], jnp.zeros((pad, d), c.dtype)], axis=0)
    mods = _mod_call(cc, w_mod, b_mod)
    tabs = _rope_tables()
    w1, w2 = ffn_w_in.astype(BF16), ffn_w_out.astype(BF16)
    w_proj, wb, wo = w_in.astype(BF16), w_branch.astype(BF16), w_out.astype(BF16)

    for l in range(DEPTH):
        with_ctx = l < DEPTH - 1
        lambda_init = 0.8 - 0.6 * math.exp(-0.3 * l)
        nrows_out = n_all if with_ctx else n_lat
        mod_l = mods[l, :nb + 1].reshape(nb + 1, N_MOD, d)
        ffn = functools.partial(_ffn_call, tm=tm, n_lat_seg=nb)

        h = ffn(h, mod_l, 0, pre_norm[l, 0], post_norm[l, 0], w1, w2, (l, 0), n_all)
        proj = _inproj_call(h, mod_l, pre_norm[l, 1], w_proj, l, min(1024, nb * CTX_LEN), nb)
        oa = _diff_call(proj, tabs, diff_lambda[l], diff_norm[l], lambda_init, nb, with_ctx)
        ob = _nbr_call(proj, _nbr_bias_call(na_rpb[l]), nb, with_ctx)
        yr = _ret_call(proj, tabs, ret_decay[l], ret_norm[l], nb, with_ctx)
        h = _merge_call(h, mod_l, post_norm[l, 1], (oa, ob, yr), proj, wb, wo, l, nrows_out, nb)
        h = ffn(h, mod_l, 6, pre_norm[l, 2], post_norm[l, 2], w1, w2, (l, 1), nrows_out)
    return h.reshape(nb, SEQ, d)
```

```python
import functools
import math

import numpy as np
import jax
import jax.numpy as jnp
from jax import lax
from jax.experimental import pallas as pl
from jax.experimental.pallas import tpu as pltpu

D_MODEL = 2048
SEQ = 2048
DEPTH = 2
GRID_W = 64
CTX_LEN = 256
N_MOD = 9
FFN_DIM = 5632
A_HEADS = 8
A_HEAD_DIM = 64
B_HEADS = 8
B_HEAD_DIM = 128
WIN_R = 8
WIN_C = 16
C_HEADS = 8
C_KEY_DIM = 64
BRANCH_W = 1024
ROPE_BASE = 10000.0
EPS = 1e-6

COL_AQ, COL_AK, COL_AV = 0, 1024, 2048
COL_BQ, COL_BK, COL_BV = 3072, 4096, 5120
COL_CQ, COL_CK, COL_CV, COL_CG = 6144, 6656, 7168, 8192
COL_GATES = 9216
IN_COLS = 15360

LANES = 128
BF16_SUBLANES = 16
QBLK = 256
ROWS_PER_QBLK = QBLK // GRID_W
NBR_KROWS = ROWS_PER_QBLK + WIN_R - 1
NBR_KEYS = NBR_KROWS * GRID_W
VMEM_LIMIT = 56 * 1024 * 1024
FFN_TILE = 512
PROJ_TILE = 1024
DIFF_ROW_SPLIT = 2
NBR_MASKED = -1e30

BF16 = jnp.bfloat16
F32 = jnp.float32
NT_DIMS = (((1,), (1,)), ((), ()))
TN_DIMS = (((0,), (0,)), ((), ()))


def _params(sem):
    return pltpu.CompilerParams(dimension_semantics=sem, vmem_limit_bytes=VMEM_LIMIT)


def _rms(x):
    return x * lax.rsqrt(jnp.mean(x * x, axis=-1, keepdims=True) + EPS)


def _silu(x):
    return x * jax.nn.sigmoid(x)


def _row_chunks(row0, n_rows, rows_per, body, inline=False):
    def step(r, carry):
        body(pl.ds(pl.multiple_of(row0 + r * rows_per, rows_per), rows_per))
        return carry
    if inline:
        for r in range(n_rows // rows_per):
            step(r, 0)
    else:
        lax.fori_loop(0, n_rows // rows_per, step, 0, unroll=4)


def _adaln_rows(u_ref, dst, h_ref, mod_ref, pre_ref, k0, seg, row0, n_rows, inline):
    gain = pre_ref[...] * (1.0 + mod_ref[seg, k0 + 1:k0 + 2, :])
    shift = mod_ref[seg, k0:k0 + 1, :]

    def body(rows):
        u_ref[dst, rows, :] = (_rms(h_ref[rows, :]) * gain + shift).astype(u_ref.dtype)

    _row_chunks(row0, n_rows, BF16_SUBLANES, body, inline)


def _mod_kernel(cc_ref, w_ref, b_ref, o_ref):
    s = _silu(cc_ref[...]).astype(BF16)
    o_ref[0] = jnp.dot(s, w_ref[0].astype(BF16), preferred_element_type=F32) + b_ref[0]


def _mod_call(cc, w_mod, b_mod):
    depth, d, n = w_mod.shape
    rows = cc.shape[0]
    tn = 1024
    return pl.pallas_call(
        _mod_kernel,
        grid=(depth, n // tn),
        in_specs=[
            pl.BlockSpec((rows, d), lambda l, j: (0, 0)),
            pl.BlockSpec((1, d, tn), lambda l, j: (l, 0, j)),
            pl.BlockSpec((1, 1, tn), lambda l, j: (l, 0, j)),
        ],
        out_specs=pl.BlockSpec((1, rows, tn), lambda l, j: (l, 0, j)),
        out_shape=jax.ShapeDtypeStruct((depth, rows, n), F32),
        compiler_params=_params(("parallel", "parallel")),
        name="mod_vectors",
    )(cc, w_mod, b_mod.reshape(depth, 1, n))


def _ffn_kernel(h_ref, mod_ref, pre_ref, post_ref, w1_hbm, w2_hbm, o_ref, w1_buf, w2_buf, sem1, sem2,
                u_ref, hm_ref, *, lk, k0, nf, tf, nn, tn, tm, n_lat_seg, n_tiles):
    i = pl.program_id(0)
    seg = jnp.minimum((i * tm) // SEQ, n_lat_seg)
    w1_all = w1_hbm.at[lk[0], lk[1]]
    w2_all = w2_hbm.at[lk[0], lk[1]]

    def w1_copy(j, slot, half):
        col = pl.multiple_of((half * nf + j) * tf, tf)
        return pltpu.make_async_copy(w1_all.at[:, pl.ds(col, tf)], w1_buf.at[slot, half], sem1.at[slot, half])

    def w2_copy(n, slot):
        return pltpu.make_async_copy(w2_all.at[:, n * tn:(n + 1) * tn], w2_buf.at[slot], sem2.at[slot])

    def start_w1(j, slot):
        w1_copy(j, slot, 0).start()
        w1_copy(j, slot, 1).start()

    @pl.when(i == 0)
    def _():
        start_w1(0, 0)

    w2_copy(0, 0).start()
    w2_copy(1, 1).start()
    _adaln_rows(u_ref, 0, h_ref, mod_ref, pre_ref, k0, seg, 0, tm, False)

    def hidden_tile(j, carry):
        slot = j % 2
        w1_copy(j, slot, 0).wait()
        w1_copy(j, slot, 1).wait()

        @pl.when(j + 1 < nf)
        def _():
            start_w1(j + 1, 1 - slot)

        u = u_ref[0]
        a = jnp.dot(u, w1_buf[slot, 0], preferred_element_type=F32)
        b = jnp.dot(u, w1_buf[slot, 1], preferred_element_type=F32)
        hm_ref[j] = (_silu(a) * b).astype(BF16)
        return carry

    lax.fori_loop(0, nf, hidden_tile, 0)

    @pl.when(i + 1 < n_tiles)
    def _():
        start_w1(0, 0)

    for n in range(nn):
        slot = n % 2
        w2_copy(n, slot).wait()
        y = jnp.dot(hm_ref[0], w2_buf[slot, 0:tf, :], preferred_element_type=F32)
        for k in range(1, nf):
            y += jnp.dot(hm_ref[k], w2_buf[slot, k * tf:(k + 1) * tf, :], preferred_element_type=F32)
        o_ref[:, n * tn:(n + 1) * tn] = y
        if n + 2 < nn:
            w2_copy(n + 2, slot).start()

    gate = mod_ref[seg, k0 + 2:k0 + 3, :]
    o_ref[...] = h_ref[...] + 0.5 * gate * (_rms(o_ref[...]) * post_ref[...])


def _ffn_call(h, mod_l, k0, pre_g, post_g, w1, w2, lk, nrows, tm, n_lat_seg):
    d = h.shape[1]
    f = w2.shape[2]
    tf = tn = FFN_TILE
    nf, nn = f // tf, d // tn
    n_tiles = nrows // tm
    return pl.pallas_call(
        functools.partial(_ffn_kernel, lk=lk, k0=k0, nf=nf, tf=tf, nn=nn, tn=tn, tm=tm, n_lat_seg=n_lat_seg,
                          n_tiles=n_tiles),
        grid=(n_tiles,),
        in_specs=[
            pl.BlockSpec((tm, d), lambda i: (i, 0)),
            pl.BlockSpec(mod_l.shape, lambda i: (0, 0, 0)),
            pl.BlockSpec((1, d), lambda i: (0, 0)),
            pl.BlockSpec((1, d), lambda i: (0, 0)),
            pl.BlockSpec(memory_space=pl.ANY),
            pl.BlockSpec(memory_space=pl.ANY),
        ],
        out_specs=pl.BlockSpec((tm, d), lambda i: (i, 0)),
        out_shape=jax.ShapeDtypeStruct((nrows, d), F32),
        scratch_shapes=[
            pltpu.VMEM((2, 2, d, tf), BF16),
            pltpu.VMEM((2, f, tn), BF16),
            pltpu.SemaphoreType.DMA((2, 2)),
            pltpu.SemaphoreType.DMA((2,)),
            pltpu.VMEM((1, tm, d), BF16),
            pltpu.VMEM((nf, tm, tf), BF16),
        ],
        compiler_params=_params(("arbitrary",)),
        name="ffn_sublayer",
    )(h, mod_l, pre_g.reshape(1, d), post_g.reshape(1, d), w1, w2)


def _inproj_kernel(h_ref, mod_ref, pre_ref, w_ref, o_ref, u_ref, *, tm, n_lat_seg, n_steps, n_pre):
    i = pl.program_id(0)
    j = pl.program_id(1)
    slot = i % 2

    def adaln_rows(tile, row0, n_rows, dst, inline):
        seg = jnp.minimum((tile * tm) // SEQ, n_lat_seg)
        _adaln_rows(u_ref, dst, h_ref, mod_ref, pre_ref, 3, seg, row0, n_rows, inline)

    def project():
        o_ref[...] = jnp.dot(u_ref[slot], w_ref[...], preferred_element_type=F32).astype(BF16)

    @pl.when((i == 0) & (j == 0))
    def _():
        adaln_rows(0, 0, tm, 0, False)

    @pl.when(j < n_steps - n_pre)
    def _():
        project()

    @pl.when(j >= n_steps - n_pre)
    def _():
        project()
        adaln_rows(i + 1, (j - (n_steps - n_pre)) * (tm // n_pre), tm // n_pre, 1 - slot, True)


def _inproj_call(h, mod_l, pre_g, w, layer, tm, n_lat_seg):
    nrows, d = h.shape
    n = w.shape[2]
    tn = PROJ_TILE
    n_tiles, n_steps = nrows // tm, n // tn
    n_pre = 8
    h_tile = lambda i, j: (jnp.minimum(i + jnp.where(j >= n_steps - n_pre, 1, 0), n_tiles - 1), 0)
    return pl.pallas_call(
        functools.partial(_inproj_kernel, tm=tm, n_lat_seg=n_lat_seg, n_steps=n_steps, n_pre=n_pre),
        grid=(n_tiles, n_steps),
        in_specs=[
            pl.BlockSpec((tm, d), h_tile),
            pl.BlockSpec(mod_l.shape, lambda i, j: (0, 0, 0)),
            pl.BlockSpec((1, d), lambda i, j: (0, 0)),
            pl.BlockSpec((None, d, tn), lambda i, j: (layer, 0, j)),
        ],
        out_specs=pl.BlockSpec((tm, tn), lambda i, j: (i, j)),
        out_shape=jax.ShapeDtypeStruct((nrows, n), BF16),
        scratch_shapes=[pltpu.VMEM((2, tm, d), BF16)],
        compiler_params=_params(("arbitrary", "arbitrary")),
        name="mixer_in_proj",
    )(h, mod_l, pre_g.reshape(1, d), w)


def _rope(t, cos, sa, sb):
    return t * cos + pltpu.roll(t, LANES - 16, 1) * sa + pltpu.roll(t, 16, 1) * sb


def _rope_tables():
    pos = np.arange(SEQ)
    prow = jnp.asarray(pos // GRID_W, F32)
    pcol = jnp.asarray(pos % GRID_W, F32)
    half = 16
    inv = ROPE_BASE ** (-jnp.arange(half, dtype=F32) / half)
    lane = np.arange(LANES)
    freq = jnp.asarray(lane % half)
    use_col = jnp.asarray((lane % 64) >= 32)
    second = jnp.asarray((lane % 32) >= half)
    p = jnp.where(use_col[None, :], pcol[:, None], prow[:, None])
    ang = p * inv[freq][None, :]
    cos, sin = jnp.cos(ang), jnp.sin(ang)
    sa = jnp.where(second[None, :], 0.0, -sin)
    sb = jnp.where(second[None, :], sin, 0.0)
    return cos, sa, sb


def _mixer_specs(nb, width, cols, order):
    ctx_blk0 = nb * SEQ // CTX_LEN

    def spec(rows, col, w, ctx):
        def index(*g):
            b, h = order(*g)
            return ((ctx_blk0 + b) if ctx else b, col // w + h)
        return pl.BlockSpec((rows, w), index)

    return ([spec(SEQ, c, w, False) for c, w in zip(cols, width)]
            + [spec(CTX_LEN, c, w, True) for c, w in zip(cols, width)])


def _mixer_outs(nb, width, order, with_ctx):
    def index(*g):
        b, h = order(*g)
        return (b, h)
    specs = [pl.BlockSpec((SEQ, width), index)]
    shapes = [jax.ShapeDtypeStruct((nb * SEQ, BRANCH_W), BF16)]
    if with_ctx:
        specs.append(pl.BlockSpec((CTX_LEN, width), index))
        shapes.append(jax.ShapeDtypeStruct((nb * CTX_LEN, BRANCH_W), BF16))
    return specs, shapes


def _diff_kernel(ql_ref, kl_ref, vl_ref, qc_ref, kc_ref, vc_ref, cos_ref, sa_ref, sb_ref, lam_ref, g_ref,
                 *rest, lambda_init, with_ctx):
    ol_ref, oc_ref = (rest[0], rest[1]) if with_ctx else (rest[0], None)
    k_all, v_ones = rest[-2:]
    k_all[0:CTX_LEN, :] = kc_ref[...]
    k_all[CTX_LEN:, :] = _rope(kl_ref[...].astype(F32), cos_ref[...], sa_ref[...], sb_ref[...]).astype(BF16)
    v_ones[0:CTX_LEN, 0:LANES] = vc_ref[...]
    v_ones[CTX_LEN:, 0:LANES] = vl_ref[...]
    v_ones[:, LANES:] = jnp.ones((CTX_LEN + SEQ, LANES), BF16)

    lv = lam_ref[...]
    lam = (jnp.exp(jnp.sum(lv[0:1] * lv[1:2], axis=-1, keepdims=True))
           - jnp.exp(jnp.sum(lv[2:3] * lv[3:4], axis=-1, keepdims=True)) + lambda_init)
    lane = lax.broadcasted_iota(jnp.int32, (1, LANES), 1)

    def attend(q, nk):
        q = q * (A_HEAD_DIM ** -0.5)
        k = k_all[0:nk, :]
        v1 = v_ones[0:nk, :]

        rows = QBLK // DIFF_ROW_SPLIT
        qms = [jnp.where(mask, q[r * rows:(r + 1) * rows], 0.0).astype(BF16)
               for r in range(DIFF_ROW_SPLIT) for mask in (lane < A_HEAD_DIM, lane >= A_HEAD_DIM)]
        ss = [lax.dot_general(qm, k, NT_DIMS, preferred_element_type=F32) for qm in qms]
        es = [jnp.exp(s - jnp.max(s, axis=-1, keepdims=True)).astype(BF16) for s in ss]
        ols = [jnp.dot(e, v1, preferred_element_type=F32) for e in es]
        os = [ol[:, :LANES] / ol[:, LANES:] for ol in ols]
        o = jnp.concatenate([os[2 * r] - lam * os[2 * r + 1] for r in range(DIFF_ROW_SPLIT)], axis=0)
        return (_rms(o) * g_ref[...] * (1.0 - lambda_init)).astype(BF16)

    def block(t, carry):
        rows = pl.ds(pl.multiple_of(t * QBLK, QBLK), QBLK)
        q = _rope(ql_ref[rows, :].astype(F32), cos_ref[rows, :], sa_ref[rows, :], sb_ref[rows, :])
        ol_ref[rows, :] = attend(q, CTX_LEN + SEQ)
        return carry

    lax.fori_loop(0, SEQ // QBLK, block, 0, unroll=True)
    if with_ctx:
        oc_ref[...] = attend(qc_ref[...].astype(F32), CTX_LEN)


def _diff_call(proj, tabs, diff_lambda, diff_norm, lambda_init, nb, with_ctx):
    order = lambda b, h: (b, h)
    whole = lambda b, h: (0, 0)
    out_specs, out_shapes = _mixer_outs(nb, LANES, order, with_ctx)
    return pl.pallas_call(
        functools.partial(_diff_kernel, lambda_init=lambda_init, with_ctx=with_ctx),
        grid=(nb, A_HEADS),
        in_specs=_mixer_specs(nb, (LANES,) * 3, (COL_AQ, COL_AK, COL_AV), order) + [
            pl.BlockSpec((SEQ, LANES), whole),
            pl.BlockSpec((SEQ, LANES), whole),
            pl.BlockSpec((SEQ, LANES), whole),
            pl.BlockSpec((4, A_HEAD_DIM), whole),
            pl.BlockSpec((1, LANES), whole),
        ],
        out_specs=out_specs,
        out_shape=out_shapes,
        scratch_shapes=[pltpu.VMEM((CTX_LEN + SEQ, LANES), BF16), pltpu.VMEM((CTX_LEN + SEQ, 2 * LANES), BF16)],
        compiler_params=_params(("parallel", "parallel")),
        name="diff_attention",
    )(*([proj] * 6), *tabs, diff_lambda, diff_norm.reshape(1, LANES))


def _nbr_block_rows(cfg, qi, kj):
    n_rows = SEQ // GRID_W
    r0 = (0, ROWS_PER_QBLK, n_rows - ROWS_PER_QBLK)[cfg]
    ks = min(max(r0 - WIN_R // 2, 0), n_rows - NBR_KROWS)
    r = r0 + qi
    rs = min(max(r - WIN_R // 2, 0), n_rows - WIN_R)
    kr = ks + kj
    return rs <= kr < rs + WIN_R, kr - r + WIN_R - 1


def _nbr_bias_kernel(rpb_ref, o_ref):
    n_dr, n_dc = 2 * WIN_R - 1, 2 * WIN_C - 1
    base = pl.program_id(0) * (n_dr * n_dc)
    qc = lax.broadcasted_iota(jnp.int32, (GRID_W, LANES), 0)
    lane = lax.broadcasted_iota(jnp.int32, (GRID_W, LANES), 1)
    kc = lane % GRID_W
    dc = kc - qc + (WIN_C - 1)
    cs = jnp.clip(qc - WIN_C // 2, 0, GRID_W - WIN_C)
    col_ok = (kc >= cs) & (kc < cs + WIN_C)
    masked = jnp.full((GRID_W, LANES), NBR_MASKED, F32)

    def row_pattern(dr):
        acc = masked
        for x in range(n_dc):
            acc = jnp.where(dc == x, rpb_ref[base + dr * n_dc + x], acc)
        return jnp.where(col_ok, acc, masked)

    pats = [row_pattern(dr) for dr in range(n_dr)]

    def pattern(cfg, qi, kj):
        if kj >= NBR_KROWS:
            return masked
        ok, dr = _nbr_block_rows(cfg, qi, kj)
        return pats[dr] if ok else masked

    for cfg in range(3):
        for qi in range(ROWS_PER_QBLK):
            rows = slice(qi * GRID_W, (qi + 1) * GRID_W)
            for m in range(pl.cdiv(NBR_KEYS, LANES)):
                tile = jnp.where(lane < GRID_W, pattern(cfg, qi, 2 * m), pattern(cfg, qi, 2 * m + 1))
                width = min(LANES, NBR_KEYS - m * LANES)
                o_ref[0, cfg, rows, m * LANES:m * LANES + width] = tile[:, :width]


def _nbr_bias_call(rpb):
    heads = rpb.shape[0]
    return pl.pallas_call(
        _nbr_bias_kernel,
        grid=(heads,),
        in_specs=[pl.BlockSpec(memory_space=pltpu.SMEM)],
        out_specs=pl.BlockSpec((1, 3, QBLK, NBR_KEYS), lambda h: (h, 0, 0, 0)),
        out_shape=jax.ShapeDtypeStruct((heads, 3, QBLK, NBR_KEYS), F32),
        compiler_params=_params(("parallel",)),
        name="nbr_bias_table",
    )(rpb.astype(F32).reshape(-1))


def _nbr_kernel(ql_ref, kl_ref, vl_ref, qc_ref, kc_ref, vc_ref, bias_ref, *rest, with_ctx):
    ol_ref, oc_ref = (rest[0], rest[1]) if with_ctx else (rest[0], None)
    v1l, v1c = rest[-2:]
    scale = B_HEAD_DIM ** -0.5
    v1l[:, 0:LANES] = vl_ref[...]
    v1l[:, LANES:] = jnp.ones((SEQ, LANES), BF16)
    v1c[:, 0:LANES] = vc_ref[...]
    v1c[:, LANES:] = jnp.ones((CTX_LEN, LANES), BF16)
    kc = kc_ref[...]
    lat_blocks = SEQ // QBLK

    for blk in range(lat_blocks):
        cfg = 0 if blk == 0 else (2 if blk == lat_blocks - 1 else 1)
        start = min(max((blk * ROWS_PER_QBLK - WIN_R // 2) * GRID_W, 0), SEQ - NBR_KEYS)
        q = ql_ref[blk * QBLK:(blk + 1) * QBLK, :]
        s_lat = (lax.dot_general(q, kl_ref[start:start + NBR_KEYS, :], NT_DIMS, preferred_element_type=F32) * scale
                 + bias_ref[0, cfg])
        s_ctx = lax.dot_general(q, kc, NT_DIMS, preferred_element_type=F32) * scale
        m = jnp.maximum(jnp.max(s_lat, axis=-1, keepdims=True), jnp.max(s_ctx, axis=-1, keepdims=True))
        ol = (jnp.dot(jnp.exp(s_lat - m).astype(BF16), v1l[start:start + NBR_KEYS, :], preferred_element_type=F32)
              + jnp.dot(jnp.exp(s_ctx - m).astype(BF16), v1c[...], preferred_element_type=F32))
        ol_ref[blk * QBLK:(blk + 1) * QBLK, :] = (ol[:, :LANES] / ol[:, LANES:]).astype(BF16)

    if with_ctx:
        s = lax.dot_general(qc_ref[...], kc, NT_DIMS, preferred_element_type=F32) * scale
        e = jnp.exp(s - jnp.max(s, axis=-1, keepdims=True)).astype(BF16)
        ol = jnp.dot(e, v1c[...], preferred_element_type=F32)
        oc_ref[...] = (ol[:, :LANES] / ol[:, LANES:]).astype(BF16)


def _nbr_call(proj, bias_tab, nb, with_ctx):
    order = lambda h, b: (b, h)
    out_specs, out_shapes = _mixer_outs(nb, LANES, order, with_ctx)
    return pl.pallas_call(
        functools.partial(_nbr_kernel, with_ctx=with_ctx),
        grid=(B_HEADS, nb),
        in_specs=_mixer_specs(nb, (LANES,) * 3, (COL_BQ, COL_BK, COL_BV), order) + [
            pl.BlockSpec((1, 3, QBLK, NBR_KEYS), lambda h, b: (h, 0, 0, 0)),
        ],
        out_specs=out_specs,
        out_shape=out_shapes,
        scratch_shapes=[pltpu.VMEM((SEQ, 2 * LANES), BF16), pltpu.VMEM((CTX_LEN, 2 * LANES), BF16)],
        compiler_params=_params(("parallel", "parallel")),
        name="nbr_attention",
    )(*([proj] * 6), bias_tab)


def _ret_kernel(ql_ref, kl_ref, vl_ref, gl_ref, qc_ref, kc_ref, vc_ref, gc_ref, cos_ref, sa_ref, sb_ref,
                dec_ref, rn_ref, *rest, with_ctx):
    ol_ref, oc_ref = (rest[0], rest[1]) if with_ctx else (rest[0], None)
    of_ref, ob_ref, st_ref = rest[-3:]
    ch = QBLK
    n_chunks = SEQ // ch
    kscale = C_KEY_DIM ** -0.5

    def run():
        ii = lax.broadcasted_iota(jnp.int32, (ch, ch), 0)
        jj = lax.broadcasted_iota(jnp.int32, (ch, ch), 1)
        dist = (ii - jj).astype(F32)
        pos = lax.broadcasted_iota(jnp.int32, (ch, 1), 0).astype(F32)
        lane = lax.broadcasted_iota(jnp.int32, (1, LANES), 1)
        rn = rn_ref[...]

        def log_sigmoid(x):
            return -(jnp.log1p(jnp.exp(-jnp.abs(x))) + jnp.maximum(-x, 0.0))

        heads = []
        for e in range(2):
            lgf = log_sigmoid(dec_ref[0, 0, e:e + 1, 0:1])
            lgb = log_sigmoid(dec_ref[0, 1, e:e + 1, 0:1])
            heads.append(dict(
                mask=(lane >= e * C_KEY_DIM) & (lane < (e + 1) * C_KEY_DIM),
                intra_f=jnp.where(dist >= 0, jnp.exp(lgf * jnp.maximum(dist, 0.0)), 0.0),
                intra_b=jnp.where(dist <= 0, jnp.exp(lgb * jnp.maximum(-dist, 0.0)), 0.0),
                qdec_f=jnp.exp(lgf * (pos + 1.0)), kdec_f=jnp.exp(lgf * (ch - 1.0 - pos)),
                cdec_f=jnp.exp(lgf * ch),
                qdec_b=jnp.exp(lgb * (ch - pos)), kdec_b=jnp.exp(lgb * pos),
                cdec_b=jnp.exp(lgb * ch),
                vs=slice(e * LANES, (e + 1) * LANES),
            ))

        def finish(o, g):
            return (_silu(g.astype(F32)) * (_rms(o) * rn)).astype(BF16)

        qc = qc_ref[...].astype(F32)
        kc = kc_ref[...].astype(F32) * kscale
        for e, hd in enumerate(heads):
            v = vc_ref[:, hd["vs"]]
            st_ref[e] = lax.dot_general((kc * hd["kdec_f"]).astype(BF16), v, TN_DIMS, preferred_element_type=F32)
            st_ref[2 + e] = lax.dot_general((kc * hd["kdec_b"]).astype(BF16), v, TN_DIMS,
                                            preferred_element_type=F32)
            if with_ctx:
                qe = jnp.where(hd["mask"], qc, 0.0).astype(BF16)
                att = lax.dot_general(qe, kc.astype(BF16), NT_DIMS, preferred_element_type=F32)
                att = att * (hd["intra_f"] + hd["intra_b"])
                o = jnp.dot(att.astype(BF16), v, preferred_element_type=F32)
                oc_ref[:, hd["vs"]] = finish(o, gc_ref[:, hd["vs"]])

        def chunk_rows(c):
            return pl.ds(pl.multiple_of(c * ch, ch), ch)

        def sweep(c, fwd):
            rows = chunk_rows(c)
            tabs = (cos_ref[rows, :], sa_ref[rows, :], sb_ref[rows, :])
            q = _rope(ql_ref[rows, :].astype(F32), *tabs)
            k = _rope(kl_ref[rows, :].astype(F32), *tabs) * kscale
            kb = k.astype(BF16)
            for e, hd in enumerate(heads):
                d = "f" if fwd else "b"
                si = e if fwd else 2 + e
                v = vl_ref[rows, hd["vs"]]
                qe = jnp.where(hd["mask"], q, 0.0)
                att = lax.dot_general(qe.astype(BF16), kb, NT_DIMS, preferred_element_type=F32) * hd["intra_" + d]
                st = st_ref[si]
                o = (jnp.dot(att.astype(BF16), v, preferred_element_type=F32)
                     + jnp.dot((qe * hd["qdec_" + d]).astype(BF16), st.astype(BF16), preferred_element_type=F32))
                st_ref[si] = st * hd["cdec_" + d] + lax.dot_general(
                    (k * hd["kdec_" + d]).astype(BF16), v, TN_DIMS, preferred_element_type=F32)
                (of_ref if fwd else ob_ref)[rows, hd["vs"]] = o

        def finish_chunk(c):
            rows = chunk_rows(c)
            for hd in heads:
                vs = hd["vs"]
                ol_ref[rows, vs] = finish(of_ref[rows, vs] + ob_ref[rows, vs], gl_ref[rows, vs])

        def first_half(t, carry):
            sweep(t, True)
            sweep(n_chunks - 1 - t, False)
            return carry

        def second_half(t, carry):
            sweep(t, True)
            sweep(n_chunks - 1 - t, False)
            finish_chunk(t)
            finish_chunk(n_chunks - 1 - t)
            return carry

        lax.fori_loop(0, n_chunks // 2, first_half, 0)
        lax.fori_loop(n_chunks // 2, n_chunks, second_half, 0)

    run()


def _ret_call(proj, tabs, ret_decay, ret_norm, nb, with_ctx):
    pairs = C_HEADS // 2
    w2 = 2 * LANES
    order = lambda b, p: (b, p)
    whole = lambda b, p: (0, 0)
    out_specs, out_shapes = _mixer_outs(nb, w2, order, with_ctx)
    dec = jnp.broadcast_to(ret_decay.astype(F32).reshape(2, pairs, 2).transpose(1, 0, 2)[..., None],
                           (pairs, 2, 2, LANES))
    return pl.pallas_call(
        functools.partial(_ret_kernel, with_ctx=with_ctx),
        grid=(nb, pairs),
        in_specs=_mixer_specs(nb, (LANES, LANES, w2, w2), (COL_CQ, COL_CK, COL_CV, COL_CG), order) + [
            pl.BlockSpec((SEQ, LANES), whole),
            pl.BlockSpec((SEQ, LANES), whole),
            pl.BlockSpec((SEQ, LANES), whole),
            pl.BlockSpec((1, 2, 2, LANES), lambda b, p: (p, 0, 0, 0)),
            pl.BlockSpec((1, LANES), whole),
        ],
        out_specs=out_specs,
        out_shape=out_shapes,
        scratch_shapes=[
            pltpu.VMEM((SEQ, w2), F32),
            pltpu.VMEM((SEQ, w2), F32),
            pltpu.VMEM((4, LANES, LANES), F32),
        ],
        compiler_params=_params(("parallel", "parallel")),
        name="retention",
    )(*([proj] * 8), *tabs, dec, ret_norm.reshape(1, LANES))


def _merge_kernel(h_ref, mod_ref, post_ref, g_lo_ref, g_hi_ref, wb_ref, wo_ref, *rest, n_lat_tiles, with_ctx):
    d = h_ref.shape[1]
    o_ref = rest[-1]
    if with_ctx:
        is_lat = pl.program_id(0) < n_lat_tiles
        xs = [jnp.where(is_lat, rest[2 * k][...], rest[2 * k + 1][...]) for k in range(3)]
    else:
        xs = [rest[k][...] for k in range(3)]
    g_lo = g_lo_ref[...]
    g_hi = g_hi_ref[...]
    gates = (g_lo[:, :d], jnp.concatenate([g_lo[:, d:], g_hi[:, :d // 2]], axis=1), g_hi[:, d // 2:])
    m = None
    for k in range(3):
        t = jax.nn.sigmoid(gates[k].astype(F32)) * jnp.dot(xs[k], wb_ref[k], preferred_element_type=F32)
        m = t if m is None else m + t
    y = jnp.dot(m.astype(BF16), wo_ref[...], preferred_element_type=F32)
    o_ref[...] = h_ref[...] + mod_ref[0, 5:6, :] * (_rms(y) * post_ref[...])


def _merge_call(h, mod_l, post_g, branches, proj, wb, wo, layer, nrows, n_lat_seg):
    d = h.shape[1]
    tm = CTX_LEN
    with_ctx = len(branches[0]) == 2
    n_lat_tiles = n_lat_seg * SEQ // tm
    gw = 3 * d // 2
    g0 = COL_GATES // gw
    resident = dict(pipeline_mode=pl.Buffered(1))
    row = lambda i: (i, 0)
    lat_row = lambda i: (jnp.minimum(i, n_lat_tiles - 1), 0)
    ctx_row = lambda i: (jnp.maximum(i - n_lat_tiles, 0), 0)
    branch_specs, branch_args = [], []
    for outs in branches:
        branch_specs.append(pl.BlockSpec((tm, BRANCH_W), lat_row))
        if with_ctx:
            branch_specs.append(pl.BlockSpec((tm, BRANCH_W), ctx_row))
        branch_args.extend(outs)
    return pl.pallas_call(
        functools.partial(_merge_kernel, n_lat_tiles=n_lat_tiles, with_ctx=with_ctx),
        grid=(nrows // tm,),
        in_specs=[
            pl.BlockSpec((tm, d), row),
            pl.BlockSpec((1, N_MOD, d), lambda i: (jnp.minimum((i * tm) // SEQ, n_lat_seg), 0, 0)),
            pl.BlockSpec((1, d), lambda i: (0, 0)),
            pl.BlockSpec((tm, gw), lambda i: (i, g0)),
            pl.BlockSpec((tm, gw), lambda i: (i, g0 + 1)),
            pl.BlockSpec((None, 3, BRANCH_W, d), lambda i: (layer, 0, 0, 0), **resident),
            pl.BlockSpec((None, d, d), lambda i: (layer, 0, 0), **resident),
        ] + branch_specs,
        out_specs=pl.BlockSpec((tm, d), row),
        out_shape=jax.ShapeDtypeStruct((nrows, d), F32),
        compiler_params=_params(("parallel",)),
        name="mixer_merge",
    )(h, mod_l, post_g.reshape(1, d), proj, proj, wb, wo, *branch_args)


def kernel(x, c, ctx, c_ctx, w_mod, b_mod, pre_norm, post_norm, ffn_w_in, ffn_w_out, w_in, diff_lambda,
           diff_norm, na_rpb, ret_decay, ret_norm, w_branch, w_out):
    nb, seq, d = x.shape
    assert (seq, d) == (SEQ, D_MODEL) and ctx.shape == (nb, CTX_LEN, d)
    n_lat = nb * SEQ
    n_all = n_lat + nb * CTX_LEN
    tm = min(512, nb * CTX_LEN)

    h = jnp.concatenate([x.reshape(n_lat, d), ctx.reshape(nb * CTX_LEN, d)], axis=0)
    pad = (-(nb + 1)) % 8
    cc = jnp.concatenate([c, c_ctx[None, :], jnp.zeros((pad, d), c.dtype)], axis=0)
    mods = _mod_call(cc, w_mod, b_mod)
    tabs = _rope_tables()
    w1, w2 = ffn_w_in.astype(BF16), ffn_w_out.astype(BF16)
    w_proj, wb, wo = w_in.astype(BF16), w_branch.astype(BF16), w_out.astype(BF16)

    for l in range(DEPTH):
        with_ctx = l < DEPTH - 1
        lambda_init = 0.8 - 0.6 * math.exp(-0.3 * l)
        nrows_out = n_all if with_ctx else n_lat
        mod_l = mods[l, :nb + 1].reshape(nb + 1, N_MOD, d)
        ffn = functools.partial(_ffn_call, tm=tm, n_lat_seg=nb)

        h = ffn(h, mod_l, 0, pre_norm[l, 0], post_norm[l, 0], w1, w2, (l, 0), n_all)
        proj = _inproj_call(h, mod_l, pre_norm[l, 1], w_proj, l, min(1024, nb * CTX_LEN), nb)
        oa = _diff_call(proj, tabs, diff_lambda[l], diff_norm[l], lambda_init, nb, with_ctx)
        ob = _nbr_call(proj, _nbr_bias_call(na_rpb[l]), nb, with_ctx)
        yr = _ret_call(proj, tabs, ret_decay[l], ret_norm[l], nb, with_ctx)
        h = _merge_call(h, mod_l, post_norm[l, 1], (oa, ob, yr), proj, wb, wo, l, nrows_out, nb)
        h = ffn(h, mod_l, 6, pre_norm[l, 2], post_norm[l, 2], w1, w2, (l, 1), nrows_out)
    return h.reshape(nb, SEQ, d)
```

```python
import functools
import math

import numpy as np
import jax
import jax.numpy as jnp
from jax import lax
from jax.experimental import pallas as pl
from jax.experimental.pallas import tpu as pltpu

D_MODEL = 2048
SEQ = 2048
DEPTH = 2
GRID_W = 64
CTX_LEN = 256
N_MOD = 9
FFN_DIM = 5632
A_HEADS = 8
A_HEAD_DIM = 64
B_HEADS = 8
B_HEAD_DIM = 128
WIN_R = 8
WIN_C = 16
C_HEADS = 8
C_KEY_DIM = 64
BRANCH_W = 1024
ROPE_BASE = 10000.0
EPS = 1e-6

COL_AQ, COL_AK, COL_AV = 0, 1024, 2048
COL_BQ, COL_BK, COL_BV = 3072, 4096, 5120
COL_CQ, COL_CK, COL_CV, COL_CG = 6144, 6656, 7168, 8192
COL_GATES = 9216
IN_COLS = 15360

LANES = 128
BF16_SUBLANES = 16
QBLK = 256
ROWS_PER_QBLK = QBLK // GRID_W
NBR_KROWS = ROWS_PER_QBLK + WIN_R - 1
NBR_KEYS = NBR_KROWS * GRID_W
VMEM_LIMIT = 56 * 1024 * 1024
FFN_TILE = 512
FFN_STAGE_ROWS = 64
MXU_COLS = 256
PROJ_TILE = 1024
DIFF_ROW_SPLIT = 2
NBR_MASKED = -1e30

BF16 = jnp.bfloat16
F32 = jnp.float32
NT_DIMS = (((1,), (1,)), ((), ()))
TN_DIMS = (((0,), (0,)), ((), ()))


def _params(sem):
    return pltpu.CompilerParams(dimension_semantics=sem, vmem_limit_bytes=VMEM_LIMIT)


def _rms(x):
    return x * lax.rsqrt(jnp.mean(x * x, axis=-1, keepdims=True) + EPS)


def _silu(x):
    return x * jax.nn.sigmoid(x)


def _row_chunks(row0, n_rows, rows_per, body, inline=False):
    def step(r, carry):
        body(pl.ds(pl.multiple_of(row0 + r * rows_per, rows_per), rows_per))
        return carry
    if inline:
        for r in range(n_rows // rows_per):
            step(r, 0)
    else:
        lax.fori_loop(0, n_rows // rows_per, step, 0, unroll=4)


def _adaln_rows(u_ref, dst, h_ref, mod_ref, pre_ref, k0, seg, row0, n_rows, inline):
    gain = pre_ref[...] * (1.0 + mod_ref[seg, k0 + 1:k0 + 2, :])
    shift = mod_ref[seg, k0:k0 + 1, :]

    def body(rows):
        u_ref[dst, rows, :] = (_rms(h_ref[rows, :]) * gain + shift).astype(u_ref.dtype)

    _row_chunks(row0, n_rows, BF16_SUBLANES, body, inline)


def _mod_kernel(cc_ref, w_ref, b_ref, o_ref):
    s = _silu(cc_ref[...]).astype(BF16)
    o_ref[0] = jnp.dot(s, w_ref[0].astype(BF16), preferred_element_type=F32) + b_ref[0]


def _mod_call(cc, w_mod, b_mod):
    depth, d, n = w_mod.shape
    rows = cc.shape[0]
    tn = 1024
    return pl.pallas_call(
        _mod_kernel,
        grid=(depth, n // tn),
        in_specs=[
            pl.BlockSpec((rows, d), lambda l, j: (0, 0)),
            pl.BlockSpec((1, d, tn), lambda l, j: (l, 0, j)),
            pl.BlockSpec((1, 1, tn), lambda l, j: (l, 0, j)),
        ],
        out_specs=pl.BlockSpec((1, rows, tn), lambda l, j: (l, 0, j)),
        out_shape=jax.ShapeDtypeStruct((depth, rows, n), F32),
        compiler_params=_params(("parallel", "parallel")),
        name="mod_vectors",
    )(cc, w_mod, b_mod.reshape(depth, 1, n))


def _ffn_kernel(h_ref, mod_ref, pre_ref, post_ref, w1a_ref, w1b_ref, w2_ref, o_ref, u_ref, hm_ref, y_ref, *,
                k0, nf, tf, nn, tn, tm, n_lat_seg):
    i = pl.program_id(0)
    j = pl.program_id(1)
    slot = i % 2
    d = h_ref.shape[1]

    def adaln_rows(tile, row0, n_rows, dst, inline):
        seg = jnp.minimum((tile * tm) // SEQ, n_lat_seg)
        _adaln_rows(u_ref, dst, h_ref, mod_ref, pre_ref, k0, seg, row0, n_rows, inline)

    @pl.when((i == 0) & (j == 0))
    def _():
        adaln_rows(0, 0, tm, 0, False)

    @pl.when(j < nf)
    def _():
        u = u_ref[slot]
        for c in range(tf // MXU_COLS):
            cols = slice(c * MXU_COLS, (c + 1) * MXU_COLS)
            a = jnp.dot(u, w1a_ref[:, cols], preferred_element_type=F32)
            b = jnp.dot(u, w1b_ref[:, cols], preferred_element_type=F32)
            hm_ref[j, :, cols] = (_silu(a) * b).astype(BF16)
        n_copy = tm // FFN_STAGE_ROWS
        rows = pl.ds(pl.multiple_of(jnp.minimum(j, n_copy - 1) * FFN_STAGE_ROWS, FFN_STAGE_ROWS), FFN_STAGE_ROWS)
        o_ref[rows, :] = h_ref[rows, :]

    @pl.when(j >= nf)
    def _():
        n = j - nf
        y = jnp.dot(hm_ref[0], w2_ref[0:tf, :], preferred_element_type=F32)
        for k in range(1, nf):
            y += jnp.dot(hm_ref[k], w2_ref[k * tf:(k + 1) * tf, :], preferred_element_type=F32)
        y_ref[n] = y
        adaln_rows(i + 1, n * (tm // nn), tm // nn, 1 - slot, True)

    @pl.when(j == nf + nn - 1)
    def _():
        seg = jnp.minimum((i * tm) // SEQ, n_lat_seg)
        gain = 0.5 * mod_ref[seg, k0 + 2:k0 + 3, :] * post_ref[...]
        ss = sum(jnp.sum(y_ref[n] * y_ref[n], axis=-1, keepdims=True) for n in range(nn))
        r = lax.rsqrt(ss / d + EPS)
        for n in range(nn):
            cols = slice(n * tn, (n + 1) * tn)
            o_ref[:, cols] = o_ref[:, cols] + (y_ref[n] * r) * gain[:, cols]


def _ffn_call(h, mod_l, k0, pre_g, post_g, w1, w2, lk, nrows, tm, n_lat_seg):
    d = h.shape[1]
    f = w2.shape[2]
    tf = tn = FFN_TILE
    nf, nn = f // tf, d // tn
    n_tiles = nrows // tm
    h_tile = lambda i, j: (jnp.minimum(i + jnp.where(j >= nf, 1, 0), n_tiles - 1), 0)
    w2_tile = lambda j: jnp.where(j < nf, nn - 1, j - nf)
    return pl.pallas_call(
        functools.partial(_ffn_kernel, k0=k0, nf=nf, tf=tf, nn=nn, tn=tn, tm=tm, n_lat_seg=n_lat_seg),
        grid=(n_tiles, nf + nn),
        in_specs=[
            pl.BlockSpec((tm, d), h_tile),
            pl.BlockSpec(mod_l.shape, lambda i, j: (0, 0, 0)),
            pl.BlockSpec((1, d), lambda i, j: (0, 0)),
            pl.BlockSpec((1, d), lambda i, j: (0, 0)),
            pl.BlockSpec((None, None, d, tf), lambda i, j: (*lk, 0, jnp.minimum(j, nf - 1))),
            pl.BlockSpec((None, None, d, tf), lambda i, j: (*lk, 0, jnp.minimum(j, nf - 1) + nf)),
            pl.BlockSpec((None, None, f, tn), lambda i, j: (*lk, 0, w2_tile(j))),
        ],
        out_specs=pl.BlockSpec((tm, d), lambda i, j: (i, 0)),
        out_shape=jax.ShapeDtypeStruct((nrows, d), F32),
        scratch_shapes=[pltpu.VMEM((2, tm, d), BF16), pltpu.VMEM((nf, tm, tf), BF16),
                        pltpu.VMEM((nn, tm, tn), F32)],
        compiler_params=_params(("arbitrary", "arbitrary")),
        name="ffn_sublayer",
    )(h, mod_l, pre_g.reshape(1, d), post_g.reshape(1, d), w1, w1, w2)


def _inproj_kernel(h_ref, mod_ref, pre_ref, w_ref, o_ref, u_ref, *, tm, n_lat_seg, n_steps, n_pre):
    i = pl.program_id(0)
    j = pl.program_id(1)
    slot = i % 2

    def adaln_rows(tile, row0, n_rows, dst, inline):
        seg = jnp.minimum((tile * tm) // SEQ, n_lat_seg)
        _adaln_rows(u_ref, dst, h_ref, mod_ref, pre_ref, 3, seg, row0, n_rows, inline)

    def project():
        o_ref[...] = jnp.dot(u_ref[slot], w_ref[...], preferred_element_type=F32).astype(BF16)

    @pl.when((i == 0) & (j == 0))
    def _():
        adaln_rows(0, 0, tm, 0, False)

    @pl.when(j < n_steps - n_pre)
    def _():
        project()

    @pl.when(j >= n_steps - n_pre)
    def _():
        project()
        adaln_rows(i + 1, (j - (n_steps - n_pre)) * (tm // n_pre), tm // n_pre, 1 - slot, True)


def _inproj_call(h, mod_l, pre_g, w, layer, tm, n_lat_seg):
    nrows, d = h.shape
    n = w.shape[2]
    tn = PROJ_TILE
    n_tiles, n_steps = nrows // tm, n // tn
    n_pre = 8
    h_tile = lambda i, j: (jnp.minimum(i + jnp.where(j >= n_steps - n_pre, 1, 0), n_tiles - 1), 0)
    return pl.pallas_call(
        functools.partial(_inproj_kernel, tm=tm, n_lat_seg=n_lat_seg, n_steps=n_steps, n_pre=n_pre),
        grid=(n_tiles, n_steps),
        in_specs=[
            pl.BlockSpec((tm, d), h_tile),
            pl.BlockSpec(mod_l.shape, lambda i, j: (0, 0, 0)),
            pl.BlockSpec((1, d), lambda i, j: (0, 0)),
            pl.BlockSpec((None, d, tn), lambda i, j: (layer, 0, j)),
        ],
        out_specs=pl.BlockSpec((tm, tn), lambda i, j: (i, j)),
        out_shape=jax.ShapeDtypeStruct((nrows, n), BF16),
        scratch_shapes=[pltpu.VMEM((2, tm, d), BF16)],
        compiler_params=_params(("arbitrary", "arbitrary")),
        name="mixer_in_proj",
    )(h, mod_l, pre_g.reshape(1, d), w)


def _rope(t, cos, sa, sb):
    return t * cos + pltpu.roll(t, LANES - 16, 1) * sa + pltpu.roll(t, 16, 1) * sb


def _rope_tables():
    pos = np.arange(SEQ)
    prow = jnp.asarray(pos // GRID_W, F32)
    pcol = jnp.asarray(pos % GRID_W, F32)
    half = 16
    inv = ROPE_BASE ** (-jnp.arange(half, dtype=F32) / half)
    lane = np.arange(LANES)
    freq = jnp.asarray(lane % half)
    use_col = jnp.asarray((lane % 64) >= 32)
    second = jnp.asarray((lane % 32) >= half)
    p = jnp.where(use_col[None, :], pcol[:, None], prow[:, None])
    ang = p * inv[freq][None, :]
    cos, sin = jnp.cos(ang), jnp.sin(ang)
    sa = jnp.where(second[None, :], 0.0, -sin)
    sb = jnp.where(second[None, :], sin, 0.0)
    return cos, sa, sb


def _mixer_specs(nb, width, cols, order):
    ctx_blk0 = nb * SEQ // CTX_LEN

    def spec(rows, col, w, ctx):
        def index(*g):
            b, h = order(*g)
            return ((ctx_blk0 + b) if ctx else b, col // w + h)
        return pl.BlockSpec((rows, w), index)

    return ([spec(SEQ, c, w, False) for c, w in zip(cols, width)]
            + [spec(CTX_LEN, c, w, True) for c, w in zip(cols, width)])


def _mixer_outs(nb, width, order, with_ctx):
    def index(*g):
        b, h = order(*g)
        return (b, h)
    specs = [pl.BlockSpec((SEQ, width), index)]
    shapes = [jax.ShapeDtypeStruct((nb * SEQ, BRANCH_W), BF16)]
    if with_ctx:
        specs.append(pl.BlockSpec((CTX_LEN, width), index))
        shapes.append(jax.ShapeDtypeStruct((nb * CTX_LEN, BRANCH_W), BF16))
    return specs, shapes


def _diff_kernel(ql_ref, kl_ref, vl_ref, qc_ref, kc_ref, vc_ref, cos_ref, sa_ref, sb_ref, lam_ref, g_ref,
                 *rest, lambda_init, with_ctx):
    ol_ref, oc_ref = (rest[0], rest[1]) if with_ctx else (rest[0], None)
    k_all, v_ones = rest[-2:]
    k_all[0:CTX_LEN, :] = kc_ref[...]
    k_all[CTX_LEN:, :] = _rope(kl_ref[...].astype(F32), cos_ref[...], sa_ref[...], sb_ref[...]).astype(BF16)
    v_ones[0:CTX_LEN, 0:LANES] = vc_ref[...]
    v_ones[CTX_LEN:, 0:LANES] = vl_ref[...]
    v_ones[:, LANES:] = jnp.ones((CTX_LEN + SEQ, LANES), BF16)

    lv = lam_ref[...]
    lam = (jnp.exp(jnp.sum(lv[0:1] * lv[1:2], axis=-1, keepdims=True))
           - jnp.exp(jnp.sum(lv[2:3] * lv[3:4], axis=-1, keepdims=True)) + lambda_init)
    lane = lax.broadcasted_iota(jnp.int32, (1, LANES), 1)

    def attend(q, nk):
        q = q * (A_HEAD_DIM ** -0.5)
        k = k_all[0:nk, :]
        v1 = v_ones[0:nk, :]

        rows = QBLK // DIFF_ROW_SPLIT
        qms = [jnp.where(mask, q[r * rows:(r + 1) * rows], 0.0).astype(BF16)
               for r in range(DIFF_ROW_SPLIT) for mask in (lane < A_HEAD_DIM, lane >= A_HEAD_DIM)]
        ss = [lax.dot_general(qm, k, NT_DIMS, preferred_element_type=F32) for qm in qms]
        es = [jnp.exp(s - jnp.max(s, axis=-1, keepdims=True)).astype(BF16) for s in ss]
        ols = [jnp.dot(e, v1, preferred_element_type=F32) for e in es]
        os = [ol[:, :LANES] / ol[:, LANES:] for ol in ols]
        o = jnp.concatenate([os[2 * r] - lam * os[2 * r + 1] for r in range(DIFF_ROW_SPLIT)], axis=0)
        return (_rms(o) * g_ref[...] * (1.0 - lambda_init)).astype(BF16)

    def block(t, carry):
        rows = pl.ds(pl.multiple_of(t * QBLK, QBLK), QBLK)
        q = _rope(ql_ref[rows, :].astype(F32), cos_ref[rows, :], sa_ref[rows, :], sb_ref[rows, :])
        ol_ref[rows, :] = attend(q, CTX_LEN + SEQ)
        return carry

    lax.fori_loop(0, SEQ // QBLK, block, 0, unroll=True)
    if with_ctx:
        oc_ref[...] = attend(qc_ref[...].astype(F32), CTX_LEN)


def _diff_call(proj, tabs, diff_lambda, diff_norm, lambda_init, nb, with_ctx):
    order = lambda b, h: (b, h)
    whole = lambda b, h: (0, 0)
    out_specs, out_shapes = _mixer_outs(nb, LANES, order, with_ctx)
    return pl.pallas_call(
        functools.partial(_diff_kernel, lambda_init=lambda_init, with_ctx=with_ctx),
        grid=(nb, A_HEADS),
        in_specs=_mixer_specs(nb, (LANES,) * 3, (COL_AQ, COL_AK, COL_AV), order) + [
            pl.BlockSpec((SEQ, LANES), whole),
            pl.BlockSpec((SEQ, LANES), whole),
            pl.BlockSpec((SEQ, LANES), whole),
            pl.BlockSpec((4, A_HEAD_DIM), whole),
            pl.BlockSpec((1, LANES), whole),
        ],
        out_specs=out_specs,
        out_shape=out_shapes,
        scratch_shapes=[pltpu.VMEM((CTX_LEN + SEQ, LANES), BF16), pltpu.VMEM((CTX_LEN + SEQ, 2 * LANES), BF16)],
        compiler_params=_params(("parallel", "parallel")),
        name="diff_attention",
    )(*([proj] * 6), *tabs, diff_lambda, diff_norm.reshape(1, LANES))


def _nbr_block_rows(cfg, qi, kj):
    n_rows = SEQ // GRID_W
    r0 = (0, ROWS_PER_QBLK, n_rows - ROWS_PER_QBLK)[cfg]
    ks = min(max(r0 - WIN_R // 2, 0), n_rows - NBR_KROWS)
    r = r0 + qi
    rs = min(max(r - WIN_R // 2, 0), n_rows - WIN_R)
    kr = ks + kj
    return rs <= kr < rs + WIN_R, kr - r + WIN_R - 1


def _nbr_bias_kernel(rpb_ref, o_ref):
    n_dr, n_dc = 2 * WIN_R - 1, 2 * WIN_C - 1
    base = pl.program_id(0) * (n_dr * n_dc)
    qc = lax.broadcasted_iota(jnp.int32, (GRID_W, LANES), 0)
    lane = lax.broadcasted_iota(jnp.int32, (GRID_W, LANES), 1)
    kc = lane % GRID_W
    dc = kc - qc + (WIN_C - 1)
    cs = jnp.clip(qc - WIN_C // 2, 0, GRID_W - WIN_C)
    col_ok = (kc >= cs) & (kc < cs + WIN_C)
    masked = jnp.full((GRID_W, LANES), NBR_MASKED, F32)

    def row_pattern(dr):
        acc = masked
        for x in range(n_dc):
            acc = jnp.where(dc == x, rpb_ref[base + dr * n_dc + x], acc)
        return jnp.where(col_ok, acc, masked)

    pats = [row_pattern(dr) for dr in range(n_dr)]

    def pattern(cfg, qi, kj):
        if kj >= NBR_KROWS:
            return masked
        ok, dr = _nbr_block_rows(cfg, qi, kj)
        return pats[dr] if ok else masked

    for cfg in range(3):
        for qi in range(ROWS_PER_QBLK):
            rows = slice(qi * GRID_W, (qi + 1) * GRID_W)
            for m in range(pl.cdiv(NBR_KEYS, LANES)):
                tile = jnp.where(lane < GRID_W, pattern(cfg, qi, 2 * m), pattern(cfg, qi, 2 * m + 1))
                width = min(LANES, NBR_KEYS - m * LANES)
                o_ref[0, cfg, rows, m * LANES:m * LANES + width] = tile[:, :width]


def _nbr_bias_call(rpb):
    heads = rpb.shape[0]
    return pl.pallas_call(
        _nbr_bias_kernel,
        grid=(heads,),
        in_specs=[pl.BlockSpec(memory_space=pltpu.SMEM)],
        out_specs=pl.BlockSpec((1, 3, QBLK, NBR_KEYS), lambda h: (h, 0, 0, 0)),
        out_shape=jax.ShapeDtypeStruct((heads, 3, QBLK, NBR_KEYS), F32),
        compiler_params=_params(("parallel",)),
        name="nbr_bias_table",
    )(rpb.astype(F32).reshape(-1))


def _nbr_kernel(ql_ref, kl_ref, vl_ref, qc_ref, kc_ref, vc_ref, bias_ref, *rest, with_ctx):
    ol_ref, oc_ref = (rest[0], rest[1]) if with_ctx else (rest[0], None)
    v1l, v1c = rest[-2:]
    scale = B_HEAD_DIM ** -0.5
    v1l[:, 0:LANES] = vl_ref[...]
    v1l[:, LANES:] = jnp.ones((SEQ, LANES), BF16)
    v1c[:, 0:LANES] = vc_ref[...]
    v1c[:, LANES:] = jnp.ones((CTX_LEN, LANES), BF16)
    kc = kc_ref[...]
    lat_blocks = SEQ // QBLK

    for blk in range(lat_blocks):
        cfg = 0 if blk == 0 else (2 if blk == lat_blocks - 1 else 1)
        start = min(max((blk * ROWS_PER_QBLK - WIN_R // 2) * GRID_W, 0), SEQ - NBR_KEYS)
        q = ql_ref[blk * QBLK:(blk + 1) * QBLK, :]
        s_lat = (lax.dot_general(q, kl_ref[start:start + NBR_KEYS, :], NT_DIMS, preferred_element_type=F32) * scale
                 + bias_ref[0, cfg])
        s_ctx = lax.dot_general(q, kc, NT_DIMS, preferred_element_type=F32) * scale
        m = jnp.maximum(jnp.max(s_lat, axis=-1, keepdims=True), jnp.max(s_ctx, axis=-1, keepdims=True))
        ol = (jnp.dot(jnp.exp(s_lat - m).astype(BF16), v1l[start:start + NBR_KEYS, :], preferred_element_type=F32)
              + jnp.dot(jnp.exp(s_ctx - m).astype(BF16), v1c[...], preferred_element_type=F32))
        ol_ref[blk * QBLK:(blk + 1) * QBLK, :] = (ol[:, :LANES] / ol[:, LANES:]).astype(BF16)

    if with_ctx:
        s = lax.dot_general(qc_ref[...], kc, NT_DIMS, preferred_element_type=F32) * scale
        e = jnp.exp(s - jnp.max(s, axis=-1, keepdims=True)).astype(BF16)
        ol = jnp.dot(e, v1c[...], preferred_element_type=F32)
        oc_ref[...] = (ol[:, :LANES] / ol[:, LANES:]).astype(BF16)


def _nbr_call(proj, bias_tab, nb, with_ctx):
    order = lambda h, b: (b, h)
    out_specs, out_shapes = _mixer_outs(nb, LANES, order, with_ctx)
    return pl.pallas_call(
        functools.partial(_nbr_kernel, with_ctx=with_ctx),
        grid=(B_HEADS, nb),
        in_specs=_mixer_specs(nb, (LANES,) * 3, (COL_BQ, COL_BK, COL_BV), order) + [
            pl.BlockSpec((1, 3, QBLK, NBR_KEYS), lambda h, b: (h, 0, 0, 0)),
        ],
        out_specs=out_specs,
        out_shape=out_shapes,
        scratch_shapes=[pltpu.VMEM((SEQ, 2 * LANES), BF16), pltpu.VMEM((CTX_LEN, 2 * LANES), BF16)],
        compiler_params=_params(("parallel", "parallel")),
        name="nbr_attention",
    )(*([proj] * 6), bias_tab)


def _ret_kernel(ql_ref, kl_ref, vl_ref, gl_ref, qc_ref, kc_ref, vc_ref, gc_ref, cos_ref, sa_ref, sb_ref,
                dec_ref, rn_ref, *rest, with_ctx):
    ol_ref, oc_ref = (rest[0], rest[1]) if with_ctx else (rest[0], None)
    of_ref, ob_ref, st_ref = rest[-3:]
    ch = QBLK
    n_chunks = SEQ // ch
    kscale = C_KEY_DIM ** -0.5

    def run():
        ii = lax.broadcasted_iota(jnp.int32, (ch, ch), 0)
        jj = lax.broadcasted_iota(jnp.int32, (ch, ch), 1)
        dist = (ii - jj).astype(F32)
        pos = lax.broadcasted_iota(jnp.int32, (ch, 1), 0).astype(F32)
        lane = lax.broadcasted_iota(jnp.int32, (1, LANES), 1)
        rn = rn_ref[...]

        def log_sigmoid(x):
            return -(jnp.log1p(jnp.exp(-jnp.abs(x))) + jnp.maximum(-x, 0.0))

        heads = []
        for e in range(2):
            lgf = log_sigmoid(dec_ref[0, 0, e:e + 1, 0:1])
            lgb = log_sigmoid(dec_ref[0, 1, e:e + 1, 0:1])
            heads.append(dict(
                mask=(lane >= e * C_KEY_DIM) & (lane < (e + 1) * C_KEY_DIM),
                intra_f=jnp.where(dist >= 0, jnp.exp(lgf * jnp.maximum(dist, 0.0)), 0.0),
                intra_b=jnp.where(dist <= 0, jnp.exp(lgb * jnp.maximum(-dist, 0.0)), 0.0),
                qdec_f=jnp.exp(lgf * (pos + 1.0)), kdec_f=jnp.exp(lgf * (ch - 1.0 - pos)),
                cdec_f=jnp.exp(lgf * ch),
                qdec_b=jnp.exp(lgb * (ch - pos)), kdec_b=jnp.exp(lgb * pos),
                cdec_b=jnp.exp(lgb * ch),
                vs=slice(e * LANES, (e + 1) * LANES),
            ))

        def finish(o, g):
            return (_silu(g.astype(F32)) * (_rms(o) * rn)).astype(BF16)

        qc = qc_ref[...].astype(F32)
        kc = kc_ref[...].astype(F32) * kscale
        for e, hd in enumerate(heads):
            v = vc_ref[:, hd["vs"]]
            st_ref[e] = lax.dot_general((kc * hd["kdec_f"]).astype(BF16), v, TN_DIMS, preferred_element_type=F32)
            st_ref[2 + e] = lax.dot_general((kc * hd["kdec_b"]).astype(BF16), v, TN_DIMS,
                                            preferred_element_type=F32)
            if with_ctx:
                qe = jnp.where(hd["mask"], qc, 0.0).astype(BF16)
                att = lax.dot_general(qe, kc.astype(BF16), NT_DIMS, preferred_element_type=F32)
                att = att * (hd["intra_f"] + hd["intra_b"])
                o = jnp.dot(att.astype(BF16), v, preferred_element_type=F32)
                oc_ref[:, hd["vs"]] = finish(o, gc_ref[:, hd["vs"]])

        def chunk_rows(c):
            return pl.ds(pl.multiple_of(c * ch, ch), ch)

        def sweep(c, fwd):
            rows = chunk_rows(c)
            tabs = (cos_ref[rows, :], sa_ref[rows, :], sb_ref[rows, :])
            q = _rope(ql_ref[rows, :].astype(F32), *tabs)
            k = _rope(kl_ref[rows, :].astype(F32), *tabs) * kscale
            kb = k.astype(BF16)
            for e, hd in enumerate(heads):
                d = "f" if fwd else "b"
                si = e if fwd else 2 + e
                v = vl_ref[rows, hd["vs"]]
                qe = jnp.where(hd["mask"], q, 0.0)
                att = lax.dot_general(qe.astype(BF16), kb, NT_DIMS, preferred_element_type=F32) * hd["intra_" + d]
                st = st_ref[si]
                o = (jnp.dot(att.astype(BF16), v, preferred_element_type=F32)
                     + jnp.dot((qe * hd["qdec_" + d]).astype(BF16), st.astype(BF16), preferred_element_type=F32))
                st_ref[si] = st * hd["cdec_" + d] + lax.dot_general(
                    (k * hd["kdec_" + d]).astype(BF16), v, TN_DIMS, preferred_element_type=F32)
                (of_ref if fwd else ob_ref)[rows, hd["vs"]] = o

        def finish_chunk(c):
            rows = chunk_rows(c)
            for hd in heads:
                vs = hd["vs"]
                ol_ref[rows, vs] = finish(of_ref[rows, vs] + ob_ref[rows, vs], gl_ref[rows, vs])

        def first_half(t, carry):
            sweep(t, True)
            sweep(n_chunks - 1 - t, False)
            return carry

        def second_half(t, carry):
            sweep(t, True)
            sweep(n_chunks - 1 - t, False)
            finish_chunk(t)
            finish_chunk(n_chunks - 1 - t)
            return carry

        lax.fori_loop(0, n_chunks // 2, first_half, 0, unroll=True)
        lax.fori_loop(n_chunks // 2, n_chunks, second_half, 0, unroll=True)

    run()


def _ret_call(proj, tabs, ret_decay, ret_norm, nb, with_ctx):
    pairs = C_HEADS // 2
    w2 = 2 * LANES
    order = lambda b, p: (b, p)
    whole = lambda b, p: (0, 0)
    out_specs, out_shapes = _mixer_outs(nb, w2, order, with_ctx)
    dec = jnp.broadcast_to(ret_decay.astype(F32).reshape(2, pairs, 2).transpose(1, 0, 2)[..., None],
                           (pairs, 2, 2, LANES))
    return pl.pallas_call(
        functools.partial(_ret_kernel, with_ctx=with_ctx),
        grid=(nb, pairs),
        in_specs=_mixer_specs(nb, (LANES, LANES, w2, w2), (COL_CQ, COL_CK, COL_CV, COL_CG), order) + [
            pl.BlockSpec((SEQ, LANES), whole),
            pl.BlockSpec((SEQ, LANES), whole),
            pl.BlockSpec((SEQ, LANES), whole),
            pl.BlockSpec((1, 2, 2, LANES), lambda b, p: (p, 0, 0, 0)),
            pl.BlockSpec((1, LANES), whole),
        ],
        out_specs=out_specs,
        out_shape=out_shapes,
        scratch_shapes=[
            pltpu.VMEM((SEQ, w2), F32),
            pltpu.VMEM((SEQ, w2), F32),
            pltpu.VMEM((4, LANES, LANES), F32),
        ],
        compiler_params=_params(("parallel", "parallel")),
        name="retention",
    )(*([proj] * 8), *tabs, dec, ret_norm.reshape(1, LANES))


def _merge_kernel(h_ref, mod_ref, post_ref, g_lo_ref, g_hi_ref, wb_ref, wo_ref, *rest, n_lat_tiles, with_ctx):
    d = h_ref.shape[1]
    o_ref = rest[-1]
    if with_ctx:
        is_lat = pl.program_id(0) < n_lat_tiles
        xs = [jnp.where(is_lat, rest[2 * k][...], rest[2 * k + 1][...]) for k in range(3)]
    else:
        xs = [rest[k][...] for k in range(3)]
    g_lo = g_lo_ref[...]
    g_hi = g_hi_ref[...]
    gates = (g_lo[:, :d], jnp.concatenate([g_lo[:, d:], g_hi[:, :d // 2]], axis=1), g_hi[:, d // 2:])
    m = None
    for k in range(3):
        t = jax.nn.sigmoid(gates[k].astype(F32)) * jnp.dot(xs[k], wb_ref[k], preferred_element_type=F32)
        m = t if m is None else m + t
    y = jnp.dot(m.astype(BF16), wo_ref[...], preferred_element_type=F32)
    o_ref[...] = h_ref[...] + mod_ref[0, 5:6, :] * (_rms(y) * post_ref[...])


def _merge_call(h, mod_l, post_g, branches, proj, wb, wo, layer, nrows, n_lat_seg):
    d = h.shape[1]
    tm = CTX_LEN
    with_ctx = len(branches[0]) == 2
    n_lat_tiles = n_lat_seg * SEQ // tm
    gw = 3 * d // 2
    g0 = COL_GATES // gw
    resident = dict(pipeline_mode=pl.Buffered(1))
    row = lambda i: (i, 0)
    lat_row = lambda i: (jnp.minimum(i, n_lat_tiles - 1), 0)
    ctx_row = lambda i: (jnp.maximum(i - n_lat_tiles, 0), 0)
    branch_specs, branch_args = [], []
    for outs in branches:
        branch_specs.append(pl.BlockSpec((tm, BRANCH_W), lat_row))
        if with_ctx:
            branch_specs.append(pl.BlockSpec((tm, BRANCH_W), ctx_row))
        branch_args.extend(outs)
    return pl.pallas_call(
        functools.partial(_merge_kernel, n_lat_tiles=n_lat_tiles, with_ctx=with_ctx),
        grid=(nrows // tm,),
        in_specs=[
            pl.BlockSpec((tm, d), row),
            pl.BlockSpec((1, N_MOD, d), lambda i: (jnp.minimum((i * tm) // SEQ, n_lat_seg), 0, 0)),
            pl.BlockSpec((1, d), lambda i: (0, 0)),
            pl.BlockSpec((tm, gw), lambda i: (i, g0)),
            pl.BlockSpec((tm, gw), lambda i: (i, g0 + 1)),
            pl.BlockSpec((None, 3, BRANCH_W, d), lambda i: (layer, 0, 0, 0), **resident),
            pl.BlockSpec((None, d, d), lambda i: (layer, 0, 0), **resident),
        ] + branch_specs,
        out_specs=pl.BlockSpec((tm, d), row),
        out_shape=jax.ShapeDtypeStruct((nrows, d), F32),
        compiler_params=_params(("parallel",)),
        name="mixer_merge",
    )(h, mod_l, post_g.reshape(1, d), proj, proj, wb, wo, *branch_args)


def kernel(x, c, ctx, c_ctx, w_mod, b_mod, pre_norm, post_norm, ffn_w_in, ffn_w_out, w_in, diff_lambda,
           diff_norm, na_rpb, ret_decay, ret_norm, w_branch, w_out):
    nb, seq, d = x.shape
    assert (seq, d) == (SEQ, D_MODEL) and ctx.shape == (nb, CTX_LEN, d)
    n_lat = nb * SEQ
    n_all = n_lat + nb * CTX_LEN
    tm = min(512, nb * CTX_LEN)

    h = jnp.concatenate([x.reshape(n_lat, d), ctx.reshape(nb * CTX_LEN, d)], axis=0)
    pad = (-(nb + 1)) % 8
    cc = jnp.concatenate([c, c_ctx[None, :], jnp.zeros((pad, d), c.dtype)], axis=0)
    mods = _mod_call(cc, w_mod, b_mod)
    tabs = _rope_tables()
    w1, w2 = ffn_w_in.astype(BF16), ffn_w_out.astype(BF16)
    w_proj, wb, wo = w_in.astype(BF16), w_branch.astype(BF16), w_out.astype(BF16)

    for l in range(DEPTH):
        with_ctx = l < DEPTH - 1
        lambda_init = 0.8 - 0.6 * math.exp(-0.3 * l)
        nrows_out = n_all if with_ctx else n_lat
        mod_l = mods[l, :nb + 1].reshape(nb + 1, N_MOD, d)
        ffn = functools.partial(_ffn_call, tm=tm, n_lat_seg=nb)

        h = ffn(h, mod_l, 0, pre_norm[l, 0], post_norm[l, 0], w1, w2, (l, 0), n_all)
        proj = _inproj_call(h, mod_l, pre_norm[l, 1], w_proj, l, min(1024, nb * CTX_LEN), nb)
        oa = _diff_call(proj, tabs, diff_lambda[l], diff_norm[l], lambda_init, nb, with_ctx)
        ob = _nbr_call(proj, _nbr_bias_call(na_rpb[l]), nb, with_ctx)
        yr = _ret_call(proj, tabs, ret_decay[l], ret_norm[l], nb, with_ctx)
        h = _merge_call(h, mod_l, post_norm[l, 1], (oa, ob, yr), proj, wb, wo, l, nrows_out, nb)
        h = ffn(h, mod_l, 6, pre_norm[l, 2], post_norm[l, 2], w1, w2, (l, 1), nrows_out)
    return h.reshape(nb, SEQ, d)
```

```python
import functools
import math

import numpy as np
import jax
import jax.numpy as jnp
from jax import lax
from jax.experimental import pallas as pl
from jax.experimental.pallas import tpu as pltpu

D_MODEL = 2048
SEQ = 2048
DEPTH = 2
GRID_W = 64
CTX_LEN = 256
N_MOD = 9
FFN_DIM = 5632
A_HEADS = 8
A_HEAD_DIM = 64
B_HEADS = 8
B_HEAD_DIM = 128
WIN_R = 8
WIN_C = 16
C_HEADS = 8
C_KEY_DIM = 64
BRANCH_W = 1024
ROPE_BASE = 10000.0
EPS = 1e-6

COL_AQ, COL_AK, COL_AV = 0, 1024, 2048
COL_BQ, COL_BK, COL_BV = 3072, 4096, 5120
COL_CQ, COL_CK, COL_CV, COL_CG = 6144, 6656, 7168, 8192
COL_GATES = 9216
IN_COLS = 15360

LANES = 128
BF16_SUBLANES = 16
QBLK = 256
ROWS_PER_QBLK = QBLK // GRID_W
NBR_KROWS = ROWS_PER_QBLK + WIN_R - 1
NBR_KEYS = NBR_KROWS * GRID_W
VMEM_LIMIT = 56 * 1024 * 1024
FFN_ROWS = 512
PROJ_ROWS = 1024
FFN_TILE = 512
FFN_STAGE_ROWS = 64
MXU_COLS = 256
PROJ_TILE = 1024
DIFF_ROW_SPLIT = 2
NBR_MASKED = -1e30

BF16 = jnp.bfloat16
F32 = jnp.float32
NT_DIMS = (((1,), (1,)), ((), ()))
TN_DIMS = (((0,), (0,)), ((), ()))


def _params(sem):
    return pltpu.CompilerParams(dimension_semantics=sem, vmem_limit_bytes=VMEM_LIMIT)


def _rms(x):
    return x * lax.rsqrt(jnp.mean(x * x, axis=-1, keepdims=True) + EPS)


def _silu(x):
    return x * jax.nn.sigmoid(x)


def _row_chunks(row0, n_rows, rows_per, body, inline=False):
    def step(r, carry):
        body(pl.ds(pl.multiple_of(row0 + r * rows_per, rows_per), rows_per))
        return carry
    if inline:
        for r in range(n_rows // rows_per):
            step(r, 0)
    else:
        lax.fori_loop(0, n_rows // rows_per, step, 0, unroll=4)


def _adaln_rows(u_ref, dst, h_ref, mod_ref, pre_ref, k0, seg, row0, n_rows, inline):
    gain = pre_ref[...] * (1.0 + mod_ref[seg, k0 + 1:k0 + 2, :])
    shift = mod_ref[seg, k0:k0 + 1, :]

    def body(rows):
        u_ref[dst, rows, :] = (_rms(h_ref[rows, :]) * gain + shift).astype(u_ref.dtype)

    _row_chunks(row0, n_rows, BF16_SUBLANES, body, inline)


def _mod_kernel(cc_ref, w_ref, b_ref, o_ref):
    s = _silu(cc_ref[...]).astype(BF16)
    o_ref[0] = jnp.dot(s, w_ref[0].astype(BF16), preferred_element_type=F32) + b_ref[0]


def _mod_call(cc, w_mod, b_mod):
    depth, d, n = w_mod.shape
    rows = cc.shape[0]
    tn = 1024
    return pl.pallas_call(
        _mod_kernel,
        grid=(depth, n // tn),
        in_specs=[
            pl.BlockSpec((rows, d), lambda l, j: (0, 0)),
            pl.BlockSpec((1, d, tn), lambda l, j: (l, 0, j)),
            pl.BlockSpec((1, 1, tn), lambda l, j: (l, 0, j)),
        ],
        out_specs=pl.BlockSpec((1, rows, tn), lambda l, j: (l, 0, j)),
        out_shape=jax.ShapeDtypeStruct((depth, rows, n), F32),
        compiler_params=_params(("parallel", "parallel")),
        name="mod_vectors",
    )(cc, w_mod, b_mod.reshape(depth, 1, n))


def _ffn_kernel(h_ref, mod_ref, pre_ref, post_ref, w1a_ref, w1b_ref, w2_ref, *rest,
                k0, nf, tf, nn, tn, tm, n_lat_seg, first_row):
    o_ref, u_ref, hm_ref, y_ref = rest[-4:]
    i = pl.program_id(0)
    j = pl.program_id(1)
    slot = i % 2
    d = h_ref.shape[1]
    segment = lambda tile: jnp.minimum((first_row + tile * tm) // SEQ, n_lat_seg)

    def adaln_rows(tile, row0, n_rows, dst, inline):
        _adaln_rows(u_ref, dst, h_ref, mod_ref, pre_ref, k0, segment(tile), row0, n_rows, inline)

    @pl.when((i == 0) & (j == 0))
    def _():
        adaln_rows(0, 0, tm, 0, False)

    @pl.when(j < nf)
    def _():
        u = u_ref[slot]
        for c in range(tf // MXU_COLS):
            cols = slice(c * MXU_COLS, (c + 1) * MXU_COLS)
            a = jnp.dot(u, w1a_ref[:, cols], preferred_element_type=F32)
            b = jnp.dot(u, w1b_ref[:, cols], preferred_element_type=F32)
            hm_ref[j, :, cols] = (_silu(a) * b).astype(BF16)
        n_copy = tm // FFN_STAGE_ROWS
        rows = pl.ds(pl.multiple_of(jnp.minimum(j, n_copy - 1) * FFN_STAGE_ROWS, FFN_STAGE_ROWS), FFN_STAGE_ROWS)
        o_ref[rows, :] = h_ref[rows, :]

    @pl.when(j >= nf)
    def _():
        n = j - nf
        y = jnp.dot(hm_ref[0], w2_ref[0:tf, :], preferred_element_type=F32)
        for k in range(1, nf):
            y += jnp.dot(hm_ref[k], w2_ref[k * tf:(k + 1) * tf, :], preferred_element_type=F32)
        y_ref[n] = y
        adaln_rows(i + 1, n * (tm // nn), tm // nn, 1 - slot, True)

    @pl.when(j == nf + nn - 1)
    def _():
        gain = 0.5 * mod_ref[segment(i), k0 + 2:k0 + 3, :] * post_ref[...]
        ss = sum(jnp.sum(y_ref[n] * y_ref[n], axis=-1, keepdims=True) for n in range(nn))
        r = lax.rsqrt(ss / d + EPS)
        for n in range(nn):
            cols = slice(n * tn, (n + 1) * tn)
            o_ref[:, cols] = o_ref[:, cols] + (y_ref[n] * r) * gain[:, cols]


def _ffn_call(h, mod_l, k0, pre_g, post_g, w1, w2, lk, nrows, tm, n_lat_seg, first_row=0, out_rows=None,
              into=None):
    d = h.shape[1]
    passthrough = [] if into is None else [into]
    out_rows = into.shape[0] if passthrough else (out_rows or nrows)
    tile0 = first_row // tm
    f = w2.shape[2]
    tf = tn = FFN_TILE
    nf, nn = f // tf, d // tn
    n_tiles = nrows // tm
    h_tile = lambda i, j: (jnp.minimum(i + jnp.where(j >= nf, 1, 0), n_tiles - 1), 0)
    w2_tile = lambda j: jnp.where(j < nf, nn - 1, j - nf)
    return pl.pallas_call(
        functools.partial(_ffn_kernel, k0=k0, nf=nf, tf=tf, nn=nn, tn=tn, tm=tm, n_lat_seg=n_lat_seg,
                          first_row=first_row),
        grid=(n_tiles, nf + nn),
        in_specs=[
            pl.BlockSpec((tm, d), h_tile),
            pl.BlockSpec(mod_l.shape, lambda i, j: (0, 0, 0)),
            pl.BlockSpec((1, d), lambda i, j: (0, 0)),
            pl.BlockSpec((1, d), lambda i, j: (0, 0)),
            pl.BlockSpec((None, None, d, tf), lambda i, j: (*lk, 0, jnp.minimum(j, nf - 1))),
            pl.BlockSpec((None, None, d, tf), lambda i, j: (*lk, 0, jnp.minimum(j, nf - 1) + nf)),
            pl.BlockSpec((None, None, f, tn), lambda i, j: (*lk, 0, w2_tile(j))),
        ] + [pl.BlockSpec(memory_space=pl.ANY)] * len(passthrough),
        out_specs=pl.BlockSpec((tm, d), lambda i, j: (tile0 + i, 0)),
        out_shape=jax.ShapeDtypeStruct((out_rows, d), F32),
        input_output_aliases={7: 0} if passthrough else {},
        scratch_shapes=[pltpu.VMEM((2, tm, d), BF16), pltpu.VMEM((nf, tm, tf), BF16),
                        pltpu.VMEM((nn, tm, tn), F32)],
        compiler_params=_params(("arbitrary", "arbitrary")),
        name="ffn_sublayer",
    )(h, mod_l, pre_g.reshape(1, d), post_g.reshape(1, d), w1, w1, w2, *passthrough)


def _inproj_kernel(h_ref, mod_ref, pre_ref, w_ref, o_ref, u_ref, *, tm, n_lat_seg, n_steps, n_pre):
    i = pl.program_id(0)
    j = pl.program_id(1)
    slot = i % 2

    def adaln_rows(tile, row0, n_rows, dst, inline):
        seg = jnp.minimum((tile * tm) // SEQ, n_lat_seg)
        _adaln_rows(u_ref, dst, h_ref, mod_ref, pre_ref, 3, seg, row0, n_rows, inline)

    def project():
        o_ref[...] = jnp.dot(u_ref[slot], w_ref[...], preferred_element_type=F32).astype(BF16)

    @pl.when((i == 0) & (j == 0))
    def _():
        adaln_rows(0, 0, tm, 0, False)

    @pl.when(j < n_steps - n_pre)
    def _():
        project()

    @pl.when(j >= n_steps - n_pre)
    def _():
        project()
        adaln_rows(i + 1, (j - (n_steps - n_pre)) * (tm // n_pre), tm // n_pre, 1 - slot, True)


def _inproj_call(h, mod_l, pre_g, w, layer, tm, n_lat_seg):
    nrows, d = h.shape
    n = w.shape[2]
    tn = PROJ_TILE
    n_tiles, n_steps = nrows // tm, n // tn
    n_pre = 8
    h_tile = lambda i, j: (jnp.minimum(i + jnp.where(j >= n_steps - n_pre, 1, 0), n_tiles - 1), 0)
    return pl.pallas_call(
        functools.partial(_inproj_kernel, tm=tm, n_lat_seg=n_lat_seg, n_steps=n_steps, n_pre=n_pre),
        grid=(n_tiles, n_steps),
        in_specs=[
            pl.BlockSpec((tm, d), h_tile),
            pl.BlockSpec(mod_l.shape, lambda i, j: (0, 0, 0)),
            pl.BlockSpec((1, d), lambda i, j: (0, 0)),
            pl.BlockSpec((None, d, tn), lambda i, j: (layer, 0, j)),
        ],
        out_specs=pl.BlockSpec((tm, tn), lambda i, j: (i, j)),
        out_shape=jax.ShapeDtypeStruct((nrows, n), BF16),
        scratch_shapes=[pltpu.VMEM((2, tm, d), BF16)],
        compiler_params=_params(("arbitrary", "arbitrary")),
        name="mixer_in_proj",
    )(h, mod_l, pre_g.reshape(1, d), w)


def _rope(t, cos, sa, sb):
    return t * cos + pltpu.roll(t, LANES - 16, 1) * sa + pltpu.roll(t, 16, 1) * sb


def _rope_tables():
    pos = np.arange(SEQ)
    prow = jnp.asarray(pos // GRID_W, F32)
    pcol = jnp.asarray(pos % GRID_W, F32)
    half = 16
    inv = ROPE_BASE ** (-jnp.arange(half, dtype=F32) / half)
    lane = np.arange(LANES)
    freq = jnp.asarray(lane % half)
    use_col = jnp.asarray((lane % 64) >= 32)
    second = jnp.asarray((lane % 32) >= half)
    p = jnp.where(use_col[None, :], pcol[:, None], prow[:, None])
    ang = p * inv[freq][None, :]
    cos, sin = jnp.cos(ang), jnp.sin(ang)
    sa = jnp.where(second[None, :], 0.0, -sin)
    sb = jnp.where(second[None, :], sin, 0.0)
    return cos, sa, sb


def _mixer_specs(nb, width, cols, order):
    ctx_blk0 = nb * SEQ // CTX_LEN

    def spec(rows, col, w, ctx):
        def index(*g):
            b, h = order(*g)
            return ((ctx_blk0 + b) if ctx else b, col // w + h)
        return pl.BlockSpec((rows, w), index)

    return ([spec(SEQ, c, w, False) for c, w in zip(cols, width)]
            + [spec(CTX_LEN, c, w, True) for c, w in zip(cols, width)])


def _mixer_outs(nb, width, order, with_ctx):
    def index(*g):
        b, h = order(*g)
        return (b, h)
    specs = [pl.BlockSpec((SEQ, width), index)]
    shapes = [jax.ShapeDtypeStruct((nb * SEQ, BRANCH_W), BF16)]
    if with_ctx:
        specs.append(pl.BlockSpec((CTX_LEN, width), index))
        shapes.append(jax.ShapeDtypeStruct((nb * CTX_LEN, BRANCH_W), BF16))
    return specs, shapes


def _diff_kernel(ql_ref, kl_ref, vl_ref, qc_ref, kc_ref, vc_ref, cos_ref, sa_ref, sb_ref, lam_ref, g_ref,
                 *rest, lambda_init, with_ctx):
    ol_ref, oc_ref = (rest[0], rest[1]) if with_ctx else (rest[0], None)
    k_all, v_ones = rest[-2:]
    k_all[0:CTX_LEN, :] = kc_ref[...]
    k_all[CTX_LEN:, :] = _rope(kl_ref[...].astype(F32), cos_ref[...], sa_ref[...], sb_ref[...]).astype(BF16)
    v_ones[0:CTX_LEN, 0:LANES] = vc_ref[...]
    v_ones[CTX_LEN:, 0:LANES] = vl_ref[...]
    v_ones[:, LANES:] = jnp.ones((CTX_LEN + SEQ, LANES), BF16)

    lv = lam_ref[...]
    lam = (jnp.exp(jnp.sum(lv[0:1] * lv[1:2], axis=-1, keepdims=True))
           - jnp.exp(jnp.sum(lv[2:3] * lv[3:4], axis=-1, keepdims=True)) + lambda_init)
    lane = lax.broadcasted_iota(jnp.int32, (1, LANES), 1)

    def attend(q, nk):
        q = q * (A_HEAD_DIM ** -0.5)
        k = k_all[0:nk, :]
        v1 = v_ones[0:nk, :]

        rows = QBLK // DIFF_ROW_SPLIT
        qms = [jnp.where(mask, q[r * rows:(r + 1) * rows], 0.0).astype(BF16)
               for r in range(DIFF_ROW_SPLIT) for mask in (lane < A_HEAD_DIM, lane >= A_HEAD_DIM)]
        ss = [lax.dot_general(qm, k, NT_DIMS, preferred_element_type=F32) for qm in qms]
        es = [jnp.exp(s - jnp.max(s, axis=-1, keepdims=True)).astype(BF16) for s in ss]
        ols = [jnp.dot(e, v1, preferred_element_type=F32) for e in es]
        os = [ol[:, :LANES] / ol[:, LANES:] for ol in ols]
        o = jnp.concatenate([os[2 * r] - lam * os[2 * r + 1] for r in range(DIFF_ROW_SPLIT)], axis=0)
        return (_rms(o) * g_ref[...] * (1.0 - lambda_init)).astype(BF16)

    def block(t, carry):
        rows = pl.ds(pl.multiple_of(t * QBLK, QBLK), QBLK)
        q = _rope(ql_ref[rows, :].astype(F32), cos_ref[rows, :], sa_ref[rows, :], sb_ref[rows, :])
        ol_ref[rows, :] = attend(q, CTX_LEN + SEQ)
        return carry

    lax.fori_loop(0, SEQ // QBLK, block, 0, unroll=True)
    if with_ctx:
        oc_ref[...] = attend(qc_ref[...].astype(F32), CTX_LEN)


def _diff_call(proj, tabs, diff_lambda, diff_norm, lambda_init, nb, with_ctx):
    order = lambda b, h: (b, h)
    whole = lambda b, h: (0, 0)
    out_specs, out_shapes = _mixer_outs(nb, LANES, order, with_ctx)
    return pl.pallas_call(
        functools.partial(_diff_kernel, lambda_init=lambda_init, with_ctx=with_ctx),
        grid=(nb, A_HEADS),
        in_specs=_mixer_specs(nb, (LANES,) * 3, (COL_AQ, COL_AK, COL_AV), order) + [
            pl.BlockSpec((SEQ, LANES), whole),
            pl.BlockSpec((SEQ, LANES), whole),
            pl.BlockSpec((SEQ, LANES), whole),
            pl.BlockSpec((4, A_HEAD_DIM), whole),
            pl.BlockSpec((1, LANES), whole),
        ],
        out_specs=out_specs,
        out_shape=out_shapes,
        scratch_shapes=[pltpu.VMEM((CTX_LEN + SEQ, LANES), BF16), pltpu.VMEM((CTX_LEN + SEQ, 2 * LANES), BF16)],
        compiler_params=_params(("parallel", "parallel")),
        name="diff_attention",
    )(*([proj] * 6), *tabs, diff_lambda, diff_norm.reshape(1, LANES))


def _nbr_block_rows(cfg, qi, kj):
    n_rows = SEQ // GRID_W
    r0 = (0, ROWS_PER_QBLK, n_rows - ROWS_PER_QBLK)[cfg]
    ks = min(max(r0 - WIN_R // 2, 0), n_rows - NBR_KROWS)
    r = r0 + qi
    rs = min(max(r - WIN_R // 2, 0), n_rows - WIN_R)
    kr = ks + kj
    return rs <= kr < rs + WIN_R, kr - r + WIN_R - 1


def _nbr_bias_kernel(rpb_ref, o_ref):
    n_dr, n_dc = 2 * WIN_R - 1, 2 * WIN_C - 1
    base = pl.program_id(0) * (n_dr * n_dc)
    qc = lax.broadcasted_iota(jnp.int32, (GRID_W, LANES), 0)
    lane = lax.broadcasted_iota(jnp.int32, (GRID_W, LANES), 1)
    kc = lane % GRID_W
    dc = kc - qc + (WIN_C - 1)
    cs = jnp.clip(qc - WIN_C // 2, 0, GRID_W - WIN_C)
    col_ok = (kc >= cs) & (kc < cs + WIN_C)
    masked = jnp.full((GRID_W, LANES), NBR_MASKED, F32)

    def row_pattern(dr):
        acc = masked
        for x in range(n_dc):
            acc = jnp.where(dc == x, rpb_ref[base + dr * n_dc + x], acc)
        return jnp.where(col_ok, acc, masked)

    pats = [row_pattern(dr) for dr in range(n_dr)]

    def pattern(cfg, qi, kj):
        if kj >= NBR_KROWS:
            return masked
        ok, dr = _nbr_block_rows(cfg, qi, kj)
        return pats[dr] if ok else masked

    for cfg in range(3):
        for qi in range(ROWS_PER_QBLK):
            rows = slice(qi * GRID_W, (qi + 1) * GRID_W)
            for m in range(pl.cdiv(NBR_KEYS, LANES)):
                tile = jnp.where(lane < GRID_W, pattern(cfg, qi, 2 * m), pattern(cfg, qi, 2 * m + 1))
                width = min(LANES, NBR_KEYS - m * LANES)
                o_ref[0, cfg, rows, m * LANES:m * LANES + width] = tile[:, :width]


def _nbr_bias_call(rpb):
    heads = rpb.shape[0]
    return pl.pallas_call(
        _nbr_bias_kernel,
        grid=(heads,),
        in_specs=[pl.BlockSpec(memory_space=pltpu.SMEM)],
        out_specs=pl.BlockSpec((1, 3, QBLK, NBR_KEYS), lambda h: (h, 0, 0, 0)),
        out_shape=jax.ShapeDtypeStruct((heads, 3, QBLK, NBR_KEYS), F32),
        compiler_params=_params(("parallel",)),
        name="nbr_bias_table",
    )(rpb.astype(F32).reshape(-1))


def _nbr_kernel(ql_ref, kl_ref, vl_ref, qc_ref, kc_ref, vc_ref, bias_ref, *rest, with_ctx):
    ol_ref, oc_ref = (rest[0], rest[1]) if with_ctx else (rest[0], None)
    v1l, v1c = rest[-2:]
    scale = B_HEAD_DIM ** -0.5
    v1l[:, 0:LANES] = vl_ref[...]
    v1l[:, LANES:] = jnp.ones((SEQ, LANES), BF16)
    v1c[:, 0:LANES] = vc_ref[...]
    v1c[:, LANES:] = jnp.ones((CTX_LEN, LANES), BF16)
    kc = kc_ref[...]
    lat_blocks = SEQ // QBLK

    for blk in range(lat_blocks):
        cfg = 0 if blk == 0 else (2 if blk == lat_blocks - 1 else 1)
        start = min(max((blk * ROWS_PER_QBLK - WIN_R // 2) * GRID_W, 0), SEQ - NBR_KEYS)
        q = ql_ref[blk * QBLK:(blk + 1) * QBLK, :]
        s_lat = (lax.dot_general(q, kl_ref[start:start + NBR_KEYS, :], NT_DIMS, preferred_element_type=F32) * scale
                 + bias_ref[0, cfg])
        s_ctx = lax.dot_general(q, kc, NT_DIMS, preferred_element_type=F32) * scale
        m = jnp.maximum(jnp.max(s_lat, axis=-1, keepdims=True), jnp.max(s_ctx, axis=-1, keepdims=True))
        ol = (jnp.dot(jnp.exp(s_lat - m).astype(BF16), v1l[start:start + NBR_KEYS, :], preferred_element_type=F32)
              + jnp.dot(jnp.exp(s_ctx - m).astype(BF16), v1c[...], preferred_element_type=F32))
        ol_ref[blk * QBLK:(blk + 1) * QBLK, :] = (ol[:, :LANES] / ol[:, LANES:]).astype(BF16)

    if with_ctx:
        s = lax.dot_general(qc_ref[...], kc, NT_DIMS, preferred_element_type=F32) * scale
        e = jnp.exp(s - jnp.max(s, axis=-1, keepdims=True)).astype(BF16)
        ol = jnp.dot(e, v1c[...], preferred_element_type=F32)
        oc_ref[...] = (ol[:, :LANES] / ol[:, LANES:]).astype(BF16)


def _nbr_call(proj, bias_tab, nb, with_ctx):
    order = lambda h, b: (b, h)
    out_specs, out_shapes = _mixer_outs(nb, LANES, order, with_ctx)
    return pl.pallas_call(
        functools.partial(_nbr_kernel, with_ctx=with_ctx),
        grid=(B_HEADS, nb),
        in_specs=_mixer_specs(nb, (LANES,) * 3, (COL_BQ, COL_BK, COL_BV), order) + [
            pl.BlockSpec((1, 3, QBLK, NBR_KEYS), lambda h, b: (h, 0, 0, 0)),
        ],
        out_specs=out_specs,
        out_shape=out_shapes,
        scratch_shapes=[pltpu.VMEM((SEQ, 2 * LANES), BF16), pltpu.VMEM((CTX_LEN, 2 * LANES), BF16)],
        compiler_params=_params(("parallel", "parallel")),
        name="nbr_attention",
    )(*([proj] * 6), bias_tab)


def _ret_kernel(ql_ref, kl_ref, vl_ref, gl_ref, qc_ref, kc_ref, vc_ref, gc_ref, cos_ref, sa_ref, sb_ref,
                dec_ref, rn_ref, *rest, with_ctx):
    ol_ref, oc_ref = (rest[0], rest[1]) if with_ctx else (rest[0], None)
    of_ref, ob_ref, st_ref = rest[-3:]
    ch = QBLK
    n_chunks = SEQ // ch
    kscale = C_KEY_DIM ** -0.5

    def run():
        ii = lax.broadcasted_iota(jnp.int32, (ch, ch), 0)
        jj = lax.broadcasted_iota(jnp.int32, (ch, ch), 1)
        dist = (ii - jj).astype(F32)
        pos = lax.broadcasted_iota(jnp.int32, (ch, 1), 0).astype(F32)
        lane = lax.broadcasted_iota(jnp.int32, (1, LANES), 1)
        rn = rn_ref[...]

        def log_sigmoid(x):
            return -(jnp.log1p(jnp.exp(-jnp.abs(x))) + jnp.maximum(-x, 0.0))

        heads = []
        for e in range(2):
            lgf = log_sigmoid(dec_ref[0, 0, e:e + 1, 0:1])
            lgb = log_sigmoid(dec_ref[0, 1, e:e + 1, 0:1])
            heads.append(dict(
                mask=(lane >= e * C_KEY_DIM) & (lane < (e + 1) * C_KEY_DIM),
                intra_f=jnp.where(dist >= 0, jnp.exp(lgf * jnp.maximum(dist, 0.0)), 0.0),
                intra_b=jnp.where(dist <= 0, jnp.exp(lgb * jnp.maximum(-dist, 0.0)), 0.0),
                qdec_f=jnp.exp(lgf * (pos + 1.0)), kdec_f=jnp.exp(lgf * (ch - 1.0 - pos)),
                cdec_f=jnp.exp(lgf * ch),
                qdec_b=jnp.exp(lgb * (ch - pos)), kdec_b=jnp.exp(lgb * pos),
                cdec_b=jnp.exp(lgb * ch),
                vs=slice(e * LANES, (e + 1) * LANES),
            ))

        def finish(o, g):
            return (_silu(g.astype(F32)) * (_rms(o) * rn)).astype(BF16)

        qc = qc_ref[...].astype(F32)
        kc = kc_ref[...].astype(F32) * kscale
        for e, hd in enumerate(heads):
            v = vc_ref[:, hd["vs"]]
            st_ref[e] = lax.dot_general((kc * hd["kdec_f"]).astype(BF16), v, TN_DIMS, preferred_element_type=F32)
            st_ref[2 + e] = lax.dot_general((kc * hd["kdec_b"]).astype(BF16), v, TN_DIMS,
                                            preferred_element_type=F32)
            if with_ctx:
                qe = jnp.where(hd["mask"], qc, 0.0).astype(BF16)
                att = lax.dot_general(qe, kc.astype(BF16), NT_DIMS, preferred_element_type=F32)
                att = att * (hd["intra_f"] + hd["intra_b"])
                o = jnp.dot(att.astype(BF16), v, preferred_element_type=F32)
                oc_ref[:, hd["vs"]] = finish(o, gc_ref[:, hd["vs"]])

        def chunk_rows(c):
            return pl.ds(pl.multiple_of(c * ch, ch), ch)

        def sweep(c, fwd):
            rows = chunk_rows(c)
            tabs = (cos_ref[rows, :], sa_ref[rows, :], sb_ref[rows, :])
            q = _rope(ql_ref[rows, :].astype(F32), *tabs)
            k = _rope(kl_ref[rows, :].astype(F32), *tabs) * kscale
            kb = k.astype(BF16)
            for e, hd in enumerate(heads):
                d = "f" if fwd else "b"
                si = e if fwd else 2 + e
                v = vl_ref[rows, hd["vs"]]
                qe = jnp.where(hd["mask"], q, 0.0)
                att = lax.dot_general(qe.astype(BF16), kb, NT_DIMS, preferred_element_type=F32) * hd["intra_" + d]
                st = st_ref[si]
                o = (jnp.dot(att.astype(BF16), v, preferred_element_type=F32)
                     + jnp.dot((qe * hd["qdec_" + d]).astype(BF16), st.astype(BF16), preferred_element_type=F32))
                st_ref[si] = st * hd["cdec_" + d] + lax.dot_general(
                    (k * hd["kdec_" + d]).astype(BF16), v, TN_DIMS, preferred_element_type=F32)
                (of_ref if fwd else ob_ref)[rows, hd["vs"]] = o

        def finish_chunk(c):
            rows = chunk_rows(c)
            for hd in heads:
                vs = hd["vs"]
                ol_ref[rows, vs] = finish(of_ref[rows, vs] + ob_ref[rows, vs], gl_ref[rows, vs])

        def first_half(t, carry):
            sweep(t, True)
            sweep(n_chunks - 1 - t, False)
            return carry

        def second_half(t, carry):
            sweep(t, True)
            sweep(n_chunks - 1 - t, False)
            finish_chunk(t)
            finish_chunk(n_chunks - 1 - t)
            return carry

        lax.fori_loop(0, n_chunks // 2, first_half, 0, unroll=True)
        lax.fori_loop(n_chunks // 2, n_chunks, second_half, 0, unroll=True)

    run()


def _ret_call(proj, tabs, ret_decay, ret_norm, nb, with_ctx):
    pairs = C_HEADS // 2
    w2 = 2 * LANES
    order = lambda b, p: (b, p)
    whole = lambda b, p: (0, 0)
    out_specs, out_shapes = _mixer_outs(nb, w2, order, with_ctx)
    dec = jnp.broadcast_to(ret_decay.astype(F32).reshape(2, pairs, 2).transpose(1, 0, 2)[..., None],
                           (pairs, 2, 2, LANES))
    return pl.pallas_call(
        functools.partial(_ret_kernel, with_ctx=with_ctx),
        grid=(nb, pairs),
        in_specs=_mixer_specs(nb, (LANES, LANES, w2, w2), (COL_CQ, COL_CK, COL_CV, COL_CG), order) + [
            pl.BlockSpec((SEQ, LANES), whole),
            pl.BlockSpec((SEQ, LANES), whole),
            pl.BlockSpec((SEQ, LANES), whole),
            pl.BlockSpec((1, 2, 2, LANES), lambda b, p: (p, 0, 0, 0)),
            pl.BlockSpec((1, LANES), whole),
        ],
        out_specs=out_specs,
        out_shape=out_shapes,
        scratch_shapes=[
            pltpu.VMEM((SEQ, w2), F32),
            pltpu.VMEM((SEQ, w2), F32),
            pltpu.VMEM((4, LANES, LANES), F32),
        ],
        compiler_params=_params(("parallel", "parallel")),
        name="retention",
    )(*([proj] * 8), *tabs, dec, ret_norm.reshape(1, LANES))


def _merge_kernel(h_ref, mod_ref, post_ref, g_lo_ref, g_hi_ref, wb_ref, wo_ref, *rest, n_lat_tiles, with_ctx):
    d = h_ref.shape[1]
    o_ref = rest[-1]
    if with_ctx:
        is_lat = pl.program_id(0) < n_lat_tiles
        xs = [jnp.where(is_lat, rest[2 * k][...], rest[2 * k + 1][...]) for k in range(3)]
    else:
        xs = [rest[k][...] for k in range(3)]
    g_lo = g_lo_ref[...]
    g_hi = g_hi_ref[...]
    gates = (g_lo[:, :d], jnp.concatenate([g_lo[:, d:], g_hi[:, :d // 2]], axis=1), g_hi[:, d // 2:])
    m = None
    for k in range(3):
        t = jax.nn.sigmoid(gates[k].astype(F32)) * jnp.dot(xs[k], wb_ref[k], preferred_element_type=F32)
        m = t if m is None else m + t
    y = jnp.dot(m.astype(BF16), wo_ref[...], preferred_element_type=F32)
    o_ref[...] = h_ref[...] + mod_ref[0, 5:6, :] * (_rms(y) * post_ref[...])


def _merge_call(h, mod_l, post_g, branches, proj, wb, wo, layer, nrows, n_lat_seg):
    d = h.shape[1]
    tm = CTX_LEN
    with_ctx = len(branches[0]) == 2
    n_lat_tiles = n_lat_seg * SEQ // tm
    gw = 3 * d // 2
    g0 = COL_GATES // gw
    resident = dict(pipeline_mode=pl.Buffered(1))
    row = lambda i: (i, 0)
    lat_row = lambda i: (jnp.minimum(i, n_lat_tiles - 1), 0)
    ctx_row = lambda i: (jnp.maximum(i - n_lat_tiles, 0), 0)
    branch_specs, branch_args = [], []
    for outs in branches:
        branch_specs.append(pl.BlockSpec((tm, BRANCH_W), lat_row))
        if with_ctx:
            branch_specs.append(pl.BlockSpec((tm, BRANCH_W), ctx_row))
        branch_args.extend(outs)
    return pl.pallas_call(
        functools.partial(_merge_kernel, n_lat_tiles=n_lat_tiles, with_ctx=with_ctx),
        grid=(nrows // tm,),
        in_specs=[
            pl.BlockSpec((tm, d), row),
            pl.BlockSpec((1, N_MOD, d), lambda i: (jnp.minimum((i * tm) // SEQ, n_lat_seg), 0, 0)),
            pl.BlockSpec((1, d), lambda i: (0, 0)),
            pl.BlockSpec((tm, gw), lambda i: (i, g0)),
            pl.BlockSpec((tm, gw), lambda i: (i, g0 + 1)),
            pl.BlockSpec((None, 3, BRANCH_W, d), lambda i: (layer, 0, 0, 0), **resident),
            pl.BlockSpec((None, d, d), lambda i: (layer, 0, 0), **resident),
        ] + branch_specs,
        out_specs=pl.BlockSpec((tm, d), row),
        out_shape=jax.ShapeDtypeStruct((nrows, d), F32),
        compiler_params=_params(("parallel",)),
        name="mixer_merge",
    )(h, mod_l, post_g.reshape(1, d), proj, proj, wb, wo, *branch_args)


def kernel(x, c, ctx, c_ctx, w_mod, b_mod, pre_norm, post_norm, ffn_w_in, ffn_w_out, w_in, diff_lambda,
           diff_norm, na_rpb, ret_decay, ret_norm, w_branch, w_out):
    nb, seq, d = x.shape
    assert (seq, d) == (SEQ, D_MODEL) and ctx.shape == (nb, CTX_LEN, d)
    n_lat, n_ctx = nb * SEQ, nb * CTX_LEN
    n_all = n_lat + n_ctx
    tm = min(FFN_ROWS, n_ctx)
    pad = (-(nb + 1)) % 8
    cc = jnp.concatenate([c, c_ctx[None, :], jnp.zeros((pad, d), c.dtype)], axis=0)
    mods = _mod_call(cc, w_mod, b_mod)
    tabs = _rope_tables()
    w1, w2 = ffn_w_in.astype(BF16), ffn_w_out.astype(BF16)
    w_proj, wb, wo = w_in.astype(BF16), w_branch.astype(BF16), w_out.astype(BF16)

    for l in range(DEPTH):
        with_ctx = l < DEPTH - 1
        lambda_init = 0.8 - 0.6 * math.exp(-0.3 * l)
        nrows_out = n_all if with_ctx else n_lat
        mod_l = mods[l, :nb + 1].reshape(nb + 1, N_MOD, d)
        ffn = functools.partial(_ffn_call, tm=tm, n_lat_seg=nb)

        ffn1 = functools.partial(ffn, mod_l=mod_l, k0=0, pre_g=pre_norm[l, 0], post_g=post_norm[l, 0], w1=w1, w2=w2,
                                 lk=(l, 0))
        if l == 0:
            h = ffn1(x.reshape(n_lat, d), nrows=n_lat, out_rows=n_all)
            h = ffn1(ctx.reshape(n_ctx, d), nrows=n_ctx, first_row=n_lat, into=h)
        else:
            h = ffn1(h, nrows=n_all)
        proj = _inproj_call(h, mod_l, pre_norm[l, 1], w_proj, l, min(PROJ_ROWS, n_ctx), nb)
        oa = _diff_call(proj, tabs, diff_lambda[l], diff_norm[l], lambda_init, nb, with_ctx)
        ob = _nbr_call(proj, _nbr_bias_call(na_rpb[l]), nb, with_ctx)
        yr = _ret_call(proj, tabs, ret_decay[l], ret_norm[l], nb, with_ctx)
        h = _merge_call(h, mod_l, post_norm[l, 1], (oa, ob, yr), proj, wb, wo, l, nrows_out, nb)
        h = ffn(h, mod_l, 6, pre_norm[l, 2], post_norm[l, 2], w1, w2, (l, 1), nrows_out)
    return h.reshape(nb, SEQ, d)
```

```python
import functools
import math

import numpy as np
import jax
import jax.numpy as jnp
from jax import lax
from jax.experimental import pallas as pl
from jax.experimental.pallas import tpu as pltpu

D_MODEL = 2048
SEQ = 2048
DEPTH = 2
GRID_W = 64
CTX_LEN = 256
N_MOD = 9
FFN_DIM = 5632
A_HEADS = 8
A_HEAD_DIM = 64
B_HEADS = 8
B_HEAD_DIM = 128
WIN_R = 8
WIN_C = 16
C_HEADS = 8
C_KEY_DIM = 64
BRANCH_W = 1024
ROPE_BASE = 10000.0
EPS = 1e-6

COL_AQ, COL_AK, COL_AV = 0, 1024, 2048
COL_BQ, COL_BK, COL_BV = 3072, 4096, 5120
COL_CQ, COL_CK, COL_CV, COL_CG = 6144, 6656, 7168, 8192
COL_GATES = 9216
IN_COLS = 15360

LANES = 128
BF16_SUBLANES = 16
QBLK = 256
ROWS_PER_QBLK = QBLK // GRID_W
NBR_KROWS = ROWS_PER_QBLK + WIN_R - 1
NBR_KEYS = NBR_KROWS * GRID_W
VMEM_LIMIT = 56 * 1024 * 1024
FFN_ROWS = 512
PROJ_ROWS = 1024
FFN_TILE = 512
FFN_STAGE_ROWS = 64
MXU_COLS = 256
PROJ_TILE = 1536
PROJ_LOOKAHEAD_STEPS = 8
DIFF_ROW_SPLIT = 2
NBR_MASKED = -1e30

BF16 = jnp.bfloat16
F32 = jnp.float32
NT_DIMS = (((1,), (1,)), ((), ()))
TN_DIMS = (((0,), (0,)), ((), ()))


def _params(sem):
    return pltpu.CompilerParams(dimension_semantics=sem, vmem_limit_bytes=VMEM_LIMIT)


def _rms(x):
    return x * lax.rsqrt(jnp.mean(x * x, axis=-1, keepdims=True) + EPS)


def _silu(x):
    return x * jax.nn.sigmoid(x)


def _row_chunks(row0, n_rows, rows_per, body, inline=False):
    def step(r, carry):
        body(pl.ds(pl.multiple_of(row0 + r * rows_per, rows_per), rows_per))
        return carry
    if inline:
        for r in range(n_rows // rows_per):
            step(r, 0)
    else:
        lax.fori_loop(0, n_rows // rows_per, step, 0, unroll=4)


def _adaln_rows(u_ref, dst, h_ref, mod_ref, pre_ref, k0, seg, row0, n_rows, inline):
    gain = pre_ref[...] * (1.0 + mod_ref[seg, k0 + 1:k0 + 2, :])
    shift = mod_ref[seg, k0:k0 + 1, :]

    def body(rows):
        u_ref[dst, rows, :] = (_rms(h_ref[rows, :]) * gain + shift).astype(u_ref.dtype)

    _row_chunks(row0, n_rows, BF16_SUBLANES, body, inline)


def _mod_kernel(cc_ref, w_ref, b_ref, o_ref):
    s = _silu(cc_ref[...]).astype(BF16)
    o_ref[0] = jnp.dot(s, w_ref[0].astype(BF16), preferred_element_type=F32) + b_ref[0]


def _mod_call(cc, w_mod, b_mod):
    depth, d, n = w_mod.shape
    rows = cc.shape[0]
    tn = 1024
    return pl.pallas_call(
        _mod_kernel,
        grid=(depth, n // tn),
        in_specs=[
            pl.BlockSpec((rows, d), lambda l, j: (0, 0)),
            pl.BlockSpec((1, d, tn), lambda l, j: (l, 0, j)),
            pl.BlockSpec((1, 1, tn), lambda l, j: (l, 0, j)),
        ],
        out_specs=pl.BlockSpec((1, rows, tn), lambda l, j: (l, 0, j)),
        out_shape=jax.ShapeDtypeStruct((depth, rows, n), F32),
        compiler_params=_params(("parallel", "parallel")),
        name="mod_vectors",
    )(cc, w_mod, b_mod.reshape(depth, 1, n))


def _ffn_kernel(h_ref, mod_ref, pre_ref, post_ref, w1a_ref, w1b_ref, w2_ref, *rest,
                k0, nf, tf, nn, tn, tm, n_lat_seg, first_row):
    o_ref, u_ref, hm_ref, y_ref = rest[-4:]
    i = pl.program_id(0)
    j = pl.program_id(1)
    slot = i % 2
    d = h_ref.shape[1]
    segment = lambda tile: jnp.minimum((first_row + tile * tm) // SEQ, n_lat_seg)

    def adaln_rows(tile, row0, n_rows, dst, inline):
        _adaln_rows(u_ref, dst, h_ref, mod_ref, pre_ref, k0, segment(tile), row0, n_rows, inline)

    @pl.when((i == 0) & (j == 0))
    def _():
        adaln_rows(0, 0, tm, 0, False)

    @pl.when(j < nf)
    def _():
        u = u_ref[slot]
        for c in range(tf // MXU_COLS):
            cols = slice(c * MXU_COLS, (c + 1) * MXU_COLS)
            a = jnp.dot(u, w1a_ref[:, cols], preferred_element_type=F32)
            b = jnp.dot(u, w1b_ref[:, cols], preferred_element_type=F32)
            hm_ref[j, :, cols] = (_silu(a) * b).astype(BF16)
        n_copy = tm // FFN_STAGE_ROWS
        rows = pl.ds(pl.multiple_of(jnp.minimum(j, n_copy - 1) * FFN_STAGE_ROWS, FFN_STAGE_ROWS), FFN_STAGE_ROWS)
        o_ref[rows, :] = h_ref[rows, :]

    @pl.when(j >= nf)
    def _():
        n = j - nf
        y = jnp.dot(hm_ref[0], w2_ref[0:tf, :], preferred_element_type=F32)
        for k in range(1, nf):
            y += jnp.dot(hm_ref[k], w2_ref[k * tf:(k + 1) * tf, :], preferred_element_type=F32)
        y_ref[n] = y
        adaln_rows(i + 1, n * (tm // nn), tm // nn, 1 - slot, True)

    @pl.when(j == nf + nn - 1)
    def _():
        gain = 0.5 * mod_ref[segment(i), k0 + 2:k0 + 3, :] * post_ref[...]
        ss = sum(jnp.sum(y_ref[n] * y_ref[n], axis=-1, keepdims=True) for n in range(nn))
        r = lax.rsqrt(ss / d + EPS)
        for n in range(nn):
            cols = slice(n * tn, (n + 1) * tn)
            o_ref[:, cols] = o_ref[:, cols] + (y_ref[n] * r) * gain[:, cols]


def _ffn_call(h, mod_l, k0, pre_g, post_g, w1, w2, lk, nrows, tm, n_lat_seg, first_row=0, out_rows=None,
              into=None):
    d = h.shape[1]
    passthrough = [] if into is None else [into]
    out_rows = into.shape[0] if passthrough else (out_rows or nrows)
    tile0 = first_row // tm
    f = w2.shape[2]
    tf = tn = FFN_TILE
    nf, nn = f // tf, d // tn
    n_tiles = nrows // tm
    h_tile = lambda i, j: (jnp.minimum(i + jnp.where(j >= nf, 1, 0), n_tiles - 1), 0)
    w2_tile = lambda j: jnp.where(j < nf, nn - 1, j - nf)
    return pl.pallas_call(
        functools.partial(_ffn_kernel, k0=k0, nf=nf, tf=tf, nn=nn, tn=tn, tm=tm, n_lat_seg=n_lat_seg,
                          first_row=first_row),
        grid=(n_tiles, nf + nn),
        in_specs=[
            pl.BlockSpec((tm, d), h_tile),
            pl.BlockSpec(mod_l.shape, lambda i, j: (0, 0, 0)),
            pl.BlockSpec((1, d), lambda i, j: (0, 0)),
            pl.BlockSpec((1, d), lambda i, j: (0, 0)),
            pl.BlockSpec((None, None, d, tf), lambda i, j: (*lk, 0, jnp.minimum(j, nf - 1))),
            pl.BlockSpec((None, None, d, tf), lambda i, j: (*lk, 0, jnp.minimum(j, nf - 1) + nf)),
            pl.BlockSpec((None, None, f, tn), lambda i, j: (*lk, 0, w2_tile(j))),
        ] + [pl.BlockSpec(memory_space=pl.ANY)] * len(passthrough),
        out_specs=pl.BlockSpec((tm, d), lambda i, j: (tile0 + i, 0)),
        out_shape=jax.ShapeDtypeStruct((out_rows, d), F32),
        input_output_aliases={7: 0} if passthrough else {},
        scratch_shapes=[pltpu.VMEM((2, tm, d), BF16), pltpu.VMEM((nf, tm, tf), BF16),
                        pltpu.VMEM((nn, tm, tn), F32)],
        compiler_params=_params(("arbitrary", "arbitrary")),
        name="ffn_sublayer",
    )(h, mod_l, pre_g.reshape(1, d), post_g.reshape(1, d), w1, w1, w2, *passthrough)


def _inproj_kernel(h_ref, mod_ref, pre_ref, w_ref, o_ref, u_ref, *, tm, n_lat_seg, n_steps, n_pre):
    i = pl.program_id(0)
    j = pl.program_id(1)
    slot = i % 2

    def adaln_rows(tile, row0, n_rows, dst, inline):
        seg = jnp.minimum((tile * tm) // SEQ, n_lat_seg)
        _adaln_rows(u_ref, dst, h_ref, mod_ref, pre_ref, 3, seg, row0, n_rows, inline)

    def project():
        o_ref[...] = jnp.dot(u_ref[slot], w_ref[...], preferred_element_type=F32).astype(BF16)

    @pl.when((i == 0) & (j == 0))
    def _():
        adaln_rows(0, 0, tm, 0, False)

    @pl.when(j < n_steps - n_pre)
    def _():
        project()

    @pl.when(j >= n_steps - n_pre)
    def _():
        project()
        adaln_rows(i + 1, (j - (n_steps - n_pre)) * (tm // n_pre), tm // n_pre, 1 - slot, True)


def _inproj_call(h, mod_l, pre_g, w, layer, tm, n_lat_seg):
    nrows, d = h.shape
    n = w.shape[2]
    tn = PROJ_TILE
    n_tiles, n_steps = nrows // tm, n // tn
    n_pre = PROJ_LOOKAHEAD_STEPS
    h_tile = lambda i, j: (jnp.minimum(i + jnp.where(j >= n_steps - n_pre, 1, 0), n_tiles - 1), 0)
    return pl.pallas_call(
        functools.partial(_inproj_kernel, tm=tm, n_lat_seg=n_lat_seg, n_steps=n_steps, n_pre=n_pre),
        grid=(n_tiles, n_steps),
        in_specs=[
            pl.BlockSpec((tm, d), h_tile),
            pl.BlockSpec(mod_l.shape, lambda i, j: (0, 0, 0)),
            pl.BlockSpec((1, d), lambda i, j: (0, 0)),
            pl.BlockSpec((None, d, tn), lambda i, j: (layer, 0, j)),
        ],
        out_specs=pl.BlockSpec((tm, tn), lambda i, j: (i, j)),
        out_shape=jax.ShapeDtypeStruct((nrows, n), BF16),
        scratch_shapes=[pltpu.VMEM((2, tm, d), BF16)],
        compiler_params=_params(("arbitrary", "arbitrary")),
        name="mixer_in_proj",
    )(h, mod_l, pre_g.reshape(1, d), w)


def _rope(t, cos, sa, sb):
    return t * cos + pltpu.roll(t, LANES - 16, 1) * sa + pltpu.roll(t, 16, 1) * sb


def _rope_tables():
    pos = np.arange(SEQ)
    prow = jnp.asarray(pos // GRID_W, F32)
    pcol = jnp.asarray(pos % GRID_W, F32)
    half = 16
    inv = ROPE_BASE ** (-jnp.arange(half, dtype=F32) / half)
    lane = np.arange(LANES)
    freq = jnp.asarray(lane % half)
    use_col = jnp.asarray((lane % 64) >= 32)
    second = jnp.asarray((lane % 32) >= half)
    p = jnp.where(use_col[None, :], pcol[:, None], prow[:, None])
    ang = p * inv[freq][None, :]
    cos, sin = jnp.cos(ang), jnp.sin(ang)
    sa = jnp.where(second[None, :], 0.0, -sin)
    sb = jnp.where(second[None, :], sin, 0.0)
    return cos, sa, sb


def _mixer_specs(nb, width, cols, order):
    ctx_blk0 = nb * SEQ // CTX_LEN

    def spec(rows, col, w, ctx):
        def index(*g):
            b, h = order(*g)
            return ((ctx_blk0 + b) if ctx else b, col // w + h)
        return pl.BlockSpec((rows, w), index)

    return ([spec(SEQ, c, w, False) for c, w in zip(cols, width)]
            + [spec(CTX_LEN, c, w, True) for c, w in zip(cols, width)])


def _mixer_outs(nb, width, order, with_ctx):
    def index(*g):
        b, h = order(*g)
        return (b, h)
    specs = [pl.BlockSpec((SEQ, width), index)]
    shapes = [jax.ShapeDtypeStruct((nb * SEQ, BRANCH_W), BF16)]
    if with_ctx:
        specs.append(pl.BlockSpec((CTX_LEN, width), index))
        shapes.append(jax.ShapeDtypeStruct((nb * CTX_LEN, BRANCH_W), BF16))
    return specs, shapes


def _diff_kernel(ql_ref, kl_ref, vl_ref, qc_ref, kc_ref, vc_ref, cos_ref, sa_ref, sb_ref, lam_ref, g_ref,
                 *rest, lambda_init, with_ctx):
    ol_ref, oc_ref = (rest[0], rest[1]) if with_ctx else (rest[0], None)
    k_all, v_ones = rest[-2:]
    k_all[0:CTX_LEN, :] = kc_ref[...]
    k_all[CTX_LEN:, :] = _rope(kl_ref[...].astype(F32), cos_ref[...], sa_ref[...], sb_ref[...]).astype(BF16)
    v_ones[0:CTX_LEN, 0:LANES] = vc_ref[...]
    v_ones[CTX_LEN:, 0:LANES] = vl_ref[...]
    v_ones[:, LANES:] = jnp.ones((CTX_LEN + SEQ, LANES), BF16)

    lv = lam_ref[...]
    lam = (jnp.exp(jnp.sum(lv[0:1] * lv[1:2], axis=-1, keepdims=True))
           - jnp.exp(jnp.sum(lv[2:3] * lv[3:4], axis=-1, keepdims=True)) + lambda_init)
    lane = lax.broadcasted_iota(jnp.int32, (1, LANES), 1)

    def attend(q, nk):
        q = q * (A_HEAD_DIM ** -0.5)
        k = k_all[0:nk, :]
        v1 = v_ones[0:nk, :]

        rows = QBLK // DIFF_ROW_SPLIT
        qms = [jnp.where(mask, q[r * rows:(r + 1) * rows], 0.0).astype(BF16)
               for r in range(DIFF_ROW_SPLIT) for mask in (lane < A_HEAD_DIM, lane >= A_HEAD_DIM)]
        ss = [lax.dot_general(qm, k, NT_DIMS, preferred_element_type=F32) for qm in qms]
        es = [jnp.exp(s - jnp.max(s, axis=-1, keepdims=True)).astype(BF16) for s in ss]
        ols = [jnp.dot(e, v1, preferred_element_type=F32) for e in es]
        os = [ol[:, :LANES] / ol[:, LANES:] for ol in ols]
        o = jnp.concatenate([os[2 * r] - lam * os[2 * r + 1] for r in range(DIFF_ROW_SPLIT)], axis=0)
        return (_rms(o) * g_ref[...] * (1.0 - lambda_init)).astype(BF16)

    def block(t, carry):
        rows = pl.ds(pl.multiple_of(t * QBLK, QBLK), QBLK)
        q = _rope(ql_ref[rows, :].astype(F32), cos_ref[rows, :], sa_ref[rows, :], sb_ref[rows, :])
        ol_ref[rows, :] = attend(q, CTX_LEN + SEQ)
        return carry

    lax.fori_loop(0, SEQ // QBLK, block, 0, unroll=True)
    if with_ctx:
        oc_ref[...] = attend(qc_ref[...].astype(F32), CTX_LEN)


def _diff_call(proj, tabs, diff_lambda, diff_norm, lambda_init, nb, with_ctx):
    order = lambda b, h: (b, h)
    whole = lambda b, h: (0, 0)
    out_specs, out_shapes = _mixer_outs(nb, LANES, order, with_ctx)
    return pl.pallas_call(
        functools.partial(_diff_kernel, lambda_init=lambda_init, with_ctx=with_ctx),
        grid=(nb, A_HEADS),
        in_specs=_mixer_specs(nb, (LANES,) * 3, (COL_AQ, COL_AK, COL_AV), order) + [
            pl.BlockSpec((SEQ, LANES), whole),
            pl.BlockSpec((SEQ, LANES), whole),
            pl.BlockSpec((SEQ, LANES), whole),
            pl.BlockSpec((4, A_HEAD_DIM), whole),
            pl.BlockSpec((1, LANES), whole),
        ],
        out_specs=out_specs,
        out_shape=out_shapes,
        scratch_shapes=[pltpu.VMEM((CTX_LEN + SEQ, LANES), BF16), pltpu.VMEM((CTX_LEN + SEQ, 2 * LANES), BF16)],
        compiler_params=_params(("parallel", "parallel")),
        name="diff_attention",
    )(*([proj] * 6), *tabs, diff_lambda, diff_norm.reshape(1, LANES))


def _nbr_block_rows(cfg, qi, kj):
    n_rows = SEQ // GRID_W
    r0 = (0, ROWS_PER_QBLK, n_rows - ROWS_PER_QBLK)[cfg]
    ks = min(max(r0 - WIN_R // 2, 0), n_rows - NBR_KROWS)
    r = r0 + qi
    rs = min(max(r - WIN_R // 2, 0), n_rows - WIN_R)
    kr = ks + kj
    return rs <= kr < rs + WIN_R, kr - r + WIN_R - 1


def _nbr_bias_kernel(rpb_ref, o_ref):
    n_dr, n_dc = 2 * WIN_R - 1, 2 * WIN_C - 1
    base = pl.program_id(0) * (n_dr * n_dc)
    qc = lax.broadcasted_iota(jnp.int32, (GRID_W, LANES), 0)
    lane = lax.broadcasted_iota(jnp.int32, (GRID_W, LANES), 1)
    kc = lane % GRID_W
    dc = kc - qc + (WIN_C - 1)
    cs = jnp.clip(qc - WIN_C // 2, 0, GRID_W - WIN_C)
    col_ok = (kc >= cs) & (kc < cs + WIN_C)
    masked = jnp.full((GRID_W, LANES), NBR_MASKED, F32)

    def row_pattern(dr):
        acc = masked
        for x in range(n_dc):
            acc = jnp.where(dc == x, rpb_ref[base + dr * n_dc + x], acc)
        return jnp.where(col_ok, acc, masked)

    pats = [row_pattern(dr) for dr in range(n_dr)]

    def pattern(cfg, qi, kj):
        if kj >= NBR_KROWS:
            return masked
        ok, dr = _nbr_block_rows(cfg, qi, kj)
        return pats[dr] if ok else masked

    for cfg in range(3):
        for qi in range(ROWS_PER_QBLK):
            rows = slice(qi * GRID_W, (qi + 1) * GRID_W)
            for m in range(pl.cdiv(NBR_KEYS, LANES)):
                tile = jnp.where(lane < GRID_W, pattern(cfg, qi, 2 * m), pattern(cfg, qi, 2 * m + 1))
                width = min(LANES, NBR_KEYS - m * LANES)
                o_ref[0, cfg, rows, m * LANES:m * LANES + width] = tile[:, :width]


def _nbr_bias_call(rpb):
    heads = rpb.shape[0]
    return pl.pallas_call(
        _nbr_bias_kernel,
        grid=(heads,),
        in_specs=[pl.BlockSpec(memory_space=pltpu.SMEM)],
        out_specs=pl.BlockSpec((1, 3, QBLK, NBR_KEYS), lambda h: (h, 0, 0, 0)),
        out_shape=jax.ShapeDtypeStruct((heads, 3, QBLK, NBR_KEYS), F32),
        compiler_params=_params(("parallel",)),
        name="nbr_bias_table",
    )(rpb.astype(F32).reshape(-1))


def _nbr_kernel(ql_ref, kl_ref, vl_ref, qc_ref, kc_ref, vc_ref, bias_ref, *rest, with_ctx):
    ol_ref, oc_ref = (rest[0], rest[1]) if with_ctx else (rest[0], None)
    v1l, v1c = rest[-2:]
    scale = B_HEAD_DIM ** -0.5
    v1l[:, 0:LANES] = vl_ref[...]
    v1l[:, LANES:] = jnp.ones((SEQ, LANES), BF16)
    v1c[:, 0:LANES] = vc_ref[...]
    v1c[:, LANES:] = jnp.ones((CTX_LEN, LANES), BF16)
    kc = kc_ref[...]
    lat_blocks = SEQ // QBLK

    for blk in range(lat_blocks):
        cfg = 0 if blk == 0 else (2 if blk == lat_blocks - 1 else 1)
        start = min(max((blk * ROWS_PER_QBLK - WIN_R // 2) * GRID_W, 0), SEQ - NBR_KEYS)
        q = ql_ref[blk * QBLK:(blk + 1) * QBLK, :]
        s_lat = (lax.dot_general(q, kl_ref[start:start + NBR_KEYS, :], NT_DIMS, preferred_element_type=F32) * scale
                 + bias_ref[0, cfg])
        s_ctx = lax.dot_general(q, kc, NT_DIMS, preferred_element_type=F32) * scale
        m = jnp.maximum(jnp.max(s_lat, axis=-1, keepdims=True), jnp.max(s_ctx, axis=-1, keepdims=True))
        ol = (jnp.dot(jnp.exp(s_lat - m).astype(BF16), v1l[start:start + NBR_KEYS, :], preferred_element_type=F32)
              + jnp.dot(jnp.exp(s_ctx - m).astype(BF16), v1c[...], preferred_element_type=F32))
        ol_ref[blk * QBLK:(blk + 1) * QBLK, :] = (ol[:, :LANES] / ol[:, LANES:]).astype(BF16)

    if with_ctx:
        s = lax.dot_general(qc_ref[...], kc, NT_DIMS, preferred_element_type=F32) * scale
        e = jnp.exp(s - jnp.max(s, axis=-1, keepdims=True)).astype(BF16)
        ol = jnp.dot(e, v1c[...], preferred_element_type=F32)
        oc_ref[...] = (ol[:, :LANES] / ol[:, LANES:]).astype(BF16)


def _nbr_call(proj, bias_tab, nb, with_ctx):
    order = lambda h, b: (b, h)
    out_specs, out_shapes = _mixer_outs(nb, LANES, order, with_ctx)
    return pl.pallas_call(
        functools.partial(_nbr_kernel, with_ctx=with_ctx),
        grid=(B_HEADS, nb),
        in_specs=_mixer_specs(nb, (LANES,) * 3, (COL_BQ, COL_BK, COL_BV), order) + [
            pl.BlockSpec((1, 3, QBLK, NBR_KEYS), lambda h, b: (h, 0, 0, 0)),
        ],
        out_specs=out_specs,
        out_shape=out_shapes,
        scratch_shapes=[pltpu.VMEM((SEQ, 2 * LANES), BF16), pltpu.VMEM((CTX_LEN, 2 * LANES), BF16)],
        compiler_params=_params(("parallel", "parallel")),
        name="nbr_attention",
    )(*([proj] * 6), bias_tab)


def _ret_kernel(ql_ref, kl_ref, vl_ref, gl_ref, qc_ref, kc_ref, vc_ref, gc_ref, cos_ref, sa_ref, sb_ref,
                dec_ref, rn_ref, *rest, with_ctx):
    ol_ref, oc_ref = (rest[0], rest[1]) if with_ctx else (rest[0], None)
    of_ref, ob_ref, st_ref = rest[-3:]
    ch = QBLK
    n_chunks = SEQ // ch
    kscale = C_KEY_DIM ** -0.5

    def run():
        ii = lax.broadcasted_iota(jnp.int32, (ch, ch), 0)
        jj = lax.broadcasted_iota(jnp.int32, (ch, ch), 1)
        dist = (ii - jj).astype(F32)
        pos = lax.broadcasted_iota(jnp.int32, (ch, 1), 0).astype(F32)
        lane = lax.broadcasted_iota(jnp.int32, (1, LANES), 1)
        rn = rn_ref[...]

        def log_sigmoid(x):
            return -(jnp.log1p(jnp.exp(-jnp.abs(x))) + jnp.maximum(-x, 0.0))

        heads = []
        for e in range(2):
            lgf = log_sigmoid(dec_ref[0, 0, e:e + 1, 0:1])
            lgb = log_sigmoid(dec_ref[0, 1, e:e + 1, 0:1])
            heads.append(dict(
                mask=(lane >= e * C_KEY_DIM) & (lane < (e + 1) * C_KEY_DIM),
                intra_f=jnp.where(dist >= 0, jnp.exp(lgf * jnp.maximum(dist, 0.0)), 0.0),
                intra_b=jnp.where(dist <= 0, jnp.exp(lgb * jnp.maximum(-dist, 0.0)), 0.0),
                qdec_f=jnp.exp(lgf * (pos + 1.0)), kdec_f=jnp.exp(lgf * (ch - 1.0 - pos)),
                cdec_f=jnp.exp(lgf * ch),
                qdec_b=jnp.exp(lgb * (ch - pos)), kdec_b=jnp.exp(lgb * pos),
                cdec_b=jnp.exp(lgb * ch),
                vs=slice(e * LANES, (e + 1) * LANES),
            ))

        def finish(o, g):
            return (_silu(g.astype(F32)) * (_rms(o) * rn)).astype(BF16)

        qc = qc_ref[...].astype(F32)
        kc = kc_ref[...].astype(F32) * kscale
        for e, hd in enumerate(heads):
            v = vc_ref[:, hd["vs"]]
            st_ref[e] = lax.dot_general((kc * hd["kdec_f"]).astype(BF16), v, TN_DIMS, preferred_element_type=F32)
            st_ref[2 + e] = lax.dot_general((kc * hd["kdec_b"]).astype(BF16), v, TN_DIMS,
                                            preferred_element_type=F32)
            if with_ctx:
                qe = jnp.where(hd["mask"], qc, 0.0).astype(BF16)
                att = lax.dot_general(qe, kc.astype(BF16), NT_DIMS, preferred_element_type=F32)
                att = att * (hd["intra_f"] + hd["intra_b"])
                o = jnp.dot(att.astype(BF16), v, preferred_element_type=F32)
                oc_ref[:, hd["vs"]] = finish(o, gc_ref[:, hd["vs"]])

        def chunk_rows(c):
            return pl.ds(pl.multiple_of(c * ch, ch), ch)

        def sweep(c, fwd):
            rows = chunk_rows(c)
            tabs = (cos_ref[rows, :], sa_ref[rows, :], sb_ref[rows, :])
            q = _rope(ql_ref[rows, :].astype(F32), *tabs)
            k = _rope(kl_ref[rows, :].astype(F32), *tabs) * kscale
            kb = k.astype(BF16)
            for e, hd in enumerate(heads):
                d = "f" if fwd else "b"
                si = e if fwd else 2 + e
                v = vl_ref[rows, hd["vs"]]
                qe = jnp.where(hd["mask"], q, 0.0)
                att = lax.dot_general(qe.astype(BF16), kb, NT_DIMS, preferred_element_type=F32) * hd["intra_" + d]
                st = st_ref[si]
                o = (jnp.dot(att.astype(BF16), v, preferred_element_type=F32)
                     + jnp.dot((qe * hd["qdec_" + d]).astype(BF16), st.astype(BF16), preferred_element_type=F32))
                st_ref[si] = st * hd["cdec_" + d] + lax.dot_general(
                    (k * hd["kdec_" + d]).astype(BF16), v, TN_DIMS, preferred_element_type=F32)
                (of_ref if fwd else ob_ref)[rows, hd["vs"]] = o

        def finish_chunk(c):
            rows = chunk_rows(c)
            for hd in heads:
                vs = hd["vs"]
                ol_ref[rows, vs] = finish(of_ref[rows, vs] + ob_ref[rows, vs], gl_ref[rows, vs])

        def first_half(t, carry):
            sweep(t, True)
            sweep(n_chunks - 1 - t, False)
            return carry

        def second_half(t, carry):
            sweep(t, True)
            sweep(n_chunks - 1 - t, False)
            finish_chunk(t)
            finish_chunk(n_chunks - 1 - t)
            return carry

        lax.fori_loop(0, n_chunks // 2, first_half, 0, unroll=True)
        lax.fori_loop(n_chunks // 2, n_chunks, second_half, 0, unroll=True)

    run()


def _ret_call(proj, tabs, ret_decay, ret_norm, nb, with_ctx):
    pairs = C_HEADS // 2
    w2 = 2 * LANES
    order = lambda b, p: (b, p)
    whole = lambda b, p: (0, 0)
    out_specs, out_shapes = _mixer_outs(nb, w2, order, with_ctx)
    dec = jnp.broadcast_to(ret_decay.astype(F32).reshape(2, pairs, 2).transpose(1, 0, 2)[..., None],
                           (pairs, 2, 2, LANES))
    return pl.pallas_call(
        functools.partial(_ret_kernel, with_ctx=with_ctx),
        grid=(nb, pairs),
        in_specs=_mixer_specs(nb, (LANES, LANES, w2, w2), (COL_CQ, COL_CK, COL_CV, COL_CG), order) + [
            pl.BlockSpec((SEQ, LANES), whole),
            pl.BlockSpec((SEQ, LANES), whole),
            pl.BlockSpec((SEQ, LANES), whole),
            pl.BlockSpec((1, 2, 2, LANES), lambda b, p: (p, 0, 0, 0)),
            pl.BlockSpec((1, LANES), whole),
        ],
        out_specs=out_specs,
        out_shape=out_shapes,
        scratch_shapes=[
            pltpu.VMEM((SEQ, w2), F32),
            pltpu.VMEM((SEQ, w2), F32),
            pltpu.VMEM((4, LANES, LANES), F32),
        ],
        compiler_params=_params(("parallel", "parallel")),
        name="retention",
    )(*([proj] * 8), *tabs, dec, ret_norm.reshape(1, LANES))


def _merge_kernel(h_ref, mod_ref, post_ref, g_lo_ref, g_hi_ref, wb_ref, wo_ref, *rest, n_lat_tiles, with_ctx):
    d = h_ref.shape[1]
    o_ref = rest[-1]
    if with_ctx:
        is_lat = pl.program_id(0) < n_lat_tiles
        xs = [jnp.where(is_lat, rest[2 * k][...], rest[2 * k + 1][...]) for k in range(3)]
    else:
        xs = [rest[k][...] for k in range(3)]
    g_lo = g_lo_ref[...]
    g_hi = g_hi_ref[...]
    gates = (g_lo[:, :d], jnp.concatenate([g_lo[:, d:], g_hi[:, :d // 2]], axis=1), g_hi[:, d // 2:])
    m = None
    for k in range(3):
        t = jax.nn.sigmoid(gates[k].astype(F32)) * jnp.dot(xs[k], wb_ref[k], preferred_element_type=F32)
        m = t if m is None else m + t
    y = jnp.dot(m.astype(BF16), wo_ref[...], preferred_element_type=F32)
    o_ref[...] = h_ref[...] + mod_ref[0, 5:6, :] * (_rms(y) * post_ref[...])


def _merge_call(h, mod_l, post_g, branches, proj, wb, wo, layer, nrows, n_lat_seg):
    d = h.shape[1]
    tm = CTX_LEN
    with_ctx = len(branches[0]) == 2
    n_lat_tiles = n_lat_seg * SEQ // tm
    gw = 3 * d // 2
    g0 = COL_GATES // gw
    resident = dict(pipeline_mode=pl.Buffered(1))
    row = lambda i: (i, 0)
    lat_row = lambda i: (jnp.minimum(i, n_lat_tiles - 1), 0)
    ctx_row = lambda i: (jnp.maximum(i - n_lat_tiles, 0), 0)
    branch_specs, branch_args = [], []
    for outs in branches:
        branch_specs.append(pl.BlockSpec((tm, BRANCH_W), lat_row))
        if with_ctx:
            branch_specs.append(pl.BlockSpec((tm, BRANCH_W), ctx_row))
        branch_args.extend(outs)
    return pl.pallas_call(
        functools.partial(_merge_kernel, n_lat_tiles=n_lat_tiles, with_ctx=with_ctx),
        grid=(nrows // tm,),
        in_specs=[
            pl.BlockSpec((tm, d), row),
            pl.BlockSpec((1, N_MOD, d), lambda i: (jnp.minimum((i * tm) // SEQ, n_lat_seg), 0, 0)),
            pl.BlockSpec((1, d), lambda i: (0, 0)),
            pl.BlockSpec((tm, gw), lambda i: (i, g0)),
            pl.BlockSpec((tm, gw), lambda i: (i, g0 + 1)),
            pl.BlockSpec((None, 3, BRANCH_W, d), lambda i: (layer, 0, 0, 0), **resident),
            pl.BlockSpec((None, d, d), lambda i: (layer, 0, 0), **resident),
        ] + branch_specs,
        out_specs=pl.BlockSpec((tm, d), row),
        out_shape=jax.ShapeDtypeStruct((nrows, d), F32),
        compiler_params=_params(("parallel",)),
        name="mixer_merge",
    )(h, mod_l, post_g.reshape(1, d), proj, proj, wb, wo, *branch_args)


def kernel(x, c, ctx, c_ctx, w_mod, b_mod, pre_norm, post_norm, ffn_w_in, ffn_w_out, w_in, diff_lambda,
           diff_norm, na_rpb, ret_decay, ret_norm, w_branch, w_out):
    nb, seq, d = x.shape
    assert (seq, d) == (SEQ, D_MODEL) and ctx.shape == (nb, CTX_LEN, d)
    n_lat, n_ctx = nb * SEQ, nb * CTX_LEN
    n_all = n_lat + n_ctx
    tm = min(FFN_ROWS, n_ctx)
    pad = (-(nb + 1)) % 8
    cc = jnp.concatenate([c, c_ctx[None, :], jnp.zeros((pad, d), c.dtype)], axis=0)
    mods = _mod_call(cc, w_mod, b_mod)
    tabs = _rope_tables()
    w1, w2 = ffn_w_in.astype(BF16), ffn_w_out.astype(BF16)
    w_proj, wb, wo = w_in.astype(BF16), w_branch.astype(BF16), w_out.astype(BF16)

    for l in range(DEPTH):
        with_ctx = l < DEPTH - 1
        lambda_init = 0.8 - 0.6 * math.exp(-0.3 * l)
        nrows_out = n_all if with_ctx else n_lat
        mod_l = mods[l, :nb + 1].reshape(nb + 1, N_MOD, d)
        ffn = functools.partial(_ffn_call, tm=tm, n_lat_seg=nb)

        ffn1 = functools.partial(ffn, mod_l=mod_l, k0=0, pre_g=pre_norm[l, 0], post_g=post_norm[l, 0], w1=w1, w2=w2,
                                 lk=(l, 0))
        if l == 0:
            h = ffn1(x.reshape(n_lat, d), nrows=n_lat, out_rows=n_all)
            h = ffn1(ctx.reshape(n_ctx, d), nrows=n_ctx, first_row=n_lat, into=h)
        else:
            h = ffn1(h, nrows=n_all)
        proj = _inproj_call(h, mod_l, pre_norm[l, 1], w_proj, l, min(PROJ_ROWS, n_ctx), nb)
        oa = _diff_call(proj, tabs, diff_lambda[l], diff_norm[l], lambda_init, nb, with_ctx)
        ob = _nbr_call(proj, _nbr_bias_call(na_rpb[l]), nb, with_ctx)
        yr = _ret_call(proj, tabs, ret_decay[l], ret_norm[l], nb, with_ctx)
        h = _merge_call(h, mod_l, post_norm[l, 1], (oa, ob, yr), proj, wb, wo, l, nrows_out, nb)
        h = ffn(h, mod_l, 6, pre_norm[l, 2], post_norm[l, 2], w1, w2, (l, 1), nrows_out)
    return h.reshape(nb, SEQ, d)
```

```python
import functools
import math

import numpy as np
import jax
import jax.numpy as jnp
from jax import lax
from jax.experimental import pallas as pl
from jax.experimental.pallas import tpu as pltpu

D_MODEL = 2048
SEQ = 2048
DEPTH = 2
GRID_W = 64
CTX_LEN = 256
N_MOD = 9
FFN_DIM = 5632
A_HEADS = 8
A_HEAD_DIM = 64
B_HEADS = 8
B_HEAD_DIM = 128
WIN_R = 8
WIN_C = 16
C_HEADS = 8
C_KEY_DIM = 64
BRANCH_W = 1024
ROPE_BASE = 10000.0
EPS = 1e-6

COL_AQ, COL_AK, COL_AV = 0, 1024, 2048
COL_BQ, COL_BK, COL_BV = 3072, 4096, 5120
COL_CQ, COL_CK, COL_CV, COL_CG = 6144, 6656, 7168, 8192
COL_GATES = 9216
IN_COLS = 15360

LANES = 128
BF16_SUBLANES = 16
QBLK = 256
ROWS_PER_QBLK = QBLK // GRID_W
NBR_KROWS = ROWS_PER_QBLK + WIN_R - 1
NBR_KEYS = NBR_KROWS * GRID_W
VMEM_LIMIT = 60 * 1024 * 1024
FFN_ROWS = 512
PROJ_ROWS = 1024
FFN_TILE = 512
FFN_STAGE_ROWS = 64
MXU_COLS = 256
PROJ_TILE = 2560
PROJ_LOOKAHEAD_STEPS = 4
NBR_HEADS_PER_STEP = 2
DIFF_ROW_SPLIT = 2
NBR_MASKED = -1e30

BF16 = jnp.bfloat16
F32 = jnp.float32
NT_DIMS = (((1,), (1,)), ((), ()))
TN_DIMS = (((0,), (0,)), ((), ()))


def _params(sem):
    return pltpu.CompilerParams(dimension_semantics=sem, vmem_limit_bytes=VMEM_LIMIT)


def _rms(x):
    return x * lax.rsqrt(jnp.mean(x * x, axis=-1, keepdims=True) + EPS)


def _silu(x):
    return x * jax.nn.sigmoid(x)


def _row_chunks(row0, n_rows, rows_per, body, inline=False):
    def step(r, carry):
        body(pl.ds(pl.multiple_of(row0 + r * rows_per, rows_per), rows_per))
        return carry
    if inline:
        for r in range(n_rows // rows_per):
            step(r, 0)
    else:
        lax.fori_loop(0, n_rows // rows_per, step, 0, unroll=4)


def _adaln_rows(u_ref, dst, h_ref, mod_ref, pre_ref, k0, seg, row0, n_rows, inline):
    gain = pre_ref[...] * (1.0 + mod_ref[seg, k0 + 1:k0 + 2, :])
    shift = mod_ref[seg, k0:k0 + 1, :]

    def body(rows):
        u_ref[dst, rows, :] = (_rms(h_ref[rows, :]) * gain + shift).astype(u_ref.dtype)

    _row_chunks(row0, n_rows, BF16_SUBLANES, body, inline)


def _mod_kernel(cc_ref, w_ref, b_ref, o_ref):
    s = _silu(cc_ref[...]).astype(BF16)
    o_ref[0] = jnp.dot(s, w_ref[0].astype(BF16), preferred_element_type=F32) + b_ref[0]


def _mod_call(cc, w_mod, b_mod):
    depth, d, n = w_mod.shape
    rows = cc.shape[0]
    tn = 1024
    return pl.pallas_call(
        _mod_kernel,
        grid=(depth, n // tn),
        in_specs=[
            pl.BlockSpec((rows, d), lambda l, j: (0, 0)),
            pl.BlockSpec((1, d, tn), lambda l, j: (l, 0, j)),
            pl.BlockSpec((1, 1, tn), lambda l, j: (l, 0, j)),
        ],
        out_specs=pl.BlockSpec((1, rows, tn), lambda l, j: (l, 0, j)),
        out_shape=jax.ShapeDtypeStruct((depth, rows, n), F32),
        compiler_params=_params(("parallel", "parallel")),
        name="mod_vectors",
    )(cc, w_mod, b_mod.reshape(depth, 1, n))


def _ffn_kernel(h_ref, mod_ref, pre_ref, post_ref, w1a_ref, w1b_ref, w2_ref, *rest,
                k0, nf, tf, nn, tn, tm, n_lat_seg, first_row):
    o_ref, u_ref, hm_ref, y_ref = rest[-4:]
    i = pl.program_id(0)
    j = pl.program_id(1)
    slot = i % 2
    d = h_ref.shape[1]
    segment = lambda tile: jnp.minimum((first_row + tile * tm) // SEQ, n_lat_seg)

    def adaln_rows(tile, row0, n_rows, dst, inline):
        _adaln_rows(u_ref, dst, h_ref, mod_ref, pre_ref, k0, segment(tile), row0, n_rows, inline)

    @pl.when((i == 0) & (j == 0))
    def _():
        adaln_rows(0, 0, tm, 0, False)

    @pl.when(j < nf)
    def _():
        u = u_ref[slot]
        for c in range(tf // MXU_COLS):
            cols = slice(c * MXU_COLS, (c + 1) * MXU_COLS)
            a = jnp.dot(u, w1a_ref[:, cols], preferred_element_type=F32)
            b = jnp.dot(u, w1b_ref[:, cols], preferred_element_type=F32)
            hm_ref[j, :, cols] = (_silu(a) * b).astype(BF16)
        n_copy = tm // FFN_STAGE_ROWS
        rows = pl.ds(pl.multiple_of(jnp.minimum(j, n_copy - 1) * FFN_STAGE_ROWS, FFN_STAGE_ROWS), FFN_STAGE_ROWS)
        o_ref[rows, :] = h_ref[rows, :]

    @pl.when(j >= nf)
    def _():
        n = j - nf
        y = jnp.dot(hm_ref[0], w2_ref[0:tf, :], preferred_element_type=F32)
        for k in range(1, nf):
            y += jnp.dot(hm_ref[k], w2_ref[k * tf:(k + 1) * tf, :], preferred_element_type=F32)
        y_ref[n] = y
        adaln_rows(i + 1, n * (tm // nn), tm // nn, 1 - slot, True)

    @pl.when(j == nf + nn - 1)
    def _():
        gain = 0.5 * mod_ref[segment(i), k0 + 2:k0 + 3, :] * post_ref[...]
        ss = sum(jnp.sum(y_ref[n] * y_ref[n], axis=-1, keepdims=True) for n in range(nn))
        r = lax.rsqrt(ss / d + EPS)
        for n in range(nn):
            cols = slice(n * tn, (n + 1) * tn)
            o_ref[:, cols] = o_ref[:, cols] + (y_ref[n] * r) * gain[:, cols]


def _ffn_call(h, mod_l, k0, pre_g, post_g, w1, w2, lk, nrows, tm, n_lat_seg, first_row=0, out_rows=None,
              into=None):
    d = h.shape[1]
    passthrough = [] if into is None else [into]
    out_rows = into.shape[0] if passthrough else (out_rows or nrows)
    tile0 = first_row // tm
    f = w2.shape[2]
    tf = tn = FFN_TILE
    nf, nn = f // tf, d // tn
    n_tiles = nrows // tm
    h_tile = lambda i, j: (jnp.minimum(i + jnp.where(j >= nf, 1, 0), n_tiles - 1), 0)
    w2_tile = lambda j: jnp.where(j < nf, nn - 1, j - nf)
    return pl.pallas_call(
        functools.partial(_ffn_kernel, k0=k0, nf=nf, tf=tf, nn=nn, tn=tn, tm=tm, n_lat_seg=n_lat_seg,
                          first_row=first_row),
        grid=(n_tiles, nf + nn),
        in_specs=[
            pl.BlockSpec((tm, d), h_tile),
            pl.BlockSpec(mod_l.shape, lambda i, j: (0, 0, 0)),
            pl.BlockSpec((1, d), lambda i, j: (0, 0)),
            pl.BlockSpec((1, d), lambda i, j: (0, 0)),
            pl.BlockSpec((None, None, d, tf), lambda i, j: (*lk, 0, jnp.minimum(j, nf - 1))),
            pl.BlockSpec((None, None, d, tf), lambda i, j: (*lk, 0, jnp.minimum(j, nf - 1) + nf)),
            pl.BlockSpec((None, None, f, tn), lambda i, j: (*lk, 0, w2_tile(j))),
        ] + [pl.BlockSpec(memory_space=pl.ANY)] * len(passthrough),
        out_specs=pl.BlockSpec((tm, d), lambda i, j: (tile0 + i, 0)),
        out_shape=jax.ShapeDtypeStruct((out_rows, d), F32),
        input_output_aliases={7: 0} if passthrough else {},
        scratch_shapes=[pltpu.VMEM((2, tm, d), BF16), pltpu.VMEM((nf, tm, tf), BF16),
                        pltpu.VMEM((nn, tm, tn), F32)],
        compiler_params=_params(("arbitrary", "arbitrary")),
        name="ffn_sublayer",
    )(h, mod_l, pre_g.reshape(1, d), post_g.reshape(1, d), w1, w1, w2, *passthrough)


def _inproj_kernel(h_ref, mod_ref, pre_ref, w_ref, o_ref, u_ref, *, tm, n_lat_seg, n_steps, n_pre):
    i = pl.program_id(0)
    j = pl.program_id(1)
    slot = i % 2

    def adaln_rows(tile, row0, n_rows, dst, inline):
        seg = jnp.minimum((tile * tm) // SEQ, n_lat_seg)
        _adaln_rows(u_ref, dst, h_ref, mod_ref, pre_ref, 3, seg, row0, n_rows, inline)

    def project():
        o_ref[...] = jnp.dot(u_ref[slot], w_ref[...], preferred_element_type=F32).astype(BF16)

    @pl.when((i == 0) & (j == 0))
    def _():
        adaln_rows(0, 0, tm, 0, False)

    @pl.when(j < n_steps - n_pre)
    def _():
        project()

    @pl.when(j >= n_steps - n_pre)
    def _():
        project()
        adaln_rows(i + 1, (j - (n_steps - n_pre)) * (tm // n_pre), tm // n_pre, 1 - slot, True)


def _inproj_call(h, mod_l, pre_g, w, layer, tm, n_lat_seg):
    nrows, d = h.shape
    n = w.shape[2]
    tn = PROJ_TILE
    n_tiles, n_steps = nrows // tm, n // tn
    n_pre = PROJ_LOOKAHEAD_STEPS
    h_tile = lambda i, j: (jnp.minimum(i + jnp.where(j >= n_steps - n_pre, 1, 0), n_tiles - 1), 0)
    return pl.pallas_call(
        functools.partial(_inproj_kernel, tm=tm, n_lat_seg=n_lat_seg, n_steps=n_steps, n_pre=n_pre),
        grid=(n_tiles, n_steps),
        in_specs=[
            pl.BlockSpec((tm, d), h_tile),
            pl.BlockSpec(mod_l.shape, lambda i, j: (0, 0, 0)),
            pl.BlockSpec((1, d), lambda i, j: (0, 0)),
            pl.BlockSpec((None, d, tn), lambda i, j: (layer, 0, j)),
        ],
        out_specs=pl.BlockSpec((tm, tn), lambda i, j: (i, j)),
        out_shape=jax.ShapeDtypeStruct((nrows, n), BF16),
        scratch_shapes=[pltpu.VMEM((2, tm, d), BF16)],
        compiler_params=_params(("arbitrary", "arbitrary")),
        name="mixer_in_proj",
    )(h, mod_l, pre_g.reshape(1, d), w)


def _rope(t, cos, sa, sb):
    return t * cos + pltpu.roll(t, LANES - 16, 1) * sa + pltpu.roll(t, 16, 1) * sb


def _rope_tables():
    pos = np.arange(SEQ)
    prow = jnp.asarray(pos // GRID_W, F32)
    pcol = jnp.asarray(pos % GRID_W, F32)
    half = 16
    inv = ROPE_BASE ** (-jnp.arange(half, dtype=F32) / half)
    lane = np.arange(LANES)
    freq = jnp.asarray(lane % half)
    use_col = jnp.asarray((lane % 64) >= 32)
    second = jnp.asarray((lane % 32) >= half)
    p = jnp.where(use_col[None, :], pcol[:, None], prow[:, None])
    ang = p * inv[freq][None, :]
    cos, sin = jnp.cos(ang), jnp.sin(ang)
    sa = jnp.where(second[None, :], 0.0, -sin)
    sb = jnp.where(second[None, :], sin, 0.0)
    return cos, sa, sb


def _mixer_specs(nb, width, cols, order):
    ctx_blk0 = nb * SEQ // CTX_LEN

    def spec(rows, col, w, ctx):
        def index(*g):
            b, h = order(*g)
            return ((ctx_blk0 + b) if ctx else b, col // w + h)
        return pl.BlockSpec((rows, w), index)

    return ([spec(SEQ, c, w, False) for c, w in zip(cols, width)]
            + [spec(CTX_LEN, c, w, True) for c, w in zip(cols, width)])


def _mixer_outs(nb, width, order, with_ctx):
    def index(*g):
        b, h = order(*g)
        return (b, h)
    specs = [pl.BlockSpec((SEQ, width), index)]
    shapes = [jax.ShapeDtypeStruct((nb * SEQ, BRANCH_W), BF16)]
    if with_ctx:
        specs.append(pl.BlockSpec((CTX_LEN, width), index))
        shapes.append(jax.ShapeDtypeStruct((nb * CTX_LEN, BRANCH_W), BF16))
    return specs, shapes


def _diff_kernel(ql_ref, kl_ref, vl_ref, qc_ref, kc_ref, vc_ref, cos_ref, sa_ref, sb_ref, lam_ref, g_ref,
                 *rest, lambda_init, with_ctx):
    ol_ref, oc_ref = (rest[0], rest[1]) if with_ctx else (rest[0], None)
    k_all, v_ones = rest[-2:]
    k_all[0:CTX_LEN, :] = kc_ref[...]
    k_all[CTX_LEN:, :] = _rope(kl_ref[...].astype(F32), cos_ref[...], sa_ref[...], sb_ref[...]).astype(BF16)
    v_ones[0:CTX_LEN, 0:LANES] = vc_ref[...]
    v_ones[CTX_LEN:, 0:LANES] = vl_ref[...]
    v_ones[:, LANES:] = jnp.ones((CTX_LEN + SEQ, LANES), BF16)

    lv = lam_ref[...]
    lam = (jnp.exp(jnp.sum(lv[0:1] * lv[1:2], axis=-1, keepdims=True))
           - jnp.exp(jnp.sum(lv[2:3] * lv[3:4], axis=-1, keepdims=True)) + lambda_init)
    lane = lax.broadcasted_iota(jnp.int32, (1, LANES), 1)

    def attend(q, nk):
        q = q * (A_HEAD_DIM ** -0.5)
        k = k_all[0:nk, :]
        v1 = v_ones[0:nk, :]

        rows = QBLK // DIFF_ROW_SPLIT
        qms = [jnp.where(mask, q[r * rows:(r + 1) * rows], 0.0).astype(BF16)
               for r in range(DIFF_ROW_SPLIT) for mask in (lane < A_HEAD_DIM, lane >= A_HEAD_DIM)]
        ss = [lax.dot_general(qm, k, NT_DIMS, preferred_element_type=F32) for qm in qms]
        es = [jnp.exp(s - jnp.max(s, axis=-1, keepdims=True)).astype(BF16) for s in ss]
        ols = [jnp.dot(e, v1, preferred_element_type=F32) for e in es]
        os = [ol[:, :LANES] / ol[:, LANES:] for ol in ols]
        o = jnp.concatenate([os[2 * r] - lam * os[2 * r + 1] for r in range(DIFF_ROW_SPLIT)], axis=0)
        return (_rms(o) * g_ref[...] * (1.0 - lambda_init)).astype(BF16)

    def block(t, carry):
        rows = pl.ds(pl.multiple_of(t * QBLK, QBLK), QBLK)
        q = _rope(ql_ref[rows, :].astype(F32), cos_ref[rows, :], sa_ref[rows, :], sb_ref[rows, :])
        ol_ref[rows, :] = attend(q, CTX_LEN + SEQ)
        return carry

    lax.fori_loop(0, SEQ // QBLK, block, 0, unroll=True)
    if with_ctx:
        oc_ref[...] = attend(qc_ref[...].astype(F32), CTX_LEN)


def _diff_call(proj, tabs, diff_lambda, diff_norm, lambda_init, nb, with_ctx):
    order = lambda b, h: (b, h)
    whole = lambda b, h: (0, 0)
    out_specs, out_shapes = _mixer_outs(nb, LANES, order, with_ctx)
    return pl.pallas_call(
        functools.partial(_diff_kernel, lambda_init=lambda_init, with_ctx=with_ctx),
        grid=(nb, A_HEADS),
        in_specs=_mixer_specs(nb, (LANES,) * 3, (COL_AQ, COL_AK, COL_AV), order) + [
            pl.BlockSpec((SEQ, LANES), whole),
            pl.BlockSpec((SEQ, LANES), whole),
            pl.BlockSpec((SEQ, LANES), whole),
            pl.BlockSpec((4, A_HEAD_DIM), whole),
            pl.BlockSpec((1, LANES), whole),
        ],
        out_specs=out_specs,
        out_shape=out_shapes,
        scratch_shapes=[pltpu.VMEM((CTX_LEN + SEQ, LANES), BF16), pltpu.VMEM((CTX_LEN + SEQ, 2 * LANES), BF16)],
        compiler_params=_params(("parallel", "parallel")),
        name="diff_attention",
    )(*([proj] * 6), *tabs, diff_lambda, diff_norm.reshape(1, LANES))


def _nbr_block_rows(cfg, qi, kj):
    n_rows = SEQ // GRID_W
    r0 = (0, ROWS_PER_QBLK, n_rows - ROWS_PER_QBLK)[cfg]
    ks = min(max(r0 - WIN_R // 2, 0), n_rows - NBR_KROWS)
    r = r0 + qi
    rs = min(max(r - WIN_R // 2, 0), n_rows - WIN_R)
    kr = ks + kj
    return rs <= kr < rs + WIN_R, kr - r + WIN_R - 1


def _nbr_bias_kernel(rpb_ref, o_ref):
    n_dr, n_dc = 2 * WIN_R - 1, 2 * WIN_C - 1
    base = pl.program_id(0) * (n_dr * n_dc)
    qc = lax.broadcasted_iota(jnp.int32, (GRID_W, LANES), 0)
    lane = lax.broadcasted_iota(jnp.int32, (GRID_W, LANES), 1)
    kc = lane % GRID_W
    dc = kc - qc + (WIN_C - 1)
    cs = jnp.clip(qc - WIN_C // 2, 0, GRID_W - WIN_C)
    col_ok = (kc >= cs) & (kc < cs + WIN_C)
    masked = jnp.full((GRID_W, LANES), NBR_MASKED, F32)

    def row_pattern(dr):
        acc = masked
        for x in range(n_dc):
            acc = jnp.where(dc == x, rpb_ref[base + dr * n_dc + x], acc)
        return jnp.where(col_ok, acc, masked)

    pats = [row_pattern(dr) for dr in range(n_dr)]

    def pattern(cfg, qi, kj):
        if kj >= NBR_KROWS:
            return masked
        ok, dr = _nbr_block_rows(cfg, qi, kj)
        return pats[dr] if ok else masked

    for cfg in range(3):
        for qi in range(ROWS_PER_QBLK):
            rows = slice(qi * GRID_W, (qi + 1) * GRID_W)
            for m in range(pl.cdiv(NBR_KEYS, LANES)):
                tile = jnp.where(lane < GRID_W, pattern(cfg, qi, 2 * m), pattern(cfg, qi, 2 * m + 1))
                width = min(LANES, NBR_KEYS - m * LANES)
                o_ref[0, cfg, rows, m * LANES:m * LANES + width] = tile[:, :width]


def _nbr_bias_call(rpb):
    heads = rpb.shape[0]
    return pl.pallas_call(
        _nbr_bias_kernel,
        grid=(heads,),
        in_specs=[pl.BlockSpec(memory_space=pltpu.SMEM)],
        out_specs=pl.BlockSpec((1, 3, QBLK, NBR_KEYS), lambda h: (h, 0, 0, 0)),
        out_shape=jax.ShapeDtypeStruct((heads, 3, QBLK, NBR_KEYS), F32),
        compiler_params=_params(("parallel",)),
        name="nbr_bias_table",
    )(rpb.astype(F32).reshape(-1))


def _nbr_kernel(ql_ref, kl_ref, vl_ref, qc_ref, kc_ref, vc_ref, bias_ref, *rest, with_ctx):
    ol_ref, oc_ref = (rest[0], rest[1]) if with_ctx else (rest[0], None)
    v1l, v1c = rest[-2:]
    scale = B_HEAD_DIM ** -0.5
    lat_blocks = SEQ // QBLK

    for e in range(NBR_HEADS_PER_STEP):
        hs = slice(e * LANES, (e + 1) * LANES)
        v1l[e, :, 0:LANES] = vl_ref[:, hs]
        v1l[e, :, LANES:] = jnp.ones((SEQ, LANES), BF16)
        v1c[e, :, 0:LANES] = vc_ref[:, hs]
        v1c[e, :, LANES:] = jnp.ones((CTX_LEN, LANES), BF16)
        kc = kc_ref[:, hs]

        for blk in range(lat_blocks):
            cfg = 0 if blk == 0 else (2 if blk == lat_blocks - 1 else 1)
            start = min(max((blk * ROWS_PER_QBLK - WIN_R // 2) * GRID_W, 0), SEQ - NBR_KEYS)
            keys = slice(start, start + NBR_KEYS)
            q = ql_ref[blk * QBLK:(blk + 1) * QBLK, hs]
            s_lat = (lax.dot_general(q, kl_ref[keys, hs], NT_DIMS, preferred_element_type=F32) * scale
                     + bias_ref[e, cfg])
            s_ctx = lax.dot_general(q, kc, NT_DIMS, preferred_element_type=F32) * scale
            m = jnp.maximum(jnp.max(s_lat, axis=-1, keepdims=True), jnp.max(s_ctx, axis=-1, keepdims=True))
            ol = (jnp.dot(jnp.exp(s_lat - m).astype(BF16), v1l[e, keys, :], preferred_element_type=F32)
                  + jnp.dot(jnp.exp(s_ctx - m).astype(BF16), v1c[e], preferred_element_type=F32))
            ol_ref[blk * QBLK:(blk + 1) * QBLK, hs] = (ol[:, :LANES] / ol[:, LANES:]).astype(BF16)

        if with_ctx:
            s = lax.dot_general(qc_ref[:, hs], kc, NT_DIMS, preferred_element_type=F32) * scale
            p = jnp.exp(s - jnp.max(s, axis=-1, keepdims=True)).astype(BF16)
            ol = jnp.dot(p, v1c[e], preferred_element_type=F32)
            oc_ref[:, hs] = (ol[:, :LANES] / ol[:, LANES:]).astype(BF16)


def _nbr_call(proj, bias_tab, nb, with_ctx):
    order = lambda g, b: (b, g)
    width = NBR_HEADS_PER_STEP * LANES
    out_specs, out_shapes = _mixer_outs(nb, width, order, with_ctx)
    return pl.pallas_call(
        functools.partial(_nbr_kernel, with_ctx=with_ctx),
        grid=(B_HEADS // NBR_HEADS_PER_STEP, nb),
        in_specs=_mixer_specs(nb, (width,) * 3, (COL_BQ, COL_BK, COL_BV), order) + [
            pl.BlockSpec((NBR_HEADS_PER_STEP, 3, QBLK, NBR_KEYS), lambda g, b: (g, 0, 0, 0)),
        ],
        out_specs=out_specs,
        out_shape=out_shapes,
        scratch_shapes=[pltpu.VMEM((NBR_HEADS_PER_STEP, SEQ, 2 * LANES), BF16),
                        pltpu.VMEM((NBR_HEADS_PER_STEP, CTX_LEN, 2 * LANES), BF16)],
        compiler_params=_params(("parallel", "parallel")),
        name="nbr_attention",
    )(*([proj] * 6), bias_tab)


def _ret_kernel(ql_ref, kl_ref, vl_ref, gl_ref, qc_ref, kc_ref, vc_ref, gc_ref, cos_ref, sa_ref, sb_ref,
                dec_ref, rn_ref, *rest, with_ctx):
    ol_ref, oc_ref = (rest[0], rest[1]) if with_ctx else (rest[0], None)
    of_ref, ob_ref, st_ref = rest[-3:]
    ch = QBLK
    n_chunks = SEQ // ch
    kscale = C_KEY_DIM ** -0.5

    def run():
        ii = lax.broadcasted_iota(jnp.int32, (ch, ch), 0)
        jj = lax.broadcasted_iota(jnp.int32, (ch, ch), 1)
        dist = (ii - jj).astype(F32)
        pos = lax.broadcasted_iota(jnp.int32, (ch, 1), 0).astype(F32)
        lane = lax.broadcasted_iota(jnp.int32, (1, LANES), 1)
        rn = rn_ref[...]

        def log_sigmoid(x):
            return -(jnp.log1p(jnp.exp(-jnp.abs(x))) + jnp.maximum(-x, 0.0))

        heads = []
        for e in range(2):
            lgf = log_sigmoid(dec_ref[0, 0, e:e + 1, 0:1])
            lgb = log_sigmoid(dec_ref[0, 1, e:e + 1, 0:1])
            heads.append(dict(
                mask=(lane >= e * C_KEY_DIM) & (lane < (e + 1) * C_KEY_DIM),
                intra_f=jnp.where(dist >= 0, jnp.exp(lgf * jnp.maximum(dist, 0.0)), 0.0),
                intra_b=jnp.where(dist <= 0, jnp.exp(lgb * jnp.maximum(-dist, 0.0)), 0.0),
                qdec_f=jnp.exp(lgf * (pos + 1.0)), kdec_f=jnp.exp(lgf * (ch - 1.0 - pos)),
                cdec_f=jnp.exp(lgf * ch),
                qdec_b=jnp.exp(lgb * (ch - pos)), kdec_b=jnp.exp(lgb * pos),
                cdec_b=jnp.exp(lgb * ch),
                vs=slice(e * LANES, (e + 1) * LANES),
            ))

        def finish(o, g):
            return (_silu(g.astype(F32)) * (_rms(o) * rn)).astype(BF16)

        qc = qc_ref[...].astype(F32)
        kc = kc_ref[...].astype(F32) * kscale
        for e, hd in enumerate(heads):
            v = vc_ref[:, hd["vs"]]
            st_ref[e] = lax.dot_general((kc * hd["kdec_f"]).astype(BF16), v, TN_DIMS, preferred_element_type=F32)
            st_ref[2 + e] = lax.dot_general((kc * hd["kdec_b"]).astype(BF16), v, TN_DIMS,
                                            preferred_element_type=F32)
            if with_ctx:
                qe = jnp.where(hd["mask"], qc, 0.0).astype(BF16)
                att = lax.dot_general(qe, kc.astype(BF16), NT_DIMS, preferred_element_type=F32)
                att = att * (hd["intra_f"] + hd["intra_b"])
                o = jnp.dot(att.astype(BF16), v, preferred_element_type=F32)
                oc_ref[:, hd["vs"]] = finish(o, gc_ref[:, hd["vs"]])

        def chunk_rows(c):
            return pl.ds(pl.multiple_of(c * ch, ch), ch)

        def sweep(c, fwd):
            rows = chunk_rows(c)
            tabs = (cos_ref[rows, :], sa_ref[rows, :], sb_ref[rows, :])
            q = _rope(ql_ref[rows, :].astype(F32), *tabs)
            k = _rope(kl_ref[rows, :].astype(F32), *tabs) * kscale
            kb = k.astype(BF16)
            for e, hd in enumerate(heads):
                d = "f" if fwd else "b"
                si = e if fwd else 2 + e
                v = vl_ref[rows, hd["vs"]]
                qe = jnp.where(hd["mask"], q, 0.0)
                att = lax.dot_general(qe.astype(BF16), kb, NT_DIMS, preferred_element_type=F32) * hd["intra_" + d]
                st = st_ref[si]
                o = (jnp.dot(att.astype(BF16), v, preferred_element_type=F32)
                     + jnp.dot((qe * hd["qdec_" + d]).astype(BF16), st.astype(BF16), preferred_element_type=F32))
                st_ref[si] = st * hd["cdec_" + d] + lax.dot_general(
                    (k * hd["kdec_" + d]).astype(BF16), v, TN_DIMS, preferred_element_type=F32)
                (of_ref if fwd else ob_ref)[rows, hd["vs"]] = o

        def finish_chunk(c):
            rows = chunk_rows(c)
            for hd in heads:
                vs = hd["vs"]
                ol_ref[rows, vs] = finish(of_ref[rows, vs] + ob_ref[rows, vs], gl_ref[rows, vs])

        def first_half(t, carry):
            sweep(t, True)
            sweep(n_chunks - 1 - t, False)
            return carry

        def second_half(t, carry):
            sweep(t, True)
            sweep(n_chunks - 1 - t, False)
            finish_chunk(t)
            finish_chunk(n_chunks - 1 - t)
            return carry

        lax.fori_loop(0, n_chunks // 2, first_half, 0, unroll=True)
        lax.fori_loop(n_chunks // 2, n_chunks, second_half, 0, unroll=True)

    run()


def _ret_call(proj, tabs, ret_decay, ret_norm, nb, with_ctx):
    pairs = C_HEADS // 2
    w2 = 2 * LANES
    order = lambda b, p: (b, p)
    whole = lambda b, p: (0, 0)
    out_specs, out_shapes = _mixer_outs(nb, w2, order, with_ctx)
    dec = jnp.broadcast_to(ret_decay.astype(F32).reshape(2, pairs, 2).transpose(1, 0, 2)[..., None],
                           (pairs, 2, 2, LANES))
    return pl.pallas_call(
        functools.partial(_ret_kernel, with_ctx=with_ctx),
        grid=(nb, pairs),
        in_specs=_mixer_specs(nb, (LANES, LANES, w2, w2), (COL_CQ, COL_CK, COL_CV, COL_CG), order) + [
            pl.BlockSpec((SEQ, LANES), whole),
            pl.BlockSpec((SEQ, LANES), whole),
            pl.BlockSpec((SEQ, LANES), whole),
            pl.BlockSpec((1, 2, 2, LANES), lambda b, p: (p, 0, 0, 0)),
            pl.BlockSpec((1, LANES), whole),
        ],
        out_specs=out_specs,
        out_shape=out_shapes,
        scratch_shapes=[
            pltpu.VMEM((SEQ, w2), F32),
            pltpu.VMEM((SEQ, w2), F32),
            pltpu.VMEM((4, LANES, LANES), F32),
        ],
        compiler_params=_params(("parallel", "parallel")),
        name="retention",
    )(*([proj] * 8), *tabs, dec, ret_norm.reshape(1, LANES))


def _merge_kernel(h_ref, mod_ref, post_ref, g_lo_ref, g_hi_ref, wb_ref, wo_ref, *rest, n_lat_tiles, with_ctx):
    d = h_ref.shape[1]
    o_ref = rest[-1]
    if with_ctx:
        is_lat = pl.program_id(0) < n_lat_tiles
        xs = [jnp.where(is_lat, rest[2 * k][...], rest[2 * k + 1][...]) for k in range(3)]
    else:
        xs = [rest[k][...] for k in range(3)]
    g_lo = g_lo_ref[...]
    g_hi = g_hi_ref[...]
    gates = (g_lo[:, :d], jnp.concatenate([g_lo[:, d:], g_hi[:, :d // 2]], axis=1), g_hi[:, d // 2:])
    m = None
    for k in range(3):
        t = jax.nn.sigmoid(gates[k].astype(F32)) * jnp.dot(xs[k], wb_ref[k], preferred_element_type=F32)
        m = t if m is None else m + t
    y = jnp.dot(m.astype(BF16), wo_ref[...], preferred_element_type=F32)
    o_ref[...] = h_ref[...] + mod_ref[0, 5:6, :] * (_rms(y) * post_ref[...])


def _merge_call(h, mod_l, post_g, branches, proj, wb, wo, layer, nrows, n_lat_seg):
    d = h.shape[1]
    tm = CTX_LEN
    with_ctx = len(branches[0]) == 2
    n_lat_tiles = n_lat_seg * SEQ // tm
    gw = 3 * d // 2
    g0 = COL_GATES // gw
    resident = dict(pipeline_mode=pl.Buffered(1))
    row = lambda i: (i, 0)
    lat_row = lambda i: (jnp.minimum(i, n_lat_tiles - 1), 0)
    ctx_row = lambda i: (jnp.maximum(i - n_lat_tiles, 0), 0)
    branch_specs, branch_args = [], []
    for outs in branches:
        branch_specs.append(pl.BlockSpec((tm, BRANCH_W), lat_row))
        if with_ctx:
            branch_specs.append(pl.BlockSpec((tm, BRANCH_W), ctx_row))
        branch_args.extend(outs)
    return pl.pallas_call(
        functools.partial(_merge_kernel, n_lat_tiles=n_lat_tiles, with_ctx=with_ctx),
        grid=(nrows // tm,),
        in_specs=[
            pl.BlockSpec((tm, d), row),
            pl.BlockSpec((1, N_MOD, d), lambda i: (jnp.minimum((i * tm) // SEQ, n_lat_seg), 0, 0)),
            pl.BlockSpec((1, d), lambda i: (0, 0)),
            pl.BlockSpec((tm, gw), lambda i: (i, g0)),
            pl.BlockSpec((tm, gw), lambda i: (i, g0 + 1)),
            pl.BlockSpec((None, 3, BRANCH_W, d), lambda i: (layer, 0, 0, 0), **resident),
            pl.BlockSpec((None, d, d), lambda i: (layer, 0, 0), **resident),
        ] + branch_specs,
        out_specs=pl.BlockSpec((tm, d), row),
        out_shape=jax.ShapeDtypeStruct((nrows, d), F32),
        compiler_params=_params(("parallel",)),
        name="mixer_merge",
    )(h, mod_l, post_g.reshape(1, d), proj, proj, wb, wo, *branch_args)


def kernel(x, c, ctx, c_ctx, w_mod, b_mod, pre_norm, post_norm, ffn_w_in, ffn_w_out, w_in, diff_lambda,
           diff_norm, na_rpb, ret_decay, ret_norm, w_branch, w_out):
    nb, seq, d = x.shape
    assert (seq, d) == (SEQ, D_MODEL) and ctx.shape == (nb, CTX_LEN, d)
    n_lat, n_ctx = nb * SEQ, nb * CTX_LEN
    n_all = n_lat + n_ctx
    tm = min(FFN_ROWS, n_ctx)
    pad = (-(nb + 1)) % 8
    cc = jnp.concatenate([c, c_ctx[None, :], jnp.zeros((pad, d), c.dtype)], axis=0)
    mods = _mod_call(cc, w_mod, b_mod)
    tabs = _rope_tables()
    w1, w2 = ffn_w_in.astype(BF16), ffn_w_out.astype(BF16)
    w_proj, wb, wo = w_in.astype(BF16), w_branch.astype(BF16), w_out.astype(BF16)

    for l in range(DEPTH):
        with_ctx = l < DEPTH - 1
        lambda_init = 0.8 - 0.6 * math.exp(-0.3 * l)
        nrows_out = n_all if with_ctx else n_lat
        mod_l = mods[l, :nb + 1].reshape(nb + 1, N_MOD, d)
        ffn = functools.partial(_ffn_call, tm=tm, n_lat_seg=nb)

        ffn1 = functools.partial(ffn, mod_l=mod_l, k0=0, pre_g=pre_norm[l, 0], post_g=post_norm[l, 0], w1=w1, w2=w2,
                                 lk=(l, 0))
        if l == 0:
            h = ffn1(x.reshape(n_lat, d), nrows=n_lat, out_rows=n_all)
            h = ffn1(ctx.reshape(n_ctx, d), nrows=n_ctx, first_row=n_lat, into=h)
        else:
            h = ffn1(h, nrows=n_all)
        proj = _inproj_call(h, mod_l, pre_norm[l, 1], w_proj, l, min(PROJ_ROWS, n_ctx), nb)
        oa = _diff_call(proj, tabs, diff_lambda[l], diff_norm[l], lambda_init, nb, with_ctx)
        ob = _nbr_call(proj, _nbr_bias_call(na_rpb[l]), nb, with_ctx)
        yr = _ret_call(proj, tabs, ret_decay[l], ret_norm[l], nb, with_ctx)
        h = _merge_call(h, mod_l, post_norm[l, 1], (oa, ob, yr), proj, wb, wo, l, nrows_out, nb)
        h = ffn(h, mod_l, 6, pre_norm[l, 2], post_norm[l, 2], w1, w2, (l, 1), nrows_out)
    return h.reshape(nb, SEQ, d)
```

```python
import functools
import math

import numpy as np
import jax
import jax.numpy as jnp
from jax import lax
from jax.experimental import pallas as pl
from jax.experimental.pallas import tpu as pltpu

D_MODEL = 2048
SEQ = 2048
DEPTH = 2
GRID_W = 64
CTX_LEN = 256
N_MOD = 9
FFN_DIM = 5632
A_HEADS = 8
A_HEAD_DIM = 64
B_HEADS = 8
B_HEAD_DIM = 128
WIN_R = 8
WIN_C = 16
C_HEADS = 8
C_KEY_DIM = 64
BRANCH_W = 1024
ROPE_BASE = 10000.0
EPS = 1e-6

COL_AQ, COL_AK, COL_AV = 0, 1024, 2048
COL_BQ, COL_BK, COL_BV = 3072, 4096, 5120
COL_CQ, COL_CK, COL_CV, COL_CG = 6144, 6656, 7168, 8192
COL_GATES = 9216
IN_COLS = 15360

LANES = 128
BF16_SUBLANES = 16
QBLK = 256
ROWS_PER_QBLK = QBLK // GRID_W
NBR_KROWS = ROWS_PER_QBLK + WIN_R - 1
NBR_KEYS = NBR_KROWS * GRID_W
VMEM_LIMIT = 60 * 1024 * 1024
FFN_ROWS = 512
PROJ_ROWS = 1024
FFN_TILE = 512
FFN_RESIDENT_TILES = 2
FFN_STAGE_ROWS = 64
MXU_COLS = 256
PROJ_TILE = 2560
PROJ_LOOKAHEAD_STEPS = 4
NBR_HEADS_PER_STEP = 2
DIFF_ROW_SPLIT = 2
NBR_MASKED = -1e30

BF16 = jnp.bfloat16
F32 = jnp.float32
NT_DIMS = (((1,), (1,)), ((), ()))
TN_DIMS = (((0,), (0,)), ((), ()))


def _params(sem):
    return pltpu.CompilerParams(dimension_semantics=sem, vmem_limit_bytes=VMEM_LIMIT)


def _rms(x):
    return x * lax.rsqrt(jnp.mean(x * x, axis=-1, keepdims=True) + EPS)


def _silu(x):
    return x * jax.nn.sigmoid(x)


def _row_chunks(row0, n_rows, rows_per, body, inline=False):
    def step(r, carry):
        body(pl.ds(pl.multiple_of(row0 + r * rows_per, rows_per), rows_per))
        return carry
    if inline:
        for r in range(n_rows // rows_per):
            step(r, 0)
    else:
        lax.fori_loop(0, n_rows // rows_per, step, 0, unroll=4)


def _adaln_rows(u_ref, dst, h_ref, mod_ref, pre_ref, k0, seg, row0, n_rows, inline):
    gain = pre_ref[...] * (1.0 + mod_ref[seg, k0 + 1:k0 + 2, :])
    shift = mod_ref[seg, k0:k0 + 1, :]

    def body(rows):
        u_ref[dst, rows, :] = (_rms(h_ref[rows, :]) * gain + shift).astype(u_ref.dtype)

    _row_chunks(row0, n_rows, BF16_SUBLANES, body, inline)


def _mod_kernel(cc_ref, w_ref, b_ref, o_ref):
    s = _silu(cc_ref[...]).astype(BF16)
    o_ref[0] = jnp.dot(s, w_ref[0].astype(BF16), preferred_element_type=F32) + b_ref[0]


def _mod_call(cc, w_mod, b_mod):
    depth, d, n = w_mod.shape
    rows = cc.shape[0]
    tn = 1024
    return pl.pallas_call(
        _mod_kernel,
        grid=(depth, n // tn),
        in_specs=[
            pl.BlockSpec((rows, d), lambda l, j: (0, 0)),
            pl.BlockSpec((1, d, tn), lambda l, j: (l, 0, j)),
            pl.BlockSpec((1, 1, tn), lambda l, j: (l, 0, j)),
        ],
        out_specs=pl.BlockSpec((1, rows, tn), lambda l, j: (l, 0, j)),
        out_shape=jax.ShapeDtypeStruct((depth, rows, n), F32),
        compiler_params=_params(("parallel", "parallel")),
        name="mod_vectors",
    )(cc, w_mod, b_mod.reshape(depth, 1, n))


def _ffn_kernel(h_ref, mod_ref, pre_ref, post_ref, w1a_ref, w1b_ref, w2_ref, *rest,
                k0, nf, tf, nn, tn, tm, n_lat_seg, first_row, n_res):
    o_ref, u_ref, hm_ref, y_ref = rest[-4:]
    i = pl.program_id(0)
    j = pl.program_id(1)
    slot = i % 2
    d = h_ref.shape[1]
    segment = lambda tile: jnp.minimum((first_row + tile * tm) // SEQ, n_lat_seg)

    def adaln_rows(tile, row0, n_rows, dst, inline):
        _adaln_rows(u_ref, dst, h_ref, mod_ref, pre_ref, k0, segment(tile), row0, n_rows, inline)

    @pl.when((i == 0) & (j == 0))
    def _():
        adaln_rows(0, 0, tm, 0, False)

    def hidden_tile(wa_ref, wb_ref):
        u = u_ref[slot]
        for c in range(tf // MXU_COLS):
            cols = slice(c * MXU_COLS, (c + 1) * MXU_COLS)
            a = jnp.dot(u, wa_ref[:, cols], preferred_element_type=F32)
            b = jnp.dot(u, wb_ref[:, cols], preferred_element_type=F32)
            hm_ref[j, :, cols] = (_silu(a) * b).astype(BF16)
        n_copy = tm // FFN_STAGE_ROWS
        rows = pl.ds(pl.multiple_of(jnp.minimum(j, n_copy - 1) * FFN_STAGE_ROWS, FFN_STAGE_ROWS), FFN_STAGE_ROWS)
        o_ref[rows, :] = h_ref[rows, :]

    for r in range(n_res):
        pl.when(j == r)(functools.partial(hidden_tile, rest[r], rest[n_res + r]))
    pl.when((j >= n_res) & (j < nf))(functools.partial(hidden_tile, w1a_ref, w1b_ref))

    @pl.when(j >= nf)
    def _():
        n = j - nf
        y = jnp.dot(hm_ref[0], w2_ref[0:tf, :], preferred_element_type=F32)
        for k in range(1, nf):
            y += jnp.dot(hm_ref[k], w2_ref[k * tf:(k + 1) * tf, :], preferred_element_type=F32)
        y_ref[n] = y
        adaln_rows(i + 1, n * (tm // nn), tm // nn, 1 - slot, True)

    @pl.when(j == nf + nn - 1)
    def _():
        gain = 0.5 * mod_ref[segment(i), k0 + 2:k0 + 3, :] * post_ref[...]
        ss = sum(jnp.sum(y_ref[n] * y_ref[n], axis=-1, keepdims=True) for n in range(nn))
        r = lax.rsqrt(ss / d + EPS)
        for n in range(nn):
            cols = slice(n * tn, (n + 1) * tn)
            o_ref[:, cols] = o_ref[:, cols] + (y_ref[n] * r) * gain[:, cols]


def _ffn_call(h, mod_l, k0, pre_g, post_g, w1, w2, lk, nrows, tm, n_lat_seg, first_row=0, out_rows=None,
              into=None):
    d = h.shape[1]
    passthrough = [] if into is None else [into]
    out_rows = into.shape[0] if passthrough else (out_rows or nrows)
    tile0 = first_row // tm
    f = w2.shape[2]
    tf = tn = FFN_TILE
    nf, nn = f // tf, d // tn
    n_tiles = nrows // tm
    h_tile = lambda i, j: (jnp.minimum(i + jnp.where(j >= nf, 1, 0), n_tiles - 1), 0)
    w2_tile = lambda j: jnp.where(j < nf, nn - 1, j - nf)
    n_res = FFN_RESIDENT_TILES
    streamed = lambda j: jnp.clip(j, n_res, nf - 1)
    resident = [pl.BlockSpec((None, None, d, tf), functools.partial(lambda t, i, j: (*lk, 0, t), half * nf + r),
                             pipeline_mode=pl.Buffered(1))
                for half in range(2) for r in range(n_res)]
    return pl.pallas_call(
        functools.partial(_ffn_kernel, k0=k0, nf=nf, tf=tf, nn=nn, tn=tn, tm=tm, n_lat_seg=n_lat_seg,
                          first_row=first_row, n_res=n_res),
        grid=(n_tiles, nf + nn),
        in_specs=[
            pl.BlockSpec((tm, d), h_tile),
            pl.BlockSpec(mod_l.shape, lambda i, j: (0, 0, 0)),
            pl.BlockSpec((1, d), lambda i, j: (0, 0)),
            pl.BlockSpec((1, d), lambda i, j: (0, 0)),
            pl.BlockSpec((None, None, d, tf), lambda i, j: (*lk, 0, streamed(j))),
            pl.BlockSpec((None, None, d, tf), lambda i, j: (*lk, 0, streamed(j) + nf)),
            pl.BlockSpec((None, None, f, tn), lambda i, j: (*lk, 0, w2_tile(j))),
        ] + resident + [pl.BlockSpec(memory_space=pl.ANY)] * len(passthrough),
        out_specs=pl.BlockSpec((tm, d), lambda i, j: (tile0 + i, 0)),
        out_shape=jax.ShapeDtypeStruct((out_rows, d), F32),
        input_output_aliases={7 + 2 * n_res: 0} if passthrough else {},
        scratch_shapes=[pltpu.VMEM((2, tm, d), BF16), pltpu.VMEM((nf, tm, tf), BF16),
                        pltpu.VMEM((nn, tm, tn), F32)],
        compiler_params=_params(("arbitrary", "arbitrary")),
        name="ffn_sublayer",
    )(h, mod_l, pre_g.reshape(1, d), post_g.reshape(1, d), w1, w1, w2, *([w1] * (2 * n_res)), *passthrough)


def _inproj_kernel(h_ref, mod_ref, pre_ref, w_ref, o_ref, u_ref, *, tm, n_lat_seg, n_steps, n_pre):
    i = pl.program_id(0)
    j = pl.program_id(1)
    slot = i % 2

    def adaln_rows(tile, row0, n_rows, dst, inline):
        seg = jnp.minimum((tile * tm) // SEQ, n_lat_seg)
        _adaln_rows(u_ref, dst, h_ref, mod_ref, pre_ref, 3, seg, row0, n_rows, inline)

    def project():
        o_ref[...] = jnp.dot(u_ref[slot], w_ref[...], preferred_element_type=F32).astype(BF16)

    @pl.when((i == 0) & (j == 0))
    def _():
        adaln_rows(0, 0, tm, 0, False)

    @pl.when(j < n_steps - n_pre)
    def _():
        project()

    @pl.when(j >= n_steps - n_pre)
    def _():
        project()
        adaln_rows(i + 1, (j - (n_steps - n_pre)) * (tm // n_pre), tm // n_pre, 1 - slot, True)


def _inproj_call(h, mod_l, pre_g, w, layer, tm, n_lat_seg):
    nrows, d = h.shape
    n = w.shape[2]
    tn = PROJ_TILE
    n_tiles, n_steps = nrows // tm, n // tn
    n_pre = PROJ_LOOKAHEAD_STEPS
    h_tile = lambda i, j: (jnp.minimum(i + jnp.where(j >= n_steps - n_pre, 1, 0), n_tiles - 1), 0)
    return pl.pallas_call(
        functools.partial(_inproj_kernel, tm=tm, n_lat_seg=n_lat_seg, n_steps=n_steps, n_pre=n_pre),
        grid=(n_tiles, n_steps),
        in_specs=[
            pl.BlockSpec((tm, d), h_tile),
            pl.BlockSpec(mod_l.shape, lambda i, j: (0, 0, 0)),
            pl.BlockSpec((1, d), lambda i, j: (0, 0)),
            pl.BlockSpec((None, d, tn), lambda i, j: (layer, 0, j)),
        ],
        out_specs=pl.BlockSpec((tm, tn), lambda i, j: (i, j)),
        out_shape=jax.ShapeDtypeStruct((nrows, n), BF16),
        scratch_shapes=[pltpu.VMEM((2, tm, d), BF16)],
        compiler_params=_params(("arbitrary", "arbitrary")),
        name="mixer_in_proj",
    )(h, mod_l, pre_g.reshape(1, d), w)


def _rope(t, cos, sa, sb):
    return t * cos + pltpu.roll(t, LANES - 16, 1) * sa + pltpu.roll(t, 16, 1) * sb


def _rope_tables():
    pos = np.arange(SEQ)
    prow = jnp.asarray(pos // GRID_W, F32)
    pcol = jnp.asarray(pos % GRID_W, F32)
    half = 16
    inv = ROPE_BASE ** (-jnp.arange(half, dtype=F32) / half)
    lane = np.arange(LANES)
    freq = jnp.asarray(lane % half)
    use_col = jnp.asarray((lane % 64) >= 32)
    second = jnp.asarray((lane % 32) >= half)
    p = jnp.where(use_col[None, :], pcol[:, None], prow[:, None])
    ang = p * inv[freq][None, :]
    cos, sin = jnp.cos(ang), jnp.sin(ang)
    sa = jnp.where(second[None, :], 0.0, -sin)
    sb = jnp.where(second[None, :], sin, 0.0)
    return cos, sa, sb


def _mixer_specs(nb, width, cols, order):
    ctx_blk0 = nb * SEQ // CTX_LEN

    def spec(rows, col, w, ctx):
        def index(*g):
            b, h = order(*g)
            return ((ctx_blk0 + b) if ctx else b, col // w + h)
        return pl.BlockSpec((rows, w), index)

    return ([spec(SEQ, c, w, False) for c, w in zip(cols, width)]
            + [spec(CTX_LEN, c, w, True) for c, w in zip(cols, width)])


def _mixer_outs(nb, width, order, with_ctx):
    def index(*g):
        b, h = order(*g)
        return (b, h)
    specs = [pl.BlockSpec((SEQ, width), index)]
    shapes = [jax.ShapeDtypeStruct((nb * SEQ, BRANCH_W), BF16)]
    if with_ctx:
        specs.append(pl.BlockSpec((CTX_LEN, width), index))
        shapes.append(jax.ShapeDtypeStruct((nb * CTX_LEN, BRANCH_W), BF16))
    return specs, shapes


def _diff_kernel(ql_ref, kl_ref, vl_ref, qc_ref, kc_ref, vc_ref, cos_ref, sa_ref, sb_ref, lam_ref, g_ref,
                 *rest, lambda_init, with_ctx):
    ol_ref, oc_ref = (rest[0], rest[1]) if with_ctx else (rest[0], None)
    k_all, v_ones = rest[-2:]
    k_all[0:CTX_LEN, :] = kc_ref[...]
    k_all[CTX_LEN:, :] = _rope(kl_ref[...].astype(F32), cos_ref[...], sa_ref[...], sb_ref[...]).astype(BF16)
    v_ones[0:CTX_LEN, 0:LANES] = vc_ref[...]
    v_ones[CTX_LEN:, 0:LANES] = vl_ref[...]
    v_ones[:, LANES:] = jnp.ones((CTX_LEN + SEQ, LANES), BF16)

    lv = lam_ref[...]
    lam = (jnp.exp(jnp.sum(lv[0:1] * lv[1:2], axis=-1, keepdims=True))
           - jnp.exp(jnp.sum(lv[2:3] * lv[3:4], axis=-1, keepdims=True)) + lambda_init)
    lane = lax.broadcasted_iota(jnp.int32, (1, LANES), 1)

    def attend(q, nk):
        q = q * (A_HEAD_DIM ** -0.5)
        k = k_all[0:nk, :]
        v1 = v_ones[0:nk, :]

        rows = QBLK // DIFF_ROW_SPLIT
        qms = [jnp.where(mask, q[r * rows:(r + 1) * rows], 0.0).astype(BF16)
               for r in range(DIFF_ROW_SPLIT) for mask in (lane < A_HEAD_DIM, lane >= A_HEAD_DIM)]
        ss = [lax.dot_general(qm, k, NT_DIMS, preferred_element_type=F32) for qm in qms]
        es = [jnp.exp(s - jnp.max(s, axis=-1, keepdims=True)).astype(BF16) for s in ss]
        ols = [jnp.dot(e, v1, preferred_element_type=F32) for e in es]
        os = [ol[:, :LANES] / ol[:, LANES:] for ol in ols]
        o = jnp.concatenate([os[2 * r] - lam * os[2 * r + 1] for r in range(DIFF_ROW_SPLIT)], axis=0)
        return (_rms(o) * g_ref[...] * (1.0 - lambda_init)).astype(BF16)

    def block(t, carry):
        rows = pl.ds(pl.multiple_of(t * QBLK, QBLK), QBLK)
        q = _rope(ql_ref[rows, :].astype(F32), cos_ref[rows, :], sa_ref[rows, :], sb_ref[rows, :])
        ol_ref[rows, :] = attend(q, CTX_LEN + SEQ)
        return carry

    lax.fori_loop(0, SEQ // QBLK, block, 0, unroll=True)
    if with_ctx:
        oc_ref[...] = attend(qc_ref[...].astype(F32), CTX_LEN)


def _diff_call(proj, tabs, diff_lambda, diff_norm, lambda_init, nb, with_ctx):
    order = lambda b, h: (b, h)
    whole = lambda b, h: (0, 0)
    out_specs, out_shapes = _mixer_outs(nb, LANES, order, with_ctx)
    return pl.pallas_call(
        functools.partial(_diff_kernel, lambda_init=lambda_init, with_ctx=with_ctx),
        grid=(nb, A_HEADS),
        in_specs=_mixer_specs(nb, (LANES,) * 3, (COL_AQ, COL_AK, COL_AV), order) + [
            pl.BlockSpec((SEQ, LANES), whole),
            pl.BlockSpec((SEQ, LANES), whole),
            pl.BlockSpec((SEQ, LANES), whole),
            pl.BlockSpec((4, A_HEAD_DIM), whole),
            pl.BlockSpec((1, LANES), whole),
        ],
        out_specs=out_specs,
        out_shape=out_shapes,
        scratch_shapes=[pltpu.VMEM((CTX_LEN + SEQ, LANES), BF16), pltpu.VMEM((CTX_LEN + SEQ, 2 * LANES), BF16)],
        compiler_params=_params(("parallel", "parallel")),
        name="diff_attention",
    )(*([proj] * 6), *tabs, diff_lambda, diff_norm.reshape(1, LANES))


def _nbr_block_rows(cfg, qi, kj):
    n_rows = SEQ // GRID_W
    r0 = (0, ROWS_PER_QBLK, n_rows - ROWS_PER_QBLK)[cfg]
    ks = min(max(r0 - WIN_R // 2, 0), n_rows - NBR_KROWS)
    r = r0 + qi
    rs = min(max(r - WIN_R // 2, 0), n_rows - WIN_R)
    kr = ks + kj
    return rs <= kr < rs + WIN_R, kr - r + WIN_R - 1


def _nbr_bias_kernel(rpb_ref, o_ref):
    n_dr, n_dc = 2 * WIN_R - 1, 2 * WIN_C - 1
    base = pl.program_id(0) * (n_dr * n_dc)
    qc = lax.broadcasted_iota(jnp.int32, (GRID_W, LANES), 0)
    lane = lax.broadcasted_iota(jnp.int32, (GRID_W, LANES), 1)
    kc = lane % GRID_W
    dc = kc - qc + (WIN_C - 1)
    cs = jnp.clip(qc - WIN_C // 2, 0, GRID_W - WIN_C)
    col_ok = (kc >= cs) & (kc < cs + WIN_C)
    masked = jnp.full((GRID_W, LANES), NBR_MASKED, F32)

    def row_pattern(dr):
        acc = masked
        for x in range(n_dc):
            acc = jnp.where(dc == x, rpb_ref[base + dr * n_dc + x], acc)
        return jnp.where(col_ok, acc, masked)

    pats = [row_pattern(dr) for dr in range(n_dr)]

    def pattern(cfg, qi, kj):
        if kj >= NBR_KROWS:
            return masked
        ok, dr = _nbr_block_rows(cfg, qi, kj)
        return pats[dr] if ok else masked

    for cfg in range(3):
        for qi in range(ROWS_PER_QBLK):
            rows = slice(qi * GRID_W, (qi + 1) * GRID_W)
            for m in range(pl.cdiv(NBR_KEYS, LANES)):
                tile = jnp.where(lane < GRID_W, pattern(cfg, qi, 2 * m), pattern(cfg, qi, 2 * m + 1))
                width = min(LANES, NBR_KEYS - m * LANES)
                o_ref[0, cfg, rows, m * LANES:m * LANES + width] = tile[:, :width]


def _nbr_bias_call(rpb):
    heads = rpb.shape[0]
    return pl.pallas_call(
        _nbr_bias_kernel,
        grid=(heads,),
        in_specs=[pl.BlockSpec(memory_space=pltpu.SMEM)],
        out_specs=pl.BlockSpec((1, 3, QBLK, NBR_KEYS), lambda h: (h, 0, 0, 0)),
        out_shape=jax.ShapeDtypeStruct((heads, 3, QBLK, NBR_KEYS), F32),
        compiler_params=_params(("parallel",)),
        name="nbr_bias_table",
    )(rpb.astype(F32).reshape(-1))


def _nbr_kernel(ql_ref, kl_ref, vl_ref, qc_ref, kc_ref, vc_ref, bias_ref, *rest, with_ctx):
    ol_ref, oc_ref = (rest[0], rest[1]) if with_ctx else (rest[0], None)
    v1l, v1c = rest[-2:]
    scale = B_HEAD_DIM ** -0.5
    lat_blocks = SEQ // QBLK

    for e in range(NBR_HEADS_PER_STEP):
        hs = slice(e * LANES, (e + 1) * LANES)
        v1l[e, :, 0:LANES] = vl_ref[:, hs]
        v1l[e, :, LANES:] = jnp.ones((SEQ, LANES), BF16)
        v1c[e, :, 0:LANES] = vc_ref[:, hs]
        v1c[e, :, LANES:] = jnp.ones((CTX_LEN, LANES), BF16)
        kc = kc_ref[:, hs]

        for blk in range(lat_blocks):
            cfg = 0 if blk == 0 else (2 if blk == lat_blocks - 1 else 1)
            start = min(max((blk * ROWS_PER_QBLK - WIN_R // 2) * GRID_W, 0), SEQ - NBR_KEYS)
            keys = slice(start, start + NBR_KEYS)
            q = ql_ref[blk * QBLK:(blk + 1) * QBLK, hs]
            s_lat = (lax.dot_general(q, kl_ref[keys, hs], NT_DIMS, preferred_element_type=F32) * scale
                     + bias_ref[e, cfg])
            s_ctx = lax.dot_general(q, kc, NT_DIMS, preferred_element_type=F32) * scale
            m = jnp.maximum(jnp.max(s_lat, axis=-1, keepdims=True), jnp.max(s_ctx, axis=-1, keepdims=True))
            ol = (jnp.dot(jnp.exp(s_lat - m).astype(BF16), v1l[e, keys, :], preferred_element_type=F32)
                  + jnp.dot(jnp.exp(s_ctx - m).astype(BF16), v1c[e], preferred_element_type=F32))
            ol_ref[blk * QBLK:(blk + 1) * QBLK, hs] = (ol[:, :LANES] / ol[:, LANES:]).astype(BF16)

        if with_ctx:
            s = lax.dot_general(qc_ref[:, hs], kc, NT_DIMS, preferred_element_type=F32) * scale
            p = jnp.exp(s - jnp.max(s, axis=-1, keepdims=True)).astype(BF16)
            ol = jnp.dot(p, v1c[e], preferred_element_type=F32)
            oc_ref[:, hs] = (ol[:, :LANES] / ol[:, LANES:]).astype(BF16)


def _nbr_call(proj, bias_tab, nb, with_ctx):
    order = lambda g, b: (b, g)
    width = NBR_HEADS_PER_STEP * LANES
    out_specs, out_shapes = _mixer_outs(nb, width, order, with_ctx)
    return pl.pallas_call(
        functools.partial(_nbr_kernel, with_ctx=with_ctx),
        grid=(B_HEADS // NBR_HEADS_PER_STEP, nb),
        in_specs=_mixer_specs(nb, (width,) * 3, (COL_BQ, COL_BK, COL_BV), order) + [
            pl.BlockSpec((NBR_HEADS_PER_STEP, 3, QBLK, NBR_KEYS), lambda g, b: (g, 0, 0, 0)),
        ],
        out_specs=out_specs,
        out_shape=out_shapes,
        scratch_shapes=[pltpu.VMEM((NBR_HEADS_PER_STEP, SEQ, 2 * LANES), BF16),
                        pltpu.VMEM((NBR_HEADS_PER_STEP, CTX_LEN, 2 * LANES), BF16)],
        compiler_params=_params(("parallel", "parallel")),
        name="nbr_attention",
    )(*([proj] * 6), bias_tab)


def _ret_kernel(ql_ref, kl_ref, vl_ref, gl_ref, qc_ref, kc_ref, vc_ref, gc_ref, cos_ref, sa_ref, sb_ref,
                dec_ref, rn_ref, *rest, with_ctx):
    ol_ref, oc_ref = (rest[0], rest[1]) if with_ctx else (rest[0], None)
    of_ref, ob_ref, st_ref = rest[-3:]
    ch = QBLK
    n_chunks = SEQ // ch
    kscale = C_KEY_DIM ** -0.5

    def run():
        ii = lax.broadcasted_iota(jnp.int32, (ch, ch), 0)
        jj = lax.broadcasted_iota(jnp.int32, (ch, ch), 1)
        dist = (ii - jj).astype(F32)
        pos = lax.broadcasted_iota(jnp.int32, (ch, 1), 0).astype(F32)
        lane = lax.broadcasted_iota(jnp.int32, (1, LANES), 1)
        rn = rn_ref[...]

        def log_sigmoid(x):
            return -(jnp.log1p(jnp.exp(-jnp.abs(x))) + jnp.maximum(-x, 0.0))

        heads = []
        for e in range(2):
            lgf = log_sigmoid(dec_ref[0, 0, e:e + 1, 0:1])
            lgb = log_sigmoid(dec_ref[0, 1, e:e + 1, 0:1])
            heads.append(dict(
                mask=(lane >= e * C_KEY_DIM) & (lane < (e + 1) * C_KEY_DIM),
                intra_f=jnp.where(dist >= 0, jnp.exp(lgf * jnp.maximum(dist, 0.0)), 0.0),
                intra_b=jnp.where(dist <= 0, jnp.exp(lgb * jnp.maximum(-dist, 0.0)), 0.0),
                qdec_f=jnp.exp(lgf * (pos + 1.0)), kdec_f=jnp.exp(lgf * (ch - 1.0 - pos)),
                cdec_f=jnp.exp(lgf * ch),
                qdec_b=jnp.exp(lgb * (ch - pos)), kdec_b=jnp.exp(lgb * pos),
                cdec_b=jnp.exp(lgb * ch),
                vs=slice(e * LANES, (e + 1) * LANES),
            ))

        def finish(o, g):
            return (_silu(g.astype(F32)) * (_rms(o) * rn)).astype(BF16)

        qc = qc_ref[...].astype(F32)
        kc = kc_ref[...].astype(F32) * kscale
        for e, hd in enumerate(heads):
            v = vc_ref[:, hd["vs"]]
            st_ref[e] = lax.dot_general((kc * hd["kdec_f"]).astype(BF16), v, TN_DIMS, preferred_element_type=F32)
            st_ref[2 + e] = lax.dot_general((kc * hd["kdec_b"]).astype(BF16), v, TN_DIMS,
                                            preferred_element_type=F32)
            if with_ctx:
                qe = jnp.where(hd["mask"], qc, 0.0).astype(BF16)
                att = lax.dot_general(qe, kc.astype(BF16), NT_DIMS, preferred_element_type=F32)
                att = att * (hd["intra_f"] + hd["intra_b"])
                o = jnp.dot(att.astype(BF16), v, preferred_element_type=F32)
                oc_ref[:, hd["vs"]] = finish(o, gc_ref[:, hd["vs"]])

        def chunk_rows(c):
            return pl.ds(pl.multiple_of(c * ch, ch), ch)

        def sweep(c, fwd):
            rows = chunk_rows(c)
            tabs = (cos_ref[rows, :], sa_ref[rows, :], sb_ref[rows, :])
            q = _rope(ql_ref[rows, :].astype(F32), *tabs)
            k = _rope(kl_ref[rows, :].astype(F32), *tabs) * kscale
            kb = k.astype(BF16)
            for e, hd in enumerate(heads):
                d = "f" if fwd else "b"
                si = e if fwd else 2 + e
                v = vl_ref[rows, hd["vs"]]
                qe = jnp.where(hd["mask"], q, 0.0)
                att = lax.dot_general(qe.astype(BF16), kb, NT_DIMS, preferred_element_type=F32) * hd["intra_" + d]
                st = st_ref[si]
                o = (jnp.dot(att.astype(BF16), v, preferred_element_type=F32)
                     + jnp.dot((qe * hd["qdec_" + d]).astype(BF16), st.astype(BF16), preferred_element_type=F32))
                st_ref[si] = st * hd["cdec_" + d] + lax.dot_general(
                    (k * hd["kdec_" + d]).astype(BF16), v, TN_DIMS, preferred_element_type=F32)
                (of_ref if fwd else ob_ref)[rows, hd["vs"]] = o

        def finish_chunk(c):
            rows = chunk_rows(c)
            for hd in heads:
                vs = hd["vs"]
                ol_ref[rows, vs] = finish(of_ref[rows, vs] + ob_ref[rows, vs], gl_ref[rows, vs])

        def first_half(t, carry):
            sweep(t, True)
            sweep(n_chunks - 1 - t, False)
            return carry

        def second_half(t, carry):
            sweep(t, True)
            sweep(n_chunks - 1 - t, False)
            finish_chunk(t)
            finish_chunk(n_chunks - 1 - t)
            return carry

        lax.fori_loop(0, n_chunks // 2, first_half, 0, unroll=True)
        lax.fori_loop(n_chunks // 2, n_chunks, second_half, 0, unroll=True)

    run()


def _ret_call(proj, tabs, ret_decay, ret_norm, nb, with_ctx):
    pairs = C_HEADS // 2
    w2 = 2 * LANES
    order = lambda b, p: (b, p)
    whole = lambda b, p: (0, 0)
    out_specs, out_shapes = _mixer_outs(nb, w2, order, with_ctx)
    dec = jnp.broadcast_to(ret_decay.astype(F32).reshape(2, pairs, 2).transpose(1, 0, 2)[..., None],
                           (pairs, 2, 2, LANES))
    return pl.pallas_call(
        functools.partial(_ret_kernel, with_ctx=with_ctx),
        grid=(nb, pairs),
        in_specs=_mixer_specs(nb, (LANES, LANES, w2, w2), (COL_CQ, COL_CK, COL_CV, COL_CG), order) + [
            pl.BlockSpec((SEQ, LANES), whole),
            pl.BlockSpec((SEQ, LANES), whole),
            pl.BlockSpec((SEQ, LANES), whole),
            pl.BlockSpec((1, 2, 2, LANES), lambda b, p: (p, 0, 0, 0)),
            pl.BlockSpec((1, LANES), whole),
        ],
        out_specs=out_specs,
        out_shape=out_shapes,
        scratch_shapes=[
            pltpu.VMEM((SEQ, w2), F32),
            pltpu.VMEM((SEQ, w2), F32),
            pltpu.VMEM((4, LANES, LANES), F32),
        ],
        compiler_params=_params(("parallel", "parallel")),
        name="retention",
    )(*([proj] * 8), *tabs, dec, ret_norm.reshape(1, LANES))


def _merge_kernel(h_ref, mod_ref, post_ref, g_lo_ref, g_hi_ref, wb_ref, wo_ref, *rest, n_lat_tiles, with_ctx):
    d = h_ref.shape[1]
    o_ref = rest[-1]
    if with_ctx:
        is_lat = pl.program_id(0) < n_lat_tiles
        xs = [jnp.where(is_lat, rest[2 * k][...], rest[2 * k + 1][...]) for k in range(3)]
    else:
        xs = [rest[k][...] for k in range(3)]
    g_lo = g_lo_ref[...]
    g_hi = g_hi_ref[...]
    gates = (g_lo[:, :d], jnp.concatenate([g_lo[:, d:], g_hi[:, :d // 2]], axis=1), g_hi[:, d // 2:])
    m = None
    for k in range(3):
        t = jax.nn.sigmoid(gates[k].astype(F32)) * jnp.dot(xs[k], wb_ref[k], preferred_element_type=F32)
        m = t if m is None else m + t
    y = jnp.dot(m.astype(BF16), wo_ref[...], preferred_element_type=F32)
    o_ref[...] = h_ref[...] + mod_ref[0, 5:6, :] * (_rms(y) * post_ref[...])


def _merge_call(h, mod_l, post_g, branches, proj, wb, wo, layer, nrows, n_lat_seg):
    d = h.shape[1]
    tm = CTX_LEN
    with_ctx = len(branches[0]) == 2
    n_lat_tiles = n_lat_seg * SEQ // tm
    gw = 3 * d // 2
    g0 = COL_GATES // gw
    resident = dict(pipeline_mode=pl.Buffered(1))
    row = lambda i: (i, 0)
    lat_row = lambda i: (jnp.minimum(i, n_lat_tiles - 1), 0)
    ctx_row = lambda i: (jnp.maximum(i - n_lat_tiles, 0), 0)
    branch_specs, branch_args = [], []
    for outs in branches:
        branch_specs.append(pl.BlockSpec((tm, BRANCH_W), lat_row))
        if with_ctx:
            branch_specs.append(pl.BlockSpec((tm, BRANCH_W), ctx_row))
        branch_args.extend(outs)
    return pl.pallas_call(
        functools.partial(_merge_kernel, n_lat_tiles=n_lat_tiles, with_ctx=with_ctx),
        grid=(nrows // tm,),
        in_specs=[
            pl.BlockSpec((tm, d), row),
            pl.BlockSpec((1, N_MOD, d), lambda i: (jnp.minimum((i * tm) // SEQ, n_lat_seg), 0, 0)),
            pl.BlockSpec((1, d), lambda i: (0, 0)),
            pl.BlockSpec((tm, gw), lambda i: (i, g0)),
            pl.BlockSpec((tm, gw), lambda i: (i, g0 + 1)),
            pl.BlockSpec((None, 3, BRANCH_W, d), lambda i: (layer, 0, 0, 0), **resident),
            pl.BlockSpec((None, d, d), lambda i: (layer, 0, 0), **resident),
        ] + branch_specs,
        out_specs=pl.BlockSpec((tm, d), row),
        out_shape=jax.ShapeDtypeStruct((nrows, d), F32),
        compiler_params=_params(("parallel",)),
        name="mixer_merge",
    )(h, mod_l, post_g.reshape(1, d), proj, proj, wb, wo, *branch_args)


def kernel(x, c, ctx, c_ctx, w_mod, b_mod, pre_norm, post_norm, ffn_w_in, ffn_w_out, w_in, diff_lambda,
           diff_norm, na_rpb, ret_decay, ret_norm, w_branch, w_out):
    nb, seq, d = x.shape
    assert (seq, d) == (SEQ, D_MODEL) and ctx.shape == (nb, CTX_LEN, d)
    n_lat, n_ctx = nb * SEQ, nb * CTX_LEN
    n_all = n_lat + n_ctx
    tm = min(FFN_ROWS, n_ctx)
    pad = (-(nb + 1)) % 8
    cc = jnp.concatenate([c, c_ctx[None, :], jnp.zeros((pad, d), c.dtype)], axis=0)
    mods = _mod_call(cc, w_mod, b_mod)
    tabs = _rope_tables()
    w1, w2 = ffn_w_in.astype(BF16), ffn_w_out.astype(BF16)
    w_proj, wb, wo = w_in.astype(BF16), w_branch.astype(BF16), w_out.astype(BF16)

    for l in range(DEPTH):
        with_ctx = l < DEPTH - 1
        lambda_init = 0.8 - 0.6 * math.exp(-0.3 * l)
        nrows_out = n_all if with_ctx else n_lat
        mod_l = mods[l, :nb + 1].reshape(nb + 1, N_MOD, d)
        ffn = functools.partial(_ffn_call, tm=tm, n_lat_seg=nb)

        ffn1 = functools.partial(ffn, mod_l=mod_l, k0=0, pre_g=pre_norm[l, 0], post_g=post_norm[l, 0], w1=w1, w2=w2,
                                 lk=(l, 0))
        if l == 0:
            h = ffn1(x.reshape(n_lat, d), nrows=n_lat, out_rows=n_all)
            h = ffn1(ctx.reshape(n_ctx, d), nrows=n_ctx, first_row=n_lat, into=h)
        else:
            h = ffn1(h, nrows=n_all)
        proj = _inproj_call(h, mod_l, pre_norm[l, 1], w_proj, l, min(PROJ_ROWS, n_ctx), nb)
        oa = _diff_call(proj, tabs, diff_lambda[l], diff_norm[l], lambda_init, nb, with_ctx)
        ob = _nbr_call(proj, _nbr_bias_call(na_rpb[l]), nb, with_ctx)
        yr = _ret_call(proj, tabs, ret_decay[l], ret_norm[l], nb, with_ctx)
        h = _merge_call(h, mod_l, post_norm[l, 1], (oa, ob, yr), proj, wb, wo, l, nrows_out, nb)
        h = ffn(h, mod_l, 6, pre_norm[l, 2], post_norm[l, 2], w1, w2, (l, 1), nrows_out)
    return h.reshape(nb, SEQ, d)
```

```python
import functools
import math

import numpy as np
import jax
import jax.numpy as jnp
from jax import lax
from jax.experimental import pallas as pl
from jax.experimental.pallas import tpu as pltpu

D_MODEL = 2048
SEQ = 2048
DEPTH = 2
GRID_W = 64
CTX_LEN = 256
N_MOD = 9
FFN_DIM = 5632
A_HEADS = 8
A_HEAD_DIM = 64
B_HEADS = 8
B_HEAD_DIM = 128
WIN_R = 8
WIN_C = 16
C_HEADS = 8
C_KEY_DIM = 64
BRANCH_W = 1024
ROPE_BASE = 10000.0
EPS = 1e-6

COL_AQ, COL_AK, COL_AV = 0, 1024, 2048
COL_BQ, COL_BK, COL_BV = 3072, 4096, 5120
COL_CQ, COL_CK, COL_CV, COL_CG = 6144, 6656, 7168, 8192
COL_GATES = 9216
IN_COLS = 15360

LANES = 128
BF16_SUBLANES = 16
QBLK = 256
ROWS_PER_QBLK = QBLK // GRID_W
NBR_KROWS = ROWS_PER_QBLK + WIN_R - 1
NBR_KEYS = NBR_KROWS * GRID_W
VMEM_LIMIT = 60 * 1024 * 1024
FFN_ROWS = 512
PROJ_ROWS = 1024
FFN_TILE = 512
FFN_RESIDENT_TILES = 2
FFN_STAGE_ROWS = 64
MXU_COLS = 256
PROJ_TILE = 2560
PROJ_LOOKAHEAD_STEPS = 4
NBR_HEADS_PER_STEP = 2
DIFF_ROW_SPLIT = 2
NBR_MASKED = -1e30

BF16 = jnp.bfloat16
F32 = jnp.float32
NT_DIMS = (((1,), (1,)), ((), ()))
TN_DIMS = (((0,), (0,)), ((), ()))


def _params(sem):
    return pltpu.CompilerParams(dimension_semantics=sem, vmem_limit_bytes=VMEM_LIMIT)


def _rms(x):
    return x * lax.rsqrt(jnp.mean(x * x, axis=-1, keepdims=True) + EPS)


def _silu(x):
    return x * jax.nn.sigmoid(x)


def _row_chunks(row0, n_rows, rows_per, body, inline=False):
    def step(r, carry):
        body(pl.ds(pl.multiple_of(row0 + r * rows_per, rows_per), rows_per))
        return carry
    if inline:
        for r in range(n_rows // rows_per):
            step(r, 0)
    else:
        lax.fori_loop(0, n_rows // rows_per, step, 0, unroll=4)


def _adaln_rows(u_ref, dst, h_ref, mod_ref, pre_ref, k0, seg, row0, n_rows, inline):
    gain = pre_ref[...] * (1.0 + mod_ref[seg, k0 + 1:k0 + 2, :])
    shift = mod_ref[seg, k0:k0 + 1, :]

    def body(rows):
        u_ref[dst, rows, :] = (_rms(h_ref[rows, :]) * gain + shift).astype(u_ref.dtype)

    _row_chunks(row0, n_rows, BF16_SUBLANES, body, inline)


def _mod_kernel(cc_ref, w_ref, b_ref, o_ref):
    s = _silu(cc_ref[...]).astype(BF16)
    o_ref[0] = jnp.dot(s, w_ref[0].astype(BF16), preferred_element_type=F32) + b_ref[0]


def _mod_call(cc, w_mod, b_mod):
    depth, d, n = w_mod.shape
    rows = cc.shape[0]
    tn = 1024
    return pl.pallas_call(
        _mod_kernel,
        grid=(depth, n // tn),
        in_specs=[
            pl.BlockSpec((rows, d), lambda l, j: (0, 0)),
            pl.BlockSpec((1, d, tn), lambda l, j: (l, 0, j)),
            pl.BlockSpec((1, 1, tn), lambda l, j: (l, 0, j)),
        ],
        out_specs=pl.BlockSpec((1, rows, tn), lambda l, j: (l, 0, j)),
        out_shape=jax.ShapeDtypeStruct((depth, rows, n), F32),
        compiler_params=_params(("parallel", "parallel")),
        name="mod_vectors",
    )(cc, w_mod, b_mod.reshape(depth, 1, n))


def _ffn_kernel(h_ref, mod_ref, pre_ref, post_ref, w1a_ref, w1b_ref, w2_ref, *rest,
                k0, nf, tf, nn, tn, tm, n_lat_seg, first_row, n_res):
    o_ref, u_ref, hm_ref, y_ref = rest[-4:]
    i = pl.program_id(0)
    j = pl.program_id(1)
    slot = i % 2
    d = h_ref.shape[1]
    segment = lambda tile: jnp.minimum((first_row + tile * tm) // SEQ, n_lat_seg)

    def adaln_rows(tile, row0, n_rows, dst, inline):
        _adaln_rows(u_ref, dst, h_ref, mod_ref, pre_ref, k0, segment(tile), row0, n_rows, inline)

    @pl.when((i == 0) & (j == 0))
    def _():
        adaln_rows(0, 0, tm, 0, False)

    def hidden_tile(wa_ref, wb_ref):
        u = u_ref[slot]
        for c in range(tf // MXU_COLS):
            cols = slice(c * MXU_COLS, (c + 1) * MXU_COLS)
            a = jnp.dot(u, wa_ref[:, cols], preferred_element_type=F32)
            b = jnp.dot(u, wb_ref[:, cols], preferred_element_type=F32)
            hm_ref[j, :, cols] = _silu(a.astype(BF16)) * b.astype(BF16)
        n_copy = tm // FFN_STAGE_ROWS
        rows = pl.ds(pl.multiple_of(jnp.minimum(j, n_copy - 1) * FFN_STAGE_ROWS, FFN_STAGE_ROWS), FFN_STAGE_ROWS)
        o_ref[rows, :] = h_ref[rows, :]

    for r in range(n_res):
        pl.when(j == r)(functools.partial(hidden_tile, rest[r], rest[n_res + r]))
    pl.when((j >= n_res) & (j < nf))(functools.partial(hidden_tile, w1a_ref, w1b_ref))

    @pl.when(j >= nf)
    def _():
        n = j - nf
        y = jnp.dot(hm_ref[0], w2_ref[0:tf, :], preferred_element_type=F32)
        for k in range(1, nf):
            y += jnp.dot(hm_ref[k], w2_ref[k * tf:(k + 1) * tf, :], preferred_element_type=F32)
        y_ref[n] = y
        adaln_rows(i + 1, n * (tm // nn), tm // nn, 1 - slot, True)

    @pl.when(j == nf + nn - 1)
    def _():
        gain = 0.5 * mod_ref[segment(i), k0 + 2:k0 + 3, :] * post_ref[...]
        ss = sum(jnp.sum(y_ref[n] * y_ref[n], axis=-1, keepdims=True) for n in range(nn))
        r = lax.rsqrt(ss / d + EPS)
        for n in range(nn):
            cols = slice(n * tn, (n + 1) * tn)
            o_ref[:, cols] = o_ref[:, cols] + (y_ref[n] * r) * gain[:, cols]


def _ffn_call(h, mod_l, k0, pre_g, post_g, w1, w2, lk, nrows, tm, n_lat_seg, first_row=0, out_rows=None,
              into=None):
    d = h.shape[1]
    passthrough = [] if into is None else [into]
    out_rows = into.shape[0] if passthrough else (out_rows or nrows)
    tile0 = first_row // tm
    f = w2.shape[2]
    tf = tn = FFN_TILE
    nf, nn = f // tf, d // tn
    n_tiles = nrows // tm
    h_tile = lambda i, j: (jnp.minimum(i + jnp.where(j >= nf, 1, 0), n_tiles - 1), 0)
    w2_tile = lambda j: jnp.where(j < nf, nn - 1, j - nf)
    n_res = FFN_RESIDENT_TILES
    streamed = lambda j: jnp.clip(j, n_res, nf - 1)
    resident = [pl.BlockSpec((None, None, d, tf), functools.partial(lambda t, i, j: (*lk, 0, t), half * nf + r),
                             pipeline_mode=pl.Buffered(1))
                for half in range(2) for r in range(n_res)]
    return pl.pallas_call(
        functools.partial(_ffn_kernel, k0=k0, nf=nf, tf=tf, nn=nn, tn=tn, tm=tm, n_lat_seg=n_lat_seg,
                          first_row=first_row, n_res=n_res),
        grid=(n_tiles, nf + nn),
        in_specs=[
            pl.BlockSpec((tm, d), h_tile),
            pl.BlockSpec(mod_l.shape, lambda i, j: (0, 0, 0)),
            pl.BlockSpec((1, d), lambda i, j: (0, 0)),
            pl.BlockSpec((1, d), lambda i, j: (0, 0)),
            pl.BlockSpec((None, None, d, tf), lambda i, j: (*lk, 0, streamed(j))),
            pl.BlockSpec((None, None, d, tf), lambda i, j: (*lk, 0, streamed(j) + nf)),
            pl.BlockSpec((None, None, f, tn), lambda i, j: (*lk, 0, w2_tile(j))),
        ] + resident + [pl.BlockSpec(memory_space=pl.ANY)] * len(passthrough),
        out_specs=pl.BlockSpec((tm, d), lambda i, j: (tile0 + i, 0)),
        out_shape=jax.ShapeDtypeStruct((out_rows, d), F32),
        input_output_aliases={7 + 2 * n_res: 0} if passthrough else {},
        scratch_shapes=[pltpu.VMEM((2, tm, d), BF16), pltpu.VMEM((nf, tm, tf), BF16),
                        pltpu.VMEM((nn, tm, tn), F32)],
        compiler_params=_params(("arbitrary", "arbitrary")),
        name="ffn_sublayer",
    )(h, mod_l, pre_g.reshape(1, d), post_g.reshape(1, d), w1, w1, w2, *([w1] * (2 * n_res)), *passthrough)


def _inproj_kernel(h_ref, mod_ref, pre_ref, w_ref, o_ref, u_ref, *, tm, n_lat_seg, n_steps, n_pre):
    i = pl.program_id(0)
    j = pl.program_id(1)
    slot = i % 2

    def adaln_rows(tile, row0, n_rows, dst, inline):
        seg = jnp.minimum((tile * tm) // SEQ, n_lat_seg)
        _adaln_rows(u_ref, dst, h_ref, mod_ref, pre_ref, 3, seg, row0, n_rows, inline)

    def project():
        o_ref[...] = jnp.dot(u_ref[slot], w_ref[...], preferred_element_type=F32).astype(BF16)

    @pl.when((i == 0) & (j == 0))
    def _():
        adaln_rows(0, 0, tm, 0, False)

    @pl.when(j < n_steps - n_pre)
    def _():
        project()

    @pl.when(j >= n_steps - n_pre)
    def _():
        project()
        adaln_rows(i + 1, (j - (n_steps - n_pre)) * (tm // n_pre), tm // n_pre, 1 - slot, True)


def _inproj_call(h, mod_l, pre_g, w, layer, tm, n_lat_seg):
    nrows, d = h.shape
    n = w.shape[2]
    tn = PROJ_TILE
    n_tiles, n_steps = nrows // tm, n // tn
    n_pre = PROJ_LOOKAHEAD_STEPS
    h_tile = lambda i, j: (jnp.minimum(i + jnp.where(j >= n_steps - n_pre, 1, 0), n_tiles - 1), 0)
    return pl.pallas_call(
        functools.partial(_inproj_kernel, tm=tm, n_lat_seg=n_lat_seg, n_steps=n_steps, n_pre=n_pre),
        grid=(n_tiles, n_steps),
        in_specs=[
            pl.BlockSpec((tm, d), h_tile),
            pl.BlockSpec(mod_l.shape, lambda i, j: (0, 0, 0)),
            pl.BlockSpec((1, d), lambda i, j: (0, 0)),
            pl.BlockSpec((None, d, tn), lambda i, j: (layer, 0, j)),
        ],
        out_specs=pl.BlockSpec((tm, tn), lambda i, j: (i, j)),
        out_shape=jax.ShapeDtypeStruct((nrows, n), BF16),
        scratch_shapes=[pltpu.VMEM((2, tm, d), BF16)],
        compiler_params=_params(("arbitrary", "arbitrary")),
        name="mixer_in_proj",
    )(h, mod_l, pre_g.reshape(1, d), w)


def _rope(t, cos, sa, sb):
    return t * cos + pltpu.roll(t, LANES - 16, 1) * sa + pltpu.roll(t, 16, 1) * sb


def _rope_tables():
    pos = np.arange(SEQ)
    prow = jnp.asarray(pos // GRID_W, F32)
    pcol = jnp.asarray(pos % GRID_W, F32)
    half = 16
    inv = ROPE_BASE ** (-jnp.arange(half, dtype=F32) / half)
    lane = np.arange(LANES)
    freq = jnp.asarray(lane % half)
    use_col = jnp.asarray((lane % 64) >= 32)
    second = jnp.asarray((lane % 32) >= half)
    p = jnp.where(use_col[None, :], pcol[:, None], prow[:, None])
    ang = p * inv[freq][None, :]
    cos, sin = jnp.cos(ang), jnp.sin(ang)
    sa = jnp.where(second[None, :], 0.0, -sin)
    sb = jnp.where(second[None, :], sin, 0.0)
    return cos, sa, sb


def _mixer_specs(nb, width, cols, order):
    ctx_blk0 = nb * SEQ // CTX_LEN

    def spec(rows, col, w, ctx):
        def index(*g):
            b, h = order(*g)
            return ((ctx_blk0 + b) if ctx else b, col // w + h)
        return pl.BlockSpec((rows, w), index)

    return ([spec(SEQ, c, w, False) for c, w in zip(cols, width)]
            + [spec(CTX_LEN, c, w, True) for c, w in zip(cols, width)])


def _mixer_outs(nb, width, order, with_ctx):
    def index(*g):
        b, h = order(*g)
        return (b, h)
    specs = [pl.BlockSpec((SEQ, width), index)]
    shapes = [jax.ShapeDtypeStruct((nb * SEQ, BRANCH_W), BF16)]
    if with_ctx:
        specs.append(pl.BlockSpec((CTX_LEN, width), index))
        shapes.append(jax.ShapeDtypeStruct((nb * CTX_LEN, BRANCH_W), BF16))
    return specs, shapes


def _diff_kernel(ql_ref, kl_ref, vl_ref, qc_ref, kc_ref, vc_ref, cos_ref, sa_ref, sb_ref, lam_ref, g_ref,
                 *rest, lambda_init, with_ctx):
    ol_ref, oc_ref = (rest[0], rest[1]) if with_ctx else (rest[0], None)
    k_all, v_ones = rest[-2:]
    k_all[0:CTX_LEN, :] = kc_ref[...]
    k_all[CTX_LEN:, :] = _rope(kl_ref[...].astype(F32), cos_ref[...], sa_ref[...], sb_ref[...]).astype(BF16)
    v_ones[0:CTX_LEN, 0:LANES] = vc_ref[...]
    v_ones[CTX_LEN:, 0:LANES] = vl_ref[...]
    v_ones[:, LANES:] = jnp.ones((CTX_LEN + SEQ, LANES), BF16)

    lv = lam_ref[...]
    lam = (jnp.exp(jnp.sum(lv[0:1] * lv[1:2], axis=-1, keepdims=True))
           - jnp.exp(jnp.sum(lv[2:3] * lv[3:4], axis=-1, keepdims=True)) + lambda_init)
    lane = lax.broadcasted_iota(jnp.int32, (1, LANES), 1)

    def attend(q, nk):
        q = q * (A_HEAD_DIM ** -0.5)
        k = k_all[0:nk, :]
        v1 = v_ones[0:nk, :]

        rows = QBLK // DIFF_ROW_SPLIT
        qms = [jnp.where(mask, q[r * rows:(r + 1) * rows], 0.0).astype(BF16)
               for r in range(DIFF_ROW_SPLIT) for mask in (lane < A_HEAD_DIM, lane >= A_HEAD_DIM)]
        ss = [lax.dot_general(qm, k, NT_DIMS, preferred_element_type=F32) for qm in qms]
        es = [jnp.exp((s - jnp.max(s, axis=-1, keepdims=True)).astype(BF16)) for s in ss]
        ols = [jnp.dot(e, v1, preferred_element_type=F32) for e in es]
        os = [ol[:, :LANES] / ol[:, LANES:] for ol in ols]
        o = jnp.concatenate([os[2 * r] - lam * os[2 * r + 1] for r in range(DIFF_ROW_SPLIT)], axis=0)
        return (_rms(o) * g_ref[...] * (1.0 - lambda_init)).astype(BF16)

    def block(t, carry):
        rows = pl.ds(pl.multiple_of(t * QBLK, QBLK), QBLK)
        q = _rope(ql_ref[rows, :].astype(F32), cos_ref[rows, :], sa_ref[rows, :], sb_ref[rows, :])
        ol_ref[rows, :] = attend(q, CTX_LEN + SEQ)
        return carry

    lax.fori_loop(0, SEQ // QBLK, block, 0, unroll=True)
    if with_ctx:
        oc_ref[...] = attend(qc_ref[...].astype(F32), CTX_LEN)


def _diff_call(proj, tabs, diff_lambda, diff_norm, lambda_init, nb, with_ctx):
    order = lambda b, h: (b, h)
    whole = lambda b, h: (0, 0)
    out_specs, out_shapes = _mixer_outs(nb, LANES, order, with_ctx)
    return pl.pallas_call(
        functools.partial(_diff_kernel, lambda_init=lambda_init, with_ctx=with_ctx),
        grid=(nb, A_HEADS),
        in_specs=_mixer_specs(nb, (LANES,) * 3, (COL_AQ, COL_AK, COL_AV), order) + [
            pl.BlockSpec((SEQ, LANES), whole),
            pl.BlockSpec((SEQ, LANES), whole),
            pl.BlockSpec((SEQ, LANES), whole),
            pl.BlockSpec((4, A_HEAD_DIM), whole),
            pl.BlockSpec((1, LANES), whole),
        ],
        out_specs=out_specs,
        out_shape=out_shapes,
        scratch_shapes=[pltpu.VMEM((CTX_LEN + SEQ, LANES), BF16), pltpu.VMEM((CTX_LEN + SEQ, 2 * LANES), BF16)],
        compiler_params=_params(("parallel", "parallel")),
        name="diff_attention",
    )(*([proj] * 6), *tabs, diff_lambda, diff_norm.reshape(1, LANES))


def _nbr_block_rows(cfg, qi, kj):
    n_rows = SEQ // GRID_W
    r0 = (0, ROWS_PER_QBLK, n_rows - ROWS_PER_QBLK)[cfg]
    ks = min(max(r0 - WIN_R // 2, 0), n_rows - NBR_KROWS)
    r = r0 + qi
    rs = min(max(r - WIN_R // 2, 0), n_rows - WIN_R)
    kr = ks + kj
    return rs <= kr < rs + WIN_R, kr - r + WIN_R - 1


def _nbr_bias_kernel(rpb_ref, o_ref):
    n_dr, n_dc = 2 * WIN_R - 1, 2 * WIN_C - 1
    base = pl.program_id(0) * (n_dr * n_dc)
    qc = lax.broadcasted_iota(jnp.int32, (GRID_W, LANES), 0)
    lane = lax.broadcasted_iota(jnp.int32, (GRID_W, LANES), 1)
    kc = lane % GRID_W
    dc = kc - qc + (WIN_C - 1)
    cs = jnp.clip(qc - WIN_C // 2, 0, GRID_W - WIN_C)
    col_ok = (kc >= cs) & (kc < cs + WIN_C)
    masked = jnp.full((GRID_W, LANES), NBR_MASKED, F32)

    def row_pattern(dr):
        acc = masked
        for x in range(n_dc):
            acc = jnp.where(dc == x, rpb_ref[base + dr * n_dc + x], acc)
        return jnp.where(col_ok, acc, masked)

    pats = [row_pattern(dr) for dr in range(n_dr)]

    def pattern(cfg, qi, kj):
        if kj >= NBR_KROWS:
            return masked
        ok, dr = _nbr_block_rows(cfg, qi, kj)
        return pats[dr] if ok else masked

    for cfg in range(3):
        for qi in range(ROWS_PER_QBLK):
            rows = slice(qi * GRID_W, (qi + 1) * GRID_W)
            for m in range(pl.cdiv(NBR_KEYS, LANES)):
                tile = jnp.where(lane < GRID_W, pattern(cfg, qi, 2 * m), pattern(cfg, qi, 2 * m + 1))
                width = min(LANES, NBR_KEYS - m * LANES)
                o_ref[0, cfg, rows, m * LANES:m * LANES + width] = tile[:, :width]


def _nbr_bias_call(rpb):
    heads = rpb.shape[0]
    return pl.pallas_call(
        _nbr_bias_kernel,
        grid=(heads,),
        in_specs=[pl.BlockSpec(memory_space=pltpu.SMEM)],
        out_specs=pl.BlockSpec((1, 3, QBLK, NBR_KEYS), lambda h: (h, 0, 0, 0)),
        out_shape=jax.ShapeDtypeStruct((heads, 3, QBLK, NBR_KEYS), F32),
        compiler_params=_params(("parallel",)),
        name="nbr_bias_table",
    )(rpb.astype(F32).reshape(-1))


def _nbr_kernel(ql_ref, kl_ref, vl_ref, qc_ref, kc_ref, vc_ref, bias_ref, *rest, with_ctx):
    ol_ref, oc_ref = (rest[0], rest[1]) if with_ctx else (rest[0], None)
    v1l, v1c = rest[-2:]
    scale = B_HEAD_DIM ** -0.5
    lat_blocks = SEQ // QBLK

    for e in range(NBR_HEADS_PER_STEP):
        hs = slice(e * LANES, (e + 1) * LANES)
        v1l[e, :, 0:LANES] = vl_ref[:, hs]
        v1l[e, :, LANES:] = jnp.ones((SEQ, LANES), BF16)
        v1c[e, :, 0:LANES] = vc_ref[:, hs]
        v1c[e, :, LANES:] = jnp.ones((CTX_LEN, LANES), BF16)
        kc = kc_ref[:, hs]

        for blk in range(lat_blocks):
            cfg = 0 if blk == 0 else (2 if blk == lat_blocks - 1 else 1)
            start = min(max((blk * ROWS_PER_QBLK - WIN_R // 2) * GRID_W, 0), SEQ - NBR_KEYS)
            keys = slice(start, start + NBR_KEYS)
            q = ql_ref[blk * QBLK:(blk + 1) * QBLK, hs]
            s_lat = (lax.dot_general(q, kl_ref[keys, hs], NT_DIMS, preferred_element_type=F32) * scale
                     + bias_ref[e, cfg])
            s_ctx = lax.dot_general(q, kc, NT_DIMS, preferred_element_type=F32) * scale
            m = jnp.maximum(jnp.max(s_lat, axis=-1, keepdims=True), jnp.max(s_ctx, axis=-1, keepdims=True))
            ol = (jnp.dot(jnp.exp(s_lat - m).astype(BF16), v1l[e, keys, :], preferred_element_type=F32)
                  + jnp.dot(jnp.exp(s_ctx - m).astype(BF16), v1c[e], preferred_element_type=F32))
            ol_ref[blk * QBLK:(blk + 1) * QBLK, hs] = (ol[:, :LANES] / ol[:, LANES:]).astype(BF16)

        if with_ctx:
            s = lax.dot_general(qc_ref[:, hs], kc, NT_DIMS, preferred_element_type=F32) * scale
            p = jnp.exp(s - jnp.max(s, axis=-1, keepdims=True)).astype(BF16)
            ol = jnp.dot(p, v1c[e], preferred_element_type=F32)
            oc_ref[:, hs] = (ol[:, :LANES] / ol[:, LANES:]).astype(BF16)


def _nbr_call(proj, bias_tab, nb, with_ctx):
    order = lambda g, b: (b, g)
    width = NBR_HEADS_PER_STEP * LANES
    out_specs, out_shapes = _mixer_outs(nb, width, order, with_ctx)
    return pl.pallas_call(
        functools.partial(_nbr_kernel, with_ctx=with_ctx),
        grid=(B_HEADS // NBR_HEADS_PER_STEP, nb),
        in_specs=_mixer_specs(nb, (width,) * 3, (COL_BQ, COL_BK, COL_BV), order) + [
            pl.BlockSpec((NBR_HEADS_PER_STEP, 3, QBLK, NBR_KEYS), lambda g, b: (g, 0, 0, 0)),
        ],
        out_specs=out_specs,
        out_shape=out_shapes,
        scratch_shapes=[pltpu.VMEM((NBR_HEADS_PER_STEP, SEQ, 2 * LANES), BF16),
                        pltpu.VMEM((NBR_HEADS_PER_STEP, CTX_LEN, 2 * LANES), BF16)],
        compiler_params=_params(("parallel", "parallel")),
        name="nbr_attention",
    )(*([proj] * 6), bias_tab)


def _ret_kernel(ql_ref, kl_ref, vl_ref, gl_ref, qc_ref, kc_ref, vc_ref, gc_ref, cos_ref, sa_ref, sb_ref,
                dec_ref, rn_ref, *rest, with_ctx):
    ol_ref, oc_ref = (rest[0], rest[1]) if with_ctx else (rest[0], None)
    of_ref, ob_ref, st_ref = rest[-3:]
    ch = QBLK
    n_chunks = SEQ // ch
    kscale = C_KEY_DIM ** -0.5

    def run():
        ii = lax.broadcasted_iota(jnp.int32, (ch, ch), 0)
        jj = lax.broadcasted_iota(jnp.int32, (ch, ch), 1)
        dist = (ii - jj).astype(F32)
        pos = lax.broadcasted_iota(jnp.int32, (ch, 1), 0).astype(F32)
        lane = lax.broadcasted_iota(jnp.int32, (1, LANES), 1)
        rn = rn_ref[...]

        def log_sigmoid(x):
            return -(jnp.log1p(jnp.exp(-jnp.abs(x))) + jnp.maximum(-x, 0.0))

        heads = []
        for e in range(2):
            lgf = log_sigmoid(dec_ref[0, 0, e:e + 1, 0:1])
            lgb = log_sigmoid(dec_ref[0, 1, e:e + 1, 0:1])
            heads.append(dict(
                mask=(lane >= e * C_KEY_DIM) & (lane < (e + 1) * C_KEY_DIM),
                intra_f=jnp.where(dist >= 0, jnp.exp(lgf * jnp.maximum(dist, 0.0)), 0.0),
                intra_b=jnp.where(dist <= 0, jnp.exp(lgb * jnp.maximum(-dist, 0.0)), 0.0),
                qdec_f=jnp.exp(lgf * (pos + 1.0)), kdec_f=jnp.exp(lgf * (ch - 1.0 - pos)),
                cdec_f=jnp.exp(lgf * ch),
                qdec_b=jnp.exp(lgb * (ch - pos)), kdec_b=jnp.exp(lgb * pos),
                cdec_b=jnp.exp(lgb * ch),
                vs=slice(e * LANES, (e + 1) * LANES),
            ))

        def finish(o, g):
            return (_silu(g.astype(F32)) * (_rms(o) * rn)).astype(BF16)

        qc = qc_ref[...].astype(F32)
        kc = kc_ref[...].astype(F32) * kscale
        for e, hd in enumerate(heads):
            v = vc_ref[:, hd["vs"]]
            st_ref[e] = lax.dot_general((kc * hd["kdec_f"]).astype(BF16), v, TN_DIMS, preferred_element_type=F32)
            st_ref[2 + e] = lax.dot_general((kc * hd["kdec_b"]).astype(BF16), v, TN_DIMS,
                                            preferred_element_type=F32)
            if with_ctx:
                qe = jnp.where(hd["mask"], qc, 0.0).astype(BF16)
                att = lax.dot_general(qe, kc.astype(BF16), NT_DIMS, preferred_element_type=F32)
                att = att * (hd["intra_f"] + hd["intra_b"])
                o = jnp.dot(att.astype(BF16), v, preferred_element_type=F32)
                oc_ref[:, hd["vs"]] = finish(o, gc_ref[:, hd["vs"]])

        def chunk_rows(c):
            return pl.ds(pl.multiple_of(c * ch, ch), ch)

        def sweep(c, fwd):
            rows = chunk_rows(c)
            tabs = (cos_ref[rows, :], sa_ref[rows, :], sb_ref[rows, :])
            q = _rope(ql_ref[rows, :].astype(F32), *tabs)
            k = _rope(kl_ref[rows, :].astype(F32), *tabs) * kscale
            kb = k.astype(BF16)
            for e, hd in enumerate(heads):
                d = "f" if fwd else "b"
                si = e if fwd else 2 + e
                v = vl_ref[rows, hd["vs"]]
                qe = jnp.where(hd["mask"], q, 0.0)
                att = lax.dot_general(qe.astype(BF16), kb, NT_DIMS, preferred_element_type=F32) * hd["intra_" + d]
                st = st_ref[si]
                o = (jnp.dot(att.astype(BF16), v, preferred_element_type=F32)
                     + jnp.dot((qe * hd["qdec_" + d]).astype(BF16), st.astype(BF16), preferred_element_type=F32))
                st_ref[si] = st * hd["cdec_" + d] + lax.dot_general(
                    (k * hd["kdec_" + d]).astype(BF16), v, TN_DIMS, preferred_element_type=F32)
                (of_ref if fwd else ob_ref)[rows, hd["vs"]] = o

        def finish_chunk(c):
            rows = chunk_rows(c)
            for hd in heads:
                vs = hd["vs"]
                ol_ref[rows, vs] = finish(of_ref[rows, vs] + ob_ref[rows, vs], gl_ref[rows, vs])

        def first_half(t, carry):
            sweep(t, True)
            sweep(n_chunks - 1 - t, False)
            return carry

        def second_half(t, carry):
            sweep(t, True)
            sweep(n_chunks - 1 - t, False)
            finish_chunk(t)
            finish_chunk(n_chunks - 1 - t)
            return carry

        lax.fori_loop(0, n_chunks // 2, first_half, 0, unroll=True)
        lax.fori_loop(n_chunks // 2, n_chunks, second_half, 0, unroll=True)

    run()


def _ret_call(proj, tabs, ret_decay, ret_norm, nb, with_ctx):
    pairs = C_HEADS // 2
    w2 = 2 * LANES
    order = lambda b, p: (b, p)
    whole = lambda b, p: (0, 0)
    out_specs, out_shapes = _mixer_outs(nb, w2, order, with_ctx)
    dec = jnp.broadcast_to(ret_decay.astype(F32).reshape(2, pairs, 2).transpose(1, 0, 2)[..., None],
                           (pairs, 2, 2, LANES))
    return pl.pallas_call(
        functools.partial(_ret_kernel, with_ctx=with_ctx),
        grid=(nb, pairs),
        in_specs=_mixer_specs(nb, (LANES, LANES, w2, w2), (COL_CQ, COL_CK, COL_CV, COL_CG), order) + [
            pl.BlockSpec((SEQ, LANES), whole),
            pl.BlockSpec((SEQ, LANES), whole),
            pl.BlockSpec((SEQ, LANES), whole),
            pl.BlockSpec((1, 2, 2, LANES), lambda b, p: (p, 0, 0, 0)),
            pl.BlockSpec((1, LANES), whole),
        ],
        out_specs=out_specs,
        out_shape=out_shapes,
        scratch_shapes=[
            pltpu.VMEM((SEQ, w2), F32),
            pltpu.VMEM((SEQ, w2), F32),
            pltpu.VMEM((4, LANES, LANES), F32),
        ],
        compiler_params=_params(("parallel", "parallel")),
        name="retention",
    )(*([proj] * 8), *tabs, dec, ret_norm.reshape(1, LANES))


def _merge_kernel(h_ref, mod_ref, post_ref, g_lo_ref, g_hi_ref, wb_ref, wo_ref, *rest, n_lat_tiles, with_ctx):
    d = h_ref.shape[1]
    o_ref = rest[-1]
    if with_ctx:
        is_lat = pl.program_id(0) < n_lat_tiles
        xs = [jnp.where(is_lat, rest[2 * k][...], rest[2 * k + 1][...]) for k in range(3)]
    else:
        xs = [rest[k][...] for k in range(3)]
    g_lo = g_lo_ref[...]
    g_hi = g_hi_ref[...]
    gates = (g_lo[:, :d], jnp.concatenate([g_lo[:, d:], g_hi[:, :d // 2]], axis=1), g_hi[:, d // 2:])
    m = None
    for k in range(3):
        t = jax.nn.sigmoid(gates[k].astype(F32)) * jnp.dot(xs[k], wb_ref[k], preferred_element_type=F32)
        m = t if m is None else m + t
    y = jnp.dot(m.astype(BF16), wo_ref[...], preferred_element_type=F32)
    o_ref[...] = h_ref[...] + mod_ref[0, 5:6, :] * (_rms(y) * post_ref[...])


def _merge_call(h, mod_l, post_g, branches, proj, wb, wo, layer, nrows, n_lat_seg):
    d = h.shape[1]
    tm = CTX_LEN
    with_ctx = len(branches[0]) == 2
    n_lat_tiles = n_lat_seg * SEQ // tm
    gw = 3 * d // 2
    g0 = COL_GATES // gw
    resident = dict(pipeline_mode=pl.Buffered(1))
    row = lambda i: (i, 0)
    lat_row = lambda i: (jnp.minimum(i, n_lat_tiles - 1), 0)
    ctx_row = lambda i: (jnp.maximum(i - n_lat_tiles, 0), 0)
    branch_specs, branch_args = [], []
    for outs in branches:
        branch_specs.append(pl.BlockSpec((tm, BRANCH_W), lat_row))
        if with_ctx:
            branch_specs.append(pl.BlockSpec((tm, BRANCH_W), ctx_row))
        branch_args.extend(outs)
    return pl.pallas_call(
        functools.partial(_merge_kernel, n_lat_tiles=n_lat_tiles, with_ctx=with_ctx),
        grid=(nrows // tm,),
        in_specs=[
            pl.BlockSpec((tm, d), row),
            pl.BlockSpec((1, N_MOD, d), lambda i: (jnp.minimum((i * tm) // SEQ, n_lat_seg), 0, 0)),
            pl.BlockSpec((1, d), lambda i: (0, 0)),
            pl.BlockSpec((tm, gw), lambda i: (i, g0)),
            pl.BlockSpec((tm, gw), lambda i: (i, g0 + 1)),
            pl.BlockSpec((None, 3, BRANCH_W, d), lambda i: (layer, 0, 0, 0), **resident),
            pl.BlockSpec((None, d, d), lambda i: (layer, 0, 0), **resident),
        ] + branch_specs,
        out_specs=pl.BlockSpec((tm, d), row),
        out_shape=jax.ShapeDtypeStruct((nrows, d), F32),
        compiler_params=_params(("parallel",)),
        name="mixer_merge",
    )(h, mod_l, post_g.reshape(1, d), proj, proj, wb, wo, *branch_args)


def kernel(x, c, ctx, c_ctx, w_mod, b_mod, pre_norm, post_norm, ffn_w_in, ffn_w_out, w_in, diff_lambda,
           diff_norm, na_rpb, ret_decay, ret_norm, w_branch, w_out):
    nb, seq, d = x.shape
    assert (seq, d) == (SEQ, D_MODEL) and ctx.shape == (nb, CTX_LEN, d)
    n_lat, n_ctx = nb * SEQ, nb * CTX_LEN
    n_all = n_lat + n_ctx
    tm = min(FFN_ROWS, n_ctx)
    pad = (-(nb + 1)) % 8
    cc = jnp.concatenate([c, c_ctx[None, :], jnp.zeros((pad, d), c.dtype)], axis=0)
    mods = _mod_call(cc, w_mod, b_mod)
    tabs = _rope_tables()
    w1, w2 = ffn_w_in.astype(BF16), ffn_w_out.astype(BF16)
    w_proj, wb, wo = w_in.astype(BF16), w_branch.astype(BF16), w_out.astype(BF16)

    for l in range(DEPTH):
        with_ctx = l < DEPTH - 1
        lambda_init = 0.8 - 0.6 * math.exp(-0.3 * l)
        nrows_out = n_all if with_ctx else n_lat
        mod_l = mods[l, :nb + 1].reshape(nb + 1, N_MOD, d)
        ffn = functools.partial(_ffn_call, tm=tm, n_lat_seg=nb)

        ffn1 = functools.partial(ffn, mod_l=mod_l, k0=0, pre_g=pre_norm[l, 0], post_g=post_norm[l, 0], w1=w1, w2=w2,
                                 lk=(l, 0))
        if l == 0:
            h = ffn1(x.reshape(n_lat, d), nrows=n_lat, out_rows=n_all)
            h = ffn1(ctx.reshape(n_ctx, d), nrows=n_ctx, first_row=n_lat, into=h)
        else:
            h = ffn1(h, nrows=n_all)
        proj = _inproj_call(h, mod_l, pre_norm[l, 1], w_proj, l, min(PROJ_ROWS, n_ctx), nb)
        oa = _diff_call(proj, tabs, diff_lambda[l], diff_norm[l], lambda_init, nb, with_ctx)
        ob = _nbr_call(proj, _nbr_bias_call(na_rpb[l]), nb, with_ctx)
        yr = _ret_call(proj, tabs, ret_decay[l], ret_norm[l], nb, with_ctx)
        h = _merge_call(h, mod_l, post_norm[l, 1], (oa, ob, yr), proj, wb, wo, l, nrows_out, nb)
        h = ffn(h, mod_l, 6, pre_norm[l, 2], post_norm[l, 2], w1, w2, (l, 1), nrows_out)
    return h.reshape(nb, SEQ, d)
```

```python
import functools
import math

import numpy as np
import jax
import jax.numpy as jnp
from jax import lax
from jax.experimental import pallas as pl
from jax.experimental.pallas import tpu as pltpu

D_MODEL = 2048
SEQ = 2048
DEPTH = 2
GRID_W = 64
CTX_LEN = 256
N_MOD = 9
FFN_DIM = 5632
A_HEADS = 8
A_HEAD_DIM = 64
B_HEADS = 8
B_HEAD_DIM = 128
WIN_R = 8
WIN_C = 16
C_HEADS = 8
C_KEY_DIM = 64
BRANCH_W = 1024
ROPE_BASE = 10000.0
EPS = 1e-6

COL_AQ, COL_AK, COL_AV = 0, 1024, 2048
COL_BQ, COL_BK, COL_BV = 3072, 4096, 5120
COL_CQ, COL_CK, COL_CV, COL_CG = 6144, 6656, 7168, 8192
COL_GATES = 9216
IN_COLS = 15360

LANES = 128
BF16_SUBLANES = 16
QBLK = 256
ROWS_PER_QBLK = QBLK // GRID_W
NBR_KROWS = ROWS_PER_QBLK + WIN_R - 1
NBR_KEYS = NBR_KROWS * GRID_W
VMEM_LIMIT = 60 * 1024 * 1024
FFN_ROWS = 512
PROJ_ROWS = 1024
FFN_TILE = 512
FFN_RESIDENT_TILES = 2
FFN_STAGE_ROWS = 64
MXU_COLS = 256
PROJ_TILE = 2560
PROJ_LOOKAHEAD_STEPS = 4
NBR_HEADS_PER_STEP = 2
DIFF_ROW_SPLIT = 2
NBR_MASKED = -1e30

BF16 = jnp.bfloat16
F32 = jnp.float32
NT_DIMS = (((1,), (1,)), ((), ()))
TN_DIMS = (((0,), (0,)), ((), ()))


def _params(sem):
    return pltpu.CompilerParams(dimension_semantics=sem, vmem_limit_bytes=VMEM_LIMIT)


def _rms(x):
    return x * lax.rsqrt(jnp.mean(x * x, axis=-1, keepdims=True) + EPS)


def _silu(x):
    return x * jax.nn.sigmoid(x)


def _row_chunks(row0, n_rows, rows_per, body, inline=False):
    def step(r, carry):
        body(pl.ds(pl.multiple_of(row0 + r * rows_per, rows_per), rows_per))
        return carry
    if inline:
        for r in range(n_rows // rows_per):
            step(r, 0)
    else:
        lax.fori_loop(0, n_rows // rows_per, step, 0, unroll=4)


def _adaln_rows(u_ref, dst, h_ref, mod_ref, pre_ref, k0, seg, row0, n_rows, inline):
    gain = pre_ref[...] * (1.0 + mod_ref[seg, k0 + 1:k0 + 2, :])
    shift = mod_ref[seg, k0:k0 + 1, :]

    def body(rows):
        u_ref[dst, rows, :] = (_rms(h_ref[rows, :]) * gain + shift).astype(u_ref.dtype)

    _row_chunks(row0, n_rows, BF16_SUBLANES, body, inline)


def _mod_kernel(cc_ref, w_ref, b_ref, o_ref):
    s = _silu(cc_ref[...]).astype(BF16)
    o_ref[0] = jnp.dot(s, w_ref[0].astype(BF16), preferred_element_type=F32) + b_ref[0]


def _mod_call(cc, w_mod, b_mod):
    depth, d, n = w_mod.shape
    rows = cc.shape[0]
    tn = 1024
    return pl.pallas_call(
        _mod_kernel,
        grid=(depth, n // tn),
        in_specs=[
            pl.BlockSpec((rows, d), lambda l, j: (0, 0)),
            pl.BlockSpec((1, d, tn), lambda l, j: (l, 0, j)),
            pl.BlockSpec((1, 1, tn), lambda l, j: (l, 0, j)),
        ],
        out_specs=pl.BlockSpec((1, rows, tn), lambda l, j: (l, 0, j)),
        out_shape=jax.ShapeDtypeStruct((depth, rows, n), F32),
        compiler_params=_params(("parallel", "parallel")),
        name="mod_vectors",
    )(cc, w_mod, b_mod.reshape(depth, 1, n))


def _ffn_kernel(h_ref, mod_ref, pre_ref, post_ref, w1a_ref, w1b_ref, w2_ref, *rest,
                k0, nf, tf, nn, tn, tm, n_lat_seg, n_res):
    o_ref, u_ref, hm_ref, y_ref = rest[-4:]
    i = pl.program_id(0)
    j = pl.program_id(1)
    slot = i % 2
    d = h_ref.shape[1]
    segment = lambda tile: jnp.minimum((tile * tm) // SEQ, n_lat_seg)

    def adaln_rows(tile, row0, n_rows, dst, inline):
        _adaln_rows(u_ref, dst, h_ref, mod_ref, pre_ref, k0, segment(tile), row0, n_rows, inline)

    @pl.when((i == 0) & (j == 0))
    def _():
        adaln_rows(0, 0, tm, 0, False)

    def hidden_tile(wa_ref, wb_ref):
        u = u_ref[slot]
        for c in range(tf // MXU_COLS):
            cols = slice(c * MXU_COLS, (c + 1) * MXU_COLS)
            a = jnp.dot(u, wa_ref[:, cols], preferred_element_type=F32)
            b = jnp.dot(u, wb_ref[:, cols], preferred_element_type=F32)
            hm_ref[j, :, cols] = (_silu(a) * b).astype(BF16)
        n_copy = tm // FFN_STAGE_ROWS
        rows = pl.ds(pl.multiple_of(jnp.minimum(j, n_copy - 1) * FFN_STAGE_ROWS, FFN_STAGE_ROWS), FFN_STAGE_ROWS)
        o_ref[rows, :] = h_ref[rows, :]

    for r in range(n_res):
        pl.when(j == r)(functools.partial(hidden_tile, rest[r], rest[n_res + r]))
    pl.when((j >= n_res) & (j < nf))(functools.partial(hidden_tile, w1a_ref, w1b_ref))

    @pl.when(j >= nf)
    def _():
        n = j - nf
        y = jnp.dot(hm_ref[0], w2_ref[0:tf, :], preferred_element_type=F32)
        for k in range(1, nf):
            y += jnp.dot(hm_ref[k], w2_ref[k * tf:(k + 1) * tf, :], preferred_element_type=F32)
        y_ref[n] = y
        adaln_rows(i + 1, n * (tm // nn), tm // nn, 1 - slot, True)

    @pl.when(j == nf + nn - 1)
    def _():
        gain = 0.5 * mod_ref[segment(i), k0 + 2:k0 + 3, :] * post_ref[...]
        ss = sum(jnp.sum(y_ref[n] * y_ref[n], axis=-1, keepdims=True) for n in range(nn))
        r = lax.rsqrt(ss / d + EPS)
        for n in range(nn):
            cols = slice(n * tn, (n + 1) * tn)
            o_ref[:, cols] = o_ref[:, cols] + (y_ref[n] * r) * gain[:, cols]


def _ffn_call(h, mod_l, k0, pre_g, post_g, w1, w2, lk, nrows, tm, n_lat_seg):
    d = h.shape[1]
    f = w2.shape[2]
    tf = tn = FFN_TILE
    nf, nn = f // tf, d // tn
    n_tiles = nrows // tm
    h_tile = lambda i, j: (jnp.minimum(i + jnp.where(j >= nf, 1, 0), n_tiles - 1), 0)
    w2_tile = lambda j: jnp.where(j < nf, nn - 1, j - nf)
    n_res = FFN_RESIDENT_TILES
    streamed = lambda j: jnp.clip(j, n_res, nf - 1)
    resident = [pl.BlockSpec((None, None, d, tf), functools.partial(lambda t, i, j: (*lk, 0, t), half * nf + r),
                             pipeline_mode=pl.Buffered(1))
                for half in range(2) for r in range(n_res)]
    return pl.pallas_call(
        functools.partial(_ffn_kernel, k0=k0, nf=nf, tf=tf, nn=nn, tn=tn, tm=tm, n_lat_seg=n_lat_seg,
                          n_res=n_res),
        grid=(n_tiles, nf + nn),
        in_specs=[
            pl.BlockSpec((tm, d), h_tile),
            pl.BlockSpec(mod_l.shape, lambda i, j: (0, 0, 0)),
            pl.BlockSpec((1, d), lambda i, j: (0, 0)),
            pl.BlockSpec((1, d), lambda i, j: (0, 0)),
            pl.BlockSpec((None, None, d, tf), lambda i, j: (*lk, 0, streamed(j))),
            pl.BlockSpec((None, None, d, tf), lambda i, j: (*lk, 0, streamed(j) + nf)),
            pl.BlockSpec((None, None, f, tn), lambda i, j: (*lk, 0, w2_tile(j))),
        ] + resident,
        out_specs=pl.BlockSpec((tm, d), lambda i, j: (i, 0)),
        out_shape=jax.ShapeDtypeStruct((nrows, d), F32),
        scratch_shapes=[pltpu.VMEM((2, tm, d), BF16), pltpu.VMEM((nf, tm, tf), BF16),
                        pltpu.VMEM((nn, tm, tn), F32)],
        compiler_params=_params(("arbitrary", "arbitrary")),
        name="ffn_sublayer",
    )(h, mod_l, pre_g.reshape(1, d), post_g.reshape(1, d), w1, w1, w2, *([w1] * (2 * n_res)))


def _inproj_kernel(h_ref, mod_ref, pre_ref, w_ref, o_ref, u_ref, *, tm, n_lat_seg, n_steps, n_pre):
    i = pl.program_id(0)
    j = pl.program_id(1)
    slot = i % 2

    def adaln_rows(tile, row0, n_rows, dst, inline):
        seg = jnp.minimum((tile * tm) // SEQ, n_lat_seg)
        _adaln_rows(u_ref, dst, h_ref, mod_ref, pre_ref, 3, seg, row0, n_rows, inline)

    def project():
        o_ref[...] = jnp.dot(u_ref[slot], w_ref[...], preferred_element_type=F32).astype(BF16)

    @pl.when((i == 0) & (j == 0))
    def _():
        adaln_rows(0, 0, tm, 0, False)

    @pl.when(j < n_steps - n_pre)
    def _():
        project()

    @pl.when(j >= n_steps - n_pre)
    def _():
        project()
        adaln_rows(i + 1, (j - (n_steps - n_pre)) * (tm // n_pre), tm // n_pre, 1 - slot, True)


def _inproj_call(h, mod_l, pre_g, w, layer, tm, n_lat_seg):
    nrows, d = h.shape
    n = w.shape[2]
    tn = PROJ_TILE
    n_tiles, n_steps = nrows // tm, n // tn
    n_pre = PROJ_LOOKAHEAD_STEPS
    h_tile = lambda i, j: (jnp.minimum(i + jnp.where(j >= n_steps - n_pre, 1, 0), n_tiles - 1), 0)
    return pl.pallas_call(
        functools.partial(_inproj_kernel, tm=tm, n_lat_seg=n_lat_seg, n_steps=n_steps, n_pre=n_pre),
        grid=(n_tiles, n_steps),
        in_specs=[
            pl.BlockSpec((tm, d), h_tile),
            pl.BlockSpec(mod_l.shape, lambda i, j: (0, 0, 0)),
            pl.BlockSpec((1, d), lambda i, j: (0, 0)),
            pl.BlockSpec((None, d, tn), lambda i, j: (layer, 0, j)),
        ],
        out_specs=pl.BlockSpec((tm, tn), lambda i, j: (i, j)),
        out_shape=jax.ShapeDtypeStruct((nrows, n), BF16),
        scratch_shapes=[pltpu.VMEM((2, tm, d), BF16)],
        compiler_params=_params(("arbitrary", "arbitrary")),
        name="mixer_in_proj",
    )(h, mod_l, pre_g.reshape(1, d), w)


def _rope(t, cos, sa, sb):
    return t * cos + pltpu.roll(t, LANES - 16, 1) * sa + pltpu.roll(t, 16, 1) * sb


def _rope_tables():
    pos = np.arange(SEQ)
    prow = jnp.asarray(pos // GRID_W, F32)
    pcol = jnp.asarray(pos % GRID_W, F32)
    half = 16
    inv = ROPE_BASE ** (-jnp.arange(half, dtype=F32) / half)
    lane = np.arange(LANES)
    freq = jnp.asarray(lane % half)
    use_col = jnp.asarray((lane % 64) >= 32)
    second = jnp.asarray((lane % 32) >= half)
    p = jnp.where(use_col[None, :], pcol[:, None], prow[:, None])
    ang = p * inv[freq][None, :]
    cos, sin = jnp.cos(ang), jnp.sin(ang)
    sa = jnp.where(second[None, :], 0.0, -sin)
    sb = jnp.where(second[None, :], sin, 0.0)
    return cos, sa, sb


def _mixer_specs(nb, width, cols, order):
    ctx_blk0 = nb * SEQ // CTX_LEN

    def spec(rows, col, w, ctx):
        def index(*g):
            b, h = order(*g)
            return ((ctx_blk0 + b) if ctx else b, col // w + h)
        return pl.BlockSpec((rows, w), index)

    return ([spec(SEQ, c, w, False) for c, w in zip(cols, width)]
            + [spec(CTX_LEN, c, w, True) for c, w in zip(cols, width)])


def _mixer_outs(nb, width, order, with_ctx):
    def index(*g):
        b, h = order(*g)
        return (b, h)
    specs = [pl.BlockSpec((SEQ, width), index)]
    shapes = [jax.ShapeDtypeStruct((nb * SEQ, BRANCH_W), BF16)]
    if with_ctx:
        specs.append(pl.BlockSpec((CTX_LEN, width), index))
        shapes.append(jax.ShapeDtypeStruct((nb * CTX_LEN, BRANCH_W), BF16))
    return specs, shapes


def _diff_kernel(ql_ref, kl_ref, vl_ref, qc_ref, kc_ref, vc_ref, cos_ref, sa_ref, sb_ref, lam_ref, g_ref,
                 *rest, lambda_init, with_ctx):
    ol_ref, oc_ref = (rest[0], rest[1]) if with_ctx else (rest[0], None)
    k_all, v_ones = rest[-2:]
    k_all[0:CTX_LEN, :] = kc_ref[...]
    k_all[CTX_LEN:, :] = _rope(kl_ref[...].astype(F32), cos_ref[...], sa_ref[...], sb_ref[...]).astype(BF16)
    v_ones[0:CTX_LEN, 0:LANES] = vc_ref[...]
    v_ones[CTX_LEN:, 0:LANES] = vl_ref[...]
    v_ones[:, LANES:] = jnp.ones((CTX_LEN + SEQ, LANES), BF16)

    lv = lam_ref[...]
    lam = (jnp.exp(jnp.sum(lv[0:1] * lv[1:2], axis=-1, keepdims=True))
           - jnp.exp(jnp.sum(lv[2:3] * lv[3:4], axis=-1, keepdims=True)) + lambda_init)
    lane = lax.broadcasted_iota(jnp.int32, (1, LANES), 1)

    def attend(q, nk):
        q = q * (A_HEAD_DIM ** -0.5)
        k = k_all[0:nk, :]
        v1 = v_ones[0:nk, :]

        rows = QBLK // DIFF_ROW_SPLIT
        qms = [jnp.where(mask, q[r * rows:(r + 1) * rows], 0.0).astype(BF16)
               for r in range(DIFF_ROW_SPLIT) for mask in (lane < A_HEAD_DIM, lane >= A_HEAD_DIM)]
        ss = [lax.dot_general(qm, k, NT_DIMS, preferred_element_type=F32) for qm in qms]
        es = [jnp.exp(s - jnp.max(s, axis=-1, keepdims=True)).astype(BF16) for s in ss]
        ols = [jnp.dot(e, v1, preferred_element_type=F32) for e in es]
        os = [ol[:, :LANES] / ol[:, LANES:] for ol in ols]
        o = jnp.concatenate([os[2 * r] - lam * os[2 * r + 1] for r in range(DIFF_ROW_SPLIT)], axis=0)
        return (_rms(o) * g_ref[...] * (1.0 - lambda_init)).astype(BF16)

    def block(t, carry):
        rows = pl.ds(pl.multiple_of(t * QBLK, QBLK), QBLK)
        q = _rope(ql_ref[rows, :].astype(F32), cos_ref[rows, :], sa_ref[rows, :], sb_ref[rows, :])
        ol_ref[rows, :] = attend(q, CTX_LEN + SEQ)
        return carry

    lax.fori_loop(0, SEQ // QBLK, block, 0, unroll=True)
    if with_ctx:
        oc_ref[...] = attend(qc_ref[...].astype(F32), CTX_LEN)


def _diff_call(proj, tabs, diff_lambda, diff_norm, lambda_init, nb, with_ctx):
    order = lambda b, h: (b, h)
    whole = lambda b, h: (0, 0)
    out_specs, out_shapes = _mixer_outs(nb, LANES, order, with_ctx)
    return pl.pallas_call(
        functools.partial(_diff_kernel, lambda_init=lambda_init, with_ctx=with_ctx),
        grid=(nb, A_HEADS),
        in_specs=_mixer_specs(nb, (LANES,) * 3, (COL_AQ, COL_AK, COL_AV), order) + [
            pl.BlockSpec((SEQ, LANES), whole),
            pl.BlockSpec((SEQ, LANES), whole),
            pl.BlockSpec((SEQ, LANES), whole),
            pl.BlockSpec((4, A_HEAD_DIM), whole),
            pl.BlockSpec((1, LANES), whole),
        ],
        out_specs=out_specs,
        out_shape=out_shapes,
        scratch_shapes=[pltpu.VMEM((CTX_LEN + SEQ, LANES), BF16), pltpu.VMEM((CTX_LEN + SEQ, 2 * LANES), BF16)],
        compiler_params=_params(("parallel", "parallel")),
        name="diff_attention",
    )(*([proj] * 6), *tabs, diff_lambda, diff_norm.reshape(1, LANES))


def _nbr_block_rows(cfg, qi, kj):
    n_rows = SEQ // GRID_W
    r0 = (0, ROWS_PER_QBLK, n_rows - ROWS_PER_QBLK)[cfg]
    ks = min(max(r0 - WIN_R // 2, 0), n_rows - NBR_KROWS)
    r = r0 + qi
    rs = min(max(r - WIN_R // 2, 0), n_rows - WIN_R)
    kr = ks + kj
    return rs <= kr < rs + WIN_R, kr - r + WIN_R - 1


def _nbr_bias_kernel(rpb_ref, o_ref):
    n_dr, n_dc = 2 * WIN_R - 1, 2 * WIN_C - 1
    base = pl.program_id(0) * (n_dr * n_dc)
    qc = lax.broadcasted_iota(jnp.int32, (GRID_W, LANES), 0)
    lane = lax.broadcasted_iota(jnp.int32, (GRID_W, LANES), 1)
    kc = lane % GRID_W
    dc = kc - qc + (WIN_C - 1)
    cs = jnp.clip(qc - WIN_C // 2, 0, GRID_W - WIN_C)
    col_ok = (kc >= cs) & (kc < cs + WIN_C)
    masked = jnp.full((GRID_W, LANES), NBR_MASKED, F32)

    def row_pattern(dr):
        acc = masked
        for x in range(n_dc):
            acc = jnp.where(dc == x, rpb_ref[base + dr * n_dc + x], acc)
        return jnp.where(col_ok, acc, masked)

    pats = [row_pattern(dr) for dr in range(n_dr)]

    def pattern(cfg, qi, kj):
        if kj >= NBR_KROWS:
            return masked
        ok, dr = _nbr_block_rows(cfg, qi, kj)
        return pats[dr] if ok else masked

    for cfg in range(3):
        for qi in range(ROWS_PER_QBLK):
            rows = slice(qi * GRID_W, (qi + 1) * GRID_W)
            for m in range(pl.cdiv(NBR_KEYS, LANES)):
                tile = jnp.where(lane < GRID_W, pattern(cfg, qi, 2 * m), pattern(cfg, qi, 2 * m + 1))
                width = min(LANES, NBR_KEYS - m * LANES)
                o_ref[0, cfg, rows, m * LANES:m * LANES + width] = tile[:, :width]


def _nbr_bias_call(rpb):
    heads = rpb.shape[0]
    return pl.pallas_call(
        _nbr_bias_kernel,
        grid=(heads,),
        in_specs=[pl.BlockSpec(memory_space=pltpu.SMEM)],
        out_specs=pl.BlockSpec((1, 3, QBLK, NBR_KEYS), lambda h: (h, 0, 0, 0)),
        out_shape=jax.ShapeDtypeStruct((heads, 3, QBLK, NBR_KEYS), F32),
        compiler_params=_params(("parallel",)),
        name="nbr_bias_table",
    )(rpb.astype(F32).reshape(-1))


def _nbr_kernel(ql_ref, kl_ref, vl_ref, qc_ref, kc_ref, vc_ref, bias_ref, *rest, with_ctx):
    ol_ref, oc_ref = (rest[0], rest[1]) if with_ctx else (rest[0], None)
    v1l, v1c = rest[-2:]
    scale = B_HEAD_DIM ** -0.5
    lat_blocks = SEQ // QBLK

    for e in range(NBR_HEADS_PER_STEP):
        hs = slice(e * LANES, (e + 1) * LANES)
        v1l[e, :, 0:LANES] = vl_ref[:, hs]
        v1l[e, :, LANES:] = jnp.ones((SEQ, LANES), BF16)
        v1c[e, :, 0:LANES] = vc_ref[:, hs]
        v1c[e, :, LANES:] = jnp.ones((CTX_LEN, LANES), BF16)
        kc = kc_ref[:, hs]

        for blk in range(lat_blocks):
            cfg = 0 if blk == 0 else (2 if blk == lat_blocks - 1 else 1)
            start = min(max((blk * ROWS_PER_QBLK - WIN_R // 2) * GRID_W, 0), SEQ - NBR_KEYS)
            keys = slice(start, start + NBR_KEYS)
            q = ql_ref[blk * QBLK:(blk + 1) * QBLK, hs]
            s_lat = (lax.dot_general(q, kl_ref[keys, hs], NT_DIMS, preferred_element_type=F32) * scale
                     + bias_ref[e, cfg])
            s_ctx = lax.dot_general(q, kc, NT_DIMS, preferred_element_type=F32) * scale
            m = jnp.maximum(jnp.max(s_lat, axis=-1, keepdims=True), jnp.max(s_ctx, axis=-1, keepdims=True))
            ol = (jnp.dot(jnp.exp(s_lat - m).astype(BF16), v1l[e, keys, :], preferred_element_type=F32)
                  + jnp.dot(jnp.exp(s_ctx - m).astype(BF16), v1c[e], preferred_element_type=F32))
            ol_ref[blk * QBLK:(blk + 1) * QBLK, hs] = (ol[:, :LANES] / ol[:, LANES:]).astype(BF16)

        if with_ctx:
            s = lax.dot_general(qc_ref[:, hs], kc, NT_DIMS, preferred_element_type=F32) * scale
            p = jnp.exp(s - jnp.max(s, axis=-1, keepdims=True)).astype(BF16)
            ol = jnp.dot(p, v1c[e], preferred_element_type=F32)
            oc_ref[:, hs] = (ol[:, :LANES] / ol[:, LANES:]).astype(BF16)


def _nbr_call(proj, bias_tab, nb, with_ctx):
    order = lambda g, b: (b, g)
    width = NBR_HEADS_PER_STEP * LANES
    out_specs, out_shapes = _mixer_outs(nb, width, order, with_ctx)
    return pl.pallas_call(
        functools.partial(_nbr_kernel, with_ctx=with_ctx),
        grid=(B_HEADS // NBR_HEADS_PER_STEP, nb),
        in_specs=_mixer_specs(nb, (width,) * 3, (COL_BQ, COL_BK, COL_BV), order) + [
            pl.BlockSpec((NBR_HEADS_PER_STEP, 3, QBLK, NBR_KEYS), lambda g, b: (g, 0, 0, 0)),
        ],
        out_specs=out_specs,
        out_shape=out_shapes,
        scratch_shapes=[pltpu.VMEM((NBR_HEADS_PER_STEP, SEQ, 2 * LANES), BF16),
                        pltpu.VMEM((NBR_HEADS_PER_STEP, CTX_LEN, 2 * LANES), BF16)],
        compiler_params=_params(("parallel", "parallel")),
        name="nbr_attention",
    )(*([proj] * 6), bias_tab)


def _ret_kernel(ql_ref, kl_ref, vl_ref, gl_ref, qc_ref, kc_ref, vc_ref, gc_ref, cos_ref, sa_ref, sb_ref,
                dec_ref, rn_ref, *rest, with_ctx):
    ol_ref, oc_ref = (rest[0], rest[1]) if with_ctx else (rest[0], None)
    of_ref, ob_ref, st_ref = rest[-3:]
    ch = QBLK
    n_chunks = SEQ // ch
    kscale = C_KEY_DIM ** -0.5

    def run():
        ii = lax.broadcasted_iota(jnp.int32, (ch, ch), 0)
        jj = lax.broadcasted_iota(jnp.int32, (ch, ch), 1)
        dist = (ii - jj).astype(F32)
        pos = lax.broadcasted_iota(jnp.int32, (ch, 1), 0).astype(F32)
        lane = lax.broadcasted_iota(jnp.int32, (1, LANES), 1)
        rn = rn_ref[...]

        def log_sigmoid(x):
            return -(jnp.log1p(jnp.exp(-jnp.abs(x))) + jnp.maximum(-x, 0.0))

        heads = []
        for e in range(2):
            lgf = log_sigmoid(dec_ref[0, 0, e:e + 1, 0:1])
            lgb = log_sigmoid(dec_ref[0, 1, e:e + 1, 0:1])
            heads.append(dict(
                mask=(lane >= e * C_KEY_DIM) & (lane < (e + 1) * C_KEY_DIM),
                intra_f=jnp.where(dist >= 0, jnp.exp(lgf * jnp.maximum(dist, 0.0)), 0.0),
                intra_b=jnp.where(dist <= 0, jnp.exp(lgb * jnp.maximum(-dist, 0.0)), 0.0),
                qdec_f=jnp.exp(lgf * (pos + 1.0)), kdec_f=jnp.exp(lgf * (ch - 1.0 - pos)),
                cdec_f=jnp.exp(lgf * ch),
                qdec_b=jnp.exp(lgb * (ch - pos)), kdec_b=jnp.exp(lgb * pos),
                cdec_b=jnp.exp(lgb * ch),
                vs=slice(e * LANES, (e + 1) * LANES),
            ))

        def finish(o, g):
            return (_silu(g.astype(F32)) * (_rms(o) * rn)).astype(BF16)

        qc = qc_ref[...].astype(F32)
        kc = kc_ref[...].astype(F32) * kscale
        for e, hd in enumerate(heads):
            v = vc_ref[:, hd["vs"]]
            st_ref[e] = lax.dot_general((kc * hd["kdec_f"]).astype(BF16), v, TN_DIMS, preferred_element_type=F32)
            st_ref[2 + e] = lax.dot_general((kc * hd["kdec_b"]).astype(BF16), v, TN_DIMS,
                                            preferred_element_type=F32)
            if with_ctx:
                qe = jnp.where(hd["mask"], qc, 0.0).astype(BF16)
                att = lax.dot_general(qe, kc.astype(BF16), NT_DIMS, preferred_element_type=F32)
                att = att * (hd["intra_f"] + hd["intra_b"])
                o = jnp.dot(att.astype(BF16), v, preferred_element_type=F32)
                oc_ref[:, hd["vs"]] = finish(o, gc_ref[:, hd["vs"]])

        def chunk_rows(c):
            return pl.ds(pl.multiple_of(c * ch, ch), ch)

        def sweep(c, fwd):
            rows = chunk_rows(c)
            tabs = (cos_ref[rows, :], sa_ref[rows, :], sb_ref[rows, :])
            q = _rope(ql_ref[rows, :].astype(F32), *tabs)
            k = _rope(kl_ref[rows, :].astype(F32), *tabs) * kscale
            kb = k.astype(BF16)
            for e, hd in enumerate(heads):
                d = "f" if fwd else "b"
                si = e if fwd else 2 + e
                v = vl_ref[rows, hd["vs"]]
                qe = jnp.where(hd["mask"], q, 0.0)
                att = lax.dot_general(qe.astype(BF16), kb, NT_DIMS, preferred_element_type=F32) * hd["intra_" + d]
                st = st_ref[si]
                o = (jnp.dot(att.astype(BF16), v, preferred_element_type=F32)
                     + jnp.dot((qe * hd["qdec_" + d]).astype(BF16), st.astype(BF16), preferred_element_type=F32))
                st_ref[si] = st * hd["cdec_" + d] + lax.dot_general(
                    (k * hd["kdec_" + d]).astype(BF16), v, TN_DIMS, preferred_element_type=F32)
                (of_ref if fwd else ob_ref)[rows, hd["vs"]] = o

        def finish_chunk(c):
            rows = chunk_rows(c)
            for hd in heads:
                vs = hd["vs"]
                ol_ref[rows, vs] = finish(of_ref[rows, vs] + ob_ref[rows, vs], gl_ref[rows, vs])

        def first_half(t, carry):
            sweep(t, True)
            sweep(n_chunks - 1 - t, False)
            return carry

        def second_half(t, carry):
            sweep(t, True)
            sweep(n_chunks - 1 - t, False)
            finish_chunk(t)
            finish_chunk(n_chunks - 1 - t)
            return carry

        lax.fori_loop(0, n_chunks // 2, first_half, 0, unroll=True)
        lax.fori_loop(n_chunks // 2, n_chunks, second_half, 0, unroll=True)

    run()


def _ret_call(proj, tabs, ret_decay, ret_norm, nb, with_ctx):
    pairs = C_HEADS // 2
    w2 = 2 * LANES
    order = lambda b, p: (b, p)
    whole = lambda b, p: (0, 0)
    out_specs, out_shapes = _mixer_outs(nb, w2, order, with_ctx)
    dec = jnp.broadcast_to(ret_decay.astype(F32).reshape(2, pairs, 2).transpose(1, 0, 2)[..., None],
                           (pairs, 2, 2, LANES))
    return pl.pallas_call(
        functools.partial(_ret_kernel, with_ctx=with_ctx),
        grid=(nb, pairs),
        in_specs=_mixer_specs(nb, (LANES, LANES, w2, w2), (COL_CQ, COL_CK, COL_CV, COL_CG), order) + [
            pl.BlockSpec((SEQ, LANES), whole),
            pl.BlockSpec((SEQ, LANES), whole),
            pl.BlockSpec((SEQ, LANES), whole),
            pl.BlockSpec((1, 2, 2, LANES), lambda b, p: (p, 0, 0, 0)),
            pl.BlockSpec((1, LANES), whole),
        ],
        out_specs=out_specs,
        out_shape=out_shapes,
        scratch_shapes=[
            pltpu.VMEM((SEQ, w2), F32),
            pltpu.VMEM((SEQ, w2), F32),
            pltpu.VMEM((4, LANES, LANES), F32),
        ],
        compiler_params=_params(("parallel", "parallel")),
        name="retention",
    )(*([proj] * 8), *tabs, dec, ret_norm.reshape(1, LANES))


def _merge_kernel(h_ref, mod_ref, post_ref, g_lo_ref, g_hi_ref, wb_ref, wo_ref, *rest, n_lat_tiles, with_ctx):
    d = h_ref.shape[1]
    o_ref = rest[-1]
    if with_ctx:
        is_lat = pl.program_id(0) < n_lat_tiles
        xs = [jnp.where(is_lat, rest[2 * k][...], rest[2 * k + 1][...]) for k in range(3)]
    else:
        xs = [rest[k][...] for k in range(3)]
    g_lo = g_lo_ref[...]
    g_hi = g_hi_ref[...]
    gates = (g_lo[:, :d], jnp.concatenate([g_lo[:, d:], g_hi[:, :d // 2]], axis=1), g_hi[:, d // 2:])
    m = None
    for k in range(3):
        t = jax.nn.sigmoid(gates[k].astype(F32)) * jnp.dot(xs[k], wb_ref[k], preferred_element_type=F32)
        m = t if m is None else m + t
    y = jnp.dot(m.astype(BF16), wo_ref[...], preferred_element_type=F32)
    o_ref[...] = h_ref[...] + mod_ref[0, 5:6, :] * (_rms(y) * post_ref[...])


def _merge_call(h, mod_l, post_g, branches, proj, wb, wo, layer, nrows, n_lat_seg):
    d = h.shape[1]
    tm = CTX_LEN
    with_ctx = len(branches[0]) == 2
    n_lat_tiles = n_lat_seg * SEQ // tm
    gw = 3 * d // 2
    g0 = COL_GATES // gw
    resident = dict(pipeline_mode=pl.Buffered(1))
    row = lambda i: (i, 0)
    lat_row = lambda i: (jnp.minimum(i, n_lat_tiles - 1), 0)
    ctx_row = lambda i: (jnp.maximum(i - n_lat_tiles, 0), 0)
    branch_specs, branch_args = [], []
    for outs in branches:
        branch_specs.append(pl.BlockSpec((tm, BRANCH_W), lat_row))
        if with_ctx:
            branch_specs.append(pl.BlockSpec((tm, BRANCH_W), ctx_row))
        branch_args.extend(outs)
    return pl.pallas_call(
        functools.partial(_merge_kernel, n_lat_tiles=n_lat_tiles, with_ctx=with_ctx),
        grid=(nrows // tm,),
        in_specs=[
            pl.BlockSpec((tm, d), row),
            pl.BlockSpec((1, N_MOD, d), lambda i: (jnp.minimum((i * tm) // SEQ, n_lat_seg), 0, 0)),
            pl.BlockSpec((1, d), lambda i: (0, 0)),
            pl.BlockSpec((tm, gw), lambda i: (i, g0)),
            pl.BlockSpec((tm, gw), lambda i: (i, g0 + 1)),
            pl.BlockSpec((None, 3, BRANCH_W, d), lambda i: (layer, 0, 0, 0), **resident),
            pl.BlockSpec((None, d, d), lambda i: (layer, 0, 0), **resident),
        ] + branch_specs,
        out_specs=pl.BlockSpec((tm, d), row),
        out_shape=jax.ShapeDtypeStruct((nrows, d), F32),
        compiler_params=_params(("parallel",)),
        name="mixer_merge",
    )(h, mod_l, post_g.reshape(1, d), proj, proj, wb, wo, *branch_args)


def kernel(x, c, ctx, c_ctx, w_mod, b_mod, pre_norm, post_norm, ffn_w_in, ffn_w_out, w_in, diff_lambda,
           diff_norm, na_rpb, ret_decay, ret_norm, w_branch, w_out):
    nb, seq, d = x.shape
    assert (seq, d) == (SEQ, D_MODEL) and ctx.shape == (nb, CTX_LEN, d)
    n_lat, n_ctx = nb * SEQ, nb * CTX_LEN
    n_all = n_lat + n_ctx
    tm = min(FFN_ROWS, n_ctx)
    h = jnp.concatenate([x.reshape(n_lat, d), ctx.reshape(n_ctx, d)], axis=0)
    pad = (-(nb + 1)) % 8
    cc = jnp.concatenate([c, c_ctx[None, :], jnp.zeros((pad, d), c.dtype)], axis=0)
    mods = _mod_call(cc, w_mod, b_mod)
    tabs = _rope_tables()
    w1, w2 = ffn_w_in.astype(BF16), ffn_w_out.astype(BF16)
    w_proj, wb, wo = w_in.astype(BF16), w_branch.astype(BF16), w_out.astype(BF16)

    for l in range(DEPTH):
        with_ctx = l < DEPTH - 1
        lambda_init = 0.8 - 0.6 * math.exp(-0.3 * l)
        nrows_out = n_all if with_ctx else n_lat
        mod_l = mods[l, :nb + 1].reshape(nb + 1, N_MOD, d)
        ffn = functools.partial(_ffn_call, tm=tm, n_lat_seg=nb)

        h = ffn(h, mod_l, 0, pre_norm[l, 0], post_norm[l, 0], w1, w2, (l, 0), n_all)
        proj = _inproj_call(h, mod_l, pre_norm[l, 1], w_proj, l, min(PROJ_ROWS, n_ctx), nb)
        oa = _diff_call(proj, tabs, diff_lambda[l], diff_norm[l], lambda_init, nb, with_ctx)
        ob = _nbr_call(proj, _nbr_bias_call(na_rpb[l]), nb, with_ctx)
        yr = _ret_call(proj, tabs, ret_decay[l], ret_norm[l], nb, with_ctx)
        h = _merge_call(h, mod_l, post_norm[l, 1], (oa, ob, yr), proj, wb, wo, l, nrows_out, nb)
        h = ffn(h, mod_l, 6, pre_norm[l, 2], post_norm[l, 2], w1, w2, (l, 1), nrows_out)
    return h.reshape(nb, SEQ, d)
```

```python
import functools
import math

import numpy as np
import jax
import jax.numpy as jnp
from jax import lax
from jax.experimental import pallas as pl
from jax.experimental.pallas import tpu as pltpu

D_MODEL = 2048
SEQ = 2048
DEPTH = 2
GRID_W = 64
CTX_LEN = 256
N_MOD = 9
FFN_DIM = 5632
A_HEADS = 8
A_HEAD_DIM = 64
B_HEADS = 8
B_HEAD_DIM = 128
WIN_R = 8
WIN_C = 16
C_HEADS = 8
C_KEY_DIM = 64
BRANCH_W = 1024
ROPE_BASE = 10000.0
EPS = 1e-6

COL_AQ, COL_AK, COL_AV = 0, 1024, 2048
COL_BQ, COL_BK, COL_BV = 3072, 4096, 5120
COL_CQ, COL_CK, COL_CV, COL_CG = 6144, 6656, 7168, 8192
COL_GATES = 9216
IN_COLS = 15360

LANES = 128
BF16_SUBLANES = 16
QBLK = 256
ROWS_PER_QBLK = QBLK // GRID_W
NBR_KROWS = ROWS_PER_QBLK + WIN_R - 1
NBR_KEYS = NBR_KROWS * GRID_W
VMEM_LIMIT = 60 * 1024 * 1024
FFN_ROWS = 512
PROJ_ROWS = 1024
FFN_TILE = 512
FFN_RESIDENT_TILES = 2
FFN_STAGE_ROWS = 64
MXU_COLS = 256
PROJ_TILE = 2560
PROJ_LOOKAHEAD_STEPS = 4
NBR_HEADS_PER_STEP = 2
DIFF_ROW_SPLIT = 2
NBR_MASKED = -1e30

BF16 = jnp.bfloat16
F32 = jnp.float32
NT_DIMS = (((1,), (1,)), ((), ()))
TN_DIMS = (((0,), (0,)), ((), ()))


def _params(sem):
    return pltpu.CompilerParams(dimension_semantics=sem, vmem_limit_bytes=VMEM_LIMIT)


def _rms(x):
    return x * lax.rsqrt(jnp.mean(x * x, axis=-1, keepdims=True) + EPS)


def _silu(x):
    return x * jax.nn.sigmoid(x)


def _row_chunks(row0, n_rows, rows_per, body, inline=False):
    def step(r, carry):
        body(pl.ds(pl.multiple_of(row0 + r * rows_per, rows_per), rows_per))
        return carry
    if inline:
        for r in range(n_rows // rows_per):
            step(r, 0)
    else:
        lax.fori_loop(0, n_rows // rows_per, step, 0, unroll=4)


def _adaln_rows(u_ref, dst, read_rows, mod_ref, pre_ref, k0, seg, row0, n_rows, inline):
    gain = pre_ref[...] * (1.0 + mod_ref[seg, k0 + 1:k0 + 2, :])
    shift = mod_ref[seg, k0:k0 + 1, :]

    def body(rows):
        u_ref[dst, rows, :] = (_rms(read_rows(rows)) * gain + shift).astype(u_ref.dtype)

    _row_chunks(row0, n_rows, BF16_SUBLANES, body, inline)


def _mod_kernel(cc_ref, w_ref, b_ref, o_ref):
    s = _silu(cc_ref[...]).astype(BF16)
    o_ref[0] = jnp.dot(s, w_ref[0].astype(BF16), preferred_element_type=F32) + b_ref[0]


def _mod_call(cc, w_mod, b_mod):
    depth, d, n = w_mod.shape
    rows = cc.shape[0]
    tn = 1024
    return pl.pallas_call(
        _mod_kernel,
        grid=(depth, n // tn),
        in_specs=[
            pl.BlockSpec((rows, d), lambda l, j: (0, 0)),
            pl.BlockSpec((1, d, tn), lambda l, j: (l, 0, j)),
            pl.BlockSpec((1, 1, tn), lambda l, j: (l, 0, j)),
        ],
        out_specs=pl.BlockSpec((1, rows, tn), lambda l, j: (l, 0, j)),
        out_shape=jax.ShapeDtypeStruct((depth, rows, n), F32),
        compiler_params=_params(("parallel", "parallel")),
        name="mod_vectors",
    )(cc, w_mod, b_mod.reshape(depth, 1, n))


def _ffn_kernel(*refs, k0, nf, tf, nn, tn, tm, n_lat_seg, n_res, n_tiles, n_lat_tiles):
    dual = n_lat_tiles is not None
    h_ref, c_ref = (refs[0], refs[1]) if dual else (refs[0], None)
    mod_ref, pre_ref, post_ref, w1a_ref, w1b_ref, w2_ref = refs[1 + dual:7 + dual]
    resident = refs[7 + dual:7 + dual + 2 * n_res]
    o_ref, u_ref, hm_ref, y_ref = refs[-4:]
    i = pl.program_id(0)
    j = pl.program_id(1)
    slot = i % 2
    d = h_ref.shape[1]
    segment = lambda tile: jnp.minimum((tile * tm) // SEQ, n_lat_seg)

    def row_reader(tile):
        if not dual:
            return lambda rows: h_ref[rows, :]
        return lambda rows: jnp.where(tile < n_lat_tiles, h_ref[rows, :], c_ref[rows, :])

    def adaln_rows(tile, row0, n_rows, dst, inline):
        _adaln_rows(u_ref, dst, row_reader(tile), mod_ref, pre_ref, k0, segment(tile), row0, n_rows, inline)

    @pl.when((i == 0) & (j == 0))
    def _():
        adaln_rows(0, 0, tm, 0, False)

    def hidden_tile(wa_ref, wb_ref):
        u = u_ref[slot]
        for c in range(tf // MXU_COLS):
            cols = slice(c * MXU_COLS, (c + 1) * MXU_COLS)
            a = jnp.dot(u, wa_ref[:, cols], preferred_element_type=F32)
            b = jnp.dot(u, wb_ref[:, cols], preferred_element_type=F32)
            hm_ref[j, :, cols] = (_silu(a) * b).astype(BF16)
        n_copy = tm // FFN_STAGE_ROWS
        rows = pl.ds(pl.multiple_of(jnp.minimum(j, n_copy - 1) * FFN_STAGE_ROWS, FFN_STAGE_ROWS), FFN_STAGE_ROWS)
        o_ref[rows, :] = row_reader(i)(rows)

    for r in range(n_res):
        pl.when(j == r)(functools.partial(hidden_tile, resident[r], resident[n_res + r]))
    pl.when((j >= n_res) & (j < nf))(functools.partial(hidden_tile, w1a_ref, w1b_ref))

    @pl.when(j >= nf)
    def _():
        n = j - nf
        y = jnp.dot(hm_ref[0], w2_ref[0:tf, :], preferred_element_type=F32)
        for k in range(1, nf):
            y += jnp.dot(hm_ref[k], w2_ref[k * tf:(k + 1) * tf, :], preferred_element_type=F32)
        y_ref[n] = y
        adaln_rows(jnp.minimum(i + 1, n_tiles - 1), n * (tm // nn), tm // nn, 1 - slot, True)

    @pl.when(j == nf + nn - 1)
    def _():
        gain = 0.5 * mod_ref[segment(i), k0 + 2:k0 + 3, :] * post_ref[...]
        ss = sum(jnp.sum(y_ref[n] * y_ref[n], axis=-1, keepdims=True) for n in range(nn))
        r = lax.rsqrt(ss / d + EPS)
        for n in range(nn):
            cols = slice(n * tn, (n + 1) * tn)
            o_ref[:, cols] = o_ref[:, cols] + (y_ref[n] * r) * gain[:, cols]


def _ffn_call(h, mod_l, k0, pre_g, post_g, w1, w2, lk, nrows, tm, n_lat_seg, ctx_rows=None):
    d = h.shape[1]
    f = w2.shape[2]
    tf = tn = FFN_TILE
    nf, nn = f // tf, d // tn
    n_tiles = nrows // tm
    dual = ctx_rows is not None
    n_lat_tiles = h.shape[0] // tm if dual else None
    next_tile = lambda i, j: jnp.minimum(i + jnp.where(j >= nf, 1, 0), n_tiles - 1)
    if dual:
        row_specs = [pl.BlockSpec((tm, d), lambda i, j: (jnp.minimum(next_tile(i, j), n_lat_tiles - 1), 0)),
                     pl.BlockSpec((tm, d), lambda i, j: (jnp.maximum(next_tile(i, j) - n_lat_tiles, 0), 0))]
    else:
        row_specs = [pl.BlockSpec((tm, d), lambda i, j: (next_tile(i, j), 0))]
    rows_in = [h, ctx_rows] if dual else [h]
    w2_tile = lambda j: jnp.where(j < nf, nn - 1, j - nf)
    n_res = 0 if dual else FFN_RESIDENT_TILES
    streamed = lambda j: jnp.clip(j, n_res, nf - 1)
    resident = [pl.BlockSpec((None, None, d, tf), functools.partial(lambda t, i, j: (*lk, 0, t), half * nf + r),
                             pipeline_mode=pl.Buffered(1))
                for half in range(2) for r in range(n_res)]
    return pl.pallas_call(
        functools.partial(_ffn_kernel, k0=k0, nf=nf, tf=tf, nn=nn, tn=tn, tm=tm, n_lat_seg=n_lat_seg,
                          n_res=n_res, n_tiles=n_tiles, n_lat_tiles=n_lat_tiles),
        grid=(n_tiles, nf + nn),
        in_specs=row_specs + [
            pl.BlockSpec(mod_l.shape, lambda i, j: (0, 0, 0)),
            pl.BlockSpec((1, d), lambda i, j: (0, 0)),
            pl.BlockSpec((1, d), lambda i, j: (0, 0)),
            pl.BlockSpec((None, None, d, tf), lambda i, j: (*lk, 0, streamed(j))),
            pl.BlockSpec((None, None, d, tf), lambda i, j: (*lk, 0, streamed(j) + nf)),
            pl.BlockSpec((None, None, f, tn), lambda i, j: (*lk, 0, w2_tile(j))),
        ] + resident,
        out_specs=pl.BlockSpec((tm, d), lambda i, j: (i, 0)),
        out_shape=jax.ShapeDtypeStruct((nrows, d), F32),
        scratch_shapes=[pltpu.VMEM((2, tm, d), BF16), pltpu.VMEM((nf, tm, tf), BF16),
                        pltpu.VMEM((nn, tm, tn), F32)],
        compiler_params=_params(("arbitrary", "arbitrary")),
        name="ffn_sublayer",
    )(*rows_in, mod_l, pre_g.reshape(1, d), post_g.reshape(1, d), w1, w1, w2, *([w1] * (2 * n_res)))


def _inproj_kernel(h_ref, mod_ref, pre_ref, w_ref, o_ref, u_ref, *, tm, n_lat_seg, n_steps, n_pre):
    i = pl.program_id(0)
    j = pl.program_id(1)
    slot = i % 2

    def adaln_rows(tile, row0, n_rows, dst, inline):
        seg = jnp.minimum((tile * tm) // SEQ, n_lat_seg)
        _adaln_rows(u_ref, dst, lambda rows: h_ref[rows, :], mod_ref, pre_ref, 3, seg, row0, n_rows, inline)

    def project():
        o_ref[...] = jnp.dot(u_ref[slot], w_ref[...], preferred_element_type=F32).astype(BF16)

    @pl.when((i == 0) & (j == 0))
    def _():
        adaln_rows(0, 0, tm, 0, False)

    @pl.when(j < n_steps - n_pre)
    def _():
        project()

    @pl.when(j >= n_steps - n_pre)
    def _():
        project()
        adaln_rows(i + 1, (j - (n_steps - n_pre)) * (tm // n_pre), tm // n_pre, 1 - slot, True)


def _inproj_call(h, mod_l, pre_g, w, layer, tm, n_lat_seg):
    nrows, d = h.shape
    n = w.shape[2]
    tn = PROJ_TILE
    n_tiles, n_steps = nrows // tm, n // tn
    n_pre = PROJ_LOOKAHEAD_STEPS
    h_tile = lambda i, j: (jnp.minimum(i + jnp.where(j >= n_steps - n_pre, 1, 0), n_tiles - 1), 0)
    return pl.pallas_call(
        functools.partial(_inproj_kernel, tm=tm, n_lat_seg=n_lat_seg, n_steps=n_steps, n_pre=n_pre),
        grid=(n_tiles, n_steps),
        in_specs=[
            pl.BlockSpec((tm, d), h_tile),
            pl.BlockSpec(mod_l.shape, lambda i, j: (0, 0, 0)),
            pl.BlockSpec((1, d), lambda i, j: (0, 0)),
            pl.BlockSpec((None, d, tn), lambda i, j: (layer, 0, j)),
        ],
        out_specs=pl.BlockSpec((tm, tn), lambda i, j: (i, j)),
        out_shape=jax.ShapeDtypeStruct((nrows, n), BF16),
        scratch_shapes=[pltpu.VMEM((2, tm, d), BF16)],
        compiler_params=_params(("arbitrary", "arbitrary")),
        name="mixer_in_proj",
    )(h, mod_l, pre_g.reshape(1, d), w)


def _rope(t, cos, sa, sb):
    return t * cos + pltpu.roll(t, LANES - 16, 1) * sa + pltpu.roll(t, 16, 1) * sb


def _rope_tables():
    pos = np.arange(SEQ)
    prow = jnp.asarray(pos // GRID_W, F32)
    pcol = jnp.asarray(pos % GRID_W, F32)
    half = 16
    inv = ROPE_BASE ** (-jnp.arange(half, dtype=F32) / half)
    lane = np.arange(LANES)
    freq = jnp.asarray(lane % half)
    use_col = jnp.asarray((lane % 64) >= 32)
    second = jnp.asarray((lane % 32) >= half)
    p = jnp.where(use_col[None, :], pcol[:, None], prow[:, None])
    ang = p * inv[freq][None, :]
    cos, sin = jnp.cos(ang), jnp.sin(ang)
    sa = jnp.where(second[None, :], 0.0, -sin)
    sb = jnp.where(second[None, :], sin, 0.0)
    return cos, sa, sb


def _mixer_specs(nb, width, cols, order):
    ctx_blk0 = nb * SEQ // CTX_LEN

    def spec(rows, col, w, ctx):
        def index(*g):
            b, h = order(*g)
            return ((ctx_blk0 + b) if ctx else b, col // w + h)
        return pl.BlockSpec((rows, w), index)

    return ([spec(SEQ, c, w, False) for c, w in zip(cols, width)]
            + [spec(CTX_LEN, c, w, True) for c, w in zip(cols, width)])


def _mixer_outs(nb, width, order, with_ctx):
    def index(*g):
        b, h = order(*g)
        return (b, h)
    specs = [pl.BlockSpec((SEQ, width), index)]
    shapes = [jax.ShapeDtypeStruct((nb * SEQ, BRANCH_W), BF16)]
    if with_ctx:
        specs.append(pl.BlockSpec((CTX_LEN, width), index))
        shapes.append(jax.ShapeDtypeStruct((nb * CTX_LEN, BRANCH_W), BF16))
    return specs, shapes


def _diff_kernel(ql_ref, kl_ref, vl_ref, qc_ref, kc_ref, vc_ref, cos_ref, sa_ref, sb_ref, lam_ref, g_ref,
                 *rest, lambda_init, with_ctx):
    ol_ref, oc_ref = (rest[0], rest[1]) if with_ctx else (rest[0], None)
    k_all, v_ones = rest[-2:]
    k_all[0:CTX_LEN, :] = kc_ref[...]
    k_all[CTX_LEN:, :] = _rope(kl_ref[...].astype(F32), cos_ref[...], sa_ref[...], sb_ref[...]).astype(BF16)
    v_ones[0:CTX_LEN, 0:LANES] = vc_ref[...]
    v_ones[CTX_LEN:, 0:LANES] = vl_ref[...]
    v_ones[:, LANES:] = jnp.ones((CTX_LEN + SEQ, LANES), BF16)

    lv = lam_ref[...]
    lam = (jnp.exp(jnp.sum(lv[0:1] * lv[1:2], axis=-1, keepdims=True))
           - jnp.exp(jnp.sum(lv[2:3] * lv[3:4], axis=-1, keepdims=True)) + lambda_init)
    lane = lax.broadcasted_iota(jnp.int32, (1, LANES), 1)

    def attend(q, nk):
        q = q * (A_HEAD_DIM ** -0.5)
        k = k_all[0:nk, :]
        v1 = v_ones[0:nk, :]

        rows = QBLK // DIFF_ROW_SPLIT
        qms = [jnp.where(mask, q[r * rows:(r + 1) * rows], 0.0).astype(BF16)
               for r in range(DIFF_ROW_SPLIT) for mask in (lane < A_HEAD_DIM, lane >= A_HEAD_DIM)]
        ss = [lax.dot_general(qm, k, NT_DIMS, preferred_element_type=F32) for qm in qms]
        es = [jnp.exp(s - jnp.max(s, axis=-1, keepdims=True)).astype(BF16) for s in ss]
        ols = [jnp.dot(e, v1, preferred_element_type=F32) for e in es]
        os = [ol[:, :LANES] / ol[:, LANES:] for ol in ols]
        o = jnp.concatenate([os[2 * r] - lam * os[2 * r + 1] for r in range(DIFF_ROW_SPLIT)], axis=0)
        return (_rms(o) * g_ref[...] * (1.0 - lambda_init)).astype(BF16)

    def block(t, carry):
        rows = pl.ds(pl.multiple_of(t * QBLK, QBLK), QBLK)
        q = _rope(ql_ref[rows, :].astype(F32), cos_ref[rows, :], sa_ref[rows, :], sb_ref[rows, :])
        ol_ref[rows, :] = attend(q, CTX_LEN + SEQ)
        return carry

    lax.fori_loop(0, SEQ // QBLK, block, 0, unroll=True)
    if with_ctx:
        oc_ref[...] = attend(qc_ref[...].astype(F32), CTX_LEN)


def _diff_call(proj, tabs, diff_lambda, diff_norm, lambda_init, nb, with_ctx):
    order = lambda b, h: (b, h)
    whole = lambda b, h: (0, 0)
    out_specs, out_shapes = _mixer_outs(nb, LANES, order, with_ctx)
    return pl.pallas_call(
        functools.partial(_diff_kernel, lambda_init=lambda_init, with_ctx=with_ctx),
        grid=(nb, A_HEADS),
        in_specs=_mixer_specs(nb, (LANES,) * 3, (COL_AQ, COL_AK, COL_AV), order) + [
            pl.BlockSpec((SEQ, LANES), whole),
            pl.BlockSpec((SEQ, LANES), whole),
            pl.BlockSpec((SEQ, LANES), whole),
            pl.BlockSpec((4, A_HEAD_DIM), whole),
            pl.BlockSpec((1, LANES), whole),
        ],
        out_specs=out_specs,
        out_shape=out_shapes,
        scratch_shapes=[pltpu.VMEM((CTX_LEN + SEQ, LANES), BF16), pltpu.VMEM((CTX_LEN + SEQ, 2 * LANES), BF16)],
        compiler_params=_params(("parallel", "parallel")),
        name="diff_attention",
    )(*([proj] * 6), *tabs, diff_lambda, diff_norm.reshape(1, LANES))


def _nbr_block_rows(cfg, qi, kj):
    n_rows = SEQ // GRID_W
    r0 = (0, ROWS_PER_QBLK, n_rows - ROWS_PER_QBLK)[cfg]
    ks = min(max(r0 - WIN_R // 2, 0), n_rows - NBR_KROWS)
    r = r0 + qi
    rs = min(max(r - WIN_R // 2, 0), n_rows - WIN_R)
    kr = ks + kj
    return rs <= kr < rs + WIN_R, kr - r + WIN_R - 1


def _nbr_bias_kernel(rpb_ref, o_ref):
    n_dr, n_dc = 2 * WIN_R - 1, 2 * WIN_C - 1
    base = pl.program_id(0) * (n_dr * n_dc)
    qc = lax.broadcasted_iota(jnp.int32, (GRID_W, LANES), 0)
    lane = lax.broadcasted_iota(jnp.int32, (GRID_W, LANES), 1)
    kc = lane % GRID_W
    dc = kc - qc + (WIN_C - 1)
    cs = jnp.clip(qc - WIN_C // 2, 0, GRID_W - WIN_C)
    col_ok = (kc >= cs) & (kc < cs + WIN_C)
    masked = jnp.full((GRID_W, LANES), NBR_MASKED, F32)

    def row_pattern(dr):
        acc = masked
        for x in range(n_dc):
            acc = jnp.where(dc == x, rpb_ref[base + dr * n_dc + x], acc)
        return jnp.where(col_ok, acc, masked)

    pats = [row_pattern(dr) for dr in range(n_dr)]

    def pattern(cfg, qi, kj):
        if kj >= NBR_KROWS:
            return masked
        ok, dr = _nbr_block_rows(cfg, qi, kj)
        return pats[dr] if ok else masked

    for cfg in range(3):
        for qi in range(ROWS_PER_QBLK):
            rows = slice(qi * GRID_W, (qi + 1) * GRID_W)
            for m in range(pl.cdiv(NBR_KEYS, LANES)):
                tile = jnp.where(lane < GRID_W, pattern(cfg, qi, 2 * m), pattern(cfg, qi, 2 * m + 1))
                width = min(LANES, NBR_KEYS - m * LANES)
                o_ref[0, cfg, rows, m * LANES:m * LANES + width] = tile[:, :width]


def _nbr_bias_call(rpb):
    heads = rpb.shape[0]
    return pl.pallas_call(
        _nbr_bias_kernel,
        grid=(heads,),
        in_specs=[pl.BlockSpec(memory_space=pltpu.SMEM)],
        out_specs=pl.BlockSpec((1, 3, QBLK, NBR_KEYS), lambda h: (h, 0, 0, 0)),
        out_shape=jax.ShapeDtypeStruct((heads, 3, QBLK, NBR_KEYS), F32),
        compiler_params=_params(("parallel",)),
        name="nbr_bias_table",
    )(rpb.astype(F32).reshape(-1))


def _nbr_kernel(ql_ref, kl_ref, vl_ref, qc_ref, kc_ref, vc_ref, bias_ref, *rest, with_ctx):
    ol_ref, oc_ref = (rest[0], rest[1]) if with_ctx else (rest[0], None)
    v1l, v1c = rest[-2:]
    scale = B_HEAD_DIM ** -0.5
    lat_blocks = SEQ // QBLK

    for e in range(NBR_HEADS_PER_STEP):
        hs = slice(e * LANES, (e + 1) * LANES)
        v1l[e, :, 0:LANES] = vl_ref[:, hs]
        v1l[e, :, LANES:] = jnp.ones((SEQ, LANES), BF16)
        v1c[e, :, 0:LANES] = vc_ref[:, hs]
        v1c[e, :, LANES:] = jnp.ones((CTX_LEN, LANES), BF16)
        kc = kc_ref[:, hs]

        for blk in range(lat_blocks):
            cfg = 0 if blk == 0 else (2 if blk == lat_blocks - 1 else 1)
            start = min(max((blk * ROWS_PER_QBLK - WIN_R // 2) * GRID_W, 0), SEQ - NBR_KEYS)
            keys = slice(start, start + NBR_KEYS)
            q = ql_ref[blk * QBLK:(blk + 1) * QBLK, hs]
            s_lat = (lax.dot_general(q, kl_ref[keys, hs], NT_DIMS, preferred_element_type=F32) * scale
                     + bias_ref[e, cfg])
            s_ctx = lax.dot_general(q, kc, NT_DIMS, preferred_element_type=F32) * scale
            m = jnp.maximum(jnp.max(s_lat, axis=-1, keepdims=True), jnp.max(s_ctx, axis=-1, keepdims=True))
            ol = (jnp.dot(jnp.exp(s_lat - m).astype(BF16), v1l[e, keys, :], preferred_element_type=F32)
                  + jnp.dot(jnp.exp(s_ctx - m).astype(BF16), v1c[e], preferred_element_type=F32))
            ol_ref[blk * QBLK:(blk + 1) * QBLK, hs] = (ol[:, :LANES] / ol[:, LANES:]).astype(BF16)

        if with_ctx:
            s = lax.dot_general(qc_ref[:, hs], kc, NT_DIMS, preferred_element_type=F32) * scale
            p = jnp.exp(s - jnp.max(s, axis=-1, keepdims=True)).astype(BF16)
            ol = jnp.dot(p, v1c[e], preferred_element_type=F32)
            oc_ref[:, hs] = (ol[:, :LANES] / ol[:, LANES:]).astype(BF16)


def _nbr_call(proj, bias_tab, nb, with_ctx):
    order = lambda g, b: (b, g)
    width = NBR_HEADS_PER_STEP * LANES
    out_specs, out_shapes = _mixer_outs(nb, width, order, with_ctx)
    return pl.pallas_call(
        functools.partial(_nbr_kernel, with_ctx=with_ctx),
        grid=(B_HEADS // NBR_HEADS_PER_STEP, nb),
        in_specs=_mixer_specs(nb, (width,) * 3, (COL_BQ, COL_BK, COL_BV), order) + [
            pl.BlockSpec((NBR_HEADS_PER_STEP, 3, QBLK, NBR_KEYS), lambda g, b: (g, 0, 0, 0)),
        ],
        out_specs=out_specs,
        out_shape=out_shapes,
        scratch_shapes=[pltpu.VMEM((NBR_HEADS_PER_STEP, SEQ, 2 * LANES), BF16),
                        pltpu.VMEM((NBR_HEADS_PER_STEP, CTX_LEN, 2 * LANES), BF16)],
        compiler_params=_params(("parallel", "parallel")),
        name="nbr_attention",
    )(*([proj] * 6), bias_tab)


def _ret_kernel(ql_ref, kl_ref, vl_ref, gl_ref, qc_ref, kc_ref, vc_ref, gc_ref, cos_ref, sa_ref, sb_ref,
                dec_ref, rn_ref, *rest, with_ctx):
    ol_ref, oc_ref = (rest[0], rest[1]) if with_ctx else (rest[0], None)
    of_ref, ob_ref, st_ref = rest[-3:]
    ch = QBLK
    n_chunks = SEQ // ch
    kscale = C_KEY_DIM ** -0.5

    def run():
        ii = lax.broadcasted_iota(jnp.int32, (ch, ch), 0)
        jj = lax.broadcasted_iota(jnp.int32, (ch, ch), 1)
        dist = (ii - jj).astype(F32)
        pos = lax.broadcasted_iota(jnp.int32, (ch, 1), 0).astype(F32)
        lane = lax.broadcasted_iota(jnp.int32, (1, LANES), 1)
        rn = rn_ref[...]

        def log_sigmoid(x):
            return -(jnp.log1p(jnp.exp(-jnp.abs(x))) + jnp.maximum(-x, 0.0))

        heads = []
        for e in range(2):
            lgf = log_sigmoid(dec_ref[0, 0, e:e + 1, 0:1])
            lgb = log_sigmoid(dec_ref[0, 1, e:e + 1, 0:1])
            heads.append(dict(
                mask=(lane >= e * C_KEY_DIM) & (lane < (e + 1) * C_KEY_DIM),
                intra_f=jnp.where(dist >= 0, jnp.exp(lgf * jnp.maximum(dist, 0.0)), 0.0),
                intra_b=jnp.where(dist <= 0, jnp.exp(lgb * jnp.maximum(-dist, 0.0)), 0.0),
                qdec_f=jnp.exp(lgf * (pos + 1.0)), kdec_f=jnp.exp(lgf * (ch - 1.0 - pos)),
                cdec_f=jnp.exp(lgf * ch),
                qdec_b=jnp.exp(lgb * (ch - pos)), kdec_b=jnp.exp(lgb * pos),
                cdec_b=jnp.exp(lgb * ch),
                vs=slice(e * LANES, (e + 1) * LANES),
            ))

        def finish(o, g):
            return (_silu(g.astype(F32)) * (_rms(o) * rn)).astype(BF16)

        qc = qc_ref[...].astype(F32)
        kc = kc_ref[...].astype(F32) * kscale
        for e, hd in enumerate(heads):
            v = vc_ref[:, hd["vs"]]
            st_ref[e] = lax.dot_general((kc * hd["kdec_f"]).astype(BF16), v, TN_DIMS, preferred_element_type=F32)
            st_ref[2 + e] = lax.dot_general((kc * hd["kdec_b"]).astype(BF16), v, TN_DIMS,
                                            preferred_element_type=F32)
            if with_ctx:
                qe = jnp.where(hd["mask"], qc, 0.0).astype(BF16)
                att = lax.dot_general(qe, kc.astype(BF16), NT_DIMS, preferred_element_type=F32)
                att = att * (hd["intra_f"] + hd["intra_b"])
                o = jnp.dot(att.astype(BF16), v, preferred_element_type=F32)
                oc_ref[:, hd["vs"]] = finish(o, gc_ref[:, hd["vs"]])

        def chunk_rows(c):
            return pl.ds(pl.multiple_of(c * ch, ch), ch)

        def sweep(c, fwd):
            rows = chunk_rows(c)
            tabs = (cos_ref[rows, :], sa_ref[rows, :], sb_ref[rows, :])
            q = _rope(ql_ref[rows, :].astype(F32), *tabs)
            k = _rope(kl_ref[rows, :].astype(F32), *tabs) * kscale
            kb = k.astype(BF16)
            for e, hd in enumerate(heads):
                d = "f" if fwd else "b"
                si = e if fwd else 2 + e
                v = vl_ref[rows, hd["vs"]]
                qe = jnp.where(hd["mask"], q, 0.0)
                att = lax.dot_general(qe.astype(BF16), kb, NT_DIMS, preferred_element_type=F32) * hd["intra_" + d]
                st = st_ref[si]
                o = (jnp.dot(att.astype(BF16), v, preferred_element_type=F32)
                     + jnp.dot((qe * hd["qdec_" + d]).astype(BF16), st.astype(BF16), preferred_element_type=F32))
                st_ref[si] = st * hd["cdec_" + d] + lax.dot_general(
                    (k * hd["kdec_" + d]).astype(BF16), v, TN_DIMS, preferred_element_type=F32)
                (of_ref if fwd else ob_ref)[rows, hd["vs"]] = o

        def finish_chunk(c):
            rows = chunk_rows(c)
            for hd in heads:
                vs = hd["vs"]
                ol_ref[rows, vs] = finish(of_ref[rows, vs] + ob_ref[rows, vs], gl_ref[rows, vs])

        def first_half(t, carry):
            sweep(t, True)
            sweep(n_chunks - 1 - t, False)
            return carry

        def second_half(t, carry):
            sweep(t, True)
            sweep(n_chunks - 1 - t, False)
            finish_chunk(t)
            finish_chunk(n_chunks - 1 - t)
            return carry

        lax.fori_loop(0, n_chunks // 2, first_half, 0, unroll=True)
        lax.fori_loop(n_chunks // 2, n_chunks, second_half, 0, unroll=True)

    run()


def _ret_call(proj, tabs, ret_decay, ret_norm, nb, with_ctx):
    pairs = C_HEADS // 2
    w2 = 2 * LANES
    order = lambda b, p: (b, p)
    whole = lambda b, p: (0, 0)
    out_specs, out_shapes = _mixer_outs(nb, w2, order, with_ctx)
    dec = jnp.broadcast_to(ret_decay.astype(F32).reshape(2, pairs, 2).transpose(1, 0, 2)[..., None],
                           (pairs, 2, 2, LANES))
    return pl.pallas_call(
        functools.partial(_ret_kernel, with_ctx=with_ctx),
        grid=(nb, pairs),
        in_specs=_mixer_specs(nb, (LANES, LANES, w2, w2), (COL_CQ, COL_CK, COL_CV, COL_CG), order) + [
            pl.BlockSpec((SEQ, LANES), whole),
            pl.BlockSpec((SEQ, LANES), whole),
            pl.BlockSpec((SEQ, LANES), whole),
            pl.BlockSpec((1, 2, 2, LANES), lambda b, p: (p, 0, 0, 0)),
            pl.BlockSpec((1, LANES), whole),
        ],
        out_specs=out_specs,
        out_shape=out_shapes,
        scratch_shapes=[
            pltpu.VMEM((SEQ, w2), F32),
            pltpu.VMEM((SEQ, w2), F32),
            pltpu.VMEM((4, LANES, LANES), F32),
        ],
        compiler_params=_params(("parallel", "parallel")),
        name="retention",
    )(*([proj] * 8), *tabs, dec, ret_norm.reshape(1, LANES))


def _merge_kernel(h_ref, mod_ref, post_ref, g_lo_ref, g_hi_ref, wb_ref, wo_ref, *rest, n_lat_tiles, with_ctx):
    d = h_ref.shape[1]
    o_ref = rest[-1]
    if with_ctx:
        is_lat = pl.program_id(0) < n_lat_tiles
        xs = [jnp.where(is_lat, rest[2 * k][...], rest[2 * k + 1][...]) for k in range(3)]
    else:
        xs = [rest[k][...] for k in range(3)]
    g_lo = g_lo_ref[...]
    g_hi = g_hi_ref[...]
    gates = (g_lo[:, :d], jnp.concatenate([g_lo[:, d:], g_hi[:, :d // 2]], axis=1), g_hi[:, d // 2:])
    m = None
    for k in range(3):
        t = jax.nn.sigmoid(gates[k].astype(F32)) * jnp.dot(xs[k], wb_ref[k], preferred_element_type=F32)
        m = t if m is None else m + t
    y = jnp.dot(m.astype(BF16), wo_ref[...], preferred_element_type=F32)
    o_ref[...] = h_ref[...] + mod_ref[0, 5:6, :] * (_rms(y) * post_ref[...])


def _merge_call(h, mod_l, post_g, branches, proj, wb, wo, layer, nrows, n_lat_seg):
    d = h.shape[1]
    tm = CTX_LEN
    with_ctx = len(branches[0]) == 2
    n_lat_tiles = n_lat_seg * SEQ // tm
    gw = 3 * d // 2
    g0 = COL_GATES // gw
    resident = dict(pipeline_mode=pl.Buffered(1))
    row = lambda i: (i, 0)
    lat_row = lambda i: (jnp.minimum(i, n_lat_tiles - 1), 0)
    ctx_row = lambda i: (jnp.maximum(i - n_lat_tiles, 0), 0)
    branch_specs, branch_args = [], []
    for outs in branches:
        branch_specs.append(pl.BlockSpec((tm, BRANCH_W), lat_row))
        if with_ctx:
            branch_specs.append(pl.BlockSpec((tm, BRANCH_W), ctx_row))
        branch_args.extend(outs)
    return pl.pallas_call(
        functools.partial(_merge_kernel, n_lat_tiles=n_lat_tiles, with_ctx=with_ctx),
        grid=(nrows // tm,),
        in_specs=[
            pl.BlockSpec((tm, d), row),
            pl.BlockSpec((1, N_MOD, d), lambda i: (jnp.minimum((i * tm) // SEQ, n_lat_seg), 0, 0)),
            pl.BlockSpec((1, d), lambda i: (0, 0)),
            pl.BlockSpec((tm, gw), lambda i: (i, g0)),
            pl.BlockSpec((tm, gw), lambda i: (i, g0 + 1)),
            pl.BlockSpec((None, 3, BRANCH_W, d), lambda i: (layer, 0, 0, 0), **resident),
            pl.BlockSpec((None, d, d), lambda i: (layer, 0, 0), **resident),
        ] + branch_specs,
        out_specs=pl.BlockSpec((tm, d), row),
        out_shape=jax.ShapeDtypeStruct((nrows, d), F32),
        compiler_params=_params(("parallel",)),
        name="mixer_merge",
    )(h, mod_l, post_g.reshape(1, d), proj, proj, wb, wo, *branch_args)


def kernel(x, c, ctx, c_ctx, w_mod, b_mod, pre_norm, post_norm, ffn_w_in, ffn_w_out, w_in, diff_lambda,
           diff_norm, na_rpb, ret_decay, ret_norm, w_branch, w_out):
    nb, seq, d = x.shape
    assert (seq, d) == (SEQ, D_MODEL) and ctx.shape == (nb, CTX_LEN, d)
    n_lat, n_ctx = nb * SEQ, nb * CTX_LEN
    n_all = n_lat + n_ctx
    tm = min(FFN_ROWS, n_ctx)
    pad = (-(nb + 1)) % 8
    cc = jnp.concatenate([c, c_ctx[None, :], jnp.zeros((pad, d), c.dtype)], axis=0)
    mods = _mod_call(cc, w_mod, b_mod)
    tabs = _rope_tables()
    w1, w2 = ffn_w_in.astype(BF16), ffn_w_out.astype(BF16)
    w_proj, wb, wo = w_in.astype(BF16), w_branch.astype(BF16), w_out.astype(BF16)

    for l in range(DEPTH):
        with_ctx = l < DEPTH - 1
        lambda_init = 0.8 - 0.6 * math.exp(-0.3 * l)
        nrows_out = n_all if with_ctx else n_lat
        mod_l = mods[l, :nb + 1].reshape(nb + 1, N_MOD, d)
        ffn = functools.partial(_ffn_call, tm=tm, n_lat_seg=nb)

        rows_in = dict(h=x.reshape(n_lat, d), ctx_rows=ctx.reshape(n_ctx, d)) if l == 0 else dict(h=h)
        h = ffn(mod_l=mod_l, k0=0, pre_g=pre_norm[l, 0], post_g=post_norm[l, 0], w1=w1, w2=w2, lk=(l, 0),
                nrows=n_all, **rows_in)
        proj = _inproj_call(h, mod_l, pre_norm[l, 1], w_proj, l, min(PROJ_ROWS, n_ctx), nb)
        oa = _diff_call(proj, tabs, diff_lambda[l], diff_norm[l], lambda_init, nb, with_ctx)
        ob = _nbr_call(proj, _nbr_bias_call(na_rpb[l]), nb, with_ctx)
        yr = _ret_call(proj, tabs, ret_decay[l], ret_norm[l], nb, with_ctx)
        h = _merge_call(h, mod_l, post_norm[l, 1], (oa, ob, yr), proj, wb, wo, l, nrows_out, nb)
        h = ffn(h, mod_l, 6, pre_norm[l, 2], post_norm[l, 2], w1, w2, (l, 1), nrows_out)
    return h.reshape(nb, SEQ, d)
```

```python
import functools
import math

import numpy as np
import jax
import jax.numpy as jnp
from jax import lax
from jax.experimental import pallas as pl
from jax.experimental.pallas import tpu as pltpu

D_MODEL = 2048
SEQ = 2048
DEPTH = 2
GRID_W = 64
CTX_LEN = 256
N_MOD = 9
FFN_DIM = 5632
A_HEADS = 8
A_HEAD_DIM = 64
B_HEADS = 8
B_HEAD_DIM = 128
WIN_R = 8
WIN_C = 16
C_HEADS = 8
C_KEY_DIM = 64
BRANCH_W = 1024
ROPE_BASE = 10000.0
EPS = 1e-6

COL_AQ, COL_AK, COL_AV = 0, 1024, 2048
COL_BQ, COL_BK, COL_BV = 3072, 4096, 5120
COL_CQ, COL_CK, COL_CV, COL_CG = 6144, 6656, 7168, 8192
COL_GATES = 9216
IN_COLS = 15360

LANES = 128
BF16_SUBLANES = 16
QBLK = 256
ROWS_PER_QBLK = QBLK // GRID_W
NBR_KROWS = ROWS_PER_QBLK + WIN_R - 1
NBR_KEYS = NBR_KROWS * GRID_W
VMEM_LIMIT = 60 * 1024 * 1024
FFN_ROWS = 512
PROJ_ROWS = 1024
FFN_TILE = 512
FFN_RESIDENT_TILES = 2
FFN_STAGE_ROWS = 64
MXU_COLS = 256
PROJ_TILE = 2560
PROJ_LOOKAHEAD_STEPS = 4
NBR_HEADS_PER_STEP = 2
DIFF_ROW_SPLIT = 2
NBR_MASKED = -1e30

BF16 = jnp.bfloat16
F32 = jnp.float32
NT_DIMS = (((1,), (1,)), ((), ()))
TN_DIMS = (((0,), (0,)), ((), ()))


def _params(sem):
    return pltpu.CompilerParams(dimension_semantics=sem, vmem_limit_bytes=VMEM_LIMIT)


def _rms(x):
    return x * lax.rsqrt(jnp.mean(x * x, axis=-1, keepdims=True) + EPS)


def _silu(x):
    return x * jax.nn.sigmoid(x)


def _row_chunks(row0, n_rows, rows_per, body, inline=False):
    def step(r, carry):
        body(pl.ds(pl.multiple_of(row0 + r * rows_per, rows_per), rows_per))
        return carry
    if inline:
        for r in range(n_rows // rows_per):
            step(r, 0)
    else:
        lax.fori_loop(0, n_rows // rows_per, step, 0, unroll=4)


def _adaln_rows(u_ref, dst, read_rows, mod_ref, pre_ref, k0, seg, row0, n_rows, inline):
    gain = pre_ref[...] * (1.0 + mod_ref[seg, k0 + 1:k0 + 2, :])
    shift = mod_ref[seg, k0:k0 + 1, :]

    def body(rows):
        u_ref[dst, rows, :] = (_rms(read_rows(rows)) * gain + shift).astype(u_ref.dtype)

    _row_chunks(row0, n_rows, BF16_SUBLANES, body, inline)


def _mod_kernel(cc_ref, w_ref, b_ref, o_ref):
    s = _silu(cc_ref[...]).astype(BF16)
    o_ref[0] = jnp.dot(s, w_ref[0].astype(BF16), preferred_element_type=F32) + b_ref[0]


def _mod_call(cc, w_mod, b_mod):
    depth, d, n = w_mod.shape
    rows = cc.shape[0]
    tn = 1024
    return pl.pallas_call(
        _mod_kernel,
        grid=(depth, n // tn),
        in_specs=[
            pl.BlockSpec((rows, d), lambda l, j: (0, 0)),
            pl.BlockSpec((1, d, tn), lambda l, j: (l, 0, j)),
            pl.BlockSpec((1, 1, tn), lambda l, j: (l, 0, j)),
        ],
        out_specs=pl.BlockSpec((1, rows, tn), lambda l, j: (l, 0, j)),
        out_shape=jax.ShapeDtypeStruct((depth, rows, n), F32),
        compiler_params=_params(("parallel", "parallel")),
        name="mod_vectors",
    )(cc, w_mod, b_mod.reshape(depth, 1, n))


def _ffn_kernel(*refs, lk, k0, nf, tf, nn, tn, tm, n_lat_seg, n_tiles, n_lat_tiles):
    dual = n_lat_tiles is not None
    h_ref, c_ref = (refs[0], refs[1]) if dual else (refs[0], None)
    mod_ref, pre_ref, post_ref, w1_hbm, w2_ref = refs[1 + dual:6 + dual]
    o_ref, u_ref, hm_ref, y_ref, w1_buf, w1_sem = refs[-6:]
    i = pl.program_id(0)
    j = pl.program_id(1)
    slot = i % 2
    d = h_ref.shape[1]
    segment = lambda tile: jnp.minimum((tile * tm) // SEQ, n_lat_seg)

    n_w1 = n_tiles * nf
    w1_all = w1_hbm.at[lk[0], lk[1]]

    def w1_copy(g, half):
        t = g % nf
        col = pl.multiple_of((half * nf + t) * tf, tf)
        return pltpu.make_async_copy(w1_all.at[:, pl.ds(col, tf)], w1_buf.at[g % 3, half], w1_sem.at[g % 3, half])

    def w1_start(g):
        w1_copy(g, 0).start()
        w1_copy(g, 1).start()

    def row_reader(tile):
        if not dual:
            return lambda rows: h_ref[rows, :]
        return lambda rows: jnp.where(tile < n_lat_tiles, h_ref[rows, :], c_ref[rows, :])

    def adaln_rows(tile, row0, n_rows, dst, inline):
        _adaln_rows(u_ref, dst, row_reader(tile), mod_ref, pre_ref, k0, segment(tile), row0, n_rows, inline)

    @pl.when((i == 0) & (j == 0))
    def _():
        w1_start(0)
        w1_start(1)
        adaln_rows(0, 0, tm, 0, False)

    @pl.when(j < nf)
    def _():
        g = i * nf + j
        w1_copy(g, 0).wait()
        w1_copy(g, 1).wait()

        @pl.when(g + 2 < n_w1)
        def _():
            w1_start(g + 2)

        u = u_ref[slot]
        ring = g % 3
        for c in range(tf // MXU_COLS):
            cols = slice(c * MXU_COLS, (c + 1) * MXU_COLS)
            a = jnp.dot(u, w1_buf[ring, 0, :, cols], preferred_element_type=F32)
            b = jnp.dot(u, w1_buf[ring, 1, :, cols], preferred_element_type=F32)
            hm_ref[j, :, cols] = (_silu(a) * b).astype(BF16)
        n_copy = tm // FFN_STAGE_ROWS
        rows = pl.ds(pl.multiple_of(jnp.minimum(j, n_copy - 1) * FFN_STAGE_ROWS, FFN_STAGE_ROWS), FFN_STAGE_ROWS)
        o_ref[rows, :] = row_reader(i)(rows)

    @pl.when(j >= nf)
    def _():
        n = j - nf
        y = jnp.dot(hm_ref[0], w2_ref[0:tf, :], preferred_element_type=F32)
        for k in range(1, nf):
            y += jnp.dot(hm_ref[k], w2_ref[k * tf:(k + 1) * tf, :], preferred_element_type=F32)
        y_ref[n] = y
        adaln_rows(jnp.minimum(i + 1, n_tiles - 1), n * (tm // nn), tm // nn, 1 - slot, True)

    @pl.when(j == nf + nn - 1)
    def _():
        gain = 0.5 * mod_ref[segment(i), k0 + 2:k0 + 3, :] * post_ref[...]
        ss = sum(jnp.sum(y_ref[n] * y_ref[n], axis=-1, keepdims=True) for n in range(nn))
        r = lax.rsqrt(ss / d + EPS)
        for n in range(nn):
            cols = slice(n * tn, (n + 1) * tn)
            o_ref[:, cols] = o_ref[:, cols] + (y_ref[n] * r) * gain[:, cols]


def _ffn_call(h, mod_l, k0, pre_g, post_g, w1, w2, lk, nrows, tm, n_lat_seg, ctx_rows=None):
    d = h.shape[1]
    f = w2.shape[2]
    tf = tn = FFN_TILE
    nf, nn = f // tf, d // tn
    n_tiles = nrows // tm
    dual = ctx_rows is not None
    n_lat_tiles = h.shape[0] // tm if dual else None
    next_tile = lambda i, j: jnp.minimum(i + jnp.where(j >= nf, 1, 0), n_tiles - 1)
    if dual:
        row_specs = [pl.BlockSpec((tm, d), lambda i, j: (jnp.minimum(next_tile(i, j), n_lat_tiles - 1), 0)),
                     pl.BlockSpec((tm, d), lambda i, j: (jnp.maximum(next_tile(i, j) - n_lat_tiles, 0), 0))]
    else:
        row_specs = [pl.BlockSpec((tm, d), lambda i, j: (next_tile(i, j), 0))]
    rows_in = [h, ctx_rows] if dual else [h]
    w2_tile = lambda j: jnp.where(j < nf, nn - 1, j - nf)
    return pl.pallas_call(
        functools.partial(_ffn_kernel, lk=lk, k0=k0, nf=nf, tf=tf, nn=nn, tn=tn, tm=tm, n_lat_seg=n_lat_seg,
                          n_tiles=n_tiles, n_lat_tiles=n_lat_tiles),
        grid=(n_tiles, nf + nn),
        in_specs=row_specs + [
            pl.BlockSpec(mod_l.shape, lambda i, j: (0, 0, 0)),
            pl.BlockSpec((1, d), lambda i, j: (0, 0)),
            pl.BlockSpec((1, d), lambda i, j: (0, 0)),
            pl.BlockSpec(memory_space=pl.ANY),
            pl.BlockSpec((None, None, f, tn), lambda i, j: (*lk, 0, w2_tile(j))),
        ],
        out_specs=pl.BlockSpec((tm, d), lambda i, j: (i, 0)),
        out_shape=jax.ShapeDtypeStruct((nrows, d), F32),
        scratch_shapes=[pltpu.VMEM((2, tm, d), BF16), pltpu.VMEM((nf, tm, tf), BF16),
                        pltpu.VMEM((nn, tm, tn), F32), pltpu.VMEM((3, 2, d, tf), BF16),
                        pltpu.SemaphoreType.DMA((3, 2))],
        compiler_params=_params(("arbitrary", "arbitrary")),
        name="ffn_sublayer",
    )(*rows_in, mod_l, pre_g.reshape(1, d), post_g.reshape(1, d), w1, w2)


def _inproj_kernel(h_ref, mod_ref, pre_ref, w_ref, o_ref, u_ref, *, tm, n_lat_seg, n_steps, n_pre):
    i = pl.program_id(0)
    j = pl.program_id(1)
    slot = i % 2

    def adaln_rows(tile, row0, n_rows, dst, inline):
        seg = jnp.minimum((tile * tm) // SEQ, n_lat_seg)
        _adaln_rows(u_ref, dst, lambda rows: h_ref[rows, :], mod_ref, pre_ref, 3, seg, row0, n_rows, inline)

    def project():
        o_ref[...] = jnp.dot(u_ref[slot], w_ref[...], preferred_element_type=F32).astype(BF16)

    @pl.when((i == 0) & (j == 0))
    def _():
        adaln_rows(0, 0, tm, 0, False)

    @pl.when(j < n_steps - n_pre)
    def _():
        project()

    @pl.when(j >= n_steps - n_pre)
    def _():
        project()
        adaln_rows(i + 1, (j - (n_steps - n_pre)) * (tm // n_pre), tm // n_pre, 1 - slot, True)


def _inproj_call(h, mod_l, pre_g, w, layer, tm, n_lat_seg):
    nrows, d = h.shape
    n = w.shape[2]
    tn = PROJ_TILE
    n_tiles, n_steps = nrows // tm, n // tn
    n_pre = PROJ_LOOKAHEAD_STEPS
    h_tile = lambda i, j: (jnp.minimum(i + jnp.where(j >= n_steps - n_pre, 1, 0), n_tiles - 1), 0)
    return pl.pallas_call(
        functools.partial(_inproj_kernel, tm=tm, n_lat_seg=n_lat_seg, n_steps=n_steps, n_pre=n_pre),
        grid=(n_tiles, n_steps),
        in_specs=[
            pl.BlockSpec((tm, d), h_tile),
            pl.BlockSpec(mod_l.shape, lambda i, j: (0, 0, 0)),
            pl.BlockSpec((1, d), lambda i, j: (0, 0)),
            pl.BlockSpec((None, d, tn), lambda i, j: (layer, 0, j)),
        ],
        out_specs=pl.BlockSpec((tm, tn), lambda i, j: (i, j)),
        out_shape=jax.ShapeDtypeStruct((nrows, n), BF16),
        scratch_shapes=[pltpu.VMEM((2, tm, d), BF16)],
        compiler_params=_params(("arbitrary", "arbitrary")),
        name="mixer_in_proj",
    )(h, mod_l, pre_g.reshape(1, d), w)


def _rope(t, cos, sa, sb):
    return t * cos + pltpu.roll(t, LANES - 16, 1) * sa + pltpu.roll(t, 16, 1) * sb


def _rope_tables():
    pos = np.arange(SEQ)
    prow = jnp.asarray(pos // GRID_W, F32)
    pcol = jnp.asarray(pos % GRID_W, F32)
    half = 16
    inv = ROPE_BASE ** (-jnp.arange(half, dtype=F32) / half)
    lane = np.arange(LANES)
    freq = jnp.asarray(lane % half)
    use_col = jnp.asarray((lane % 64) >= 32)
    second = jnp.asarray((lane % 32) >= half)
    p = jnp.where(use_col[None, :], pcol[:, None], prow[:, None])
    ang = p * inv[freq][None, :]
    cos, sin = jnp.cos(ang), jnp.sin(ang)
    sa = jnp.where(second[None, :], 0.0, -sin)
    sb = jnp.where(second[None, :], sin, 0.0)
    return cos, sa, sb


def _mixer_specs(nb, width, cols, order):
    ctx_blk0 = nb * SEQ // CTX_LEN

    def spec(rows, col, w, ctx):
        def index(*g):
            b, h = order(*g)
            return ((ctx_blk0 + b) if ctx else b, col // w + h)
        return pl.BlockSpec((rows, w), index)

    return ([spec(SEQ, c, w, False) for c, w in zip(cols, width)]
            + [spec(CTX_LEN, c, w, True) for c, w in zip(cols, width)])


def _mixer_outs(nb, width, order, with_ctx):
    def index(*g):
        b, h = order(*g)
        return (b, h)
    specs = [pl.BlockSpec((SEQ, width), index)]
    shapes = [jax.ShapeDtypeStruct((nb * SEQ, BRANCH_W), BF16)]
    if with_ctx:
        specs.append(pl.BlockSpec((CTX_LEN, width), index))
        shapes.append(jax.ShapeDtypeStruct((nb * CTX_LEN, BRANCH_W), BF16))
    return specs, shapes


def _diff_kernel(ql_ref, kl_ref, vl_ref, qc_ref, kc_ref, vc_ref, cos_ref, sa_ref, sb_ref, lam_ref, g_ref,
                 *rest, lambda_init, with_ctx):
    ol_ref, oc_ref = (rest[0], rest[1]) if with_ctx else (rest[0], None)
    k_all, v_ones = rest[-2:]
    k_all[0:CTX_LEN, :] = kc_ref[...]
    k_all[CTX_LEN:, :] = _rope(kl_ref[...].astype(F32), cos_ref[...], sa_ref[...], sb_ref[...]).astype(BF16)
    v_ones[0:CTX_LEN, 0:LANES] = vc_ref[...]
    v_ones[CTX_LEN:, 0:LANES] = vl_ref[...]
    v_ones[:, LANES:] = jnp.ones((CTX_LEN + SEQ, LANES), BF16)

    lv = lam_ref[...]
    lam = (jnp.exp(jnp.sum(lv[0:1] * lv[1:2], axis=-1, keepdims=True))
           - jnp.exp(jnp.sum(lv[2:3] * lv[3:4], axis=-1, keepdims=True)) + lambda_init)
    lane = lax.broadcasted_iota(jnp.int32, (1, LANES), 1)

    def attend(q, nk):
        q = q * (A_HEAD_DIM ** -0.5)
        k = k_all[0:nk, :]
        v1 = v_ones[0:nk, :]

        rows = QBLK // DIFF_ROW_SPLIT
        qms = [jnp.where(mask, q[r * rows:(r + 1) * rows], 0.0).astype(BF16)
               for r in range(DIFF_ROW_SPLIT) for mask in (lane < A_HEAD_DIM, lane >= A_HEAD_DIM)]
        ss = [lax.dot_general(qm, k, NT_DIMS, preferred_element_type=F32) for qm in qms]
        es = [jnp.exp(s - jnp.max(s, axis=-1, keepdims=True)).astype(BF16) for s in ss]
        ols = [jnp.dot(e, v1, preferred_element_type=F32) for e in es]
        os = [ol[:, :LANES] / ol[:, LANES:] for ol in ols]
        o = jnp.concatenate([os[2 * r] - lam * os[2 * r + 1] for r in range(DIFF_ROW_SPLIT)], axis=0)
        return (_rms(o) * g_ref[...] * (1.0 - lambda_init)).astype(BF16)

    def block(t, carry):
        rows = pl.ds(pl.multiple_of(t * QBLK, QBLK), QBLK)
        q = _rope(ql_ref[rows, :].astype(F32), cos_ref[rows, :], sa_ref[rows, :], sb_ref[rows, :])
        ol_ref[rows, :] = attend(q, CTX_LEN + SEQ)
        return carry

    lax.fori_loop(0, SEQ // QBLK, block, 0, unroll=True)
    if with_ctx:
        oc_ref[...] = attend(qc_ref[...].astype(F32), CTX_LEN)


def _diff_call(proj, tabs, diff_lambda, diff_norm, lambda_init, nb, with_ctx):
    order = lambda b, h: (b, h)
    whole = lambda b, h: (0, 0)
    out_specs, out_shapes = _mixer_outs(nb, LANES, order, with_ctx)
    return pl.pallas_call(
        functools.partial(_diff_kernel, lambda_init=lambda_init, with_ctx=with_ctx),
        grid=(nb, A_HEADS),
        in_specs=_mixer_specs(nb, (LANES,) * 3, (COL_AQ, COL_AK, COL_AV), order) + [
            pl.BlockSpec((SEQ, LANES), whole),
            pl.BlockSpec((SEQ, LANES), whole),
            pl.BlockSpec((SEQ, LANES), whole),
            pl.BlockSpec((4, A_HEAD_DIM), whole),
            pl.BlockSpec((1, LANES), whole),
        ],
        out_specs=out_specs,
        out_shape=out_shapes,
        scratch_shapes=[pltpu.VMEM((CTX_LEN + SEQ, LANES), BF16), pltpu.VMEM((CTX_LEN + SEQ, 2 * LANES), BF16)],
        compiler_params=_params(("parallel", "parallel")),
        name="diff_attention",
    )(*([proj] * 6), *tabs, diff_lambda, diff_norm.reshape(1, LANES))


def _nbr_block_rows(cfg, qi, kj):
    n_rows = SEQ // GRID_W
    r0 = (0, ROWS_PER_QBLK, n_rows - ROWS_PER_QBLK)[cfg]
    ks = min(max(r0 - WIN_R // 2, 0), n_rows - NBR_KROWS)
    r = r0 + qi
    rs = min(max(r - WIN_R // 2, 0), n_rows - WIN_R)
    kr = ks + kj
    return rs <= kr < rs + WIN_R, kr - r + WIN_R - 1


def _nbr_bias_kernel(rpb_ref, o_ref):
    n_dr, n_dc = 2 * WIN_R - 1, 2 * WIN_C - 1
    base = pl.program_id(0) * (n_dr * n_dc)
    qc = lax.broadcasted_iota(jnp.int32, (GRID_W, LANES), 0)
    lane = lax.broadcasted_iota(jnp.int32, (GRID_W, LANES), 1)
    kc = lane % GRID_W
    dc = kc - qc + (WIN_C - 1)
    cs = jnp.clip(qc - WIN_C // 2, 0, GRID_W - WIN_C)
    col_ok = (kc >= cs) & (kc < cs + WIN_C)
    masked = jnp.full((GRID_W, LANES), NBR_MASKED, F32)

    def row_pattern(dr):
        acc = masked
        for x in range(n_dc):
            acc = jnp.where(dc == x, rpb_ref[base + dr * n_dc + x], acc)
        return jnp.where(col_ok, acc, masked)

    pats = [row_pattern(dr) for dr in range(n_dr)]

    def pattern(cfg, qi, kj):
        if kj >= NBR_KROWS:
            return masked
        ok, dr = _nbr_block_rows(cfg, qi, kj)
        return pats[dr] if ok else masked

    for cfg in range(3):
        for qi in range(ROWS_PER_QBLK):
            rows = slice(qi * GRID_W, (qi + 1) * GRID_W)
            for m in range(pl.cdiv(NBR_KEYS, LANES)):
                tile = jnp.where(lane < GRID_W, pattern(cfg, qi, 2 * m), pattern(cfg, qi, 2 * m + 1))
                width = min(LANES, NBR_KEYS - m * LANES)
                o_ref[0, cfg, rows, m * LANES:m * LANES + width] = tile[:, :width]


def _nbr_bias_call(rpb):
    heads = rpb.shape[0]
    return pl.pallas_call(
        _nbr_bias_kernel,
        grid=(heads,),
        in_specs=[pl.BlockSpec(memory_space=pltpu.SMEM)],
        out_specs=pl.BlockSpec((1, 3, QBLK, NBR_KEYS), lambda h: (h, 0, 0, 0)),
        out_shape=jax.ShapeDtypeStruct((heads, 3, QBLK, NBR_KEYS), F32),
        compiler_params=_params(("parallel",)),
        name="nbr_bias_table",
    )(rpb.astype(F32).reshape(-1))


def _nbr_kernel(ql_ref, kl_ref, vl_ref, qc_ref, kc_ref, vc_ref, bias_ref, *rest, with_ctx):
    ol_ref, oc_ref = (rest[0], rest[1]) if with_ctx else (rest[0], None)
    v1l, v1c = rest[-2:]
    scale = B_HEAD_DIM ** -0.5
    lat_blocks = SEQ // QBLK

    for e in range(NBR_HEADS_PER_STEP):
        hs = slice(e * LANES, (e + 1) * LANES)
        v1l[e, :, 0:LANES] = vl_ref[:, hs]
        v1l[e, :, LANES:] = jnp.ones((SEQ, LANES), BF16)
        v1c[e, :, 0:LANES] = vc_ref[:, hs]
        v1c[e, :, LANES:] = jnp.ones((CTX_LEN, LANES), BF16)
        kc = kc_ref[:, hs]

        for blk in range(lat_blocks):
            cfg = 0 if blk == 0 else (2 if blk == lat_blocks - 1 else 1)
            start = min(max((blk * ROWS_PER_QBLK - WIN_R // 2) * GRID_W, 0), SEQ - NBR_KEYS)
            keys = slice(start, start + NBR_KEYS)
            q = ql_ref[blk * QBLK:(blk + 1) * QBLK, hs]
            s_lat = (lax.dot_general(q, kl_ref[keys, hs], NT_DIMS, preferred_element_type=F32) * scale
                     + bias_ref[e, cfg])
            s_ctx = lax.dot_general(q, kc, NT_DIMS, preferred_element_type=F32) * scale
            m = jnp.maximum(jnp.max(s_lat, axis=-1, keepdims=True), jnp.max(s_ctx, axis=-1, keepdims=True))
            ol = (jnp.dot(jnp.exp(s_lat - m).astype(BF16), v1l[e, keys, :], preferred_element_type=F32)
                  + jnp.dot(jnp.exp(s_ctx - m).astype(BF16), v1c[e], preferred_element_type=F32))
            ol_ref[blk * QBLK:(blk + 1) * QBLK, hs] = (ol[:, :LANES] / ol[:, LANES:]).astype(BF16)

        if with_ctx:
            s = lax.dot_general(qc_ref[:, hs], kc, NT_DIMS, preferred_element_type=F32) * scale
            p = jnp.exp(s - jnp.max(s, axis=-1, keepdims=True)).astype(BF16)
            ol = jnp.dot(p, v1c[e], preferred_element_type=F32)
            oc_ref[:, hs] = (ol[:, :LANES] / ol[:, LANES:]).astype(BF16)


def _nbr_call(proj, bias_tab, nb, with_ctx):
    order = lambda g, b: (b, g)
    width = NBR_HEADS_PER_STEP * LANES
    out_specs, out_shapes = _mixer_outs(nb, width, order, with_ctx)
    return pl.pallas_call(
        functools.partial(_nbr_kernel, with_ctx=with_ctx),
        grid=(B_HEADS // NBR_HEADS_PER_STEP, nb),
        in_specs=_mixer_specs(nb, (width,) * 3, (COL_BQ, COL_BK, COL_BV), order) + [
            pl.BlockSpec((NBR_HEADS_PER_STEP, 3, QBLK, NBR_KEYS), lambda g, b: (g, 0, 0, 0)),
        ],
        out_specs=out_specs,
        out_shape=out_shapes,
        scratch_shapes=[pltpu.VMEM((NBR_HEADS_PER_STEP, SEQ, 2 * LANES), BF16),
                        pltpu.VMEM((NBR_HEADS_PER_STEP, CTX_LEN, 2 * LANES), BF16)],
        compiler_params=_params(("parallel", "parallel")),
        name="nbr_attention",
    )(*([proj] * 6), bias_tab)


def _ret_kernel(ql_ref, kl_ref, vl_ref, gl_ref, qc_ref, kc_ref, vc_ref, gc_ref, cos_ref, sa_ref, sb_ref,
                dec_ref, rn_ref, *rest, with_ctx):
    ol_ref, oc_ref = (rest[0], rest[1]) if with_ctx else (rest[0], None)
    of_ref, ob_ref, st_ref = rest[-3:]
    ch = QBLK
    n_chunks = SEQ // ch
    kscale = C_KEY_DIM ** -0.5

    def run():
        ii = lax.broadcasted_iota(jnp.int32, (ch, ch), 0)
        jj = lax.broadcasted_iota(jnp.int32, (ch, ch), 1)
        dist = (ii - jj).astype(F32)
        pos = lax.broadcasted_iota(jnp.int32, (ch, 1), 0).astype(F32)
        lane = lax.broadcasted_iota(jnp.int32, (1, LANES), 1)
        rn = rn_ref[...]

        def log_sigmoid(x):
            return -(jnp.log1p(jnp.exp(-jnp.abs(x))) + jnp.maximum(-x, 0.0))

        heads = []
        for e in range(2):
            lgf = log_sigmoid(dec_ref[0, 0, e:e + 1, 0:1])
            lgb = log_sigmoid(dec_ref[0, 1, e:e + 1, 0:1])
            heads.append(dict(
                mask=(lane >= e * C_KEY_DIM) & (lane < (e + 1) * C_KEY_DIM),
                intra_f=jnp.where(dist >= 0, jnp.exp(lgf * jnp.maximum(dist, 0.0)), 0.0),
                intra_b=jnp.where(dist <= 0, jnp.exp(lgb * jnp.maximum(-dist, 0.0)), 0.0),
                qdec_f=jnp.exp(lgf * (pos + 1.0)), kdec_f=jnp.exp(lgf * (ch - 1.0 - pos)),
                cdec_f=jnp.exp(lgf * ch),
                qdec_b=jnp.exp(lgb * (ch - pos)), kdec_b=jnp.exp(lgb * pos),
                cdec_b=jnp.exp(lgb * ch),
                vs=slice(e * LANES, (e + 1) * LANES),
            ))

        def finish(o, g):
            return (_silu(g.astype(F32)) * (_rms(o) * rn)).astype(BF16)

        qc = qc_ref[...].astype(F32)
        kc = kc_ref[...].astype(F32) * kscale
        for e, hd in enumerate(heads):
            v = vc_ref[:, hd["vs"]]
            st_ref[e] = lax.dot_general((kc * hd["kdec_f"]).astype(BF16), v, TN_DIMS, preferred_element_type=F32)
            st_ref[2 + e] = lax.dot_general((kc * hd["kdec_b"]).astype(BF16), v, TN_DIMS,
                                            preferred_element_type=F32)
            if with_ctx:
                qe = jnp.where(hd["mask"], qc, 0.0).astype(BF16)
                att = lax.dot_general(qe, kc.astype(BF16), NT_DIMS, preferred_element_type=F32)
                att = att * (hd["intra_f"] + hd["intra_b"])
                o = jnp.dot(att.astype(BF16), v, preferred_element_type=F32)
                oc_ref[:, hd["vs"]] = finish(o, gc_ref[:, hd["vs"]])

        def chunk_rows(c):
            return pl.ds(pl.multiple_of(c * ch, ch), ch)

        def sweep(c, fwd):
            rows = chunk_rows(c)
            tabs = (cos_ref[rows, :], sa_ref[rows, :], sb_ref[rows, :])
            q = _rope(ql_ref[rows, :].astype(F32), *tabs)
            k = _rope(kl_ref[rows, :].astype(F32), *tabs) * kscale
            kb = k.astype(BF16)
            for e, hd in enumerate(heads):
                d = "f" if fwd else "b"
                si = e if fwd else 2 + e
                v = vl_ref[rows, hd["vs"]]
                qe = jnp.where(hd["mask"], q, 0.0)
                att = lax.dot_general(qe.astype(BF16), kb, NT_DIMS, preferred_element_type=F32) * hd["intra_" + d]
                st = st_ref[si]
                o = (jnp.dot(att.astype(BF16), v, preferred_element_type=F32)
                     + jnp.dot((qe * hd["qdec_" + d]).astype(BF16), st.astype(BF16), preferred_element_type=F32))
                st_ref[si] = st * hd["cdec_" + d] + lax.dot_general(
                    (k * hd["kdec_" + d]).astype(BF16), v, TN_DIMS, preferred_element_type=F32)
                (of_ref if fwd else ob_ref)[rows, hd["vs"]] = o

        def finish_chunk(c):
            rows = chunk_rows(c)
            for hd in heads:
                vs = hd["vs"]
                ol_ref[rows, vs] = finish(of_ref[rows, vs] + ob_ref[rows, vs], gl_ref[rows, vs])

        def first_half(t, carry):
            sweep(t, True)
            sweep(n_chunks - 1 - t, False)
            return carry

        def second_half(t, carry):
            sweep(t, True)
            sweep(n_chunks - 1 - t, False)
            finish_chunk(t)
            finish_chunk(n_chunks - 1 - t)
            return carry

        lax.fori_loop(0, n_chunks // 2, first_half, 0, unroll=True)
        lax.fori_loop(n_chunks // 2, n_chunks, second_half, 0, unroll=True)

    run()


def _ret_call(proj, tabs, ret_decay, ret_norm, nb, with_ctx):
    pairs = C_HEADS // 2
    w2 = 2 * LANES
    order = lambda b, p: (b, p)
    whole = lambda b, p: (0, 0)
    out_specs, out_shapes = _mixer_outs(nb, w2, order, with_ctx)
    dec = jnp.broadcast_to(ret_decay.astype(F32).reshape(2, pairs, 2).transpose(1, 0, 2)[..., None],
                           (pairs, 2, 2, LANES))
    return pl.pallas_call(
        functools.partial(_ret_kernel, with_ctx=with_ctx),
        grid=(nb, pairs),
        in_specs=_mixer_specs(nb, (LANES, LANES, w2, w2), (COL_CQ, COL_CK, COL_CV, COL_CG), order) + [
            pl.BlockSpec((SEQ, LANES), whole),
            pl.BlockSpec((SEQ, LANES), whole),
            pl.BlockSpec((SEQ, LANES), whole),
            pl.BlockSpec((1, 2, 2, LANES), lambda b, p: (p, 0, 0, 0)),
            pl.BlockSpec((1, LANES), whole),
        ],
        out_specs=out_specs,
        out_shape=out_shapes,
        scratch_shapes=[
            pltpu.VMEM((SEQ, w2), F32),
            pltpu.VMEM((SEQ, w2), F32),
            pltpu.VMEM((4, LANES, LANES), F32),
        ],
        compiler_params=_params(("parallel", "parallel")),
        name="retention",
    )(*([proj] * 8), *tabs, dec, ret_norm.reshape(1, LANES))


def _merge_kernel(h_ref, mod_ref, post_ref, g_lo_ref, g_hi_ref, wb_ref, wo_ref, *rest, n_lat_tiles, with_ctx):
    d = h_ref.shape[1]
    o_ref = rest[-1]
    if with_ctx:
        is_lat = pl.program_id(0) < n_lat_tiles
        xs = [jnp.where(is_lat, rest[2 * k][...], rest[2 * k + 1][...]) for k in range(3)]
    else:
        xs = [rest[k][...] for k in range(3)]
    g_lo = g_lo_ref[...]
    g_hi = g_hi_ref[...]
    gates = (g_lo[:, :d], jnp.concatenate([g_lo[:, d:], g_hi[:, :d // 2]], axis=1), g_hi[:, d // 2:])
    m = None
    for k in range(3):
        t = jax.nn.sigmoid(gates[k].astype(F32)) * jnp.dot(xs[k], wb_ref[k], preferred_element_type=F32)
        m = t if m is None else m + t
    y = jnp.dot(m.astype(BF16), wo_ref[...], preferred_element_type=F32)
    o_ref[...] = h_ref[...] + mod_ref[0, 5:6, :] * (_rms(y) * post_ref[...])


def _merge_call(h, mod_l, post_g, branches, proj, wb, wo, layer, nrows, n_lat_seg):
    d = h.shape[1]
    tm = CTX_LEN
    with_ctx = len(branches[0]) == 2
    n_lat_tiles = n_lat_seg * SEQ // tm
    gw = 3 * d // 2
    g0 = COL_GATES // gw
    resident = dict(pipeline_mode=pl.Buffered(1))
    row = lambda i: (i, 0)
    lat_row = lambda i: (jnp.minimum(i, n_lat_tiles - 1), 0)
    ctx_row = lambda i: (jnp.maximum(i - n_lat_tiles, 0), 0)
    branch_specs, branch_args = [], []
    for outs in branches:
        branch_specs.append(pl.BlockSpec((tm, BRANCH_W), lat_row))
        if with_ctx:
            branch_specs.append(pl.BlockSpec((tm, BRANCH_W), ctx_row))
        branch_args.extend(outs)
    return pl.pallas_call(
        functools.partial(_merge_kernel, n_lat_tiles=n_lat_tiles, with_ctx=with_ctx),
        grid=(nrows // tm,),
        in_specs=[
            pl.BlockSpec((tm, d), row),
            pl.BlockSpec((1, N_MOD, d), lambda i: (jnp.minimum((i * tm) // SEQ, n_lat_seg), 0, 0)),
            pl.BlockSpec((1, d), lambda i: (0, 0)),
            pl.BlockSpec((tm, gw), lambda i: (i, g0)),
            pl.BlockSpec((tm, gw), lambda i: (i, g0 + 1)),
            pl.BlockSpec((None, 3, BRANCH_W, d), lambda i: (layer, 0, 0, 0), **resident),
            pl.BlockSpec((None, d, d), lambda i: (layer, 0, 0), **resident),
        ] + branch_specs,
        out_specs=pl.BlockSpec((tm, d), row),
        out_shape=jax.ShapeDtypeStruct((nrows, d), F32),
        compiler_params=_params(("parallel",)),
        name="mixer_merge",
    )(h, mod_l, post_g.reshape(1, d), proj, proj, wb, wo, *branch_args)


def kernel(x, c, ctx, c_ctx, w_mod, b_mod, pre_norm, post_norm, ffn_w_in, ffn_w_out, w_in, diff_lambda,
           diff_norm, na_rpb, ret_decay, ret_norm, w_branch, w_out):
    nb, seq, d = x.shape
    assert (seq, d) == (SEQ, D_MODEL) and ctx.shape == (nb, CTX_LEN, d)
    n_lat, n_ctx = nb * SEQ, nb * CTX_LEN
    n_all = n_lat + n_ctx
    tm = min(FFN_ROWS, n_ctx)
    pad = (-(nb + 1)) % 8
    cc = jnp.concatenate([c, c_ctx[None, :], jnp.zeros((pad, d), c.dtype)], axis=0)
    mods = _mod_call(cc, w_mod, b_mod)
    tabs = _rope_tables()
    w1, w2 = ffn_w_in.astype(BF16), ffn_w_out.astype(BF16)
    w_proj, wb, wo = w_in.astype(BF16), w_branch.astype(BF16), w_out.astype(BF16)

    for l in range(DEPTH):
        with_ctx = l < DEPTH - 1
        lambda_init = 0.8 - 0.6 * math.exp(-0.3 * l)
        nrows_out = n_all if with_ctx else n_lat
        mod_l = mods[l, :nb + 1].reshape(nb + 1, N_MOD, d)
        ffn = functools.partial(_ffn_call, tm=tm, n_lat_seg=nb)

        rows_in = dict(h=jnp.concatenate([x.reshape(n_lat, d), ctx.reshape(n_ctx, d)], axis=0) if l == 0 else h)
        h = ffn(mod_l=mod_l, k0=0, pre_g=pre_norm[l, 0], post_g=post_norm[l, 0], w1=w1, w2=w2, lk=(l, 0),
                nrows=n_all, **rows_in)
        proj = _inproj_call(h, mod_l, pre_norm[l, 1], w_proj, l, min(PROJ_ROWS, n_ctx), nb)
        oa = _diff_call(proj, tabs, diff_lambda[l], diff_norm[l], lambda_init, nb, with_ctx)
        ob = _nbr_call(proj, _nbr_bias_call(na_rpb[l]), nb, with_ctx)
        yr = _ret_call(proj, tabs, ret_decay[l], ret_norm[l], nb, with_ctx)
        h = _merge_call(h, mod_l, post_norm[l, 1], (oa, ob, yr), proj, wb, wo, l, nrows_out, nb)
        h = ffn(h, mod_l, 6, pre_norm[l, 2], post_norm[l, 2], w1, w2, (l, 1), nrows_out)
    return h.reshape(nb, SEQ, d)
```

```python
import functools
import math

import numpy as np
import jax
import jax.numpy as jnp
from jax import lax
from jax.experimental import pallas as pl
from jax.experimental.pallas import tpu as pltpu

D_MODEL = 2048
SEQ = 2048
DEPTH = 2
GRID_W = 64
CTX_LEN = 256
N_MOD = 9
FFN_DIM = 5632
A_HEADS = 8
A_HEAD_DIM = 64
B_HEADS = 8
B_HEAD_DIM = 128
WIN_R = 8
WIN_C = 16
C_HEADS = 8
C_KEY_DIM = 64
BRANCH_W = 1024
ROPE_BASE = 10000.0
EPS = 1e-6

COL_AQ, COL_AK, COL_AV = 0, 1024, 2048
COL_BQ, COL_BK, COL_BV = 3072, 4096, 5120
COL_CQ, COL_CK, COL_CV, COL_CG = 6144, 6656, 7168, 8192
COL_GATES = 9216
IN_COLS = 15360

LANES = 128
BF16_SUBLANES = 16
QBLK = 256
ROWS_PER_QBLK = QBLK // GRID_W
NBR_KROWS = ROWS_PER_QBLK + WIN_R - 1
NBR_KEYS = NBR_KROWS * GRID_W
VMEM_LIMIT = 60 * 1024 * 1024
FFN_ROWS = 512
PROJ_ROWS = 1024
FFN_TILE = 512
FFN_RESIDENT_TILES = 2
FFN_STAGE_ROWS = 64
MXU_COLS = 256
PROJ_TILE = 2560
PROJ_LOOKAHEAD_STEPS = 4
NBR_HEADS_PER_STEP = 4
DIFF_ROW_SPLIT = 2
NBR_MASKED = -1e30

BF16 = jnp.bfloat16
F32 = jnp.float32
NT_DIMS = (((1,), (1,)), ((), ()))
TN_DIMS = (((0,), (0,)), ((), ()))


def _params(sem):
    return pltpu.CompilerParams(dimension_semantics=sem, vmem_limit_bytes=VMEM_LIMIT)


def _rms(x):
    return x * lax.rsqrt(jnp.mean(x * x, axis=-1, keepdims=True) + EPS)


def _silu(x):
    return x * jax.nn.sigmoid(x)


def _row_chunks(row0, n_rows, rows_per, body, inline=False):
    def step(r, carry):
        body(pl.ds(pl.multiple_of(row0 + r * rows_per, rows_per), rows_per))
        return carry
    if inline:
        for r in range(n_rows // rows_per):
            step(r, 0)
    else:
        lax.fori_loop(0, n_rows // rows_per, step, 0, unroll=4)


def _adaln_rows(u_ref, dst, read_rows, mod_ref, pre_ref, k0, seg, row0, n_rows, inline):
    gain = pre_ref[...] * (1.0 + mod_ref[seg, k0 + 1:k0 + 2, :])
    shift = mod_ref[seg, k0:k0 + 1, :]

    def body(rows):
        u_ref[dst, rows, :] = (_rms(read_rows(rows)) * gain + shift).astype(u_ref.dtype)

    _row_chunks(row0, n_rows, BF16_SUBLANES, body, inline)


def _mod_kernel(cc_ref, w_ref, b_ref, o_ref):
    s = _silu(cc_ref[...]).astype(BF16)
    o_ref[0] = jnp.dot(s, w_ref[0].astype(BF16), preferred_element_type=F32) + b_ref[0]


def _mod_call(cc, w_mod, b_mod):
    depth, d, n = w_mod.shape
    rows = cc.shape[0]
    tn = 1024
    return pl.pallas_call(
        _mod_kernel,
        grid=(depth, n // tn),
        in_specs=[
            pl.BlockSpec((rows, d), lambda l, j: (0, 0)),
            pl.BlockSpec((1, d, tn), lambda l, j: (l, 0, j)),
            pl.BlockSpec((1, 1, tn), lambda l, j: (l, 0, j)),
        ],
        out_specs=pl.BlockSpec((1, rows, tn), lambda l, j: (l, 0, j)),
        out_shape=jax.ShapeDtypeStruct((depth, rows, n), F32),
        compiler_params=_params(("parallel", "parallel")),
        name="mod_vectors",
    )(cc, w_mod, b_mod.reshape(depth, 1, n))


def _ffn_kernel(*refs, k0, nf, tf, nn, tn, tm, n_lat_seg, n_res, n_tiles, n_lat_tiles):
    dual = n_lat_tiles is not None
    h_ref, c_ref = (refs[0], refs[1]) if dual else (refs[0], None)
    mod_ref, pre_ref, post_ref, w1a_ref, w1b_ref, w2_ref = refs[1 + dual:7 + dual]
    resident = refs[7 + dual:7 + dual + 2 * n_res]
    o_ref, u_ref, hm_ref, y_ref = refs[-4:]
    i = pl.program_id(0)
    j = pl.program_id(1)
    slot = i % 2
    d = h_ref.shape[1]
    segment = lambda tile: jnp.minimum((tile * tm) // SEQ, n_lat_seg)

    def row_reader(tile):
        if not dual:
            return lambda rows: h_ref[rows, :]
        return lambda rows: jnp.where(tile < n_lat_tiles, h_ref[rows, :], c_ref[rows, :])

    def adaln_rows(tile, row0, n_rows, dst, inline):
        _adaln_rows(u_ref, dst, row_reader(tile), mod_ref, pre_ref, k0, segment(tile), row0, n_rows, inline)

    @pl.when((i == 0) & (j == 0))
    def _():
        adaln_rows(0, 0, tm, 0, False)

    def hidden_tile(wa_ref, wb_ref):
        u = u_ref[slot]
        for c in range(tf // MXU_COLS):
            cols = slice(c * MXU_COLS, (c + 1) * MXU_COLS)
            a = jnp.dot(u, wa_ref[:, cols], preferred_element_type=F32)
            b = jnp.dot(u, wb_ref[:, cols], preferred_element_type=F32)
            hm_ref[j, :, cols] = (_silu(a) * b).astype(BF16)
        n_copy = tm // FFN_STAGE_ROWS
        rows = pl.ds(pl.multiple_of(jnp.minimum(j, n_copy - 1) * FFN_STAGE_ROWS, FFN_STAGE_ROWS), FFN_STAGE_ROWS)
        o_ref[rows, :] = row_reader(i)(rows)

    for r in range(n_res):
        pl.when(j == r)(functools.partial(hidden_tile, resident[r], resident[n_res + r]))
    pl.when((j >= n_res) & (j < nf))(functools.partial(hidden_tile, w1a_ref, w1b_ref))

    @pl.when(j >= nf)
    def _():
        n = j - nf
        y = jnp.dot(hm_ref[0], w2_ref[0:tf, :], preferred_element_type=F32)
        for k in range(1, nf):
            y += jnp.dot(hm_ref[k], w2_ref[k * tf:(k + 1) * tf, :], preferred_element_type=F32)
        y_ref[n] = y
        adaln_rows(jnp.minimum(i + 1, n_tiles - 1), n * (tm // nn), tm // nn, 1 - slot, True)

    @pl.when(j == nf + nn - 1)
    def _():
        gain = 0.5 * mod_ref[segment(i), k0 + 2:k0 + 3, :] * post_ref[...]
        ss = sum(jnp.sum(y_ref[n] * y_ref[n], axis=-1, keepdims=True) for n in range(nn))
        r = lax.rsqrt(ss / d + EPS)
        for n in range(nn):
            cols = slice(n * tn, (n + 1) * tn)
            o_ref[:, cols] = o_ref[:, cols] + (y_ref[n] * r) * gain[:, cols]


def _ffn_call(h, mod_l, k0, pre_g, post_g, w1, w2, lk, nrows, tm, n_lat_seg, ctx_rows=None):
    d = h.shape[1]
    f = w2.shape[2]
    tf = tn = FFN_TILE
    nf, nn = f // tf, d // tn
    n_tiles = nrows // tm
    dual = ctx_rows is not None
    n_lat_tiles = h.shape[0] // tm if dual else None
    next_tile = lambda i, j: jnp.minimum(i + jnp.where(j >= nf, 1, 0), n_tiles - 1)
    if dual:
        row_specs = [pl.BlockSpec((tm, d), lambda i, j: (jnp.minimum(next_tile(i, j), n_lat_tiles - 1), 0)),
                     pl.BlockSpec((tm, d), lambda i, j: (jnp.maximum(next_tile(i, j) - n_lat_tiles, 0), 0))]
    else:
        row_specs = [pl.BlockSpec((tm, d), lambda i, j: (next_tile(i, j), 0))]
    rows_in = [h, ctx_rows] if dual else [h]
    w2_tile = lambda j: jnp.where(j < nf, nn - 1, j - nf)
    n_res = 0 if dual else FFN_RESIDENT_TILES
    streamed = lambda j: jnp.clip(j, n_res, nf - 1)
    resident = [pl.BlockSpec((None, None, d, tf), functools.partial(lambda t, i, j: (*lk, 0, t), half * nf + r),
                             pipeline_mode=pl.Buffered(1))
                for half in range(2) for r in range(n_res)]
    return pl.pallas_call(
        functools.partial(_ffn_kernel, k0=k0, nf=nf, tf=tf, nn=nn, tn=tn, tm=tm, n_lat_seg=n_lat_seg,
                          n_res=n_res, n_tiles=n_tiles, n_lat_tiles=n_lat_tiles),
        grid=(n_tiles, nf + nn),
        in_specs=row_specs + [
            pl.BlockSpec(mod_l.shape, lambda i, j: (0, 0, 0)),
            pl.BlockSpec((1, d), lambda i, j: (0, 0)),
            pl.BlockSpec((1, d), lambda i, j: (0, 0)),
            pl.BlockSpec((None, None, d, tf), lambda i, j: (*lk, 0, streamed(j))),
            pl.BlockSpec((None, None, d, tf), lambda i, j: (*lk, 0, streamed(j) + nf)),
            pl.BlockSpec((None, None, f, tn), lambda i, j: (*lk, 0, w2_tile(j))),
        ] + resident,
        out_specs=pl.BlockSpec((tm, d), lambda i, j: (i, 0)),
        out_shape=jax.ShapeDtypeStruct((nrows, d), F32),
        scratch_shapes=[pltpu.VMEM((2, tm, d), BF16), pltpu.VMEM((nf, tm, tf), BF16),
                        pltpu.VMEM((nn, tm, tn), F32)],
        compiler_params=_params(("arbitrary", "arbitrary")),
        name="ffn_sublayer",
    )(*rows_in, mod_l, pre_g.reshape(1, d), post_g.reshape(1, d), w1, w1, w2, *([w1] * (2 * n_res)))


def _inproj_kernel(h_ref, mod_ref, pre_ref, w_ref, o_ref, u_ref, *, tm, n_lat_seg, n_steps, n_pre):
    i = pl.program_id(0)
    j = pl.program_id(1)
    slot = i % 2

    def adaln_rows(tile, row0, n_rows, dst, inline):
        seg = jnp.minimum((tile * tm) // SEQ, n_lat_seg)
        _adaln_rows(u_ref, dst, lambda rows: h_ref[rows, :], mod_ref, pre_ref, 3, seg, row0, n_rows, inline)

    def project():
        o_ref[...] = jnp.dot(u_ref[slot], w_ref[...], preferred_element_type=F32).astype(BF16)

    @pl.when((i == 0) & (j == 0))
    def _():
        adaln_rows(0, 0, tm, 0, False)

    @pl.when(j < n_steps - n_pre)
    def _():
        project()

    @pl.when(j >= n_steps - n_pre)
    def _():
        project()
        adaln_rows(i + 1, (j - (n_steps - n_pre)) * (tm // n_pre), tm // n_pre, 1 - slot, True)


def _inproj_call(h, mod_l, pre_g, w, layer, tm, n_lat_seg):
    nrows, d = h.shape
    n = w.shape[2]
    tn = PROJ_TILE
    n_tiles, n_steps = nrows // tm, n // tn
    n_pre = PROJ_LOOKAHEAD_STEPS
    h_tile = lambda i, j: (jnp.minimum(i + jnp.where(j >= n_steps - n_pre, 1, 0), n_tiles - 1), 0)
    return pl.pallas_call(
        functools.partial(_inproj_kernel, tm=tm, n_lat_seg=n_lat_seg, n_steps=n_steps, n_pre=n_pre),
        grid=(n_tiles, n_steps),
        in_specs=[
            pl.BlockSpec((tm, d), h_tile),
            pl.BlockSpec(mod_l.shape, lambda i, j: (0, 0, 0)),
            pl.BlockSpec((1, d), lambda i, j: (0, 0)),
            pl.BlockSpec((None, d, tn), lambda i, j: (layer, 0, j)),
        ],
        out_specs=pl.BlockSpec((tm, tn), lambda i, j: (i, j)),
        out_shape=jax.ShapeDtypeStruct((nrows, n), BF16),
        scratch_shapes=[pltpu.VMEM((2, tm, d), BF16)],
        compiler_params=_params(("arbitrary", "arbitrary")),
        name="mixer_in_proj",
    )(h, mod_l, pre_g.reshape(1, d), w)


def _rope(t, cos, sa, sb):
    return t * cos + pltpu.roll(t, LANES - 16, 1) * sa + pltpu.roll(t, 16, 1) * sb


def _rope_tables():
    pos = np.arange(SEQ)
    prow = jnp.asarray(pos // GRID_W, F32)
    pcol = jnp.asarray(pos % GRID_W, F32)
    half = 16
    inv = ROPE_BASE ** (-jnp.arange(half, dtype=F32) / half)
    lane = np.arange(LANES)
    freq = jnp.asarray(lane % half)
    use_col = jnp.asarray((lane % 64) >= 32)
    second = jnp.asarray((lane % 32) >= half)
    p = jnp.where(use_col[None, :], pcol[:, None], prow[:, None])
    ang = p * inv[freq][None, :]
    cos, sin = jnp.cos(ang), jnp.sin(ang)
    sa = jnp.where(second[None, :], 0.0, -sin)
    sb = jnp.where(second[None, :], sin, 0.0)
    return cos, sa, sb


def _mixer_specs(nb, width, cols, order):
    ctx_blk0 = nb * SEQ // CTX_LEN

    def spec(rows, col, w, ctx):
        def index(*g):
            b, h = order(*g)
            return ((ctx_blk0 + b) if ctx else b, col // w + h)
        return pl.BlockSpec((rows, w), index)

    return ([spec(SEQ, c, w, False) for c, w in zip(cols, width)]
            + [spec(CTX_LEN, c, w, True) for c, w in zip(cols, width)])


def _mixer_outs(nb, width, order, with_ctx):
    def index(*g):
        b, h = order(*g)
        return (b, h)
    specs = [pl.BlockSpec((SEQ, width), index)]
    shapes = [jax.ShapeDtypeStruct((nb * SEQ, BRANCH_W), BF16)]
    if with_ctx:
        specs.append(pl.BlockSpec((CTX_LEN, width), index))
        shapes.append(jax.ShapeDtypeStruct((nb * CTX_LEN, BRANCH_W), BF16))
    return specs, shapes


def _diff_kernel(ql_ref, kl_ref, vl_ref, qc_ref, kc_ref, vc_ref, cos_ref, sa_ref, sb_ref, lam_ref, g_ref,
                 *rest, lambda_init, with_ctx):
    ol_ref, oc_ref = (rest[0], rest[1]) if with_ctx else (rest[0], None)
    k_all, v_ones = rest[-2:]
    k_all[0:CTX_LEN, :] = kc_ref[...]
    k_all[CTX_LEN:, :] = _rope(kl_ref[...].astype(F32), cos_ref[...], sa_ref[...], sb_ref[...]).astype(BF16)
    v_ones[0:CTX_LEN, 0:LANES] = vc_ref[...]
    v_ones[CTX_LEN:, 0:LANES] = vl_ref[...]
    v_ones[:, LANES:] = jnp.ones((CTX_LEN + SEQ, LANES), BF16)

    lv = lam_ref[...]
    lam = (jnp.exp(jnp.sum(lv[0:1] * lv[1:2], axis=-1, keepdims=True))
           - jnp.exp(jnp.sum(lv[2:3] * lv[3:4], axis=-1, keepdims=True)) + lambda_init)
    lane = lax.broadcasted_iota(jnp.int32, (1, LANES), 1)

    def attend(q, nk):
        q = q * (A_HEAD_DIM ** -0.5)
        k = k_all[0:nk, :]
        v1 = v_ones[0:nk, :]

        rows = QBLK // DIFF_ROW_SPLIT
        qms = [jnp.where(mask, q[r * rows:(r + 1) * rows], 0.0).astype(BF16)
               for r in range(DIFF_ROW_SPLIT) for mask in (lane < A_HEAD_DIM, lane >= A_HEAD_DIM)]
        ss = [lax.dot_general(qm, k, NT_DIMS, preferred_element_type=F32) for qm in qms]
        es = [jnp.exp(s - jnp.max(s, axis=-1, keepdims=True)).astype(BF16) for s in ss]
        ols = [jnp.dot(e, v1, preferred_element_type=F32) for e in es]
        os = [ol[:, :LANES] / ol[:, LANES:] for ol in ols]
        o = jnp.concatenate([os[2 * r] - lam * os[2 * r + 1] for r in range(DIFF_ROW_SPLIT)], axis=0)
        return (_rms(o) * g_ref[...] * (1.0 - lambda_init)).astype(BF16)

    def block(t, carry):
        rows = pl.ds(pl.multiple_of(t * QBLK, QBLK), QBLK)
        q = _rope(ql_ref[rows, :].astype(F32), cos_ref[rows, :], sa_ref[rows, :], sb_ref[rows, :])
        ol_ref[rows, :] = attend(q, CTX_LEN + SEQ)
        return carry

    lax.fori_loop(0, SEQ // QBLK, block, 0, unroll=True)
    if with_ctx:
        oc_ref[...] = attend(qc_ref[...].astype(F32), CTX_LEN)


def _diff_call(proj, tabs, diff_lambda, diff_norm, lambda_init, nb, with_ctx):
    order = lambda b, h: (b, h)
    whole = lambda b, h: (0, 0)
    out_specs, out_shapes = _mixer_outs(nb, LANES, order, with_ctx)
    return pl.pallas_call(
        functools.partial(_diff_kernel, lambda_init=lambda_init, with_ctx=with_ctx),
        grid=(nb, A_HEADS),
        in_specs=_mixer_specs(nb, (LANES,) * 3, (COL_AQ, COL_AK, COL_AV), order) + [
            pl.BlockSpec((SEQ, LANES), whole),
            pl.BlockSpec((SEQ, LANES), whole),
            pl.BlockSpec((SEQ, LANES), whole),
            pl.BlockSpec((4, A_HEAD_DIM), whole),
            pl.BlockSpec((1, LANES), whole),
        ],
        out_specs=out_specs,
        out_shape=out_shapes,
        scratch_shapes=[pltpu.VMEM((CTX_LEN + SEQ, LANES), BF16), pltpu.VMEM((CTX_LEN + SEQ, 2 * LANES), BF16)],
        compiler_params=_params(("parallel", "parallel")),
        name="diff_attention",
    )(*([proj] * 6), *tabs, diff_lambda, diff_norm.reshape(1, LANES))


def _nbr_block_rows(cfg, qi, kj):
    n_rows = SEQ // GRID_W
    r0 = (0, ROWS_PER_QBLK, n_rows - ROWS_PER_QBLK)[cfg]
    ks = min(max(r0 - WIN_R // 2, 0), n_rows - NBR_KROWS)
    r = r0 + qi
    rs = min(max(r - WIN_R // 2, 0), n_rows - WIN_R)
    kr = ks + kj
    return rs <= kr < rs + WIN_R, kr - r + WIN_R - 1


def _nbr_bias_kernel(rpb_ref, o_ref):
    n_dr, n_dc = 2 * WIN_R - 1, 2 * WIN_C - 1
    base = pl.program_id(0) * (n_dr * n_dc)
    qc = lax.broadcasted_iota(jnp.int32, (GRID_W, LANES), 0)
    lane = lax.broadcasted_iota(jnp.int32, (GRID_W, LANES), 1)
    kc = lane % GRID_W
    dc = kc - qc + (WIN_C - 1)
    cs = jnp.clip(qc - WIN_C // 2, 0, GRID_W - WIN_C)
    col_ok = (kc >= cs) & (kc < cs + WIN_C)
    masked = jnp.full((GRID_W, LANES), NBR_MASKED, F32)

    def row_pattern(dr):
        acc = masked
        for x in range(n_dc):
            acc = jnp.where(dc == x, rpb_ref[base + dr * n_dc + x], acc)
        return jnp.where(col_ok, acc, masked)

    pats = [row_pattern(dr) for dr in range(n_dr)]

    def pattern(cfg, qi, kj):
        if kj >= NBR_KROWS:
            return masked
        ok, dr = _nbr_block_rows(cfg, qi, kj)
        return pats[dr] if ok else masked

    for cfg in range(3):
        for qi in range(ROWS_PER_QBLK):
            rows = slice(qi * GRID_W, (qi + 1) * GRID_W)
            for m in range(pl.cdiv(NBR_KEYS, LANES)):
                tile = jnp.where(lane < GRID_W, pattern(cfg, qi, 2 * m), pattern(cfg, qi, 2 * m + 1))
                width = min(LANES, NBR_KEYS - m * LANES)
                o_ref[0, cfg, rows, m * LANES:m * LANES + width] = tile[:, :width]


def _nbr_bias_call(rpb):
    heads = rpb.shape[0]
    return pl.pallas_call(
        _nbr_bias_kernel,
        grid=(heads,),
        in_specs=[pl.BlockSpec(memory_space=pltpu.SMEM)],
        out_specs=pl.BlockSpec((1, 3, QBLK, NBR_KEYS), lambda h: (h, 0, 0, 0)),
        out_shape=jax.ShapeDtypeStruct((heads, 3, QBLK, NBR_KEYS), F32),
        compiler_params=_params(("parallel",)),
        name="nbr_bias_table",
    )(rpb.astype(F32).reshape(-1))


def _nbr_kernel(ql_ref, kl_ref, vl_ref, qc_ref, kc_ref, vc_ref, bias_ref, *rest, with_ctx):
    ol_ref, oc_ref = (rest[0], rest[1]) if with_ctx else (rest[0], None)
    v1l, v1c = rest[-2:]
    scale = B_HEAD_DIM ** -0.5
    lat_blocks = SEQ // QBLK

    for e in range(NBR_HEADS_PER_STEP):
        hs = slice(e * LANES, (e + 1) * LANES)
        v1l[e, :, 0:LANES] = vl_ref[:, hs]
        v1l[e, :, LANES:] = jnp.ones((SEQ, LANES), BF16)
        v1c[e, :, 0:LANES] = vc_ref[:, hs]
        v1c[e, :, LANES:] = jnp.ones((CTX_LEN, LANES), BF16)
        kc = kc_ref[:, hs]

        for blk in range(lat_blocks):
            cfg = 0 if blk == 0 else (2 if blk == lat_blocks - 1 else 1)
            start = min(max((blk * ROWS_PER_QBLK - WIN_R // 2) * GRID_W, 0), SEQ - NBR_KEYS)
            keys = slice(start, start + NBR_KEYS)
            q = ql_ref[blk * QBLK:(blk + 1) * QBLK, hs]
            s_lat = (lax.dot_general(q, kl_ref[keys, hs], NT_DIMS, preferred_element_type=F32) * scale
                     + bias_ref[e, cfg])
            s_ctx = lax.dot_general(q, kc, NT_DIMS, preferred_element_type=F32) * scale
            m = jnp.maximum(jnp.max(s_lat, axis=-1, keepdims=True), jnp.max(s_ctx, axis=-1, keepdims=True))
            ol = (jnp.dot(jnp.exp(s_lat - m).astype(BF16), v1l[e, keys, :], preferred_element_type=F32)
                  + jnp.dot(jnp.exp(s_ctx - m).astype(BF16), v1c[e], preferred_element_type=F32))
            ol_ref[blk * QBLK:(blk + 1) * QBLK, hs] = (ol[:, :LANES] / ol[:, LANES:]).astype(BF16)

        if with_ctx:
            s = lax.dot_general(qc_ref[:, hs], kc, NT_DIMS, preferred_element_type=F32) * scale
            p = jnp.exp(s - jnp.max(s, axis=-1, keepdims=True)).astype(BF16)
            ol = jnp.dot(p, v1c[e], preferred_element_type=F32)
            oc_ref[:, hs] = (ol[:, :LANES] / ol[:, LANES:]).astype(BF16)


def _nbr_call(proj, bias_tab, nb, with_ctx):
    order = lambda g, b: (b, g)
    width = NBR_HEADS_PER_STEP * LANES
    out_specs, out_shapes = _mixer_outs(nb, width, order, with_ctx)
    return pl.pallas_call(
        functools.partial(_nbr_kernel, with_ctx=with_ctx),
        grid=(B_HEADS // NBR_HEADS_PER_STEP, nb),
        in_specs=_mixer_specs(nb, (width,) * 3, (COL_BQ, COL_BK, COL_BV), order) + [
            pl.BlockSpec((NBR_HEADS_PER_STEP, 3, QBLK, NBR_KEYS), lambda g, b: (g, 0, 0, 0)),
        ],
        out_specs=out_specs,
        out_shape=out_shapes,
        scratch_shapes=[pltpu.VMEM((NBR_HEADS_PER_STEP, SEQ, 2 * LANES), BF16),
                        pltpu.VMEM((NBR_HEADS_PER_STEP, CTX_LEN, 2 * LANES), BF16)],
        compiler_params=_params(("parallel", "parallel")),
        name="nbr_attention",
    )(*([proj] * 6), bias_tab)


def _ret_kernel(ql_ref, kl_ref, vl_ref, gl_ref, qc_ref, kc_ref, vc_ref, gc_ref, cos_ref, sa_ref, sb_ref,
                dec_ref, rn_ref, *rest, with_ctx):
    ol_ref, oc_ref = (rest[0], rest[1]) if with_ctx else (rest[0], None)
    of_ref, ob_ref, st_ref = rest[-3:]
    ch = QBLK
    n_chunks = SEQ // ch
    kscale = C_KEY_DIM ** -0.5

    def run():
        ii = lax.broadcasted_iota(jnp.int32, (ch, ch), 0)
        jj = lax.broadcasted_iota(jnp.int32, (ch, ch), 1)
        dist = (ii - jj).astype(F32)
        pos = lax.broadcasted_iota(jnp.int32, (ch, 1), 0).astype(F32)
        lane = lax.broadcasted_iota(jnp.int32, (1, LANES), 1)
        rn = rn_ref[...]

        def log_sigmoid(x):
            return -(jnp.log1p(jnp.exp(-jnp.abs(x))) + jnp.maximum(-x, 0.0))

        heads = []
        for e in range(2):
            lgf = log_sigmoid(dec_ref[0, 0, e:e + 1, 0:1])
            lgb = log_sigmoid(dec_ref[0, 1, e:e + 1, 0:1])
            heads.append(dict(
                mask=(lane >= e * C_KEY_DIM) & (lane < (e + 1) * C_KEY_DIM),
                intra_f=jnp.where(dist >= 0, jnp.exp(lgf * jnp.maximum(dist, 0.0)), 0.0),
                intra_b=jnp.where(dist <= 0, jnp.exp(lgb * jnp.maximum(-dist, 0.0)), 0.0),
                qdec_f=jnp.exp(lgf * (pos + 1.0)), kdec_f=jnp.exp(lgf * (ch - 1.0 - pos)),
                cdec_f=jnp.exp(lgf * ch),
                qdec_b=jnp.exp(lgb * (ch - pos)), kdec_b=jnp.exp(lgb * pos),
                cdec_b=jnp.exp(lgb * ch),
                vs=slice(e * LANES, (e + 1) * LANES),
            ))

        def finish(o, g):
            return (_silu(g.astype(F32)) * (_rms(o) * rn)).astype(BF16)

        qc = qc_ref[...].astype(F32)
        kc = kc_ref[...].astype(F32) * kscale
        for e, hd in enumerate(heads):
            v = vc_ref[:, hd["vs"]]
            st_ref[e] = lax.dot_general((kc * hd["kdec_f"]).astype(BF16), v, TN_DIMS, preferred_element_type=F32)
            st_ref[2 + e] = lax.dot_general((kc * hd["kdec_b"]).astype(BF16), v, TN_DIMS,
                                            preferred_element_type=F32)
            if with_ctx:
                qe = jnp.where(hd["mask"], qc, 0.0).astype(BF16)
                att = lax.dot_general(qe, kc.astype(BF16), NT_DIMS, preferred_element_type=F32)
                att = att * (hd["intra_f"] + hd["intra_b"])
                o = jnp.dot(att.astype(BF16), v, preferred_element_type=F32)
                oc_ref[:, hd["vs"]] = finish(o, gc_ref[:, hd["vs"]])

        def chunk_rows(c):
            return pl.ds(pl.multiple_of(c * ch, ch), ch)

        def sweep(c, fwd):
            rows = chunk_rows(c)
            tabs = (cos_ref[rows, :], sa_ref[rows, :], sb_ref[rows, :])
            q = _rope(ql_ref[rows, :].astype(F32), *tabs)
            k = _rope(kl_ref[rows, :].astype(F32), *tabs) * kscale
            kb = k.astype(BF16)
            for e, hd in enumerate(heads):
                d = "f" if fwd else "b"
                si = e if fwd else 2 + e
                v = vl_ref[rows, hd["vs"]]
                qe = jnp.where(hd["mask"], q, 0.0)
                att = lax.dot_general(qe.astype(BF16), kb, NT_DIMS, preferred_element_type=F32) * hd["intra_" + d]
                st = st_ref[si]
                o = (jnp.dot(att.astype(BF16), v, preferred_element_type=F32)
                     + jnp.dot((qe * hd["qdec_" + d]).astype(BF16), st.astype(BF16), preferred_element_type=F32))
                st_ref[si] = st * hd["cdec_" + d] + lax.dot_general(
                    (k * hd["kdec_" + d]).astype(BF16), v, TN_DIMS, preferred_element_type=F32)
                (of_ref if fwd else ob_ref)[rows, hd["vs"]] = o

        def finish_chunk(c):
            rows = chunk_rows(c)
            for hd in heads:
                vs = hd["vs"]
                ol_ref[rows, vs] = finish(of_ref[rows, vs] + ob_ref[rows, vs], gl_ref[rows, vs])

        def first_half(t, carry):
            sweep(t, True)
            sweep(n_chunks - 1 - t, False)
            return carry

        def second_half(t, carry):
            sweep(t, True)
            sweep(n_chunks - 1 - t, False)
            finish_chunk(t)
            finish_chunk(n_chunks - 1 - t)
            return carry

        lax.fori_loop(0, n_chunks // 2, first_half, 0, unroll=True)
        lax.fori_loop(n_chunks // 2, n_chunks, second_half, 0, unroll=True)

    run()


def _ret_call(proj, tabs, ret_decay, ret_norm, nb, with_ctx):
    pairs = C_HEADS // 2
    w2 = 2 * LANES
    order = lambda b, p: (b, p)
    whole = lambda b, p: (0, 0)
    out_specs, out_shapes = _mixer_outs(nb, w2, order, with_ctx)
    dec = jnp.broadcast_to(ret_decay.astype(F32).reshape(2, pairs, 2).transpose(1, 0, 2)[..., None],
                           (pairs, 2, 2, LANES))
    return pl.pallas_call(
        functools.partial(_ret_kernel, with_ctx=with_ctx),
        grid=(nb, pairs),
        in_specs=_mixer_specs(nb, (LANES, LANES, w2, w2), (COL_CQ, COL_CK, COL_CV, COL_CG), order) + [
            pl.BlockSpec((SEQ, LANES), whole),
            pl.BlockSpec((SEQ, LANES), whole),
            pl.BlockSpec((SEQ, LANES), whole),
            pl.BlockSpec((1, 2, 2, LANES), lambda b, p: (p, 0, 0, 0)),
            pl.BlockSpec((1, LANES), whole),
        ],
        out_specs=out_specs,
        out_shape=out_shapes,
        scratch_shapes=[
            pltpu.VMEM((SEQ, w2), F32),
            pltpu.VMEM((SEQ, w2), F32),
            pltpu.VMEM((4, LANES, LANES), F32),
        ],
        compiler_params=_params(("parallel", "parallel")),
        name="retention",
    )(*([proj] * 8), *tabs, dec, ret_norm.reshape(1, LANES))


def _merge_kernel(h_ref, mod_ref, post_ref, g_lo_ref, g_hi_ref, wb_ref, wo_ref, *rest, n_lat_tiles, with_ctx):
    d = h_ref.shape[1]
    o_ref = rest[-1]
    if with_ctx:
        is_lat = pl.program_id(0) < n_lat_tiles
        xs = [jnp.where(is_lat, rest[2 * k][...], rest[2 * k + 1][...]) for k in range(3)]
    else:
        xs = [rest[k][...] for k in range(3)]
    g_lo = g_lo_ref[...]
    g_hi = g_hi_ref[...]
    gates = (g_lo[:, :d], jnp.concatenate([g_lo[:, d:], g_hi[:, :d // 2]], axis=1), g_hi[:, d // 2:])
    m = None
    for k in range(3):
        t = jax.nn.sigmoid(gates[k].astype(F32)) * jnp.dot(xs[k], wb_ref[k], preferred_element_type=F32)
        m = t if m is None else m + t
    y = jnp.dot(m.astype(BF16), wo_ref[...], preferred_element_type=F32)
    o_ref[...] = h_ref[...] + mod_ref[0, 5:6, :] * (_rms(y) * post_ref[...])


def _merge_call(h, mod_l, post_g, branches, proj, wb, wo, layer, nrows, n_lat_seg):
    d = h.shape[1]
    tm = CTX_LEN
    with_ctx = len(branches[0]) == 2
    n_lat_tiles = n_lat_seg * SEQ // tm
    gw = 3 * d // 2
    g0 = COL_GATES // gw
    resident = dict(pipeline_mode=pl.Buffered(1))
    row = lambda i: (i, 0)
    lat_row = lambda i: (jnp.minimum(i, n_lat_tiles - 1), 0)
    ctx_row = lambda i: (jnp.maximum(i - n_lat_tiles, 0), 0)
    branch_specs, branch_args = [], []
    for outs in branches:
        branch_specs.append(pl.BlockSpec((tm, BRANCH_W), lat_row))
        if with_ctx:
            branch_specs.append(pl.BlockSpec((tm, BRANCH_W), ctx_row))
        branch_args.extend(outs)
    return pl.pallas_call(
        functools.partial(_merge_kernel, n_lat_tiles=n_lat_tiles, with_ctx=with_ctx),
        grid=(nrows // tm,),
        in_specs=[
            pl.BlockSpec((tm, d), row),
            pl.BlockSpec((1, N_MOD, d), lambda i: (jnp.minimum((i * tm) // SEQ, n_lat_seg), 0, 0)),
            pl.BlockSpec((1, d), lambda i: (0, 0)),
            pl.BlockSpec((tm, gw), lambda i: (i, g0)),
            pl.BlockSpec((tm, gw), lambda i: (i, g0 + 1)),
            pl.BlockSpec((None, 3, BRANCH_W, d), lambda i: (layer, 0, 0, 0), **resident),
            pl.BlockSpec((None, d, d), lambda i: (layer, 0, 0), **resident),
        ] + branch_specs,
        out_specs=pl.BlockSpec((tm, d), row),
        out_shape=jax.ShapeDtypeStruct((nrows, d), F32),
        compiler_params=_params(("parallel",)),
        name="mixer_merge",
    )(h, mod_l, post_g.reshape(1, d), proj, proj, wb, wo, *branch_args)


def kernel(x, c, ctx, c_ctx, w_mod, b_mod, pre_norm, post_norm, ffn_w_in, ffn_w_out, w_in, diff_lambda,
           diff_norm, na_rpb, ret_decay, ret_norm, w_branch, w_out):
    nb, seq, d = x.shape
    assert (seq, d) == (SEQ, D_MODEL) and ctx.shape == (nb, CTX_LEN, d)
    n_lat, n_ctx = nb * SEQ, nb * CTX_LEN
    n_all = n_lat + n_ctx
    tm = min(FFN_ROWS, n_ctx)
    pad = (-(nb + 1)) % 8
    cc = jnp.concatenate([c, c_ctx[None, :], jnp.zeros((pad, d), c.dtype)], axis=0)
    mods = _mod_call(cc, w_mod, b_mod)
    tabs = _rope_tables()
    w1, w2 = ffn_w_in.astype(BF16), ffn_w_out.astype(BF16)
    w_proj, wb, wo = w_in.astype(BF16), w_branch.astype(BF16), w_out.astype(BF16)

    for l in range(DEPTH):
        with_ctx = l < DEPTH - 1
        lambda_init = 0.8 - 0.6 * math.exp(-0.3 * l)
        nrows_out = n_all if with_ctx else n_lat
        mod_l = mods[l, :nb + 1].reshape(nb + 1, N_MOD, d)
        ffn = functools.partial(_ffn_call, tm=tm, n_lat_seg=nb)

        rows_in = dict(h=x.reshape(n_lat, d), ctx_rows=ctx.reshape(n_ctx, d)) if l == 0 else dict(h=h)
        h = ffn(mod_l=mod_l, k0=0, pre_g=pre_norm[l, 0], post_g=post_norm[l, 0], w1=w1, w2=w2, lk=(l, 0),
                nrows=n_all, **rows_in)
        proj = _inproj_call(h, mod_l, pre_norm[l, 1], w_proj, l, min(PROJ_ROWS, n_ctx), nb)
        oa = _diff_call(proj, tabs, diff_lambda[l], diff_norm[l], lambda_init, nb, with_ctx)
        ob = _nbr_call(proj, _nbr_bias_call(na_rpb[l]), nb, with_ctx)
        yr = _ret_call(proj, tabs, ret_decay[l], ret_norm[l], nb, with_ctx)
        h = _merge_call(h, mod_l, post_norm[l, 1], (oa, ob, yr), proj, wb, wo, l, nrows_out, nb)
        h = ffn(h, mod_l, 6, pre_norm[l, 2], post_norm[l, 2], w1, w2, (l, 1), nrows_out)
    return h.reshape(nb, SEQ, d)
```
